```python
import math
import jax, jax.numpy as jnp
from jax import lax
import numpy as np

D_MODEL = 2048
BATCH = 4
SEQ = 2048
DEPTH = 4
DEC_BATCH = 32
DEC_SEQ = 8
PAST_LEN = 16384
PAGE_SIZE = 128

HEAD_DIM = 128
N_Q_HEADS = 8
N_KV_HEADS = 2
ATTN_W = N_Q_HEADS * HEAD_DIM
KV_W = N_KV_HEADS * HEAD_DIM
LRU_W = D_MODEL // 2
N_LRU_BLOCKS = 8
LRU_BLOCK = LRU_W // N_LRU_BLOCKS
CONV_W = 4
LRU_C = 8.0
MIX_W = ATTN_W + LRU_W
IN_W = ATTN_W + 2 * KV_W + 2 * LRU_W
D_FF = 4 * D_MODEL
WINDOW = 128
ATTN_BLOCK = WINDOW
N_BUCKETS = 32
MAX_DISTANCE = 128
N_META = 16
EPS = 1e-6

kernel_name = "hymba_griffin_swa_sink_hybrid_step"


def rmsnorm(x, g):
    xf = x.astype(jnp.float32)
    y = xf * lax.rsqrt(jnp.mean(xf * xf, axis=-1, keepdims=True) + EPS)
    return (y * g.astype(jnp.float32)).astype(x.dtype)


def rel_bucket(dist):
    n = jnp.maximum(dist, 0)
    max_exact = N_BUCKETS // 2
    nf = jnp.maximum(n, 1).astype(jnp.float32)
    large = max_exact + (jnp.log(nf / max_exact) / math.log(MAX_DISTANCE / max_exact)
                         * (N_BUCKETS - max_exact)).astype(jnp.int32)
    large = jnp.minimum(large, N_BUCKETS - 1)
    return jnp.where(n < max_exact, n, large)


def sink_attention(q, k, v, q_pos, k_pos, k_valid, sinks, rel_bias):
    B, N, Q, H, D = q.shape
    S = k.shape[2]
    G = H // N_KV_HEADS
    qf = q.astype(jnp.float32).reshape(B, N, Q, N_KV_HEADS, G, D)
    s = jnp.einsum('bnqkgd,bnskd->bnkgqs', qf, k.astype(jnp.float32)) * (D ** -0.5)
    dist = q_pos[:, :, None] - k_pos[:, None, :]
    mask = k_valid[:, None, :] & (dist >= 0) & (dist < WINDOW)
    bias = rel_bias.astype(jnp.float32)[rel_bucket(dist)]
    bias = jnp.transpose(bias, (0, 3, 1, 2)).reshape(N, N_KV_HEADS, G, Q, S)
    s = jnp.where(mask[:, None, None], s + bias[None], -jnp.inf)
    sink = sinks.astype(jnp.float32).reshape(1, 1, N_KV_HEADS, G, 1, 1)
    m = jnp.maximum(jnp.max(s, axis=-1, keepdims=True), sink)
    p = jnp.exp(s - m)
    denom = jnp.sum(p, axis=-1, keepdims=True) + jnp.exp(sink - m)
    o = jnp.einsum('bnkgqs,bnskd->bnqkgd', p / denom, v.astype(jnp.float32))
    return o.reshape(B, N, Q, H, D).astype(q.dtype)


def prompt_window_attention(q, k, v, sinks, rel_bias):
    B, T = q.shape[:2]
    p0 = (-N_META) % ATTN_BLOCK
    nb = -(-(p0 + T) // ATTN_BLOCK)
    p1 = nb * ATTN_BLOCK - p0 - T
    qb = jnp.pad(q, ((0, 0), (p0, p1), (0, 0), (0, 0))).reshape(B, nb, ATTN_BLOCK, N_Q_HEADS, HEAD_DIM)

    def key_blocks(t):
        tp = jnp.pad(t, ((0, 0), (p0 + ATTN_BLOCK, p1), (0, 0), (0, 0)))
        prev = tp[:, :nb * ATTN_BLOCK].reshape(B, nb, ATTN_BLOCK, N_KV_HEADS, HEAD_DIM)
        cur = tp[:, ATTN_BLOCK:].reshape(B, nb, ATTN_BLOCK, N_KV_HEADS, HEAD_DIM)
        return jnp.concatenate([prev, cur], axis=2)

    start = jnp.arange(nb, dtype=jnp.int32)[:, None] * ATTN_BLOCK - p0
    q_pos = start + jnp.arange(ATTN_BLOCK, dtype=jnp.int32)[None]
    k_pos = start - ATTN_BLOCK + jnp.arange(2 * ATTN_BLOCK, dtype=jnp.int32)[None]
    k_valid = (k_pos >= 0) & (k_pos < T)
    o = sink_attention(qb, key_blocks(k), key_blocks(v), q_pos, k_pos, k_valid, sinks, rel_bias)
    return o.reshape(B, nb * ATTN_BLOCK, N_Q_HEADS, HEAD_DIM)[:, p0:p0 + T]


def causal_conv(x, prev, w, b):
    T = x.shape[1]
    xp = jnp.concatenate([prev.astype(x.dtype), x], axis=1)
    out = b.astype(x.dtype) + xp[:, 0:T] * w[0].astype(x.dtype)
    for j in range(1, CONV_W):
        out = out + xp[:, j:j + T] * w[j].astype(x.dtype)
    return out, xp[:, -(CONV_W - 1):]


def rg_lru(x, pos, h0, w_a, b_a, w_x, b_x, lam):
    B, T, _ = x.shape
    xf = x.astype(jnp.float32)
    xb = xf.reshape(B, T, N_LRU_BLOCKS, LRU_BLOCK)
    gate_a = jax.nn.sigmoid(jnp.einsum('btnc,ncd->btnd', xb, w_a.astype(jnp.float32)).reshape(B, T, LRU_W)
                            + b_a.astype(jnp.float32))
    gate_x = jax.nn.sigmoid(jnp.einsum('btnc,ncd->btnd', xb, w_x.astype(jnp.float32)).reshape(B, T, LRU_W)
                            + b_x.astype(jnp.float32))
    log_a = -LRU_C * gate_a * jax.nn.softplus(-lam.astype(jnp.float32))
    a = jnp.exp(log_a)
    mult = jnp.sqrt(-jnp.expm1(2.0 * log_a))
    mult = jnp.where((pos == 0)[None, :, None], 1.0, mult)
    bterm = xf * gate_x * mult
    bterm = bterm.at[:, 0].add(a[:, 0] * h0.astype(jnp.float32))

    def combine(l, r):
        return (l[0] * r[0], r[0] * l[1] + r[1])

    _, h = lax.associative_scan(combine, (a, bterm), axis=1)
    return h, h[:, -1]


def setup_inputs(seed: int = 0) -> dict:
    key = jax.random.key(seed)
    ks = jax.random.split(key, 32)
    f32 = jnp.float32

    def nrm(k, shape, scale):
        return jax.random.normal(k, shape, f32) * scale

    kv_buf = min(WINDOW, PAST_LEN)
    u = jax.random.uniform(ks[15], (DEPTH, LRU_W), f32, 0.9, 0.999)
    a0 = u ** (1.0 / LRU_C)
    lru_lambda = jnp.log(a0) - jnp.log1p(-a0)
    return {
        "x_prompt": nrm(ks[0], (BATCH, SEQ, D_MODEL), 1.0),
        "x_sample": nrm(ks[1], (DEC_BATCH, DEC_SEQ, D_MODEL), 1.0),
        "cache_k_win": nrm(ks[2], (DEPTH, DEC_BATCH, kv_buf, N_KV_HEADS, HEAD_DIM), 1.0),
        "cache_v_win": nrm(ks[3], (DEPTH, DEC_BATCH, kv_buf, N_KV_HEADS, HEAD_DIM), 1.0),
        "state_conv": nrm(ks[4], (DEPTH, DEC_BATCH, CONV_W - 1, LRU_W), 1.0),
        "state_lru": nrm(ks[5], (DEPTH, DEC_BATCH, LRU_W), 0.5),
        "meta_tokens": nrm(ks[6], (N_META, D_MODEL), 1.0),
        "norm_mix_g": 1.0 + nrm(ks[7], (DEPTH, D_MODEL), 0.02),
        "w_in": nrm(ks[8], (DEPTH, D_MODEL, IN_W), D_MODEL ** -0.5),
        "conv_w": nrm(ks[9], (DEPTH, CONV_W, LRU_W), CONV_W ** -0.5),
        "conv_b": nrm(ks[10], (DEPTH, LRU_W), 0.01),
        "w_gate_a": nrm(ks[11], (DEPTH, N_LRU_BLOCKS, LRU_BLOCK, LRU_BLOCK), LRU_BLOCK ** -0.5),
        "b_gate_a": nrm(ks[12], (DEPTH, LRU_W), 0.01),
        "w_gate_x": nrm(ks[13], (DEPTH, N_LRU_BLOCKS, LRU_BLOCK, LRU_BLOCK), LRU_BLOCK ** -0.5),
        "b_gate_x": nrm(ks[14], (DEPTH, LRU_W), 0.01),
        "lru_lambda": lru_lambda,
        "attn_sinks": nrm(ks[16], (DEPTH, N_Q_HEADS), 1.0),
        "rel_bias": nrm(ks[17], (N_BUCKETS, N_Q_HEADS), 0.5),
        "attn_out_g": 1.0 + nrm(ks[18], (DEPTH, ATTN_W), 0.02),
        "rec_out_g": 1.0 + nrm(ks[19], (DEPTH, LRU_W), 0.02),
        "w_out": nrm(ks[20], (DEPTH, MIX_W, D_MODEL), MIX_W ** -0.5),
        "norm_mlp_g": 1.0 + nrm(ks[21], (DEPTH, D_MODEL), 0.02),
        "w_up": nrm(ks[22], (DEPTH, D_MODEL, D_FF), D_MODEL ** -0.5),
        "w_down": nrm(ks[23], (DEPTH, D_FF, D_MODEL), D_FF ** -0.5),
        "final_norm_g": 1.0 + nrm(ks[24], (D_MODEL,), 0.02),
    }


def reference(x_prompt, x_sample, cache_k_win, cache_v_win, state_conv, state_lru,
              meta_tokens, norm_mix_g, w_in, conv_w, conv_b, w_gate_a, b_gate_a, w_gate_x, b_gate_x,
              lru_lambda, attn_sinks, rel_bias, attn_out_g, rec_out_g, w_out, norm_mlp_g, w_up, w_down,
              final_norm_g):
    split_at = [ATTN_W, ATTN_W + KV_W, ATTN_W + 2 * KV_W, ATTN_W + 2 * KV_W + LRU_W]

    def layer(h, l, pos, attend, conv_prev, lru_prev):
        B, T = h.shape[:2]
        u = rmsnorm(h, norm_mix_g[l])
        z = u @ w_in[l]
        q, k, v, xr, gr = jnp.split(z, split_at, axis=-1)
        q = q.reshape(B, T, N_Q_HEADS, HEAD_DIM)
        k = k.reshape(B, T, N_KV_HEADS, HEAD_DIM)
        v = v.reshape(B, T, N_KV_HEADS, HEAD_DIM)
        attn, k_buf, v_buf = attend(q, k, v, l)
        xc, conv_new = causal_conv(xr, conv_prev, conv_w[l], conv_b[l])
        hr, lru_new = rg_lru(xc, pos, lru_prev, w_gate_a[l], b_gate_a[l], w_gate_x[l], b_gate_x[l],
                             lru_lambda[l])
        rec = (hr * jax.nn.gelu(gr.astype(jnp.float32))).astype(h.dtype)
        mixed = jnp.concatenate([rmsnorm(attn.reshape(B, T, ATTN_W), attn_out_g[l]),
                                 rmsnorm(rec, rec_out_g[l])], axis=-1)
        h = h + mixed @ w_out[l]
        hid = jnp.square(jax.nn.relu(rmsnorm(h, norm_mlp_g[l]) @ w_up[l]))
        h = h + hid @ w_down[l]
        return h, k_buf, v_buf, conv_new, lru_new.astype(h.dtype)

    def attend_prompt(q, k, v, l):
        o = prompt_window_attention(q, k, v, attn_sinks[l], rel_bias)
        return o, k[:, -WINDOW:], v[:, -WINDOW:]

    def attend_sample(q, k, v, l):
        kk = jnp.concatenate([cache_k_win[l].astype(k.dtype), k], axis=1)
        vv = jnp.concatenate([cache_v_win[l].astype(v.dtype), v], axis=1)
        buf = cache_k_win.shape[2]
        T = q.shape[1]
        q_pos = (PAST_LEN + jnp.arange(T, dtype=jnp.int32))[None]
        k_pos = (PAST_LEN - buf + jnp.arange(buf + T, dtype=jnp.int32))[None]
        o = sink_attention(q[:, None], kk[:, None], vv[:, None], q_pos, k_pos, k_pos >= 0,
                           attn_sinks[l], rel_bias)
        return o[:, 0], kk[:, -buf:], vv[:, -buf:]

    meta = jnp.broadcast_to(meta_tokens.astype(x_prompt.dtype)[None], (x_prompt.shape[0], N_META, D_MODEL))
    hp = jnp.concatenate([meta, x_prompt], axis=1)
    Bp, Tp = hp.shape[:2]
    pos_p = jnp.arange(Tp, dtype=jnp.int32)
    conv0 = jnp.zeros((Bp, CONV_W - 1, LRU_W), hp.dtype)
    lru0 = jnp.zeros((Bp, LRU_W), jnp.float32)

    hs = x_sample
    pos_s = PAST_LEN + jnp.arange(x_sample.shape[1], dtype=jnp.int32)

    kp_l, vp_l, cp_l, lp_l = [], [], [], []
    ks_l, vs_l, cs_l, ls_l = [], [], [], []
    for l in range(DEPTH):
        hp, kb, vb, cb, lb = layer(hp, l, pos_p, attend_prompt, conv0, lru0)
        kp_l.append(kb); vp_l.append(vb); cp_l.append(cb); lp_l.append(lb)
        hs, kb, vb, cb, lb = layer(hs, l, pos_s, attend_sample, state_conv[l], state_lru[l])
        ks_l.append(kb); vs_l.append(vb); cs_l.append(cb); ls_l.append(lb)

    y_prompt = rmsnorm(hp, final_norm_g)[:, N_META:]
    y_sample = rmsnorm(hs, final_norm_g)
    return (y_prompt, y_sample,
            jnp.stack(kp_l), jnp.stack(vp_l), jnp.stack(cp_l), jnp.stack(lp_l),
            jnp.stack(ks_l), jnp.stack(vs_l), jnp.stack(cs_l), jnp.stack(ls_l))
```

```python
import functools
import math

import jax
import jax.numpy as jnp
from jax import lax
from jax.experimental import pallas as pl
from jax.experimental.pallas import tpu as pltpu

f32 = jnp.float32
bf16 = jnp.bfloat16

D_MODEL = 2048
DEPTH = 4
PAST_LEN = 16384
HEAD_DIM = 128
N_Q_HEADS = 8
N_KV_HEADS = 2
Q_PER_KV = N_Q_HEADS // N_KV_HEADS
ATTN_W = N_Q_HEADS * HEAD_DIM
KV_W = N_KV_HEADS * HEAD_DIM
LRU_W = D_MODEL // 2
N_LRU_BLOCKS = 8
LRU_BLOCK = LRU_W // N_LRU_BLOCKS
CONV_W = 4
LRU_C = 8.0
IN_W = ATTN_W + 2 * KV_W + 2 * LRU_W
D_FF = 4 * D_MODEL
WINDOW = 128
N_BUCKETS = 32
MAX_DISTANCE = 128
N_META = 16
EPS = 1e-6
ATTN_SCALE = HEAD_DIM ** -0.5

Z_Q = 0
Z_XR = ATTN_W
Z_GR = ATTN_W + LRU_W
Z_K = ATTN_W + 2 * LRU_W
Z_V = Z_K + KV_W

SUBLANES = 8
BF16_ROWS = 16
V7X_VMEM_BYTES = 64 * 1024 * 1024
VMEM_CAP_BYTES = V7X_VMEM_BYTES - 6 * 1024 * 1024


def _vmem_limit(pipelined_bytes, scratch_bytes=0):
    est = 2 * pipelined_bytes + scratch_bytes
    return int(min(VMEM_CAP_BYTES, est + est // 2 + (8 << 20)))


def _params(semantics, vmem_bytes):
    return pltpu.CompilerParams(dimension_semantics=semantics, vmem_limit_bytes=vmem_bytes)


def _rms_scale(x):
    return lax.rsqrt(jnp.mean(x * x, axis=-1, keepdims=True) + EPS)


def _rmsnorm_rows(src_ref, g_ref, dst_ref, n_rows):
    def step(i, _):
        r = pl.multiple_of(i * BF16_ROWS, BF16_ROWS)
        x = src_ref[pl.ds(r, BF16_ROWS), :]
        dst_ref[pl.ds(r, BF16_ROWS), :] = (x * _rms_scale(x) * g_ref[...]).astype(dst_ref.dtype)
        return 0
    lax.fori_loop(0, n_rows // BF16_ROWS, step, 0)


def _inproj_body(h_ref, g_ref, w_ref, z_ref, u_ref, *, tm):
    @pl.when(pl.program_id(1) == 0)
    def _():
        _rmsnorm_rows(h_ref, g_ref, u_ref, tm)
    z_ref[...] = jnp.dot(u_ref[...], w_ref[...], preferred_element_type=f32)


def _inproj(h, g, w, *, tm, tn):
    m = h.shape[0]
    blocks = tm * D_MODEL * 4 + D_MODEL * tn * 2 + tm * tn * 4
    return pl.pallas_call(
        functools.partial(_inproj_body, tm=tm),
        grid=(m // tm, IN_W // tn),
        in_specs=[pl.BlockSpec((tm, D_MODEL), lambda i, j: (i, 0)),
                  pl.BlockSpec((1, D_MODEL), lambda i, j: (0, 0)),
                  pl.BlockSpec((D_MODEL, tn), lambda i, j: (0, j))],
        out_specs=pl.BlockSpec((tm, tn), lambda i, j: (i, j)),
        out_shape=jax.ShapeDtypeStruct((m, IN_W), f32),
        scratch_shapes=[pltpu.VMEM((tm, D_MODEL), bf16)],
        compiler_params=_params(("parallel", "arbitrary"), _vmem_limit(blocks, tm * D_MODEL * 2)),
        name="inproj",
    )(h, g, w)


def _bias_table_body(rel_ref, o_ref):
    qi = lax.broadcasted_iota(jnp.int32, (WINDOW, 2 * WINDOW), 0)
    kj = lax.broadcasted_iota(jnp.int32, (WINDOW, 2 * WINDOW), 1)
    dist = qi + WINDOW - kj
    n = jnp.maximum(dist, 0)
    max_exact = N_BUCKETS // 2
    nf = jnp.maximum(n, 1).astype(f32)
    large = max_exact + (jnp.log(nf / max_exact) / math.log(MAX_DISTANCE / max_exact)
                         * (N_BUCKETS - max_exact)).astype(jnp.int32)
    large = jnp.minimum(large, N_BUCKETS - 1)
    bucket = jnp.where(n < max_exact, n, large)
    visible = (dist >= 0) & (dist < WINDOW)
    for h in range(N_Q_HEADS):
        acc = jnp.zeros((WINDOW, 2 * WINDOW), f32)
        for b in range(N_BUCKETS):
            acc = jnp.where(bucket == b, rel_ref[b, h], acc)
        o_ref[h] = jnp.where(visible, acc, -jnp.inf)


def _bias_table(rel_bias):
    return pl.pallas_call(
        _bias_table_body,
        in_specs=[pl.BlockSpec(memory_space=pltpu.SMEM)],
        out_shape=jax.ShapeDtypeStruct((N_Q_HEADS, WINDOW, 2 * WINDOW), f32),
        name="bias_table",
    )(rel_bias)


def _attn_block(q_rows, kwin, vwin, bias_ref, sinks_ref, n_rows, mask_prev):
    outs = []
    for kv in range(N_KV_HEADS):
        heads = range(kv * Q_PER_KV, (kv + 1) * Q_PER_KV)
        k = kwin[:, kv * HEAD_DIM:(kv + 1) * HEAD_DIM]
        v = vwin[:, kv * HEAD_DIM:(kv + 1) * HEAD_DIM]
        q = jnp.concatenate([q_rows[:, h * HEAD_DIM:(h + 1) * HEAD_DIM] for h in heads], axis=0).astype(bf16)
        bias = jnp.concatenate([bias_ref[h * WINDOW:h * WINDOW + n_rows, :] for h in heads], axis=0)
        s = lax.dot_general(q, k, (((1,), (1,)), ((), ())), preferred_element_type=f32) * ATTN_SCALE + bias
        if mask_prev:
            col = lax.broadcasted_iota(jnp.int32, s.shape, 1)
            s = jnp.where(col < WINDOW, -jnp.inf, s)
        sink = jnp.concatenate([jnp.full((n_rows, 1), sinks_ref[h], f32) for h in heads], axis=0)
        m = jnp.maximum(jnp.max(s, axis=-1, keepdims=True), sink)
        p = jnp.exp(s - m)
        denom = jnp.sum(p, axis=-1, keepdims=True) + jnp.exp(sink - m)
        o = jnp.dot(p.astype(bf16), v, preferred_element_type=f32) / denom
        outs.extend(o[g * n_rows:(g + 1) * n_rows] for g in range(Q_PER_KV))
    return jnp.concatenate(outs, axis=1)


def _attn_prompt_body(q_ref, k_ref, v_ref, bias_ref, sinks_ref, g_ref, o_ref, kpad_ref, vpad_ref, *, t_len):
    n_full = t_len // WINDOW
    tail = t_len - n_full * WINDOW
    pad_rows = kpad_ref.shape[0]
    for src, dst in ((k_ref, kpad_ref), (v_ref, vpad_ref)):
        dst[0:WINDOW, :] = jnp.zeros((WINDOW, KV_W), bf16)
        dst[WINDOW:WINDOW + t_len, :] = src[...].astype(bf16)
        dst[WINDOW + t_len:pad_rows, :] = jnp.zeros((pad_rows - WINDOW - t_len, KV_W), bf16)

    def block(r0, n_rows, mask_prev):
        q_rows = q_ref[pl.ds(r0, n_rows), :]
        o = _attn_block(q_rows, kpad_ref[pl.ds(r0, 2 * WINDOW), :], vpad_ref[pl.ds(r0, 2 * WINDOW), :],
                        bias_ref, sinks_ref, n_rows, mask_prev)
        o_ref[pl.ds(r0, n_rows), :] = (o * _rms_scale(o) * g_ref[...]).astype(o_ref.dtype)

    block(0, WINDOW, True)

    def step(j, _):
        block(pl.multiple_of(j * WINDOW, WINDOW), WINDOW, False)
        return 0
    lax.fori_loop(1, n_full, step, 0)
    if tail:
        block(n_full * WINDOW, tail, False)


def _attn_prompt(z, bias, sinks, g, *, n_batch, t_len):
    pad_rows = (pl.cdiv(t_len, WINDOW) + 1) * WINDOW
    blocks = t_len * (ATTN_W + 2 * KV_W) * 4 + t_len * ATTN_W * 2 + bias.size * 4
    return pl.pallas_call(
        functools.partial(_attn_prompt_body, t_len=t_len),
        grid=(n_batch,),
        in_specs=[pl.BlockSpec((t_len, ATTN_W), lambda b: (b, Z_Q // ATTN_W)),
                  pl.BlockSpec((t_len, KV_W), lambda b: (b, Z_K // KV_W)),
                  pl.BlockSpec((t_len, KV_W), lambda b: (b, Z_V // KV_W)),
                  pl.BlockSpec(bias.shape, lambda b: (0, 0)),
                  pl.BlockSpec(memory_space=pltpu.SMEM),
                  pl.BlockSpec((1, ATTN_W), lambda b: (0, 0))],
        out_specs=pl.BlockSpec((t_len, ATTN_W), lambda b: (b, 0)),
        out_shape=jax.ShapeDtypeStruct((n_batch * t_len, ATTN_W), bf16),
        scratch_shapes=[pltpu.VMEM((pad_rows, KV_W), bf16), pltpu.VMEM((pad_rows, KV_W), bf16)],
        compiler_params=_params(("parallel",), _vmem_limit(blocks, 2 * pad_rows * KV_W * 2)),
        name="attn_prompt",
    )(z, z, z, bias, sinks, g)


def _attn_sample_body(z_ref, ck_ref, cv_ref, bias_ref, sinks_ref, g_ref, o_ref, nk_ref, nv_ref, acc_ref,
                      *, n_batch, t_len):
    buf = ck_ref.shape[1]
    zeros = jnp.zeros((2 * WINDOW - buf - t_len, KV_W), f32)

    def step(b, _):
        r0 = pl.multiple_of(b * t_len, t_len)
        q_rows = z_ref[pl.ds(r0, t_len), Z_Q:Z_Q + ATTN_W]
        k_new = z_ref[pl.ds(r0, t_len), Z_K:Z_K + KV_W]
        v_new = z_ref[pl.ds(r0, t_len), Z_V:Z_V + KV_W]
        kwin = jnp.concatenate([ck_ref[b], k_new, zeros], axis=0).astype(bf16)
        vwin = jnp.concatenate([cv_ref[b], v_new, zeros], axis=0).astype(bf16)
        acc_ref[pl.ds(r0, t_len), :] = _attn_block(q_rows, kwin, vwin, bias_ref, sinks_ref, t_len, False)
        nk_ref[b, 0:buf - t_len, :] = ck_ref[b, t_len:buf, :]
        nk_ref[b, buf - t_len:buf, :] = k_new
        nv_ref[b, 0:buf - t_len, :] = cv_ref[b, t_len:buf, :]
        nv_ref[b, buf - t_len:buf, :] = v_new
        return 0
    lax.fori_loop(0, n_batch, step, 0)
    _rmsnorm_rows(acc_ref, g_ref, o_ref, n_batch * t_len)


def _attn_sample(z, ck, cv, bias, sinks, g, *, n_batch, t_len):
    assert ck.shape[1] == WINDOW and t_len == SUBLANES
    rows = n_batch * t_len
    total = (z.size + 2 * ck.size + 2 * cv.size + bias.size + rows * ATTN_W) * 4 + rows * ATTN_W * 2
    vmem = pl.BlockSpec(memory_space=pltpu.VMEM)
    return pl.pallas_call(
        functools.partial(_attn_sample_body, n_batch=n_batch, t_len=t_len),
        in_specs=[vmem, vmem, vmem, vmem, pl.BlockSpec(memory_space=pltpu.SMEM), vmem],
        out_specs=[vmem, vmem, vmem],
        out_shape=[jax.ShapeDtypeStruct((rows, ATTN_W), bf16),
                   jax.ShapeDtypeStruct(ck.shape, f32),
                   jax.ShapeDtypeStruct(cv.shape, f32)],
        scratch_shapes=[pltpu.VMEM((rows, ATTN_W), f32)],
        compiler_params=pltpu.CompilerParams(vmem_limit_bytes=_vmem_limit(0, total)),
        name="attn_sample",
    )(z, ck, cv, bias, sinks, g)


def _lru_gates(xc, n, c0, wg_ref, ba_ref, bx_ref, lam_ref, reset_row0):
    lanes = pl.ds(c0, LRU_BLOCK)
    gates = jnp.dot(xc.astype(bf16), wg_ref[n], preferred_element_type=f32)
    gate_a = jax.nn.sigmoid(gates[:, :LRU_BLOCK] + ba_ref[:, lanes])
    gate_x = jax.nn.sigmoid(gates[:, LRU_BLOCK:] + bx_ref[:, lanes])
    log_a = -LRU_C * gate_a * jax.nn.softplus(-lam_ref[:, lanes])
    a = jnp.exp(log_a)
    mult = jnp.sqrt(-jnp.tanh(log_a) * (1.0 + a * a))
    if reset_row0 is not None:
        row = lax.broadcasted_iota(jnp.int32, mult.shape, 0)
        mult = jnp.where((row == 0) & reset_row0, 1.0, mult)
    return a, xc * gate_x * mult


def _scan_tile(a, b, h_in, row):
    for d in (1, 2, 4):
        a_prev = pltpu.roll(a, d, axis=0)
        b_prev = pltpu.roll(b, d, axis=0)
        keep = row >= d
        b = jnp.where(keep, a * b_prev + b, b)
        a = jnp.where(keep, a * a_prev, a)
    return a * h_in + b


def _last_row(h):
    return jnp.broadcast_to(h[SUBLANES - 1:SUBLANES, :], h.shape)


def _conv_taps(x, shifted, cw, cb):
    out = cb + shifted[CONV_W - 1] * cw[0:1]
    for j in range(1, CONV_W - 1):
        out = out + shifted[CONV_W - 1 - j] * cw[j:j + 1]
    return out + x * cw[CONV_W - 1:CONV_W]


def _rec_prompt_body(x_ref, gr_ref, prev_ref, h0_ref, cw_ref, cb_ref, wg_ref, ba_ref, bx_ref, lam_ref, g_ref,
                     o_ref, lru_ref, xbig_ref, a_ref, b_ref, carry_ref, *, tc, n_chunks):
    c = pl.program_id(1)

    @pl.when(c == 0)
    def _():
        xbig_ref[0:SUBLANES, :] = prev_ref[...]
        carry_ref[...] = jnp.broadcast_to(h0_ref[...], carry_ref.shape)

    @pl.when(c > 0)
    def _():
        xbig_ref[0:SUBLANES, :] = xbig_ref[tc:tc + SUBLANES, :]
    xbig_ref[SUBLANES:SUBLANES + tc, :] = x_ref[...]

    def gates_step(n, _):
        c0 = pl.multiple_of(n * LRU_BLOCK, LRU_BLOCK)
        lanes = pl.ds(c0, LRU_BLOCK)
        big = xbig_ref[:, lanes]
        shifted = {k: pltpu.roll(big, k, axis=0)[SUBLANES:] for k in range(1, CONV_W)}
        xc = _conv_taps(big[SUBLANES:], shifted, cw_ref[:, lanes], cb_ref[:, lanes])
        a, b = _lru_gates(xc, n, c0, wg_ref, ba_ref, bx_ref, lam_ref, c == 0)
        a_ref[:, lanes] = a
        b_ref[:, lanes] = b
        return 0
    lax.fori_loop(0, N_LRU_BLOCKS, gates_step, 0)

    row = lax.broadcasted_iota(jnp.int32, (SUBLANES, LRU_W), 0)

    def scan_step(i, h):
        ys = []
        for half in range(BF16_ROWS // SUBLANES):
            r = pl.multiple_of(i * BF16_ROWS + half * SUBLANES, SUBLANES)
            hh = _scan_tile(a_ref[pl.ds(r, SUBLANES), :], b_ref[pl.ds(r, SUBLANES), :], h, row)
            ys.append(hh * jax.nn.gelu(gr_ref[pl.ds(r, SUBLANES), :]))
            h = _last_row(hh)
        y = jnp.concatenate(ys, axis=0)
        r16 = pl.multiple_of(i * BF16_ROWS, BF16_ROWS)
        o_ref[pl.ds(r16, BF16_ROWS), :] = (y * _rms_scale(y) * g_ref[...]).astype(o_ref.dtype)
        return h
    h = lax.fori_loop(0, tc // BF16_ROWS, scan_step, carry_ref[...])
    carry_ref[...] = h

    @pl.when(c == n_chunks - 1)
    def _():
        lru_ref[...] = h


def _rec_prompt(z, prev8, h0, cw, cb, wg, ba, bx, lam, g, *, n_batch, t_len, tc):
    n_chunks = t_len // tc
    row_vec = pl.BlockSpec((1, LRU_W), lambda b, c: (0, 0))
    blocks = 2 * tc * LRU_W * 4 + tc * LRU_W * 2 + wg.size * 2 + 2 * SUBLANES * LRU_W * 4
    scratch = (3 * tc + 2 * SUBLANES) * LRU_W * 4
    return pl.pallas_call(
        functools.partial(_rec_prompt_body, tc=tc, n_chunks=n_chunks),
        grid=(n_batch, n_chunks),
        in_specs=[pl.BlockSpec((tc, LRU_W), lambda b, c: (b * n_chunks + c, Z_XR // LRU_W)),
                  pl.BlockSpec((tc, LRU_W), lambda b, c: (b * n_chunks + c, Z_GR // LRU_W)),
                  pl.BlockSpec((None, SUBLANES, LRU_W), lambda b, c: (b, 0, 0)),
                  pl.BlockSpec((None, 1, LRU_W), lambda b, c: (b, 0, 0)),
                  pl.BlockSpec((CONV_W, LRU_W), lambda b, c: (0, 0)),
                  row_vec,
                  pl.BlockSpec(wg.shape, lambda b, c: (0, 0, 0)),
                  row_vec, row_vec, row_vec, row_vec],
        out_specs=[pl.BlockSpec((tc, LRU_W), lambda b, c: (b * n_chunks + c, 0)),
                   pl.BlockSpec((None, SUBLANES, LRU_W), lambda b, c: (b, 0, 0))],
        out_shape=[jax.ShapeDtypeStruct((n_batch * t_len, LRU_W), bf16),
                   jax.ShapeDtypeStruct((n_batch, SUBLANES, LRU_W), f32)],
        scratch_shapes=[pltpu.VMEM((tc + SUBLANES, LRU_W), f32),
                        pltpu.VMEM((tc, LRU_W), f32),
                        pltpu.VMEM((tc, LRU_W), f32),
                        pltpu.VMEM((SUBLANES, LRU_W), f32)],
        compiler_params=_params(("parallel", "arbitrary"), _vmem_limit(blocks, scratch)),
        name="rec_prompt",
    )(z, z, prev8, h0, cw, cb, wg, ba, bx, lam, g)


def _rec_sample_body(z_ref, prev_ref, h0_ref, cw_ref, cb_ref, wg_ref, ba_ref, bx_ref, lam_ref, g_ref,
                     o_ref, lru_ref, xc_ref, a_ref, b_ref, y_ref, *, n_batch):
    row = lax.broadcasted_iota(jnp.int32, (SUBLANES, LRU_W), 0)

    def conv_step(b, _):
        r = pl.multiple_of(b * SUBLANES, SUBLANES)
        x = z_ref[pl.ds(r, SUBLANES), Z_XR:Z_XR + LRU_W]
        prev = prev_ref[pl.ds(r, SUBLANES), :]
        shifted = {k: jnp.where(row >= k, pltpu.roll(x, k, axis=0), pltpu.roll(prev, k, axis=0))
                   for k in range(1, CONV_W)}
        xc_ref[pl.ds(r, SUBLANES), :] = _conv_taps(x, shifted, cw_ref[...], cb_ref[...])
        return 0
    lax.fori_loop(0, n_batch, conv_step, 0)

    def gates_step(n, _):
        c0 = pl.multiple_of(n * LRU_BLOCK, LRU_BLOCK)
        lanes = pl.ds(c0, LRU_BLOCK)
        a, b = _lru_gates(xc_ref[:, lanes], n, c0, wg_ref, ba_ref, bx_ref, lam_ref, None)
        a_ref[:, lanes] = a
        b_ref[:, lanes] = b
        return 0
    lax.fori_loop(0, N_LRU_BLOCKS, gates_step, 0)

    def scan_step(b, _):
        r = pl.multiple_of(b * SUBLANES, SUBLANES)
        h_in = jnp.broadcast_to(h0_ref[pl.ds(b, 1), :], (SUBLANES, LRU_W))
        hh = _scan_tile(a_ref[pl.ds(r, SUBLANES), :], b_ref[pl.ds(r, SUBLANES), :], h_in, row)
        y_ref[pl.ds(r, SUBLANES), :] = hh * jax.nn.gelu(z_ref[pl.ds(r, SUBLANES), Z_GR:Z_GR + LRU_W])
        lru_ref[pl.ds(b, 1), :] = hh[SUBLANES - 1:SUBLANES, :]
        return 0
    lax.fori_loop(0, n_batch, scan_step, 0)
    _rmsnorm_rows(y_ref, g_ref, o_ref, n_batch * SUBLANES)


def _rec_sample(z, prev8, h0, cw, cb, wg, ba, bx, lam, g, *, n_batch, t_len):
    assert t_len == SUBLANES and PAST_LEN > 0
    rows = n_batch * t_len
    total = z.size * 4 + (6 * rows + 2 * n_batch) * LRU_W * 4 + wg.size * 2
    vmem = pl.BlockSpec(memory_space=pltpu.VMEM)
    return pl.pallas_call(
        functools.partial(_rec_sample_body, n_batch=n_batch),
        in_specs=[vmem] * 10,
        out_specs=[vmem, vmem],
        out_shape=[jax.ShapeDtypeStruct((rows, LRU_W), bf16),
                   jax.ShapeDtypeStruct((n_batch, LRU_W), f32)],
        scratch_shapes=[pltpu.VMEM((rows, LRU_W), f32)] * 4,
        compiler_params=pltpu.CompilerParams(vmem_limit_bytes=_vmem_limit(0, total)),
        name="rec_sample",
    )(z, prev8, h0, cw, cb, wg, ba, bx, lam, g)


def _outproj_body(h_ref, attn_ref, rec_ref, w_ref, o_ref):
    o_ref[...] = (h_ref[...]
                  + jnp.dot(attn_ref[...], w_ref[0:ATTN_W, :], preferred_element_type=f32)
                  + jnp.dot(rec_ref[...], w_ref[ATTN_W:ATTN_W + LRU_W, :], preferred_element_type=f32))


def _outproj(h, attn, rec, w, *, tm):
    m = h.shape[0]
    blocks = 2 * tm * D_MODEL * 4 + tm * (ATTN_W + LRU_W) * 2 + w.size * 2
    return pl.pallas_call(
        _outproj_body,
        grid=(m // tm,),
        in_specs=[pl.BlockSpec((tm, D_MODEL), lambda i: (i, 0)),
                  pl.BlockSpec((tm, ATTN_W), lambda i: (i, 0)),
                  pl.BlockSpec((tm, LRU_W), lambda i: (i, 0)),
                  pl.BlockSpec(w.shape, lambda i: (0, 0))],
        out_specs=pl.BlockSpec((tm, D_MODEL), lambda i: (i, 0)),
        out_shape=jax.ShapeDtypeStruct((m, D_MODEL), f32),
        compiler_params=_params(("parallel",), _vmem_limit(blocks)),
        name="outproj",
    )(h, attn, rec, w)


def _mlp_body(h_ref, g_ref, wu_ref, wd_ref, o_ref, u_ref, *, tm):
    f = pl.program_id(1)

    @pl.when(f == 0)
    def _():
        _rmsnorm_rows(h_ref, g_ref, u_ref, tm)
    hid = jnp.dot(u_ref[...], wu_ref[...], preferred_element_type=f32)
    hid = jnp.square(jnp.maximum(hid, 0.0)).astype(bf16)
    part = jnp.dot(hid, wd_ref[...], preferred_element_type=f32)

    @pl.when(f == 0)
    def _():
        o_ref[...] = h_ref[...] + part

    @pl.when(f > 0)
    def _():
        o_ref[...] += part


def _mlp(h, g, wu, wd, *, tm, tf):
    m = h.shape[0]
    blocks = 2 * tm * D_MODEL * 4 + 2 * D_MODEL * tf * 2
    return pl.pallas_call(
        functools.partial(_mlp_body, tm=tm),
        grid=(m // tm, D_FF // tf),
        in_specs=[pl.BlockSpec((tm, D_MODEL), lambda i, f: (i, 0)),
                  pl.BlockSpec((1, D_MODEL), lambda i, f: (0, 0)),
                  pl.BlockSpec((D_MODEL, tf), lambda i, f: (0, f)),
                  pl.BlockSpec((tf, D_MODEL), lambda i, f: (f, 0))],
        out_specs=pl.BlockSpec((tm, D_MODEL), lambda i, f: (i, 0)),
        out_shape=jax.ShapeDtypeStruct((m, D_MODEL), f32),
        scratch_shapes=[pltpu.VMEM((tm, D_MODEL), bf16)],
        compiler_params=_params(("parallel", "arbitrary"), _vmem_limit(blocks, tm * D_MODEL * 2 + tm * tf * 6)),
        name="mlp",
    )(h, g, wu, wd)


def _final_norm_body(h_ref, g_ref, o_ref, *, tiles):
    def step(i, _):
        x = h_ref[0, i]
        o_ref[0, i] = x * _rms_scale(x) * g_ref[...]
        return 0
    lax.fori_loop(0, tiles, step, 0)


def _final_norm(h, g, *, skip, tr):
    n_batch, t_len, _ = h.shape
    s_len = t_len - skip
    tiles = tr // SUBLANES
    skip_tiles = skip // SUBLANES
    h4 = h.reshape(n_batch, t_len // SUBLANES, SUBLANES, D_MODEL)
    out = pl.pallas_call(
        functools.partial(_final_norm_body, tiles=tiles),
        grid=(n_batch, s_len // tr),
        in_specs=[pl.BlockSpec((pl.Element(1), pl.Element(tiles), pl.Element(SUBLANES), pl.Element(D_MODEL)),
                               lambda b, r: (b, skip_tiles + r * tiles, 0, 0)),
                  pl.BlockSpec((1, D_MODEL), lambda b, r: (0, 0))],
        out_specs=pl.BlockSpec((1, tiles, SUBLANES, D_MODEL), lambda b, r: (b, r, 0, 0)),
        out_shape=jax.ShapeDtypeStruct((n_batch, s_len // SUBLANES, SUBLANES, D_MODEL), f32),
        compiler_params=_params(("parallel", "parallel"), _vmem_limit(2 * tr * D_MODEL * 4)),
        name="final_norm",
    )(h4, g)
    return out.reshape(n_batch, s_len, D_MODEL)


def _row_tile(m, candidates):
    for tm in candidates:
        if m % tm == 0:
            return tm
    raise ValueError(f"no row tile for {m} rows")


def kernel(x_prompt, x_sample, cache_k_win, cache_v_win, state_conv, state_lru, meta_tokens, norm_mix_g, w_in,
           conv_w, conv_b, w_gate_a, b_gate_a, w_gate_x, b_gate_x, lru_lambda, attn_sinks, rel_bias, attn_out_g,
           rec_out_g, w_out, norm_mlp_g, w_up, w_down, final_norm_g):
    n_p, s_p, _ = x_prompt.shape
    n_s, t_s, _ = x_sample.shape
    t_p = N_META + s_p
    buf = cache_k_win.shape[2]
    assert t_p % BF16_ROWS == 0 and t_p >= WINDOW and buf == WINDOW

    w_in_r = jnp.concatenate([w_in[..., :ATTN_W], w_in[..., ATTN_W + 2 * KV_W:],
                              w_in[..., ATTN_W:ATTN_W + 2 * KV_W]], axis=-1).astype(bf16)
    w_gates = jnp.concatenate([w_gate_a, w_gate_x], axis=-1).astype(bf16)
    w_out_b, w_up_b, w_down_b = w_out.astype(bf16), w_up.astype(bf16), w_down.astype(bf16)
    vec = lambda p, l: p[l][None, :]

    meta = jnp.broadcast_to(meta_tokens.astype(x_prompt.dtype)[None], (n_p, N_META, D_MODEL))
    hp = jnp.concatenate([meta, x_prompt], axis=1).reshape(n_p * t_p, D_MODEL)
    hs = x_sample.reshape(n_s * t_s, D_MODEL)

    bias = _bias_table(rel_bias).reshape(N_Q_HEADS * WINDOW, 2 * WINDOW)
    prev_p = jnp.zeros((n_p, SUBLANES, LRU_W), f32)
    h0_p = jnp.zeros((n_p, 1, LRU_W), f32)
    ck = cache_k_win.reshape(DEPTH, n_s, buf, KV_W)
    cv = cache_v_win.reshape(DEPTH, n_s, buf, KV_W)
    prev_s = jnp.pad(state_conv, ((0, 0), (0, 0), (SUBLANES - (CONV_W - 1), 0), (0, 0)))
    prev_s = prev_s.reshape(DEPTH, n_s * SUBLANES, LRU_W)

    tm_p = _row_tile(n_p * t_p, (688, 344))
    tm_s = n_s * t_s
    tc_p = _row_tile(t_p, (688, 344, 48, 16))

    kp_l, vp_l, cp_l, lp_l, ks_l, vs_l, cs_l, ls_l = ([] for _ in range(8))
    for l in range(DEPTH):
        rec_params = (conv_w[l], vec(conv_b, l), w_gates[l], vec(b_gate_a, l), vec(b_gate_x, l),
                      vec(lru_lambda, l), vec(rec_out_g, l))
        z = _inproj(hp, vec(norm_mix_g, l), w_in_r[l], tm=tm_p, tn=512)
        attn = _attn_prompt(z, bias, attn_sinks[l], vec(attn_out_g, l), n_batch=n_p, t_len=t_p)
        rec, lru = _rec_prompt(z, prev_p, h0_p, *rec_params, n_batch=n_p, t_len=t_p, tc=tc_p)
        hp = _outproj(hp, attn, rec, w_out_b[l], tm=tm_p)
        hp = _mlp(hp, vec(norm_mlp_g, l), w_up_b[l], w_down_b[l], tm=tm_p, tf=512)
        z3 = z.reshape(n_p, t_p, IN_W)
        kp_l.append(z3[:, t_p - WINDOW:, Z_K:Z_K + KV_W].reshape(n_p, WINDOW, N_KV_HEADS, HEAD_DIM))
        vp_l.append(z3[:, t_p - WINDOW:, Z_V:Z_V + KV_W].reshape(n_p, WINDOW, N_KV_HEADS, HEAD_DIM))
        cp_l.append(z3[:, t_p - (CONV_W - 1):, Z_XR:Z_XR + LRU_W])
        lp_l.append(lru[:, 0])
        z = _inproj(hs, vec(norm_mix_g, l), w_in_r[l], tm=tm_s, tn=512)
        attn, nk, nv = _attn_sample(z, ck[l], cv[l], bias, attn_sinks[l], vec(attn_out_g, l),
                                    n_batch=n_s, t_len=t_s)
        rec, lru = _rec_sample(z, prev_s[l], state_lru[l], *rec_params, n_batch=n_s, t_len=t_s)
        hs = _outproj(hs, attn, rec, w_out_b[l], tm=tm_s)
        hs = _mlp(hs, vec(norm_mlp_g, l), w_up_b[l], w_down_b[l], tm=tm_s, tf=512)
        z3 = z.reshape(n_s, t_s, IN_W)
        ks_l.append(nk.reshape(n_s, buf, N_KV_HEADS, HEAD_DIM))
        vs_l.append(nv.reshape(n_s, buf, N_KV_HEADS, HEAD_DIM))
        cs_l.append(z3[:, t_s - (CONV_W - 1):, Z_XR:Z_XR + LRU_W])
        ls_l.append(lru)

    g_fin = final_norm_g[None, :]
    y_prompt = _final_norm(hp.reshape(n_p, t_p, D_MODEL), g_fin, skip=N_META, tr=256)
    y_sample = _final_norm(hs.reshape(n_s, t_s, D_MODEL).reshape(1, n_s * t_s, D_MODEL), g_fin,
                           skip=0, tr=n_s * t_s).reshape(n_s, t_s, D_MODEL)
    return (y_prompt, y_sample,
            jnp.stack(kp_l), jnp.stack(vp_l), jnp.stack(cp_l), jnp.stack(lp_l),
            jnp.stack(ks_l), jnp.stack(vs_l), jnp.stack(cs_l), jnp.stack(ls_l))
```

```python
import functools
import math

import jax
import jax.numpy as jnp
from jax import lax
from jax.experimental import pallas as pl
from jax.experimental.pallas import tpu as pltpu

f32 = jnp.float32
bf16 = jnp.bfloat16

D_MODEL = 2048
DEPTH = 4
PAST_LEN = 16384
HEAD_DIM = 128
N_Q_HEADS = 8
N_KV_HEADS = 2
Q_PER_KV = N_Q_HEADS // N_KV_HEADS
ATTN_W = N_Q_HEADS * HEAD_DIM
KV_W = N_KV_HEADS * HEAD_DIM
LRU_W = D_MODEL // 2
N_LRU_BLOCKS = 8
LRU_BLOCK = LRU_W // N_LRU_BLOCKS
CONV_W = 4
LRU_C = 8.0
IN_W = ATTN_W + 2 * KV_W + 2 * LRU_W
D_FF = 4 * D_MODEL
WINDOW = 128
N_BUCKETS = 32
MAX_DISTANCE = 128
N_META = 16
EPS = 1e-6
ATTN_SCALE = HEAD_DIM ** -0.5

Z_Q = 0
Z_K = ATTN_W
Z_V = Z_K + KV_W
Z_XR = Z_V + KV_W
Z_GR = Z_XR + LRU_W
LRU_HALF = LRU_W // 2

ATTN_LEAD = WINDOW + (-N_META) % WINDOW

SUBLANES = 8
BF16_ROWS = 16
V7X_VMEM_BYTES = 64 * 1024 * 1024
VMEM_CAP_BYTES = V7X_VMEM_BYTES - 6 * 1024 * 1024


def _vmem_limit(pipelined_bytes, scratch_bytes=0):
    est = 2 * pipelined_bytes + scratch_bytes
    return int(min(VMEM_CAP_BYTES, est + est // 2 + (8 << 20)))


def _params(semantics, vmem_bytes):
    return pltpu.CompilerParams(dimension_semantics=semantics, vmem_limit_bytes=vmem_bytes)


def _rms_scale(x):
    return lax.rsqrt(jnp.mean(x * x, axis=-1, keepdims=True) + EPS)


def _rmsnorm(x, g):
    return x * _rms_scale(x) * g


def _inproj_body(h_ref, g_ref, w_ref, z_ref, u_ref):
    @pl.when(pl.program_id(1) == 0)
    def _():
        u_ref[...] = _rmsnorm(h_ref[...], g_ref[...]).astype(bf16)
    z_ref[...] = jnp.dot(u_ref[...], w_ref[...], preferred_element_type=f32)


def _inproj(h, g, w, l, *, tm, tn):
    m = h.shape[0]
    blocks = tm * D_MODEL * 4 + D_MODEL * tn * 2 + tm * tn * 4
    return pl.pallas_call(
        _inproj_body,
        grid=(m // tm, IN_W // tn),
        in_specs=[pl.BlockSpec((tm, D_MODEL), lambda i, j: (i, 0)),
                  pl.BlockSpec((None, 1, D_MODEL), lambda i, j: (l, 0, 0)),
                  pl.BlockSpec((None, D_MODEL, tn), lambda i, j: (l, 0, j))],
        out_specs=pl.BlockSpec((tm, tn), lambda i, j: (i, j)),
        out_shape=jax.ShapeDtypeStruct((m, IN_W), f32),
        scratch_shapes=[pltpu.VMEM((tm, D_MODEL), bf16)],
        compiler_params=_params(("parallel", "arbitrary"), _vmem_limit(blocks, tm * D_MODEL * 2)),
        name="inproj",
    )(h, g, w)


def _rel_bias_of(dist, rel_ref, h):
    n = jnp.maximum(dist, 0)
    max_exact = N_BUCKETS // 2
    nf = jnp.maximum(n, 1).astype(f32)
    large = max_exact + (jnp.log(nf / max_exact) / math.log(MAX_DISTANCE / max_exact)
                         * (N_BUCKETS - max_exact)).astype(jnp.int32)
    large = jnp.minimum(large, N_BUCKETS - 1)
    bucket = jnp.where(n < max_exact, n, large)
    acc = jnp.zeros(dist.shape, f32)
    for b in range(N_BUCKETS):
        acc = jnp.where(bucket == b, rel_ref[b, h], acc)
    return jnp.where((dist >= 0) & (dist < WINDOW), acc, -jnp.inf)


def _bias_table_body(rel_ref, qk_ref, kq_ref):
    shape_qk = (WINDOW, 2 * WINDOW)
    dist_qk = (lax.broadcasted_iota(jnp.int32, shape_qk, 0) + WINDOW
               - lax.broadcasted_iota(jnp.int32, shape_qk, 1))
    shape_kq = (2 * WINDOW, WINDOW)
    dist_kq = (lax.broadcasted_iota(jnp.int32, shape_kq, 1) + WINDOW
               - lax.broadcasted_iota(jnp.int32, shape_kq, 0))
    for h in range(N_Q_HEADS):
        kv, g = divmod(h, Q_PER_KV)
        qk_ref[h] = _rel_bias_of(dist_qk, rel_ref, h)
        kq_ref[kv, :, g * WINDOW:(g + 1) * WINDOW] = _rel_bias_of(dist_kq, rel_ref, h)


def _bias_tables(rel_bias):
    return pl.pallas_call(
        _bias_table_body,
        in_specs=[pl.BlockSpec(memory_space=pltpu.SMEM)],
        out_shape=[jax.ShapeDtypeStruct((N_Q_HEADS, WINDOW, 2 * WINDOW), f32),
                   jax.ShapeDtypeStruct((N_KV_HEADS, 2 * WINDOW, Q_PER_KV * WINDOW), f32)],
        name="bias_tables",
    )(rel_bias)


def _attn_block_t(q_rows, kpad_ref, vt_ref, r0, bias_ref, sinks_ref, l, gt_ref, lead_keys):
    outs = []
    for kv in range(N_KV_HEADS):
        heads = range(kv * Q_PER_KV, (kv + 1) * Q_PER_KV)
        kwin = kpad_ref[pl.ds(r0, 2 * WINDOW), kv * HEAD_DIM:(kv + 1) * HEAD_DIM]
        q = jnp.concatenate([q_rows[:, h * HEAD_DIM:(h + 1) * HEAD_DIM] for h in heads], axis=0).astype(bf16)
        s = lax.dot_general(kwin, q, (((1,), (1,)), ((), ())), preferred_element_type=f32) * ATTN_SCALE
        s = s + bias_ref[kv]
        if lead_keys:
            key = lax.broadcasted_iota(jnp.int32, s.shape, 0)
            s = jnp.where(key < lead_keys, -jnp.inf, s)
        sink = jnp.concatenate([jnp.full((1, WINDOW), sinks_ref[l, h], f32) for h in heads], axis=1)
        m = jnp.maximum(jnp.max(s, axis=0, keepdims=True), sink)
        p = jnp.exp(s - m)
        denom = jnp.sum(p, axis=0, keepdims=True) + jnp.exp(sink - m)
        vt = vt_ref[kv * HEAD_DIM:(kv + 1) * HEAD_DIM, pl.ds(r0, 2 * WINDOW)]
        outs.append(jnp.dot(vt, p.astype(bf16), preferred_element_type=f32) / denom)
    ot = jnp.concatenate(outs, axis=1)
    sq = jnp.sum(ot * ot, axis=0, keepdims=True)
    ssq = sq[:, 0:WINDOW]
    for h in range(1, N_Q_HEADS):
        ssq = ssq + sq[:, h * WINDOW:(h + 1) * WINDOW]
    r = lax.rsqrt(ssq * (1.0 / ATTN_W) + EPS)
    yt = ot * jnp.concatenate([r] * N_Q_HEADS, axis=1) * gt_ref[...]
    return jnp.concatenate([yt[:, h * WINDOW:(h + 1) * WINDOW].T for h in range(N_Q_HEADS)], axis=1)


def _attn_prompt_body(q_ref, k_ref, v_ref, bias_ref, sinks_ref, gt_ref, o_ref, kpad_ref, vpad_ref, vt_ref,
                      *, t_len, l):
    pad_rows = kpad_ref.shape[0]
    kpad_ref[0:ATTN_LEAD, :] = jnp.zeros((ATTN_LEAD, KV_W), bf16)
    kpad_ref[ATTN_LEAD:pad_rows, :] = k_ref[...].astype(bf16)
    vpad_ref[0:ATTN_LEAD, :] = jnp.zeros((ATTN_LEAD, KV_W), f32)
    vpad_ref[ATTN_LEAD:pad_rows, :] = v_ref[...]

    def transpose_step(c, _):
        r = pl.multiple_of(c * WINDOW, WINDOW)
        chunk = vpad_ref[pl.ds(r, WINDOW), :]
        for kv in range(N_KV_HEADS):
            vt_ref[kv * HEAD_DIM:(kv + 1) * HEAD_DIM, pl.ds(r, WINDOW)] = (
                chunk[:, kv * HEAD_DIM:(kv + 1) * HEAD_DIM].T.astype(bf16))
        return 0
    lax.fori_loop(0, pad_rows // WINDOW, transpose_step, 0)

    n_first = 2 * WINDOW - ATTN_LEAD
    q0 = jnp.concatenate([jnp.zeros((WINDOW - n_first, ATTN_W), f32), q_ref[0:n_first, :]], axis=0)
    y0 = _attn_block_t(q0, kpad_ref, vt_ref, 0, bias_ref, sinks_ref, l, gt_ref, ATTN_LEAD)
    o_ref[0:n_first, :] = y0[WINDOW - n_first:, :].astype(o_ref.dtype)

    def block(j, lead_keys):
        r0 = j * WINDOW
        q0_row = j * WINDOW - (ATTN_LEAD - WINDOW)
        if not isinstance(j, int):
            r0, q0_row = pl.multiple_of(r0, WINDOW), pl.multiple_of(q0_row, BF16_ROWS)
        rows = pl.ds(q0_row, WINDOW)
        y = _attn_block_t(q_ref[rows, :], kpad_ref, vt_ref, r0, bias_ref, sinks_ref, l, gt_ref, lead_keys)
        o_ref[rows, :] = y.astype(o_ref.dtype)

    block(1, ATTN_LEAD - WINDOW)

    def step(j, _):
        block(j, 0)
        return 0
    lax.fori_loop(2, pad_rows // WINDOW - 1, step, 0, unroll=2)


def _attn_prompt(z, bias_kq, sinks, gt, l, *, n_batch, t_len):
    pad_rows = ATTN_LEAD + t_len
    assert pad_rows % WINDOW == 0 and (ATTN_LEAD - WINDOW) % BF16_ROWS == 0
    blocks = t_len * (ATTN_W + 2 * KV_W) * 4 + t_len * ATTN_W * 2 + bias_kq.size * 4 + HEAD_DIM * ATTN_W * 4
    scratch = pad_rows * KV_W * (2 + 4 + 2)
    return pl.pallas_call(
        functools.partial(_attn_prompt_body, t_len=t_len, l=l),
        grid=(n_batch,),
        in_specs=[pl.BlockSpec((t_len, ATTN_W), lambda b: (b, Z_Q // ATTN_W)),
                  pl.BlockSpec((t_len, KV_W), lambda b: (b, Z_K // KV_W)),
                  pl.BlockSpec((t_len, KV_W), lambda b: (b, Z_V // KV_W)),
                  pl.BlockSpec(bias_kq.shape, lambda b: (0, 0, 0)),
                  pl.BlockSpec(memory_space=pltpu.SMEM),
                  pl.BlockSpec((None, HEAD_DIM, ATTN_W), lambda b: (l, 0, 0))],
        out_specs=pl.BlockSpec((t_len, ATTN_W), lambda b: (b, 0)),
        out_shape=jax.ShapeDtypeStruct((n_batch * t_len, ATTN_W), bf16),
        scratch_shapes=[pltpu.VMEM((pad_rows, KV_W), bf16), pltpu.VMEM((pad_rows, KV_W), f32),
                        pltpu.VMEM((KV_W, pad_rows), bf16)],
        compiler_params=_params(("parallel",), _vmem_limit(blocks, scratch)),
        name="attn_prompt",
    )(z, z, z, bias_kq, sinks, gt)


def _attn_block(q_rows, kwin, vwin, bias_ref, sinks_ref, n_rows):
    outs = []
    for kv in range(N_KV_HEADS):
        heads = range(kv * Q_PER_KV, (kv + 1) * Q_PER_KV)
        k = kwin[:, kv * HEAD_DIM:(kv + 1) * HEAD_DIM]
        v = vwin[:, kv * HEAD_DIM:(kv + 1) * HEAD_DIM]
        q = jnp.concatenate([q_rows[:, h * HEAD_DIM:(h + 1) * HEAD_DIM] for h in heads], axis=0).astype(bf16)
        bias = jnp.concatenate([bias_ref[h * WINDOW:h * WINDOW + n_rows, :] for h in heads], axis=0)
        s = lax.dot_general(q, k, (((1,), (1,)), ((), ())), preferred_element_type=f32) * ATTN_SCALE + bias
        sink = jnp.concatenate([jnp.full((n_rows, 1), sinks_ref[h], f32) for h in heads], axis=0)
        m = jnp.maximum(jnp.max(s, axis=-1, keepdims=True), sink)
        p = jnp.exp(s - m)
        denom = jnp.sum(p, axis=-1, keepdims=True) + jnp.exp(sink - m)
        o = jnp.dot(p.astype(bf16), v, preferred_element_type=f32) / denom
        outs.extend(o[g * n_rows:(g + 1) * n_rows] for g in range(Q_PER_KV))
    return jnp.concatenate(outs, axis=1)


def _attn_sample_body(z_ref, ck_ref, cv_ref, bias_ref, sinks_ref, g_ref, o_ref, nk_ref, nv_ref, acc_ref,
                      *, n_batch, t_len):
    buf = ck_ref.shape[1]
    zeros = jnp.zeros((2 * WINDOW - buf - t_len, KV_W), f32)

    def step(b, _):
        r0 = pl.multiple_of(b * t_len, t_len)
        q_rows = z_ref[pl.ds(r0, t_len), Z_Q:Z_Q + ATTN_W]
        k_new = z_ref[pl.ds(r0, t_len), Z_K:Z_K + KV_W]
        v_new = z_ref[pl.ds(r0, t_len), Z_V:Z_V + KV_W]
        kwin = jnp.concatenate([ck_ref[b], k_new, zeros], axis=0).astype(bf16)
        vwin = jnp.concatenate([cv_ref[b], v_new, zeros], axis=0).astype(bf16)
        acc_ref[pl.ds(r0, t_len), :] = _attn_block(q_rows, kwin, vwin, bias_ref, sinks_ref, t_len)
        nk_ref[b, 0:buf - t_len, :] = ck_ref[b, t_len:buf, :]
        nk_ref[b, buf - t_len:buf, :] = k_new
        nv_ref[b, 0:buf - t_len, :] = cv_ref[b, t_len:buf, :]
        nv_ref[b, buf - t_len:buf, :] = v_new
        return 0
    lax.fori_loop(0, n_batch, step, 0)
    o_ref[...] = _rmsnorm(acc_ref[...], g_ref[...]).astype(o_ref.dtype)


def _attn_sample(z, ck, cv, bias_qk, sinks, g, *, n_batch, t_len):
    assert ck.shape[1] == WINDOW and t_len == SUBLANES
    rows = n_batch * t_len
    total = (z.size + 2 * ck.size + 2 * cv.size + bias_qk.size + rows * ATTN_W) * 4 + rows * ATTN_W * 2
    vmem = pl.BlockSpec(memory_space=pltpu.VMEM)
    return pl.pallas_call(
        functools.partial(_attn_sample_body, n_batch=n_batch, t_len=t_len),
        in_specs=[vmem, vmem, vmem, vmem, pl.BlockSpec(memory_space=pltpu.SMEM), vmem],
        out_specs=[vmem, vmem, vmem],
        out_shape=[jax.ShapeDtypeStruct((rows, ATTN_W), bf16),
                   jax.ShapeDtypeStruct(ck.shape, f32),
                   jax.ShapeDtypeStruct(cv.shape, f32)],
        scratch_shapes=[pltpu.VMEM((rows, ATTN_W), f32)],
        compiler_params=pltpu.CompilerParams(vmem_limit_bytes=_vmem_limit(0, total)),
        name="attn_sample",
    )(z, ck, cv, bias_qk, sinks, g)


def _lru_gates(xc, n, c0, wg_ref, ba_ref, bx_ref, lam_ref, reset_row0):
    lanes = pl.ds(c0, LRU_BLOCK)
    gates = jnp.dot(xc.astype(bf16), wg_ref[n], preferred_element_type=f32)
    gate_a = jax.nn.sigmoid(gates[:, :LRU_BLOCK] + ba_ref[:, lanes])
    gate_x = jax.nn.sigmoid(gates[:, LRU_BLOCK:] + bx_ref[:, lanes])
    log_a = -LRU_C * gate_a * jax.nn.softplus(-lam_ref[:, lanes])
    a = jnp.exp(log_a)
    mult = jnp.sqrt(-jnp.tanh(log_a) * (1.0 + a * a))
    if reset_row0 is not None:
        row = lax.broadcasted_iota(jnp.int32, mult.shape, 0)
        mult = jnp.where((row == 0) & reset_row0, 1.0, mult)
    return a, xc * gate_x * mult


def _scan_tile(a, b, h_in, row):
    for d in (1, 2, 4):
        a_prev = pltpu.roll(a, d, axis=0)
        b_prev = pltpu.roll(b, d, axis=0)
        keep = row >= d
        b = jnp.where(keep, a * b_prev + b, b)
        a = jnp.where(keep, a * a_prev, a)
    return a * h_in + b


def _last_row(h):
    return jnp.broadcast_to(h[SUBLANES - 1:SUBLANES, :], h.shape)


def _conv_taps(x, shifted, cw, cb):
    out = cb + shifted[CONV_W - 1] * cw[0:1]
    for j in range(1, CONV_W - 1):
        out = out + shifted[CONV_W - 1 - j] * cw[j:j + 1]
    return out + x * cw[CONV_W - 1:CONV_W]


def _rec_prompt_body(xl_ref, xh_ref, gl_ref, gh_ref, prev_ref, h0_ref, cw_ref, cb_ref, wg_ref, ba_ref, bx_ref,
                     lam_ref, g_ref, o_ref, lru_ref, xbig_ref, a_ref, b_ref, carry_ref, *, tc, n_chunks):
    c = pl.program_id(1)

    @pl.when(c == 0)
    def _():
        xbig_ref[0:SUBLANES, :] = prev_ref[...]
        carry_ref[...] = jnp.broadcast_to(h0_ref[...], carry_ref.shape)

    @pl.when(c > 0)
    def _():
        xbig_ref[0:SUBLANES, :] = xbig_ref[tc:tc + SUBLANES, :]
    xbig_ref[SUBLANES:SUBLANES + tc, 0:LRU_HALF] = xl_ref[...]
    xbig_ref[SUBLANES:SUBLANES + tc, LRU_HALF:LRU_W] = xh_ref[...]

    def gates_step(n, _):
        c0 = pl.multiple_of(n * LRU_BLOCK, LRU_BLOCK)
        lanes = pl.ds(c0, LRU_BLOCK)
        shifted = {k: xbig_ref[SUBLANES - k:SUBLANES - k + tc, lanes] for k in range(1, CONV_W)}
        xc = _conv_taps(xbig_ref[SUBLANES:SUBLANES + tc, lanes], shifted, cw_ref[:, lanes], cb_ref[:, lanes])
        a, b = _lru_gates(xc, n, c0, wg_ref, ba_ref, bx_ref, lam_ref, c == 0)
        a_ref[:, lanes] = a
        b_ref[:, lanes] = b
        return 0
    lax.fori_loop(0, N_LRU_BLOCKS, gates_step, 0)

    row = lax.broadcasted_iota(jnp.int32, (SUBLANES, LRU_W), 0)

    def scan_step(i, h):
        rows = pl.ds(pl.multiple_of(i * SUBLANES, SUBLANES), SUBLANES)
        hh = _scan_tile(a_ref[rows, :], b_ref[rows, :], h, row)
        gate = jnp.concatenate([gl_ref[rows, :], gh_ref[rows, :]], axis=1)
        b_ref[rows, :] = hh * jax.nn.gelu(gate)
        return _last_row(hh)
    h = lax.fori_loop(0, tc // SUBLANES, scan_step, carry_ref[...], unroll=2)
    carry_ref[...] = h
    o_ref[...] = _rmsnorm(b_ref[...], g_ref[...]).astype(o_ref.dtype)

    @pl.when(c == n_chunks - 1)
    def _():
        lru_ref[...] = h


def _rec_prompt(z, prev8, h0, cw, cb, wg, ba, bx, lam, g, l, *, n_batch, t_len, tc):
    n_chunks = t_len // tc
    row_vec = pl.BlockSpec((None, 1, LRU_W), lambda b, c: (l, 0, 0))
    z_half = lambda col: pl.BlockSpec((tc, LRU_HALF), lambda b, c: (b * n_chunks + c, col // LRU_HALF))
    blocks = 2 * tc * LRU_W * 4 + tc * LRU_W * 2 + wg[0].size * 2 + 2 * SUBLANES * LRU_W * 4
    scratch = (3 * tc + 2 * SUBLANES) * LRU_W * 4
    return pl.pallas_call(
        functools.partial(_rec_prompt_body, tc=tc, n_chunks=n_chunks),
        grid=(n_batch, n_chunks),
        in_specs=[z_half(Z_XR), z_half(Z_XR + LRU_HALF), z_half(Z_GR), z_half(Z_GR + LRU_HALF),
                  pl.BlockSpec((None, SUBLANES, LRU_W), lambda b, c: (b, 0, 0)),
                  pl.BlockSpec((None, 1, LRU_W), lambda b, c: (b, 0, 0)),
                  pl.BlockSpec((None, CONV_W, LRU_W), lambda b, c: (l, 0, 0)),
                  row_vec,
                  pl.BlockSpec((None,) + wg.shape[1:], lambda b, c: (l, 0, 0, 0)),
                  row_vec, row_vec, row_vec, row_vec],
        out_specs=[pl.BlockSpec((tc, LRU_W), lambda b, c: (b * n_chunks + c, 0)),
                   pl.BlockSpec((None, SUBLANES, LRU_W), lambda b, c: (b, 0, 0))],
        out_shape=[jax.ShapeDtypeStruct((n_batch * t_len, LRU_W), bf16),
                   jax.ShapeDtypeStruct((n_batch, SUBLANES, LRU_W), f32)],
        scratch_shapes=[pltpu.VMEM((tc + SUBLANES, LRU_W), f32),
                        pltpu.VMEM((tc, LRU_W), f32),
                        pltpu.VMEM((tc, LRU_W), f32),
                        pltpu.VMEM((SUBLANES, LRU_W), f32)],
        compiler_params=_params(("parallel", "arbitrary"), _vmem_limit(blocks, scratch)),
        name="rec_prompt",
    )(z, z, z, z, prev8, h0, cw, cb, wg, ba, bx, lam, g)


def _rec_sample_body(z_ref, prev_ref, h0_ref, cw_ref, cb_ref, wg_ref, ba_ref, bx_ref, lam_ref, g_ref,
                     o_ref, lru_ref, xc_ref, a_ref, b_ref, *, n_batch):
    row = lax.broadcasted_iota(jnp.int32, (SUBLANES, LRU_W), 0)

    def conv_step(b, _):
        r = pl.multiple_of(b * SUBLANES, SUBLANES)
        x = z_ref[pl.ds(r, SUBLANES), Z_XR:Z_XR + LRU_W]
        prev = prev_ref[pl.ds(r, SUBLANES), :]
        shifted = {k: jnp.where(row >= k, pltpu.roll(x, k, axis=0), pltpu.roll(prev, k, axis=0))
                   for k in range(1, CONV_W)}
        xc_ref[pl.ds(r, SUBLANES), :] = _conv_taps(x, shifted, cw_ref[...], cb_ref[...])
        return 0
    lax.fori_loop(0, n_batch, conv_step, 0)

    def gates_step(n, _):
        c0 = pl.multiple_of(n * LRU_BLOCK, LRU_BLOCK)
        lanes = pl.ds(c0, LRU_BLOCK)
        a, b = _lru_gates(xc_ref[:, lanes], n, c0, wg_ref, ba_ref, bx_ref, lam_ref, None)
        a_ref[:, lanes] = a
        b_ref[:, lanes] = b
        return 0
    lax.fori_loop(0, N_LRU_BLOCKS, gates_step, 0)

    def scan_step(b, _):
        rows = pl.ds(pl.multiple_of(b * SUBLANES, SUBLANES), SUBLANES)
        h_in = jnp.broadcast_to(h0_ref[pl.ds(b, 1), :], (SUBLANES, LRU_W))
        hh = _scan_tile(a_ref[rows, :], b_ref[rows, :], h_in, row)
        b_ref[rows, :] = hh * jax.nn.gelu(z_ref[rows, Z_GR:Z_GR + LRU_W])
        lru_ref[pl.ds(b, 1), :] = hh[SUBLANES - 1:SUBLANES, :]
        return 0
    lax.fori_loop(0, n_batch, scan_step, 0)
    o_ref[...] = _rmsnorm(b_ref[...], g_ref[...]).astype(o_ref.dtype)


def _rec_sample(z, prev8, h0, cw, cb, wg, ba, bx, lam, g, *, n_batch, t_len):
    assert t_len == SUBLANES and PAST_LEN > 0
    rows = n_batch * t_len
    total = z.size * 4 + (5 * rows + 2 * n_batch) * LRU_W * 4 + wg.size * 2
    vmem = pl.BlockSpec(memory_space=pltpu.VMEM)
    return pl.pallas_call(
        functools.partial(_rec_sample_body, n_batch=n_batch),
        in_specs=[vmem] * 10,
        out_specs=[vmem, vmem],
        out_shape=[jax.ShapeDtypeStruct((rows, LRU_W), bf16),
                   jax.ShapeDtypeStruct((n_batch, LRU_W), f32)],
        scratch_shapes=[pltpu.VMEM((rows, LRU_W), f32)] * 3,
        compiler_params=pltpu.CompilerParams(vmem_limit_bytes=_vmem_limit(0, total)),
        name="rec_sample",
    )(z, prev8, h0, cw, cb, wg, ba, bx, lam, g)


def _outproj_body(h_ref, attn_ref, rec_ref, w_ref, o_ref):
    o_ref[...] = (h_ref[...]
                  + jnp.dot(attn_ref[...], w_ref[0:ATTN_W, :], preferred_element_type=f32)
                  + jnp.dot(rec_ref[...], w_ref[ATTN_W:ATTN_W + LRU_W, :], preferred_element_type=f32))


def _outproj(h, attn, rec, w, l, *, tm):
    m = h.shape[0]
    blocks = 2 * tm * D_MODEL * 4 + tm * (ATTN_W + LRU_W) * 2 + w[0].size * 2
    return pl.pallas_call(
        _outproj_body,
        grid=(m // tm,),
        in_specs=[pl.BlockSpec((tm, D_MODEL), lambda i: (i, 0)),
                  pl.BlockSpec((tm, ATTN_W), lambda i: (i, 0)),
                  pl.BlockSpec((tm, LRU_W), lambda i: (i, 0)),
                  pl.BlockSpec((None,) + w.shape[1:], lambda i: (l, 0, 0))],
        out_specs=pl.BlockSpec((tm, D_MODEL), lambda i: (i, 0)),
        out_shape=jax.ShapeDtypeStruct((m, D_MODEL), f32),
        compiler_params=_params(("parallel",), _vmem_limit(blocks)),
        name="outproj",
    )(h, attn, rec, w)


def _mlp_body(h_ref, g_ref, wu_ref, wd_ref, o_ref, u_ref):
    @pl.when(pl.program_id(1) == 0)
    def _():
        x = h_ref[...]
        u_ref[...] = _rmsnorm(x, g_ref[...]).astype(bf16)
        o_ref[...] = x
    hid = jnp.dot(u_ref[...], wu_ref[...], preferred_element_type=f32)
    hid = jnp.square(jnp.maximum(hid, 0.0)).astype(bf16)
    o_ref[...] += jnp.dot(hid, wd_ref[...], preferred_element_type=f32)


def _mlp(h, g, wu, wd, l, *, tm, tf):
    m = h.shape[0]
    blocks = 2 * tm * D_MODEL * 4 + 2 * D_MODEL * tf * 2
    return pl.pallas_call(
        _mlp_body,
        grid=(m // tm, D_FF // tf),
        in_specs=[pl.BlockSpec((tm, D_MODEL), lambda i, f: (i, 0)),
                  pl.BlockSpec((None, 1, D_MODEL), lambda i, f: (l, 0, 0)),
                  pl.BlockSpec((None, D_MODEL, tf), lambda i, f: (l, 0, f)),
                  pl.BlockSpec((None, tf, D_MODEL), lambda i, f: (l, f, 0))],
        out_specs=pl.BlockSpec((tm, D_MODEL), lambda i, f: (i, 0)),
        out_shape=jax.ShapeDtypeStruct((m, D_MODEL), f32),
        scratch_shapes=[pltpu.VMEM((tm, D_MODEL), bf16)],
        compiler_params=_params(("parallel", "arbitrary"), _vmem_limit(blocks, tm * D_MODEL * 2 + tm * tf * 6)),
        name="mlp",
    )(h, g, wu, wd)


def _final_norm_body(h_ref, g_ref, o_ref):
    o_ref[0] = _rmsnorm(h_ref[0], g_ref[...])


def _final_norm(h, g, *, skip, tr):
    n_batch, t_len, _ = h.shape
    s_len = t_len - skip
    tiles = tr // SUBLANES
    skip_tiles = skip // SUBLANES
    h4 = h.reshape(n_batch, t_len // SUBLANES, SUBLANES, D_MODEL)
    out = pl.pallas_call(
        _final_norm_body,
        grid=(n_batch, s_len // tr),
        in_specs=[pl.BlockSpec((pl.Element(1), pl.Element(tiles), pl.Element(SUBLANES), pl.Element(D_MODEL)),
                               lambda b, r: (b, skip_tiles + r * tiles, 0, 0)),
                  pl.BlockSpec((1, D_MODEL), lambda b, r: (0, 0))],
        out_specs=pl.BlockSpec((1, tiles, SUBLANES, D_MODEL), lambda b, r: (b, r, 0, 0)),
        out_shape=jax.ShapeDtypeStruct((n_batch, s_len // SUBLANES, SUBLANES, D_MODEL), f32),
        compiler_params=_params(("parallel", "parallel"), _vmem_limit(2 * tr * D_MODEL * 4)),
        name="final_norm",
    )(h4, g)
    return out.reshape(n_batch, s_len, D_MODEL)


def _row_tile(m, candidates):
    for tm in candidates:
        if m % tm == 0:
            return tm
    raise ValueError(f"no row tile for {m} rows")


def kernel(x_prompt, x_sample, cache_k_win, cache_v_win, state_conv, state_lru, meta_tokens, norm_mix_g, w_in,
           conv_w, conv_b, w_gate_a, b_gate_a, w_gate_x, b_gate_x, lru_lambda, attn_sinks, rel_bias, attn_out_g,
           rec_out_g, w_out, norm_mlp_g, w_up, w_down, final_norm_g):
    n_p, s_p, _ = x_prompt.shape
    n_s, t_s, _ = x_sample.shape
    t_p = N_META + s_p
    buf = cache_k_win.shape[2]
    assert t_p % BF16_ROWS == 0 and buf == WINDOW

    w_in_b, w_out_b = w_in.astype(bf16), w_out.astype(bf16)
    w_up_b, w_down_b = w_up.astype(bf16), w_down.astype(bf16)
    w_gates = jnp.concatenate([w_gate_a, w_gate_x], axis=-1).astype(bf16)
    rows3 = lambda p: p[:, None, :]
    g_mix, g_mlp, g_attn, g_rec = rows3(norm_mix_g), rows3(norm_mlp_g), rows3(attn_out_g), rows3(rec_out_g)
    cb3, ba3, bx3, lam3 = rows3(conv_b), rows3(b_gate_a), rows3(b_gate_x), rows3(lru_lambda)
    g_attn_t = jnp.broadcast_to(
        attn_out_g.reshape(DEPTH, N_Q_HEADS, HEAD_DIM).transpose(0, 2, 1)[..., None],
        (DEPTH, HEAD_DIM, N_Q_HEADS, WINDOW)).reshape(DEPTH, HEAD_DIM, ATTN_W)

    meta = jnp.broadcast_to(meta_tokens.astype(x_prompt.dtype)[None], (n_p, N_META, D_MODEL))
    hp = jnp.concatenate([meta, x_prompt], axis=1).reshape(n_p * t_p, D_MODEL)
    hs = x_sample.reshape(n_s * t_s, D_MODEL)

    bias_qk, bias_kq = _bias_tables(rel_bias)
    bias_qk = bias_qk.reshape(N_Q_HEADS * WINDOW, 2 * WINDOW)
    prev_p = jnp.zeros((n_p, SUBLANES, LRU_W), f32)
    h0_p = jnp.zeros((n_p, 1, LRU_W), f32)
    ck = cache_k_win.reshape(DEPTH, n_s, buf, KV_W)
    cv = cache_v_win.reshape(DEPTH, n_s, buf, KV_W)
    prev_s = jnp.pad(state_conv, ((0, 0), (0, 0), (SUBLANES - (CONV_W - 1), 0), (0, 0)))
    prev_s = prev_s.reshape(DEPTH, n_s * SUBLANES, LRU_W)

    tm_p = _row_tile(n_p * t_p, (688, 344))
    tm_s = n_s * t_s
    tc_p = _row_tile(t_p, (688, 344, 48, 16))
    tn = IN_W // 2

    kp_l, vp_l, cp_l, lp_l, ks_l, vs_l, cs_l, ls_l = ([] for _ in range(8))
    for l in range(DEPTH):
        z = _inproj(hp, g_mix, w_in_b, l, tm=tm_p, tn=tn)
        attn = _attn_prompt(z, bias_kq, attn_sinks, g_attn_t, l, n_batch=n_p, t_len=t_p)
        rec, lru = _rec_prompt(z, prev_p, h0_p, conv_w, cb3, w_gates, ba3, bx3, lam3, g_rec, l,
                               n_batch=n_p, t_len=t_p, tc=tc_p)
        hp = _outproj(hp, attn, rec, w_out_b, l, tm=tm_p)
        hp = _mlp(hp, g_mlp, w_up_b, w_down_b, l, tm=tm_p, tf=1024)
        z3 = z.reshape(n_p, t_p, IN_W)
        kp_l.append(z3[:, t_p - WINDOW:, Z_K:Z_K + KV_W].reshape(n_p, WINDOW, N_KV_HEADS, HEAD_DIM))
        vp_l.append(z3[:, t_p - WINDOW:, Z_V:Z_V + KV_W].reshape(n_p, WINDOW, N_KV_HEADS, HEAD_DIM))
        cp_l.append(z3[:, t_p - (CONV_W - 1):, Z_XR:Z_XR + LRU_W])
        lp_l.append(lru[:, 0])
        z = _inproj(hs, g_mix, w_in_b, l, tm=tm_s, tn=tn)
        attn, nk, nv = _attn_sample(z, ck[l], cv[l], bias_qk, attn_sinks[l], g_attn[l], n_batch=n_s, t_len=t_s)
        rec, lru = _rec_sample(z, prev_s[l], state_lru[l], conv_w[l], cb3[l], w_gates[l], ba3[l], bx3[l],
                               lam3[l], g_rec[l], n_batch=n_s, t_len=t_s)
        hs = _outproj(hs, attn, rec, w_out_b, l, tm=tm_s)
        hs = _mlp(hs, g_mlp, w_up_b, w_down_b, l, tm=tm_s, tf=1024)
        z3 = z.reshape(n_s, t_s, IN_W)
        ks_l.append(nk.reshape(n_s, buf, N_KV_HEADS, HEAD_DIM))
        vs_l.append(nv.reshape(n_s, buf, N_KV_HEADS, HEAD_DIM))
        cs_l.append(z3[:, t_s - (CONV_W - 1):, Z_XR:Z_XR + LRU_W])
        ls_l.append(lru)

    g_fin = final_norm_g[None, :]
    y_prompt = _final_norm(hp.reshape(n_p, t_p, D_MODEL), g_fin, skip=N_META, tr=256)
    y_sample = _final_norm(hs.reshape(1, n_s * t_s, D_MODEL), g_fin, skip=0, tr=n_s * t_s)
    y_sample = y_sample.reshape(n_s, t_s, D_MODEL)
    return (y_prompt, y_sample,
            jnp.stack(kp_l), jnp.stack(vp_l), jnp.stack(cp_l), jnp.stack(lp_l),
            jnp.stack(ks_l), jnp.stack(vs_l), jnp.stack(cs_l), jnp.stack(ls_l))
```

```python
import functools
import math

import jax
import jax.numpy as jnp
from jax import lax
from jax.experimental import pallas as pl
from jax.experimental.pallas import tpu as pltpu

f32 = jnp.float32
bf16 = jnp.bfloat16

D_MODEL = 2048
DEPTH = 4
PAST_LEN = 16384
HEAD_DIM = 128
N_Q_HEADS = 8
N_KV_HEADS = 2
Q_PER_KV = N_Q_HEADS // N_KV_HEADS
ATTN_W = N_Q_HEADS * HEAD_DIM
KV_W = N_KV_HEADS * HEAD_DIM
LRU_W = D_MODEL // 2
N_LRU_BLOCKS = 8
LRU_BLOCK = LRU_W // N_LRU_BLOCKS
CONV_W = 4
LRU_C = 8.0
IN_W = ATTN_W + 2 * KV_W + 2 * LRU_W
D_FF = 4 * D_MODEL
WINDOW = 128
N_BUCKETS = 32
MAX_DISTANCE = 128
N_META = 16
EPS = 1e-6
ATTN_SCALE = HEAD_DIM ** -0.5

Z_Q = 0
Z_K = ATTN_W
Z_V = Z_K + KV_W
Z_XR = Z_V + KV_W
Z_GR = Z_XR + LRU_W
LRU_HALF = LRU_W // 2

ATTN_LEAD = WINDOW + (-N_META) % WINDOW

SUBLANES = 8
BF16_ROWS = 16
V7X_VMEM_BYTES = 64 * 1024 * 1024
VMEM_CAP_BYTES = V7X_VMEM_BYTES - 2 * 1024 * 1024


def _vmem_limit(pipelined_bytes, scratch_bytes=0):
    est = 2 * pipelined_bytes + scratch_bytes
    return int(min(VMEM_CAP_BYTES, est + est // 2 + (8 << 20)))


def _params(semantics, vmem_bytes):
    return pltpu.CompilerParams(dimension_semantics=semantics, vmem_limit_bytes=vmem_bytes)


def _rms_scale(x):
    return lax.rsqrt(jnp.mean(x * x, axis=-1, keepdims=True) + EPS)


def _rmsnorm(x, g):
    return x * _rms_scale(x) * g


def _inproj_body(h_ref, g_ref, w_ref, z_ref, u_ref):
    @pl.when(pl.program_id(1) == 0)
    def _():
        u_ref[...] = _rmsnorm(h_ref[...], g_ref[...]).astype(bf16)
    z_ref[...] = jnp.dot(u_ref[...], w_ref[...], preferred_element_type=f32)


def _inproj(h, g, w, l, *, tm, tn):
    m = h.shape[0]
    blocks = tm * D_MODEL * 4 + D_MODEL * tn * 2 + tm * tn * 4
    return pl.pallas_call(
        _inproj_body,
        grid=(m // tm, IN_W // tn),
        in_specs=[pl.BlockSpec((tm, D_MODEL), lambda i, j: (i, 0)),
                  pl.BlockSpec((None, 1, D_MODEL), lambda i, j: (l, 0, 0)),
                  pl.BlockSpec((None, D_MODEL, tn), lambda i, j: (l, 0, j))],
        out_specs=pl.BlockSpec((tm, tn), lambda i, j: (i, j)),
        out_shape=jax.ShapeDtypeStruct((m, IN_W), f32),
        scratch_shapes=[pltpu.VMEM((tm, D_MODEL), bf16)],
        compiler_params=_params(("parallel", "arbitrary"), _vmem_limit(blocks, tm * D_MODEL * 2)),
        name="inproj",
    )(h, g, w)


def _rel_bias_of(dist, rel_ref, h):
    n = jnp.maximum(dist, 0)
    max_exact = N_BUCKETS // 2
    nf = jnp.maximum(n, 1).astype(f32)
    large = max_exact + (jnp.log(nf / max_exact) / math.log(MAX_DISTANCE / max_exact)
                         * (N_BUCKETS - max_exact)).astype(jnp.int32)
    large = jnp.minimum(large, N_BUCKETS - 1)
    bucket = jnp.where(n < max_exact, n, large)
    acc = jnp.zeros(dist.shape, f32)
    for b in range(N_BUCKETS):
        acc = jnp.where(bucket == b, rel_ref[b, h], acc)
    return jnp.where((dist >= 0) & (dist < WINDOW), acc, -jnp.inf)


def _bias_table_body(rel_ref, qk_ref, kq_ref):
    shape_qk = (WINDOW, 2 * WINDOW)
    dist_qk = (lax.broadcasted_iota(jnp.int32, shape_qk, 0) + WINDOW
               - lax.broadcasted_iota(jnp.int32, shape_qk, 1))
    shape_kq = (2 * WINDOW, WINDOW)
    dist_kq = (lax.broadcasted_iota(jnp.int32, shape_kq, 1) + WINDOW
               - lax.broadcasted_iota(jnp.int32, shape_kq, 0))
    for h in range(N_Q_HEADS):
        kv, g = divmod(h, Q_PER_KV)
        qk_ref[h] = _rel_bias_of(dist_qk, rel_ref, h)
        kq_ref[kv, :, g * WINDOW:(g + 1) * WINDOW] = _rel_bias_of(dist_kq, rel_ref, h)


def _bias_tables(rel_bias):
    return pl.pallas_call(
        _bias_table_body,
        in_specs=[pl.BlockSpec(memory_space=pltpu.SMEM)],
        out_shape=[jax.ShapeDtypeStruct((N_Q_HEADS, WINDOW, 2 * WINDOW), f32),
                   jax.ShapeDtypeStruct((N_KV_HEADS, 2 * WINDOW, Q_PER_KV * WINDOW), f32)],
        name="bias_tables",
    )(rel_bias)


def _attn_block_t(q_rows, kpad_ref, vt_ref, r0, bias_ref, sinks_ref, l, gt_ref, lead_keys):
    outs = []
    for kv in range(N_KV_HEADS):
        heads = range(kv * Q_PER_KV, (kv + 1) * Q_PER_KV)
        kwin = kpad_ref[pl.ds(r0, 2 * WINDOW), kv * HEAD_DIM:(kv + 1) * HEAD_DIM]
        q = jnp.concatenate([q_rows[:, h * HEAD_DIM:(h + 1) * HEAD_DIM] for h in heads], axis=0).astype(bf16)
        s = lax.dot_general(kwin, q, (((1,), (1,)), ((), ())), preferred_element_type=f32) * ATTN_SCALE
        s = s + bias_ref[kv]
        if lead_keys:
            key = lax.broadcasted_iota(jnp.int32, s.shape, 0)
            s = jnp.where(key < lead_keys, -jnp.inf, s)
        sink = jnp.concatenate([jnp.full((1, WINDOW), sinks_ref[l, h], f32) for h in heads], axis=1)
        m = jnp.maximum(jnp.max(s, axis=0, keepdims=True), sink)
        p = jnp.exp(s - m)
        denom = jnp.sum(p, axis=0, keepdims=True) + jnp.exp(sink - m)
        vt = vt_ref[kv * HEAD_DIM:(kv + 1) * HEAD_DIM, pl.ds(r0, 2 * WINDOW)]
        outs.append(jnp.dot(vt, p.astype(bf16), preferred_element_type=f32) / denom)
    ot = jnp.concatenate(outs, axis=1)
    sq = jnp.sum(ot * ot, axis=0, keepdims=True)
    ssq = sq[:, 0:WINDOW]
    for h in range(1, N_Q_HEADS):
        ssq = ssq + sq[:, h * WINDOW:(h + 1) * WINDOW]
    r = lax.rsqrt(ssq * (1.0 / ATTN_W) + EPS)
    yt = ot * jnp.concatenate([r] * N_Q_HEADS, axis=1) * gt_ref[...]
    return jnp.concatenate([yt[:, h * WINDOW:(h + 1) * WINDOW].T for h in range(N_Q_HEADS)], axis=1)


def _attn_prompt_body(q_ref, k_ref, v_ref, bias_ref, sinks_ref, gt_ref, o_ref, kpad_ref, vpad_ref, vt_ref,
                      *, t_len, l):
    pad_rows = kpad_ref.shape[0]
    kpad_ref[0:ATTN_LEAD, :] = jnp.zeros((ATTN_LEAD, KV_W), bf16)
    kpad_ref[ATTN_LEAD:pad_rows, :] = k_ref[...].astype(bf16)
    vpad_ref[0:ATTN_LEAD, :] = jnp.zeros((ATTN_LEAD, KV_W), f32)
    vpad_ref[ATTN_LEAD:pad_rows, :] = v_ref[...]

    def transpose_step(c, _):
        r = pl.multiple_of(c * WINDOW, WINDOW)
        chunk = vpad_ref[pl.ds(r, WINDOW), :]
        for kv in range(N_KV_HEADS):
            vt_ref[kv * HEAD_DIM:(kv + 1) * HEAD_DIM, pl.ds(r, WINDOW)] = (
                chunk[:, kv * HEAD_DIM:(kv + 1) * HEAD_DIM].T.astype(bf16))
        return 0
    lax.fori_loop(0, pad_rows // WINDOW, transpose_step, 0)

    n_first = 2 * WINDOW - ATTN_LEAD
    q0 = jnp.concatenate([jnp.zeros((WINDOW - n_first, ATTN_W), f32), q_ref[0:n_first, :]], axis=0)
    y0 = _attn_block_t(q0, kpad_ref, vt_ref, 0, bias_ref, sinks_ref, l, gt_ref, ATTN_LEAD)
    o_ref[0:n_first, :] = y0[WINDOW - n_first:, :].astype(o_ref.dtype)

    def block(j, lead_keys):
        r0 = j * WINDOW
        q0_row = j * WINDOW - (ATTN_LEAD - WINDOW)
        if not isinstance(j, int):
            r0, q0_row = pl.multiple_of(r0, WINDOW), pl.multiple_of(q0_row, BF16_ROWS)
        rows = pl.ds(q0_row, WINDOW)
        y = _attn_block_t(q_ref[rows, :], kpad_ref, vt_ref, r0, bias_ref, sinks_ref, l, gt_ref, lead_keys)
        o_ref[rows, :] = y.astype(o_ref.dtype)

    block(1, ATTN_LEAD - WINDOW)

    def step(j, _):
        block(j, 0)
        return 0
    lax.fori_loop(2, pad_rows // WINDOW - 1, step, 0, unroll=2)


def _attn_prompt(z, bias_kq, sinks, gt, l, *, n_batch, t_len):
    pad_rows = ATTN_LEAD + t_len
    assert pad_rows % WINDOW == 0 and (ATTN_LEAD - WINDOW) % BF16_ROWS == 0
    blocks = t_len * (ATTN_W + 2 * KV_W) * 4 + t_len * ATTN_W * 2 + bias_kq.size * 4 + HEAD_DIM * ATTN_W * 4
    scratch = pad_rows * KV_W * (2 + 4 + 2)
    return pl.pallas_call(
        functools.partial(_attn_prompt_body, t_len=t_len, l=l),
        grid=(n_batch,),
        in_specs=[pl.BlockSpec((t_len, ATTN_W), lambda b: (b, Z_Q // ATTN_W)),
                  pl.BlockSpec((t_len, KV_W), lambda b: (b, Z_K // KV_W)),
                  pl.BlockSpec((t_len, KV_W), lambda b: (b, Z_V // KV_W)),
                  pl.BlockSpec(bias_kq.shape, lambda b: (0, 0, 0)),
                  pl.BlockSpec(memory_space=pltpu.SMEM),
                  pl.BlockSpec((None, HEAD_DIM, ATTN_W), lambda b: (l, 0, 0))],
        out_specs=pl.BlockSpec((t_len, ATTN_W), lambda b: (b, 0)),
        out_shape=jax.ShapeDtypeStruct((n_batch * t_len, ATTN_W), bf16),
        scratch_shapes=[pltpu.VMEM((pad_rows, KV_W), bf16), pltpu.VMEM((pad_rows, KV_W), f32),
                        pltpu.VMEM((KV_W, pad_rows), bf16)],
        compiler_params=_params(("parallel",), _vmem_limit(blocks, scratch)),
        name="attn_prompt",
    )(z, z, z, bias_kq, sinks, gt)


def _attn_block(q_rows, kwin, vwin, bias_ref, sinks_ref, n_rows):
    outs = []
    for kv in range(N_KV_HEADS):
        heads = range(kv * Q_PER_KV, (kv + 1) * Q_PER_KV)
        k = kwin[:, kv * HEAD_DIM:(kv + 1) * HEAD_DIM]
        v = vwin[:, kv * HEAD_DIM:(kv + 1) * HEAD_DIM]
        q = jnp.concatenate([q_rows[:, h * HEAD_DIM:(h + 1) * HEAD_DIM] for h in heads], axis=0).astype(bf16)
        bias = jnp.concatenate([bias_ref[h * WINDOW:h * WINDOW + n_rows, :] for h in heads], axis=0)
        s = lax.dot_general(q, k, (((1,), (1,)), ((), ())), preferred_element_type=f32) * ATTN_SCALE + bias
        sink = jnp.concatenate([jnp.full((n_rows, 1), sinks_ref[h], f32) for h in heads], axis=0)
        m = jnp.maximum(jnp.max(s, axis=-1, keepdims=True), sink)
        p = jnp.exp(s - m)
        denom = jnp.sum(p, axis=-1, keepdims=True) + jnp.exp(sink - m)
        o = jnp.dot(p.astype(bf16), v, preferred_element_type=f32) / denom
        outs.extend(o[g * n_rows:(g + 1) * n_rows] for g in range(Q_PER_KV))
    return jnp.concatenate(outs, axis=1)


def _attn_sample_body(z_ref, ck_ref, cv_ref, bias_ref, sinks_ref, g_ref, o_ref, nk_ref, nv_ref, acc_ref,
                      *, n_batch, t_len):
    buf = ck_ref.shape[1]
    zeros = jnp.zeros((2 * WINDOW - buf - t_len, KV_W), f32)

    def step(b, _):
        r0 = pl.multiple_of(b * t_len, t_len)
        q_rows = z_ref[pl.ds(r0, t_len), Z_Q:Z_Q + ATTN_W]
        k_new = z_ref[pl.ds(r0, t_len), Z_K:Z_K + KV_W]
        v_new = z_ref[pl.ds(r0, t_len), Z_V:Z_V + KV_W]
        kwin = jnp.concatenate([ck_ref[b], k_new, zeros], axis=0).astype(bf16)
        vwin = jnp.concatenate([cv_ref[b], v_new, zeros], axis=0).astype(bf16)
        acc_ref[pl.ds(r0, t_len), :] = _attn_block(q_rows, kwin, vwin, bias_ref, sinks_ref, t_len)
        nk_ref[b, 0:buf - t_len, :] = ck_ref[b, t_len:buf, :]
        nk_ref[b, buf - t_len:buf, :] = k_new
        nv_ref[b, 0:buf - t_len, :] = cv_ref[b, t_len:buf, :]
        nv_ref[b, buf - t_len:buf, :] = v_new
        return 0
    lax.fori_loop(0, n_batch, step, 0, unroll=4)
    o_ref[...] = _rmsnorm(acc_ref[...], g_ref[...]).astype(o_ref.dtype)


def _attn_sample(z, ck, cv, bias_qk, sinks, g, *, n_batch, t_len):
    assert ck.shape[1] == WINDOW and t_len == SUBLANES
    rows = n_batch * t_len
    total = (z.size + 2 * ck.size + 2 * cv.size + bias_qk.size + rows * ATTN_W) * 4 + rows * ATTN_W * 2
    vmem = pl.BlockSpec(memory_space=pltpu.VMEM)
    return pl.pallas_call(
        functools.partial(_attn_sample_body, n_batch=n_batch, t_len=t_len),
        in_specs=[vmem, vmem, vmem, vmem, pl.BlockSpec(memory_space=pltpu.SMEM), vmem],
        out_specs=[vmem, vmem, vmem],
        out_shape=[jax.ShapeDtypeStruct((rows, ATTN_W), bf16),
                   jax.ShapeDtypeStruct(ck.shape, f32),
                   jax.ShapeDtypeStruct(cv.shape, f32)],
        scratch_shapes=[pltpu.VMEM((rows, ATTN_W), f32)],
        compiler_params=pltpu.CompilerParams(vmem_limit_bytes=_vmem_limit(0, total)),
        name="attn_sample",
    )(z, ck, cv, bias_qk, sinks, g)


def _lru_gates(xc, n, c0, wg_ref, ba_ref, bx_ref, lam_ref):
    lanes = pl.ds(c0, LRU_BLOCK)
    gates = jnp.dot(xc.astype(bf16), wg_ref[n], preferred_element_type=f32)
    gate_a = jax.nn.sigmoid(gates[:, :LRU_BLOCK] + ba_ref[:, lanes])
    gate_x = jax.nn.sigmoid(gates[:, LRU_BLOCK:] + bx_ref[:, lanes])
    log_a = -LRU_C * gate_a * jax.nn.softplus(-lam_ref[:, lanes])
    a = jnp.exp(log_a)
    y = -jnp.tanh(log_a) * (1.0 + a * a)
    mult = jnp.where(y > 0.0, y * lax.rsqrt(y), 0.0)
    return a, xc * gate_x, mult


def _tile_prefix(a, b, row):
    for d in (1, 2, 4):
        a_prev = pltpu.roll(a, d, axis=0)
        b_prev = pltpu.roll(b, d, axis=0)
        keep = row >= d
        b = jnp.where(keep, a * b_prev + b, b)
        a = jnp.where(keep, a * a_prev, a)
    return a, b


def _scan_tile(a, b, h_in, row):
    a, b = _tile_prefix(a, b, row)
    return a * h_in + b


def _last_row(h):
    return jnp.broadcast_to(h[SUBLANES - 1:SUBLANES, :], h.shape)


def _conv_taps(x, shifted, cw, cb):
    out = cb + shifted[CONV_W - 1] * cw[0:1]
    for j in range(1, CONV_W - 1):
        out = out + shifted[CONV_W - 1 - j] * cw[j:j + 1]
    return out + x * cw[CONV_W - 1:CONV_W]


def _rec_prompt_body(xl_ref, xh_ref, gl_ref, gh_ref, prev_ref, h0_ref, cw_ref, cb_ref, wg_ref, ba_ref, bx_ref,
                     lam_ref, g_ref, o_ref, lru_ref, xbig_ref, a_ref, b_ref, gate_ref, carry_ref, *, tc, n_chunks):
    c = pl.program_id(1)
    blocks_per_half = LRU_HALF // LRU_BLOCK

    @pl.when(c == 0)
    def _():
        xbig_ref[0:SUBLANES, :] = prev_ref[...]
        for n in range(N_LRU_BLOCKS):
            carry_ref[n] = jnp.broadcast_to(h0_ref[:, n * LRU_BLOCK:(n + 1) * LRU_BLOCK], (SUBLANES, LRU_BLOCK))

    @pl.when(c > 0)
    def _():
        xbig_ref[0:SUBLANES, :] = xbig_ref[tc:tc + SUBLANES, :]
    xbig_ref[SUBLANES:SUBLANES + tc, 0:LRU_HALF] = xl_ref[...]
    xbig_ref[SUBLANES:SUBLANES + tc, LRU_HALF:LRU_W] = xh_ref[...]

    def gates_step(n, gr_ref, n0):
        c0 = pl.multiple_of(n * LRU_BLOCK, LRU_BLOCK)
        lanes = pl.ds(c0, LRU_BLOCK)
        shifted = {k: xbig_ref[SUBLANES - k:SUBLANES - k + tc, lanes] for k in range(1, CONV_W)}
        xc = _conv_taps(xbig_ref[SUBLANES:SUBLANES + tc, lanes], shifted, cw_ref[:, lanes], cb_ref[:, lanes])
        a, gated, mult = _lru_gates(xc, n, c0, wg_ref, ba_ref, bx_ref, lam_ref)
        a_ref[n] = a
        b_ref[n] = gated * mult

        @pl.when(c == 0)
        def _():
            b_ref[n, 0:1, :] = gated[0:1, :]
        gate_ref[n] = jax.nn.gelu(gr_ref[:, pl.ds(pl.multiple_of((n - n0) * LRU_BLOCK, LRU_BLOCK), LRU_BLOCK)])
        return 0
    lax.fori_loop(0, blocks_per_half, lambda n, _: gates_step(n, gl_ref, 0), 0)
    lax.fori_loop(blocks_per_half, N_LRU_BLOCKS, lambda n, _: gates_step(n, gh_ref, blocks_per_half), 0)

    seg = tc // SUBLANES
    seg_rows = lambda r: pl.ds(r, SUBLANES, stride=seg)

    def local_scan(r, maps):
        out = []
        for n in range(N_LRU_BLOCKS):
            a = a_ref[n, seg_rows(r), :]
            a_cum = a * maps[2 * n]
            b_cum = a * maps[2 * n + 1] + b_ref[n, seg_rows(r), :]
            a_ref[n, seg_rows(r), :] = a_cum
            b_ref[n, seg_rows(r), :] = b_cum
            out += [a_cum, b_cum]
        return tuple(out)
    identity = (jnp.ones((SUBLANES, LRU_BLOCK), f32), jnp.zeros((SUBLANES, LRU_BLOCK), f32)) * N_LRU_BLOCKS
    seg_maps = lax.fori_loop(0, seg, local_scan, identity, unroll=2)

    row = lax.broadcasted_iota(jnp.int32, (SUBLANES, LRU_BLOCK), 0)
    h_in = []
    for n in range(N_LRU_BLOCKS):
        h_prev = carry_ref[n]
        a_cum, b_cum = _tile_prefix(seg_maps[2 * n], seg_maps[2 * n + 1], row)
        h_end = a_cum * h_prev + b_cum
        h_in.append(jnp.where(row == 0, h_prev, pltpu.roll(h_end, 1, axis=0)))
        carry_ref[n] = _last_row(h_end)

    def apply_scan(r, _):
        for n in range(N_LRU_BLOCKS):
            h = a_ref[n, seg_rows(r), :] * h_in[n] + b_ref[n, seg_rows(r), :]
            b_ref[n, seg_rows(r), :] = h * gate_ref[n, seg_rows(r), :]
        return 0
    lax.fori_loop(0, seg, apply_scan, 0, unroll=2)

    y = jnp.concatenate([b_ref[n] for n in range(N_LRU_BLOCKS)], axis=1)
    o_ref[...] = _rmsnorm(y, g_ref[...]).astype(o_ref.dtype)

    @pl.when(c == n_chunks - 1)
    def _():
        lru_ref[...] = jnp.concatenate([carry_ref[n] for n in range(N_LRU_BLOCKS)], axis=1)


def _rec_prompt(z, prev8, h0, cw, cb, wg, ba, bx, lam, g, l, *, n_batch, t_len, tc):
    n_chunks = t_len // tc
    row_vec = pl.BlockSpec((None, 1, LRU_W), lambda b, c: (l, 0, 0))
    z_half = lambda col: pl.BlockSpec((tc, LRU_HALF), lambda b, c: (b * n_chunks + c, col // LRU_HALF))
    blocks = 2 * tc * LRU_W * 4 + tc * LRU_W * 2 + wg[0].size * 2 + 2 * SUBLANES * LRU_W * 4
    scratch = (4 * tc + 2 * SUBLANES) * LRU_W * 4
    by_block = pltpu.VMEM((N_LRU_BLOCKS, tc, LRU_BLOCK), f32)
    assert tc % SUBLANES == 0
    return pl.pallas_call(
        functools.partial(_rec_prompt_body, tc=tc, n_chunks=n_chunks),
        grid=(n_batch, n_chunks),
        in_specs=[z_half(Z_XR), z_half(Z_XR + LRU_HALF), z_half(Z_GR), z_half(Z_GR + LRU_HALF),
                  pl.BlockSpec((None, SUBLANES, LRU_W), lambda b, c: (b, 0, 0)),
                  pl.BlockSpec((None, 1, LRU_W), lambda b, c: (b, 0, 0)),
                  pl.BlockSpec((None, CONV_W, LRU_W), lambda b, c: (l, 0, 0)),
                  row_vec,
                  pl.BlockSpec((None,) + wg.shape[1:], lambda b, c: (l, 0, 0, 0)),
                  row_vec, row_vec, row_vec, row_vec],
        out_specs=[pl.BlockSpec((tc, LRU_W), lambda b, c: (b * n_chunks + c, 0)),
                   pl.BlockSpec((None, SUBLANES, LRU_W), lambda b, c: (b, 0, 0))],
        out_shape=[jax.ShapeDtypeStruct((n_batch * t_len, LRU_W), bf16),
                   jax.ShapeDtypeStruct((n_batch, SUBLANES, LRU_W), f32)],
        scratch_shapes=[pltpu.VMEM((tc + SUBLANES, LRU_W), f32), by_block, by_block, by_block,
                        pltpu.VMEM((N_LRU_BLOCKS, SUBLANES, LRU_BLOCK), f32)],
        compiler_params=_params(("parallel", "arbitrary"), _vmem_limit(blocks, scratch)),
        name="rec_prompt",
    )(z, z, z, z, prev8, h0, cw, cb, wg, ba, bx, lam, g)


def _rec_sample_body(z_ref, prev_ref, h0_ref, cw_ref, cb_ref, wg_ref, ba_ref, bx_ref, lam_ref, g_ref,
                     o_ref, lru_ref, xc_ref, a_ref, b_ref, *, n_batch):
    row = lax.broadcasted_iota(jnp.int32, (SUBLANES, LRU_W), 0)

    def conv_step(b, _):
        r = pl.multiple_of(b * SUBLANES, SUBLANES)
        x = z_ref[pl.ds(r, SUBLANES), Z_XR:Z_XR + LRU_W]
        prev = prev_ref[pl.ds(r, SUBLANES), :]
        shifted = {k: jnp.where(row >= k, pltpu.roll(x, k, axis=0), pltpu.roll(prev, k, axis=0))
                   for k in range(1, CONV_W)}
        xc_ref[pl.ds(r, SUBLANES), :] = _conv_taps(x, shifted, cw_ref[...], cb_ref[...])
        return 0
    lax.fori_loop(0, n_batch, conv_step, 0)

    def gates_step(n, _):
        c0 = pl.multiple_of(n * LRU_BLOCK, LRU_BLOCK)
        lanes = pl.ds(c0, LRU_BLOCK)
        a, gated, mult = _lru_gates(xc_ref[:, lanes], n, c0, wg_ref, ba_ref, bx_ref, lam_ref)
        a_ref[:, lanes] = a
        b_ref[:, lanes] = gated * mult
        return 0
    lax.fori_loop(0, N_LRU_BLOCKS, gates_step, 0)

    def scan_step(b, _):
        rows = pl.ds(pl.multiple_of(b * SUBLANES, SUBLANES), SUBLANES)
        h_in = jnp.broadcast_to(h0_ref[pl.ds(b, 1), :], (SUBLANES, LRU_W))
        hh = _scan_tile(a_ref[rows, :], b_ref[rows, :], h_in, row)
        b_ref[rows, :] = hh * jax.nn.gelu(z_ref[rows, Z_GR:Z_GR + LRU_W])
        lru_ref[pl.ds(b, 1), :] = hh[SUBLANES - 1:SUBLANES, :]
        return 0
    lax.fori_loop(0, n_batch, scan_step, 0)
    o_ref[...] = _rmsnorm(b_ref[...], g_ref[...]).astype(o_ref.dtype)


def _rec_sample(z, prev8, h0, cw, cb, wg, ba, bx, lam, g, *, n_batch, t_len):
    assert t_len == SUBLANES and PAST_LEN > 0
    rows = n_batch * t_len
    total = z.size * 4 + (5 * rows + 2 * n_batch) * LRU_W * 4 + wg.size * 2
    vmem = pl.BlockSpec(memory_space=pltpu.VMEM)
    return pl.pallas_call(
        functools.partial(_rec_sample_body, n_batch=n_batch),
        in_specs=[vmem] * 10,
        out_specs=[vmem, vmem],
        out_shape=[jax.ShapeDtypeStruct((rows, LRU_W), bf16),
                   jax.ShapeDtypeStruct((n_batch, LRU_W), f32)],
        scratch_shapes=[pltpu.VMEM((rows, LRU_W), f32)] * 3,
        compiler_params=pltpu.CompilerParams(vmem_limit_bytes=_vmem_limit(0, total)),
        name="rec_sample",
    )(z, prev8, h0, cw, cb, wg, ba, bx, lam, g)


def _mix_mlp_body(h_ref, attn_ref, rec_ref, wo_ref, g_ref, wu_ref, wd_ref, o_ref, u_ref):
    @pl.when(pl.program_id(1) == 0)
    def _():
        o_ref[...] = h_ref[...] + jnp.dot(attn_ref[...], wo_ref[0:ATTN_W, :], preferred_element_type=f32)
        o_ref[...] += jnp.dot(rec_ref[...], wo_ref[ATTN_W:ATTN_W + LRU_W, :], preferred_element_type=f32)
        u_ref[...] = _rmsnorm(o_ref[...], g_ref[...]).astype(bf16)
    hid = jnp.dot(u_ref[...], wu_ref[...], preferred_element_type=f32)
    hid = jnp.square(jnp.maximum(hid, 0.0)).astype(bf16)
    o_ref[...] += jnp.dot(hid, wd_ref[...], preferred_element_type=f32)


def _mix_mlp(h, attn, rec, wo, g, wu, wd, l, *, tm, tf):
    m = h.shape[0]
    once = pl.Buffered(1)
    blocks = 2 * tm * D_MODEL * 4 + tm * (ATTN_W + LRU_W) * 2 + 2 * D_MODEL * tf * 2
    scratch = wo[0].size * 2 + tm * D_MODEL * 2 + tm * tf * 6
    return pl.pallas_call(
        _mix_mlp_body,
        grid=(m // tm, D_FF // tf),
        in_specs=[pl.BlockSpec((tm, D_MODEL), lambda i, f: (i, 0)),
                  pl.BlockSpec((tm, ATTN_W), lambda i, f: (i, 0)),
                  pl.BlockSpec((tm, LRU_W), lambda i, f: (i, 0)),
                  pl.BlockSpec((None,) + wo.shape[1:], lambda i, f: (l, 0, 0), pipeline_mode=once),
                  pl.BlockSpec((None, 1, D_MODEL), lambda i, f: (l, 0, 0), pipeline_mode=once),
                  pl.BlockSpec((None, D_MODEL, tf), lambda i, f: (l, 0, f)),
                  pl.BlockSpec((None, tf, D_MODEL), lambda i, f: (l, f, 0))],
        out_specs=pl.BlockSpec((tm, D_MODEL), lambda i, f: (i, 0)),
        out_shape=jax.ShapeDtypeStruct((m, D_MODEL), f32),
        scratch_shapes=[pltpu.VMEM((tm, D_MODEL), bf16)],
        compiler_params=_params(("parallel", "arbitrary"), _vmem_limit(blocks, scratch)),
        name="mix_mlp",
    )(h, attn, rec, wo, g, wu, wd)


def _final_norm_body(h_ref, g_ref, o_ref):
    o_ref[0] = _rmsnorm(h_ref[0], g_ref[...])


def _final_norm(h, g, *, skip, tr):
    n_batch, t_len, _ = h.shape
    s_len = t_len - skip
    tiles = tr // SUBLANES
    skip_tiles = skip // SUBLANES
    h4 = h.reshape(n_batch, t_len // SUBLANES, SUBLANES, D_MODEL)
    out = pl.pallas_call(
        _final_norm_body,
        grid=(n_batch, s_len // tr),
        in_specs=[pl.BlockSpec((pl.Element(1), pl.Element(tiles), pl.Element(SUBLANES), pl.Element(D_MODEL)),
                               lambda b, r: (b, skip_tiles + r * tiles, 0, 0)),
                  pl.BlockSpec((1, D_MODEL), lambda b, r: (0, 0))],
        out_specs=pl.BlockSpec((1, tiles, SUBLANES, D_MODEL), lambda b, r: (b, r, 0, 0)),
        out_shape=jax.ShapeDtypeStruct((n_batch, s_len // SUBLANES, SUBLANES, D_MODEL), f32),
        compiler_params=_params(("parallel", "parallel"), _vmem_limit(2 * tr * D_MODEL * 4)),
        name="final_norm",
    )(h4, g)
    return out.reshape(n_batch, s_len, D_MODEL)


def _row_tile(m, candidates):
    for tm in candidates:
        if m % tm == 0:
            return tm
    raise ValueError(f"no row tile for {m} rows")


def kernel(x_prompt, x_sample, cache_k_win, cache_v_win, state_conv, state_lru, meta_tokens, norm_mix_g, w_in,
           conv_w, conv_b, w_gate_a, b_gate_a, w_gate_x, b_gate_x, lru_lambda, attn_sinks, rel_bias, attn_out_g,
           rec_out_g, w_out, norm_mlp_g, w_up, w_down, final_norm_g):
    n_p, s_p, _ = x_prompt.shape
    n_s, t_s, _ = x_sample.shape
    t_p = N_META + s_p
    buf = cache_k_win.shape[2]
    assert t_p % BF16_ROWS == 0 and buf == WINDOW

    w_in_b, w_out_b = w_in.astype(bf16), w_out.astype(bf16)
    w_up_b, w_down_b = w_up.astype(bf16), w_down.astype(bf16)
    w_gates = jnp.concatenate([w_gate_a, w_gate_x], axis=-1).astype(bf16)
    rows3 = lambda p: p[:, None, :]
    g_mix, g_mlp, g_attn, g_rec = rows3(norm_mix_g), rows3(norm_mlp_g), rows3(attn_out_g), rows3(rec_out_g)
    cb3, ba3, bx3, lam3 = rows3(conv_b), rows3(b_gate_a), rows3(b_gate_x), rows3(lru_lambda)
    g_attn_t = jnp.broadcast_to(
        attn_out_g.reshape(DEPTH, N_Q_HEADS, HEAD_DIM).transpose(0, 2, 1)[..., None],
        (DEPTH, HEAD_DIM, N_Q_HEADS, WINDOW)).reshape(DEPTH, HEAD_DIM, ATTN_W)

    meta = jnp.broadcast_to(meta_tokens.astype(x_prompt.dtype)[None], (n_p, N_META, D_MODEL))
    hp = jnp.concatenate([meta, x_prompt], axis=1).reshape(n_p * t_p, D_MODEL)
    hs = x_sample.reshape(n_s * t_s, D_MODEL)

    bias_qk, bias_kq = _bias_tables(rel_bias)
    bias_qk = bias_qk.reshape(N_Q_HEADS * WINDOW, 2 * WINDOW)
    prev_p = jnp.zeros((n_p, SUBLANES, LRU_W), f32)
    h0_p = jnp.zeros((n_p, 1, LRU_W), f32)
    ck = cache_k_win.reshape(DEPTH, n_s, buf, KV_W)
    cv = cache_v_win.reshape(DEPTH, n_s, buf, KV_W)
    prev_s = jnp.pad(state_conv, ((0, 0), (0, 0), (SUBLANES - (CONV_W - 1), 0), (0, 0)))
    prev_s = prev_s.reshape(DEPTH, n_s * SUBLANES, LRU_W)

    tm_p = _row_tile(n_p * t_p, (688, 344))
    tm_s = n_s * t_s
    tc_p = _row_tile(t_p, (688, 344, 48, 16))
    tn = IN_W // 2

    kp_l, vp_l, cp_l, lp_l, ks_l, vs_l, cs_l, ls_l = ([] for _ in range(8))
    for l in range(DEPTH):
        z = _inproj(hp, g_mix, w_in_b, l, tm=tm_p, tn=tn)
        attn = _attn_prompt(z, bias_kq, attn_sinks, g_attn_t, l, n_batch=n_p, t_len=t_p)
        rec, lru = _rec_prompt(z, prev_p, h0_p, conv_w, cb3, w_gates, ba3, bx3, lam3, g_rec, l,
                               n_batch=n_p, t_len=t_p, tc=tc_p)
        hp = _mix_mlp(hp, attn, rec, w_out_b, g_mlp, w_up_b, w_down_b, l, tm=tm_p, tf=1024)
        z3 = z.reshape(n_p, t_p, IN_W)
        kp_l.append(z3[:, t_p - WINDOW:, Z_K:Z_K + KV_W].reshape(n_p, WINDOW, N_KV_HEADS, HEAD_DIM))
        vp_l.append(z3[:, t_p - WINDOW:, Z_V:Z_V + KV_W].reshape(n_p, WINDOW, N_KV_HEADS, HEAD_DIM))
        cp_l.append(z3[:, t_p - (CONV_W - 1):, Z_XR:Z_XR + LRU_W])
        lp_l.append(lru[:, 0])
        z = _inproj(hs, g_mix, w_in_b, l, tm=tm_s, tn=tn)
        attn, nk, nv = _attn_sample(z, ck[l], cv[l], bias_qk, attn_sinks[l], g_attn[l], n_batch=n_s, t_len=t_s)
        rec, lru = _rec_sample(z, prev_s[l], state_lru[l], conv_w[l], cb3[l], w_gates[l], ba3[l], bx3[l],
                               lam3[l], g_rec[l], n_batch=n_s, t_len=t_s)
        hs = _mix_mlp(hs, attn, rec, w_out_b, g_mlp, w_up_b, w_down_b, l, tm=tm_s, tf=1024)
        z3 = z.reshape(n_s, t_s, IN_W)
        ks_l.append(nk.reshape(n_s, buf, N_KV_HEADS, HEAD_DIM))
        vs_l.append(nv.reshape(n_s, buf, N_KV_HEADS, HEAD_DIM))
        cs_l.append(z3[:, t_s - (CONV_W - 1):, Z_XR:Z_XR + LRU_W])
        ls_l.append(lru)

    g_fin = final_norm_g[None, :]
    y_prompt = _final_norm(hp.reshape(n_p, t_p, D_MODEL), g_fin, skip=N_META, tr=256)
    y_sample = _final_norm(hs.reshape(1, n_s * t_s, D_MODEL), g_fin, skip=0, tr=n_s * t_s)
    y_sample = y_sample.reshape(n_s, t_s, D_MODEL)
    return (y_prompt, y_sample,
            jnp.stack(kp_l), jnp.stack(vp_l), jnp.stack(cp_l), jnp.stack(lp_l),
            jnp.stack(ks_l), jnp.stack(vs_l), jnp.stack(cs_l), jnp.stack(ls_l))
```

```python
import functools
import math

import jax
import jax.numpy as jnp
from jax import lax
from jax.experimental import pallas as pl
from jax.experimental.pallas import tpu as pltpu

f32 = jnp.float32
bf16 = jnp.bfloat16

D_MODEL = 2048
DEPTH = 4
PAST_LEN = 16384
HEAD_DIM = 128
N_Q_HEADS = 8
N_KV_HEADS = 2
Q_PER_KV = N_Q_HEADS // N_KV_HEADS
ATTN_W = N_Q_HEADS * HEAD_DIM
KV_W = N_KV_HEADS * HEAD_DIM
LRU_W = D_MODEL // 2
N_LRU_BLOCKS = 8
LRU_BLOCK = LRU_W // N_LRU_BLOCKS
CONV_W = 4
LRU_C = 8.0
IN_W = ATTN_W + 2 * KV_W + 2 * LRU_W
D_FF = 4 * D_MODEL
WINDOW = 128
N_BUCKETS = 32
MAX_DISTANCE = 128
N_META = 16
EPS = 1e-6
ATTN_SCALE = HEAD_DIM ** -0.5

Z_Q = 0
Z_K = ATTN_W
Z_V = Z_K + KV_W
Z_XR = Z_V + KV_W
Z_GR = Z_XR + LRU_W
LRU_HALF = LRU_W // 2

ATTN_LEAD = WINDOW + (-N_META) % WINDOW

SUBLANES = 8
BF16_ROWS = 16
V7X_VMEM_BYTES = 64 * 1024 * 1024
VMEM_CAP_BYTES = V7X_VMEM_BYTES - 2 * 1024 * 1024


def _vmem_limit(pipelined_bytes, scratch_bytes=0):
    est = 2 * pipelined_bytes + scratch_bytes
    return int(min(VMEM_CAP_BYTES, est + est // 2 + (8 << 20)))


def _params(semantics, vmem_bytes):
    return pltpu.CompilerParams(dimension_semantics=semantics, vmem_limit_bytes=vmem_bytes)


def _rms_scale(x):
    return lax.rsqrt(jnp.mean(x * x, axis=-1, keepdims=True) + EPS)


def _rmsnorm(x, g):
    return x * _rms_scale(x) * g


def _inproj_body(h_ref, g_ref, w_ref, z_ref, u_ref):
    @pl.when(pl.program_id(1) == 0)
    def _():
        u_ref[...] = _rmsnorm(h_ref[...], g_ref[...]).astype(bf16)
    z_ref[...] = jnp.dot(u_ref[...], w_ref[...], preferred_element_type=f32)


def _inproj(h, g, w, l, *, tm, tn):
    m = h.shape[0]
    blocks = tm * D_MODEL * 4 + D_MODEL * tn * 2 + tm * tn * 4
    return pl.pallas_call(
        _inproj_body,
        grid=(m // tm, IN_W // tn),
        in_specs=[pl.BlockSpec((tm, D_MODEL), lambda i, j: (i, 0)),
                  pl.BlockSpec((None, 1, D_MODEL), lambda i, j: (l, 0, 0)),
                  pl.BlockSpec((None, D_MODEL, tn), lambda i, j: (l, 0, j))],
        out_specs=pl.BlockSpec((tm, tn), lambda i, j: (i, j)),
        out_shape=jax.ShapeDtypeStruct((m, IN_W), f32),
        scratch_shapes=[pltpu.VMEM((tm, D_MODEL), bf16)],
        compiler_params=_params(("parallel", "arbitrary"), _vmem_limit(blocks, tm * D_MODEL * 2)),
        name="inproj",
    )(h, g, w)


def _inproj_qkv_body(h_ref, g_ref, w_ref, z_ref, u_ref):
    u_ref[...] = _rmsnorm(h_ref[...], g_ref[...]).astype(bf16)
    z_ref[...] = jnp.dot(u_ref[...], w_ref[...], preferred_element_type=f32)


def _inproj_qkv(h, g, w, l, *, tm):
    m = h.shape[0]
    once = pl.Buffered(1)
    blocks = tm * D_MODEL * 4 + tm * Z_XR * 4 + tm * D_MODEL * 2
    return pl.pallas_call(
        _inproj_qkv_body,
        grid=(m // tm,),
        in_specs=[pl.BlockSpec((tm, D_MODEL), lambda i: (i, 0)),
                  pl.BlockSpec((None, 1, D_MODEL), lambda i: (l, 0, 0), pipeline_mode=once),
                  pl.BlockSpec((None, D_MODEL, Z_XR), lambda i: (l, 0, 0), pipeline_mode=once)],
        out_specs=[pl.BlockSpec((tm, Z_XR), lambda i: (i, 0)),
                   pl.BlockSpec((tm, D_MODEL), lambda i: (i, 0))],
        out_shape=[jax.ShapeDtypeStruct((m, Z_XR), f32), jax.ShapeDtypeStruct((m, D_MODEL), bf16)],
        compiler_params=_params(("parallel",), _vmem_limit(blocks, D_MODEL * Z_XR * 2)),
        name="inproj_qkv",
    )(h, g, w)


def _rel_bias_of(dist, rel_ref, h):
    n = jnp.maximum(dist, 0)
    max_exact = N_BUCKETS // 2
    nf = jnp.maximum(n, 1).astype(f32)
    large = max_exact + (jnp.log(nf / max_exact) / math.log(MAX_DISTANCE / max_exact)
                         * (N_BUCKETS - max_exact)).astype(jnp.int32)
    large = jnp.minimum(large, N_BUCKETS - 1)
    bucket = jnp.where(n < max_exact, n, large)
    acc = jnp.zeros(dist.shape, f32)
    for b in range(N_BUCKETS):
        acc = jnp.where(bucket == b, rel_ref[b, h], acc)
    return jnp.where((dist >= 0) & (dist < WINDOW), acc, -jnp.inf)


def _bias_table_body(rel_ref, qk_ref, kq_ref):
    shape_qk = (WINDOW, 2 * WINDOW)
    dist_qk = (lax.broadcasted_iota(jnp.int32, shape_qk, 0) + WINDOW
               - lax.broadcasted_iota(jnp.int32, shape_qk, 1))
    shape_kq = (2 * WINDOW, WINDOW)
    dist_kq = (lax.broadcasted_iota(jnp.int32, shape_kq, 1) + WINDOW
               - lax.broadcasted_iota(jnp.int32, shape_kq, 0))
    for h in range(N_Q_HEADS):
        kv, g = divmod(h, Q_PER_KV)
        qk_ref[h] = _rel_bias_of(dist_qk, rel_ref, h)
        kq_ref[kv, :, g * WINDOW:(g + 1) * WINDOW] = _rel_bias_of(dist_kq, rel_ref, h)


def _bias_tables(rel_bias):
    return pl.pallas_call(
        _bias_table_body,
        in_specs=[pl.BlockSpec(memory_space=pltpu.SMEM)],
        out_shape=[jax.ShapeDtypeStruct((N_Q_HEADS, WINDOW, 2 * WINDOW), f32),
                   jax.ShapeDtypeStruct((N_KV_HEADS, 2 * WINDOW, Q_PER_KV * WINDOW), f32)],
        name="bias_tables",
    )(rel_bias)


def _attn_block_t(q_rows, kpad_ref, vt_ref, r0, bias_ref, sinks_ref, l, gt_ref, lead_keys):
    outs = []
    for kv in range(N_KV_HEADS):
        heads = range(kv * Q_PER_KV, (kv + 1) * Q_PER_KV)
        kwin = kpad_ref[pl.ds(r0, 2 * WINDOW), kv * HEAD_DIM:(kv + 1) * HEAD_DIM]
        q = jnp.concatenate([q_rows[:, h * HEAD_DIM:(h + 1) * HEAD_DIM] for h in heads], axis=0).astype(bf16)
        s = lax.dot_general(kwin, q, (((1,), (1,)), ((), ())), preferred_element_type=f32) * ATTN_SCALE
        s = s + bias_ref[kv]
        if lead_keys:
            key = lax.broadcasted_iota(jnp.int32, s.shape, 0)
            s = jnp.where(key < lead_keys, -jnp.inf, s)
        sink = jnp.concatenate([jnp.full((1, WINDOW), sinks_ref[l, h], f32) for h in heads], axis=1)
        m = jnp.maximum(jnp.max(s, axis=0, keepdims=True), sink)
        p = jnp.exp(s - m)
        denom = jnp.sum(p, axis=0, keepdims=True) + jnp.exp(sink - m)
        vt = vt_ref[kv * HEAD_DIM:(kv + 1) * HEAD_DIM, pl.ds(r0, 2 * WINDOW)]
        outs.append(jnp.dot(vt, p.astype(bf16), preferred_element_type=f32) / denom)
    ot = jnp.concatenate(outs, axis=1)
    sq = jnp.sum(ot * ot, axis=0, keepdims=True)
    ssq = sq[:, 0:WINDOW]
    for h in range(1, N_Q_HEADS):
        ssq = ssq + sq[:, h * WINDOW:(h + 1) * WINDOW]
    r = lax.rsqrt(ssq * (1.0 / ATTN_W) + EPS)
    yt = ot * jnp.concatenate([r] * N_Q_HEADS, axis=1) * gt_ref[...]
    return jnp.concatenate([yt[:, h * WINDOW:(h + 1) * WINDOW].T for h in range(N_Q_HEADS)], axis=1)


def _attn_prompt_body(q_ref, k_ref, v_ref, bias_ref, sinks_ref, gt_ref, o_ref, kpad_ref, vpad_ref, vt_ref,
                      *, t_len, l):
    pad_rows = kpad_ref.shape[0]
    kpad_ref[0:ATTN_LEAD, :] = jnp.zeros((ATTN_LEAD, KV_W), bf16)
    kpad_ref[ATTN_LEAD:pad_rows, :] = k_ref[...].astype(bf16)
    vpad_ref[0:ATTN_LEAD, :] = jnp.zeros((ATTN_LEAD, KV_W), f32)
    vpad_ref[ATTN_LEAD:pad_rows, :] = v_ref[...]

    def transpose_step(c, _):
        r = pl.multiple_of(c * WINDOW, WINDOW)
        chunk = vpad_ref[pl.ds(r, WINDOW), :]
        for kv in range(N_KV_HEADS):
            vt_ref[kv * HEAD_DIM:(kv + 1) * HEAD_DIM, pl.ds(r, WINDOW)] = (
                chunk[:, kv * HEAD_DIM:(kv + 1) * HEAD_DIM].T.astype(bf16))
        return 0
    lax.fori_loop(0, pad_rows // WINDOW, transpose_step, 0)

    n_first = 2 * WINDOW - ATTN_LEAD
    q0 = jnp.concatenate([jnp.zeros((WINDOW - n_first, ATTN_W), f32), q_ref[0:n_first, :]], axis=0)
    y0 = _attn_block_t(q0, kpad_ref, vt_ref, 0, bias_ref, sinks_ref, l, gt_ref, ATTN_LEAD)
    o_ref[0:n_first, :] = y0[WINDOW - n_first:, :].astype(o_ref.dtype)

    def block(j, lead_keys):
        r0 = j * WINDOW
        q0_row = j * WINDOW - (ATTN_LEAD - WINDOW)
        if not isinstance(j, int):
            r0, q0_row = pl.multiple_of(r0, WINDOW), pl.multiple_of(q0_row, BF16_ROWS)
        rows = pl.ds(q0_row, WINDOW)
        y = _attn_block_t(q_ref[rows, :], kpad_ref, vt_ref, r0, bias_ref, sinks_ref, l, gt_ref, lead_keys)
        o_ref[rows, :] = y.astype(o_ref.dtype)

    block(1, ATTN_LEAD - WINDOW)

    def step(j, _):
        block(j, 0)
        return 0
    lax.fori_loop(2, pad_rows // WINDOW - 1, step, 0, unroll=2)


def _attn_prompt(z, bias_kq, sinks, gt, l, *, n_batch, t_len):
    pad_rows = ATTN_LEAD + t_len
    assert pad_rows % WINDOW == 0 and (ATTN_LEAD - WINDOW) % BF16_ROWS == 0
    blocks = t_len * (ATTN_W + 2 * KV_W) * 4 + t_len * ATTN_W * 2 + bias_kq.size * 4 + HEAD_DIM * ATTN_W * 4
    scratch = pad_rows * KV_W * (2 + 4 + 2)
    return pl.pallas_call(
        functools.partial(_attn_prompt_body, t_len=t_len, l=l),
        grid=(n_batch,),
        in_specs=[pl.BlockSpec((t_len, ATTN_W), lambda b: (b, Z_Q // ATTN_W)),
                  pl.BlockSpec((t_len, KV_W), lambda b: (b, Z_K // KV_W)),
                  pl.BlockSpec((t_len, KV_W), lambda b: (b, Z_V // KV_W)),
                  pl.BlockSpec(bias_kq.shape, lambda b: (0, 0, 0)),
                  pl.BlockSpec(memory_space=pltpu.SMEM),
                  pl.BlockSpec((None, HEAD_DIM, ATTN_W), lambda b: (l, 0, 0))],
        out_specs=pl.BlockSpec((t_len, ATTN_W), lambda b: (b, 0)),
        out_shape=jax.ShapeDtypeStruct((n_batch * t_len, ATTN_W), bf16),
        scratch_shapes=[pltpu.VMEM((pad_rows, KV_W), bf16), pltpu.VMEM((pad_rows, KV_W), f32),
                        pltpu.VMEM((KV_W, pad_rows), bf16)],
        compiler_params=_params(("parallel",), _vmem_limit(blocks, scratch)),
        name="attn_prompt",
    )(z, z, z, bias_kq, sinks, gt)


def _attn_block(q_rows, kwin, vwin, bias_ref, sinks_ref, n_rows):
    outs = []
    for kv in range(N_KV_HEADS):
        heads = range(kv * Q_PER_KV, (kv + 1) * Q_PER_KV)
        k = kwin[:, kv * HEAD_DIM:(kv + 1) * HEAD_DIM]
        v = vwin[:, kv * HEAD_DIM:(kv + 1) * HEAD_DIM]
        q = jnp.concatenate([q_rows[:, h * HEAD_DIM:(h + 1) * HEAD_DIM] for h in heads], axis=0).astype(bf16)
        bias = jnp.concatenate([bias_ref[h * WINDOW:h * WINDOW + n_rows, :] for h in heads], axis=0)
        s = lax.dot_general(q, k, (((1,), (1,)), ((), ())), preferred_element_type=f32) * ATTN_SCALE + bias
        sink = jnp.concatenate([jnp.full((n_rows, 1), sinks_ref[h], f32) for h in heads], axis=0)
        m = jnp.maximum(jnp.max(s, axis=-1, keepdims=True), sink)
        p = jnp.exp(s - m)
        denom = jnp.sum(p, axis=-1, keepdims=True) + jnp.exp(sink - m)
        o = jnp.dot(p.astype(bf16), v, preferred_element_type=f32) / denom
        outs.extend(o[g * n_rows:(g + 1) * n_rows] for g in range(Q_PER_KV))
    return jnp.concatenate(outs, axis=1)


def _attn_sample_body(z_ref, ck_ref, cv_ref, bias_ref, sinks_ref, g_ref, o_ref, nk_ref, nv_ref, acc_ref,
                      *, n_batch, t_len):
    buf = ck_ref.shape[1]
    zeros = jnp.zeros((2 * WINDOW - buf - t_len, KV_W), f32)

    def step(b, _):
        r0 = pl.multiple_of(b * t_len, t_len)
        q_rows = z_ref[pl.ds(r0, t_len), Z_Q:Z_Q + ATTN_W]
        k_new = z_ref[pl.ds(r0, t_len), Z_K:Z_K + KV_W]
        v_new = z_ref[pl.ds(r0, t_len), Z_V:Z_V + KV_W]
        kwin = jnp.concatenate([ck_ref[b], k_new, zeros], axis=0).astype(bf16)
        vwin = jnp.concatenate([cv_ref[b], v_new, zeros], axis=0).astype(bf16)
        acc_ref[pl.ds(r0, t_len), :] = _attn_block(q_rows, kwin, vwin, bias_ref, sinks_ref, t_len)
        nk_ref[b, 0:buf - t_len, :] = ck_ref[b, t_len:buf, :]
        nk_ref[b, buf - t_len:buf, :] = k_new
        nv_ref[b, 0:buf - t_len, :] = cv_ref[b, t_len:buf, :]
        nv_ref[b, buf - t_len:buf, :] = v_new
        return 0
    lax.fori_loop(0, n_batch, step, 0, unroll=4)
    o_ref[...] = _rmsnorm(acc_ref[...], g_ref[...]).astype(o_ref.dtype)


def _attn_sample(z, ck, cv, bias_qk, sinks, g, *, n_batch, t_len):
    assert ck.shape[1] == WINDOW and t_len == SUBLANES
    rows = n_batch * t_len
    total = (z.size + 2 * ck.size + 2 * cv.size + bias_qk.size + rows * ATTN_W) * 4 + rows * ATTN_W * 2
    vmem = pl.BlockSpec(memory_space=pltpu.VMEM)
    return pl.pallas_call(
        functools.partial(_attn_sample_body, n_batch=n_batch, t_len=t_len),
        in_specs=[vmem, vmem, vmem, vmem, pl.BlockSpec(memory_space=pltpu.SMEM), vmem],
        out_specs=[vmem, vmem, vmem],
        out_shape=[jax.ShapeDtypeStruct((rows, ATTN_W), bf16),
                   jax.ShapeDtypeStruct(ck.shape, f32),
                   jax.ShapeDtypeStruct(cv.shape, f32)],
        scratch_shapes=[pltpu.VMEM((rows, ATTN_W), f32)],
        compiler_params=pltpu.CompilerParams(vmem_limit_bytes=_vmem_limit(0, total)),
        name="attn_sample",
    )(z, ck, cv, bias_qk, sinks, g)


def _lru_gates(xc, n, c0, wg_ref, ba_ref, bx_ref, lam_ref):
    lanes = pl.ds(c0, LRU_BLOCK)
    gates = jnp.dot(xc.astype(bf16), wg_ref[n], preferred_element_type=f32)
    gate_a = jax.nn.sigmoid(gates[:, :LRU_BLOCK] + ba_ref[:, lanes])
    gate_x = jax.nn.sigmoid(gates[:, LRU_BLOCK:] + bx_ref[:, lanes])
    log_a = -LRU_C * gate_a * jax.nn.softplus(-lam_ref[:, lanes])
    a = jnp.exp(log_a)
    y = -jnp.tanh(log_a) * (1.0 + a * a)
    mult = jnp.where(y > 0.0, y * lax.rsqrt(y), 0.0)
    return a, xc * gate_x, mult


def _tile_prefix(a, b, row):
    for d in (1, 2, 4):
        a_prev = pltpu.roll(a, d, axis=0)
        b_prev = pltpu.roll(b, d, axis=0)
        keep = row >= d
        b = jnp.where(keep, a * b_prev + b, b)
        a = jnp.where(keep, a * a_prev, a)
    return a, b


def _scan_tile(a, b, h_in, row):
    a, b = _tile_prefix(a, b, row)
    return a * h_in + b


def _last_row(h):
    return jnp.broadcast_to(h[SUBLANES - 1:SUBLANES, :], h.shape)


def _conv_taps(x, shifted, cw, cb):
    out = cb + shifted[CONV_W - 1] * cw[0:1]
    for j in range(1, CONV_W - 1):
        out = out + shifted[CONV_W - 1 - j] * cw[j:j + 1]
    return out + x * cw[CONV_W - 1:CONV_W]


def _rec_prompt_body(u_ref, wxl_ref, wxh_ref, wgl_ref, wgh_ref, prev_ref, h0_ref, cw_ref, cb_ref, wg_ref, ba_ref,
                     bx_ref, lam_ref, g_ref, o_ref, lru_ref, tail_ref, xprev_ref, a_ref, b_ref, gate_ref,
                     carry_ref, *, tc, n_chunks):
    c = pl.program_id(1)

    @pl.when(c == 0)
    def _():
        xprev_ref[...] = prev_ref[...]
        for n in range(N_LRU_BLOCKS):
            carry_ref[n] = jnp.broadcast_to(h0_ref[:, n * LRU_BLOCK:(n + 1) * LRU_BLOCK], (SUBLANES, LRU_BLOCK))

    u = u_ref[...]
    pair = 2 * LRU_BLOCK
    pairs_per_half = LRU_HALF // pair
    n_pairs = LRU_W // pair

    def project(halves, p):
        cols = slice((p % pairs_per_half) * pair, (p % pairs_per_half + 1) * pair)
        return jnp.dot(u, halves[p // pairs_per_half][:, cols], preferred_element_type=f32)

    def gates(p, x):
        lanes = slice(p * pair, (p + 1) * pair)
        big = jnp.concatenate([xprev_ref[:, lanes], x], axis=0)
        shifted = {k: pltpu.roll(big, k, axis=0)[SUBLANES:] for k in range(1, CONV_W)}
        xc = _conv_taps(x, shifted, cw_ref[:, lanes], cb_ref[:, lanes])
        for q in range(2):
            n = 2 * p + q
            sub = slice(q * LRU_BLOCK, (q + 1) * LRU_BLOCK)
            a, gated, mult = _lru_gates(xc[:, sub], n, n * LRU_BLOCK, wg_ref, ba_ref, bx_ref, lam_ref)
            b = gated * mult
            a_ref[n] = a
            b_ref[n] = b
            b_ref[n, 0:1, :] = jnp.where(c == 0, gated[0:1, :], b[0:1, :])
        return x[tc - SUBLANES:, :]

    def gelu_gate(p, gate):
        for q in range(2):
            gate_ref[2 * p + q] = jax.nn.gelu(gate[:, q * LRU_BLOCK:(q + 1) * LRU_BLOCK])

    x_halves, gate_halves = (wxl_ref, wxh_ref), (wgl_ref, wgh_ref)
    steps = [(gates, x_halves, p) for p in range(n_pairs)] + [(gelu_gate, gate_halves, p) for p in range(n_pairs)]
    tails = []
    pending = project(steps[0][1], steps[0][2])
    for i, (consume, _, p) in enumerate(steps):
        current = pending
        if i + 1 < len(steps):
            pending = project(steps[i + 1][1], steps[i + 1][2])
        out = consume(p, current)
        if consume is gates:
            tails.append(out)
    tail = jnp.concatenate(tails, axis=1)
    xprev_ref[...] = tail
    tail_ref[...] = tail

    seg = tc // SUBLANES
    seg_rows = lambda r: pl.ds(r, SUBLANES, stride=seg)

    def local_scan(r, maps):
        out = []
        for n in range(N_LRU_BLOCKS):
            a = a_ref[n, seg_rows(r), :]
            a_cum = a * maps[2 * n]
            b_cum = a * maps[2 * n + 1] + b_ref[n, seg_rows(r), :]
            a_ref[n, seg_rows(r), :] = a_cum
            b_ref[n, seg_rows(r), :] = b_cum
            out += [a_cum, b_cum]
        return tuple(out)
    identity = (jnp.ones((SUBLANES, LRU_BLOCK), f32), jnp.zeros((SUBLANES, LRU_BLOCK), f32)) * N_LRU_BLOCKS
    seg_maps = lax.fori_loop(0, seg, local_scan, identity, unroll=2)

    row = lax.broadcasted_iota(jnp.int32, (SUBLANES, LRU_BLOCK), 0)
    h_in = []
    for n in range(N_LRU_BLOCKS):
        h_prev = carry_ref[n]
        a_cum, b_cum = _tile_prefix(seg_maps[2 * n], seg_maps[2 * n + 1], row)
        h_end = a_cum * h_prev + b_cum
        h_in.append(jnp.where(row == 0, h_prev, pltpu.roll(h_end, 1, axis=0)))
        carry_ref[n] = _last_row(h_end)

    def apply_scan(r, _):
        for n in range(N_LRU_BLOCKS):
            h = a_ref[n, seg_rows(r), :] * h_in[n] + b_ref[n, seg_rows(r), :]
            b_ref[n, seg_rows(r), :] = h * gate_ref[n, seg_rows(r), :]
        return 0
    lax.fori_loop(0, seg, apply_scan, 0, unroll=2)

    y = jnp.concatenate([b_ref[n] for n in range(N_LRU_BLOCKS)], axis=1)
    o_ref[...] = _rmsnorm(y, g_ref[...]).astype(o_ref.dtype)

    @pl.when(c == n_chunks - 1)
    def _():
        lru_ref[...] = jnp.concatenate([carry_ref[n] for n in range(N_LRU_BLOCKS)], axis=1)


def _rec_prompt(u, w_in, prev8, h0, cw, cb, wg, ba, bx, lam, g, l, *, n_batch, t_len, tc):
    n_chunks = t_len // tc
    once = pl.Buffered(1)
    row_vec = pl.BlockSpec((None, 1, LRU_W), lambda b, c: (l, 0, 0), pipeline_mode=once)
    w_half = lambda col: pl.BlockSpec((None, D_MODEL, LRU_HALF), lambda b, c: (l, 0, col // LRU_HALF),
                                      pipeline_mode=once)
    state = pl.BlockSpec((None, SUBLANES, LRU_W), lambda b, c: (b, 0, 0))
    blocks = tc * D_MODEL * 2 + tc * LRU_W * 2 + 3 * SUBLANES * LRU_W * 4
    scratch = (4 * D_MODEL * LRU_HALF * 2 + wg[0].size * 2 + (3 * tc + 2 * SUBLANES) * LRU_W * 4
               + 8 * tc * 2 * LRU_BLOCK * 4)
    by_block = pltpu.VMEM((N_LRU_BLOCKS, tc, LRU_BLOCK), f32)
    assert tc % SUBLANES == 0
    return pl.pallas_call(
        functools.partial(_rec_prompt_body, tc=tc, n_chunks=n_chunks),
        grid=(n_batch, n_chunks),
        in_specs=[pl.BlockSpec((tc, D_MODEL), lambda b, c: (b * n_chunks + c, 0)),
                  w_half(Z_XR), w_half(Z_XR + LRU_HALF), w_half(Z_GR), w_half(Z_GR + LRU_HALF),
                  state,
                  pl.BlockSpec((None, 1, LRU_W), lambda b, c: (b, 0, 0)),
                  pl.BlockSpec((None, CONV_W, LRU_W), lambda b, c: (l, 0, 0), pipeline_mode=once),
                  row_vec,
                  pl.BlockSpec((None,) + wg.shape[1:], lambda b, c: (l, 0, 0, 0), pipeline_mode=once),
                  row_vec, row_vec, row_vec, row_vec],
        out_specs=[pl.BlockSpec((tc, LRU_W), lambda b, c: (b * n_chunks + c, 0)), state, state],
        out_shape=[jax.ShapeDtypeStruct((n_batch * t_len, LRU_W), bf16),
                   jax.ShapeDtypeStruct((n_batch, SUBLANES, LRU_W), f32),
                   jax.ShapeDtypeStruct((n_batch, SUBLANES, LRU_W), f32)],
        scratch_shapes=[pltpu.VMEM((SUBLANES, LRU_W), f32), by_block, by_block, by_block,
                        pltpu.VMEM((N_LRU_BLOCKS, SUBLANES, LRU_BLOCK), f32)],
        compiler_params=_params(("parallel", "arbitrary"), _vmem_limit(blocks, scratch)),
        name="rec_prompt",
    )(u, w_in, w_in, w_in, w_in, prev8, h0, cw, cb, wg, ba, bx, lam, g)


def _rec_sample_body(z_ref, prev_ref, h0_ref, cw_ref, cb_ref, wg_ref, ba_ref, bx_ref, lam_ref, g_ref,
                     o_ref, lru_ref, xc_ref, a_ref, b_ref, *, n_batch):
    row = lax.broadcasted_iota(jnp.int32, (SUBLANES, LRU_W), 0)

    def conv_step(b, _):
        r = pl.multiple_of(b * SUBLANES, SUBLANES)
        x = z_ref[pl.ds(r, SUBLANES), Z_XR:Z_XR + LRU_W]
        prev = prev_ref[pl.ds(r, SUBLANES), :]
        shifted = {k: jnp.where(row >= k, pltpu.roll(x, k, axis=0), pltpu.roll(prev, k, axis=0))
                   for k in range(1, CONV_W)}
        xc_ref[pl.ds(r, SUBLANES), :] = _conv_taps(x, shifted, cw_ref[...], cb_ref[...])
        return 0
    lax.fori_loop(0, n_batch, conv_step, 0)

    def gates_step(n, _):
        c0 = pl.multiple_of(n * LRU_BLOCK, LRU_BLOCK)
        lanes = pl.ds(c0, LRU_BLOCK)
        a, gated, mult = _lru_gates(xc_ref[:, lanes], n, c0, wg_ref, ba_ref, bx_ref, lam_ref)
        a_ref[:, lanes] = a
        b_ref[:, lanes] = gated * mult
        return 0
    lax.fori_loop(0, N_LRU_BLOCKS, gates_step, 0)

    def scan_step(b, _):
        rows = pl.ds(pl.multiple_of(b * SUBLANES, SUBLANES), SUBLANES)
        h_in = jnp.broadcast_to(h0_ref[pl.ds(b, 1), :], (SUBLANES, LRU_W))
        hh = _scan_tile(a_ref[rows, :], b_ref[rows, :], h_in, row)
        b_ref[rows, :] = hh * jax.nn.gelu(z_ref[rows, Z_GR:Z_GR + LRU_W])
        lru_ref[pl.ds(b, 1), :] = hh[SUBLANES - 1:SUBLANES, :]
        return 0
    lax.fori_loop(0, n_batch, scan_step, 0)
    o_ref[...] = _rmsnorm(b_ref[...], g_ref[...]).astype(o_ref.dtype)


def _rec_sample(z, prev8, h0, cw, cb, wg, ba, bx, lam, g, *, n_batch, t_len):
    assert t_len == SUBLANES and PAST_LEN > 0
    rows = n_batch * t_len
    total = z.size * 4 + (5 * rows + 2 * n_batch) * LRU_W * 4 + wg.size * 2
    vmem = pl.BlockSpec(memory_space=pltpu.VMEM)
    return pl.pallas_call(
        functools.partial(_rec_sample_body, n_batch=n_batch),
        in_specs=[vmem] * 10,
        out_specs=[vmem, vmem],
        out_shape=[jax.ShapeDtypeStruct((rows, LRU_W), bf16),
                   jax.ShapeDtypeStruct((n_batch, LRU_W), f32)],
        scratch_shapes=[pltpu.VMEM((rows, LRU_W), f32)] * 3,
        compiler_params=pltpu.CompilerParams(vmem_limit_bytes=_vmem_limit(0, total)),
        name="rec_sample",
    )(z, prev8, h0, cw, cb, wg, ba, bx, lam, g)


def _mix_mlp_body(h_ref, attn_ref, rec_ref, wo_ref, g_ref, wu_ref, wd_ref, o_ref, u_ref):
    @pl.when(pl.program_id(1) == 0)
    def _():
        o_ref[...] = h_ref[...] + jnp.dot(attn_ref[...], wo_ref[0:ATTN_W, :], preferred_element_type=f32)
        o_ref[...] += jnp.dot(rec_ref[...], wo_ref[ATTN_W:ATTN_W + LRU_W, :], preferred_element_type=f32)
        u_ref[...] = _rmsnorm(o_ref[...], g_ref[...]).astype(bf16)
    hid = jnp.dot(u_ref[...], wu_ref[...], preferred_element_type=f32)
    hid = jnp.square(jnp.maximum(hid, 0.0)).astype(bf16)
    o_ref[...] += jnp.dot(hid, wd_ref[...], preferred_element_type=f32)


def _mix_mlp(h, attn, rec, wo, g, wu, wd, l, *, tm, tf):
    m = h.shape[0]
    once = pl.Buffered(1)
    blocks = 2 * tm * D_MODEL * 4 + tm * (ATTN_W + LRU_W) * 2 + 2 * D_MODEL * tf * 2
    scratch = wo[0].size * 2 + tm * D_MODEL * 2 + tm * tf * 6
    return pl.pallas_call(
        _mix_mlp_body,
        grid=(m // tm, D_FF // tf),
        in_specs=[pl.BlockSpec((tm, D_MODEL), lambda i, f: (i, 0)),
                  pl.BlockSpec((tm, ATTN_W), lambda i, f: (i, 0)),
                  pl.BlockSpec((tm, LRU_W), lambda i, f: (i, 0)),
                  pl.BlockSpec((None,) + wo.shape[1:], lambda i, f: (l, 0, 0), pipeline_mode=once),
                  pl.BlockSpec((None, 1, D_MODEL), lambda i, f: (l, 0, 0), pipeline_mode=once),
                  pl.BlockSpec((None, D_MODEL, tf), lambda i, f: (l, 0, f)),
                  pl.BlockSpec((None, tf, D_MODEL), lambda i, f: (l, f, 0))],
        out_specs=pl.BlockSpec((tm, D_MODEL), lambda i, f: (i, 0)),
        out_shape=jax.ShapeDtypeStruct((m, D_MODEL), f32),
        scratch_shapes=[pltpu.VMEM((tm, D_MODEL), bf16)],
        compiler_params=_params(("parallel", "arbitrary"), _vmem_limit(blocks, scratch)),
        name="mix_mlp",
    )(h, attn, rec, wo, g, wu, wd)


def _final_norm_body(h_ref, g_ref, o_ref):
    o_ref[0] = _rmsnorm(h_ref[0], g_ref[...])


def _final_norm(h, g, *, skip, tr):
    n_batch, t_len, _ = h.shape
    s_len = t_len - skip
    tiles = tr // SUBLANES
    skip_tiles = skip // SUBLANES
    h4 = h.reshape(n_batch, t_len // SUBLANES, SUBLANES, D_MODEL)
    out = pl.pallas_call(
        _final_norm_body,
        grid=(n_batch, s_len // tr),
        in_specs=[pl.BlockSpec((pl.Element(1), pl.Element(tiles), pl.Element(SUBLANES), pl.Element(D_MODEL)),
                               lambda b, r: (b, skip_tiles + r * tiles, 0, 0)),
                  pl.BlockSpec((1, D_MODEL), lambda b, r: (0, 0))],
        out_specs=pl.BlockSpec((1, tiles, SUBLANES, D_MODEL), lambda b, r: (b, r, 0, 0)),
        out_shape=jax.ShapeDtypeStruct((n_batch, s_len // SUBLANES, SUBLANES, D_MODEL), f32),
        compiler_params=_params(("parallel", "parallel"), _vmem_limit(2 * tr * D_MODEL * 4)),
        name="final_norm",
    )(h4, g)
    return out.reshape(n_batch, s_len, D_MODEL)


def _row_tile(m, candidates):
    for tm in candidates:
        if m % tm == 0:
            return tm
    raise ValueError(f"no row tile for {m} rows")


def kernel(x_prompt, x_sample, cache_k_win, cache_v_win, state_conv, state_lru, meta_tokens, norm_mix_g, w_in,
           conv_w, conv_b, w_gate_a, b_gate_a, w_gate_x, b_gate_x, lru_lambda, attn_sinks, rel_bias, attn_out_g,
           rec_out_g, w_out, norm_mlp_g, w_up, w_down, final_norm_g):
    n_p, s_p, _ = x_prompt.shape
    n_s, t_s, _ = x_sample.shape
    t_p = N_META + s_p
    buf = cache_k_win.shape[2]
    assert t_p % BF16_ROWS == 0 and buf == WINDOW

    w_in_b, w_out_b = w_in.astype(bf16), w_out.astype(bf16)
    w_up_b, w_down_b = w_up.astype(bf16), w_down.astype(bf16)
    w_gates = jnp.concatenate([w_gate_a, w_gate_x], axis=-1).astype(bf16)
    rows3 = lambda p: p[:, None, :]
    g_mix, g_mlp, g_attn, g_rec = rows3(norm_mix_g), rows3(norm_mlp_g), rows3(attn_out_g), rows3(rec_out_g)
    cb3, ba3, bx3, lam3 = rows3(conv_b), rows3(b_gate_a), rows3(b_gate_x), rows3(lru_lambda)
    g_attn_t = jnp.broadcast_to(
        attn_out_g.reshape(DEPTH, N_Q_HEADS, HEAD_DIM).transpose(0, 2, 1)[..., None],
        (DEPTH, HEAD_DIM, N_Q_HEADS, WINDOW)).reshape(DEPTH, HEAD_DIM, ATTN_W)

    meta = jnp.broadcast_to(meta_tokens.astype(x_prompt.dtype)[None], (n_p, N_META, D_MODEL))
    hp = jnp.concatenate([meta, x_prompt], axis=1).reshape(n_p * t_p, D_MODEL)
    hs = x_sample.reshape(n_s * t_s, D_MODEL)

    bias_qk, bias_kq = _bias_tables(rel_bias)
    bias_qk = bias_qk.reshape(N_Q_HEADS * WINDOW, 2 * WINDOW)
    prev_p = jnp.zeros((n_p, SUBLANES, LRU_W), f32)
    h0_p = jnp.zeros((n_p, 1, LRU_W), f32)
    ck = cache_k_win.reshape(DEPTH, n_s, buf, KV_W)
    cv = cache_v_win.reshape(DEPTH, n_s, buf, KV_W)
    prev_s = jnp.pad(state_conv, ((0, 0), (0, 0), (SUBLANES - (CONV_W - 1), 0), (0, 0)))
    prev_s = prev_s.reshape(DEPTH, n_s * SUBLANES, LRU_W)

    tm_p = _row_tile(n_p * t_p, (688, 344))
    tm_s = n_s * t_s
    tc_p = _row_tile(t_p, (688, 344, 48, 16))
    tn = IN_W // 2

    kp_l, vp_l, cp_l, lp_l, ks_l, vs_l, cs_l, ls_l = ([] for _ in range(8))
    for l in range(DEPTH):
        z, u = _inproj_qkv(hp, g_mix, w_in_b, l, tm=tm_p)
        attn = _attn_prompt(z, bias_kq, attn_sinks, g_attn_t, l, n_batch=n_p, t_len=t_p)
        rec, lru, x_tail = _rec_prompt(u, w_in_b, prev_p, h0_p, conv_w, cb3, w_gates, ba3, bx3, lam3, g_rec, l,
                                       n_batch=n_p, t_len=t_p, tc=tc_p)
        hp = _mix_mlp(hp, attn, rec, w_out_b, g_mlp, w_up_b, w_down_b, l, tm=tm_p, tf=1024)
        z3 = z.reshape(n_p, t_p, Z_XR)
        kp_l.append(z3[:, t_p - WINDOW:, Z_K:Z_K + KV_W].reshape(n_p, WINDOW, N_KV_HEADS, HEAD_DIM))
        vp_l.append(z3[:, t_p - WINDOW:, Z_V:Z_V + KV_W].reshape(n_p, WINDOW, N_KV_HEADS, HEAD_DIM))
        cp_l.append(x_tail[:, SUBLANES - (CONV_W - 1):])
        lp_l.append(lru[:, 0])
        z = _inproj(hs, g_mix, w_in_b, l, tm=tm_s, tn=tn)
        attn, nk, nv = _attn_sample(z, ck[l], cv[l], bias_qk, attn_sinks[l], g_attn[l], n_batch=n_s, t_len=t_s)
        rec, lru = _rec_sample(z, prev_s[l], state_lru[l], conv_w[l], cb3[l], w_gates[l], ba3[l], bx3[l],
                               lam3[l], g_rec[l], n_batch=n_s, t_len=t_s)
        hs = _mix_mlp(hs, attn, rec, w_out_b, g_mlp, w_up_b, w_down_b, l, tm=tm_s, tf=1024)
        z3 = z.reshape(n_s, t_s, IN_W)
        ks_l.append(nk.reshape(n_s, buf, N_KV_HEADS, HEAD_DIM))
        vs_l.append(nv.reshape(n_s, buf, N_KV_HEADS, HEAD_DIM))
        cs_l.append(z3[:, t_s - (CONV_W - 1):, Z_XR:Z_XR + LRU_W])
        ls_l.append(lru)

    g_fin = final_norm_g[None, :]
    y_prompt = _final_norm(hp.reshape(n_p, t_p, D_MODEL), g_fin, skip=N_META, tr=256)
    y_sample = _final_norm(hs.reshape(1, n_s * t_s, D_MODEL), g_fin, skip=0, tr=n_s * t_s)
    y_sample = y_sample.reshape(n_s, t_s, D_MODEL)
    return (y_prompt, y_sample,
            jnp.stack(kp_l), jnp.stack(vp_l), jnp.stack(cp_l), jnp.stack(lp_l),
            jnp.stack(ks_l), jnp.stack(vs_l), jnp.stack(cs_l), jnp.stack(ls_l))
```

```python
import functools
import math

import jax
import jax.numpy as jnp
from jax import lax
from jax.experimental import pallas as pl
from jax.experimental.pallas import tpu as pltpu

f32 = jnp.float32
bf16 = jnp.bfloat16

D_MODEL = 2048
DEPTH = 4
PAST_LEN = 16384
HEAD_DIM = 128
N_Q_HEADS = 8
N_KV_HEADS = 2
Q_PER_KV = N_Q_HEADS // N_KV_HEADS
ATTN_W = N_Q_HEADS * HEAD_DIM
KV_W = N_KV_HEADS * HEAD_DIM
LRU_W = D_MODEL // 2
N_LRU_BLOCKS = 8
LRU_BLOCK = LRU_W // N_LRU_BLOCKS
CONV_W = 4
LRU_C = 8.0
IN_W = ATTN_W + 2 * KV_W + 2 * LRU_W
D_FF = 4 * D_MODEL
WINDOW = 128
N_BUCKETS = 32
MAX_DISTANCE = 128
N_META = 16
EPS = 1e-6
ATTN_SCALE = HEAD_DIM ** -0.5
INV_ATTN_SCALE = HEAD_DIM ** 0.5
EXP2_PER_T = math.log2(math.e) / INV_ATTN_SCALE

Z_Q = 0
Z_K = ATTN_W
Z_V = Z_K + KV_W
Z_XR = Z_V + KV_W
Z_GR = Z_XR + LRU_W
LRU_HALF = LRU_W // 2

ATTN_LEAD = WINDOW + (-N_META) % WINDOW

SUBLANES = 8
BF16_ROWS = 16
V7X_VMEM_BYTES = 64 * 1024 * 1024
VMEM_CAP_BYTES = V7X_VMEM_BYTES - 2 * 1024 * 1024


def _vmem_limit(pipelined_bytes, scratch_bytes=0):
    est = 2 * pipelined_bytes + scratch_bytes
    return int(min(VMEM_CAP_BYTES, est + est // 2 + (8 << 20)))


def _params(semantics, vmem_bytes):
    return pltpu.CompilerParams(dimension_semantics=semantics, vmem_limit_bytes=vmem_bytes)


def _rms_scale(x):
    return lax.rsqrt(jnp.mean(x * x, axis=-1, keepdims=True) + EPS)


def _rmsnorm(x, g):
    return x * _rms_scale(x) * g


def _inproj_body(h_ref, g_ref, w_ref, z_ref, u_ref):
    @pl.when(pl.program_id(1) == 0)
    def _():
        u_ref[...] = _rmsnorm(h_ref[...], g_ref[...]).astype(bf16)
    z_ref[...] = jnp.dot(u_ref[...], w_ref[...].astype(bf16), preferred_element_type=f32)


def _inproj(h, g, w, l, *, tm, tn):
    m = h.shape[0]
    blocks = tm * D_MODEL * 4 + D_MODEL * tn * 4 + tm * tn * 4 + D_MODEL * tn
    return pl.pallas_call(
        _inproj_body,
        grid=(m // tm, IN_W // tn),
        in_specs=[pl.BlockSpec((tm, D_MODEL), lambda i, j: (i, 0)),
                  pl.BlockSpec((None, 1, D_MODEL), lambda i, j: (l, 0, 0)),
                  pl.BlockSpec((None, D_MODEL, tn), lambda i, j: (l, 0, j))],
        out_specs=pl.BlockSpec((tm, tn), lambda i, j: (i, j)),
        out_shape=jax.ShapeDtypeStruct((m, IN_W), f32),
        scratch_shapes=[pltpu.VMEM((tm, D_MODEL), bf16)],
        compiler_params=_params(("parallel", "arbitrary"), _vmem_limit(blocks, tm * D_MODEL * 2)),
        name="inproj",
    )(h, g, w)


def _inproj_qkv_body(h_ref, g_ref, w_ref, z_ref, u_ref, wb_ref):
    @pl.when(pl.program_id(0) == 0)
    def _():
        wb_ref[...] = w_ref[...].astype(bf16)
    u_ref[...] = _rmsnorm(h_ref[...], g_ref[...]).astype(bf16)
    z_ref[...] = jnp.dot(u_ref[...], wb_ref[...], preferred_element_type=f32)


def _inproj_qkv(h, g, w, l, *, tm):
    m = h.shape[0]
    once = pl.Buffered(1)
    blocks = tm * D_MODEL * 4 + tm * Z_XR * 4 + tm * D_MODEL * 2
    return pl.pallas_call(
        _inproj_qkv_body,
        grid=(m // tm,),
        in_specs=[pl.BlockSpec((tm, D_MODEL), lambda i: (i, 0)),
                  pl.BlockSpec((None, 1, D_MODEL), lambda i: (l, 0, 0), pipeline_mode=once),
                  pl.BlockSpec((None, D_MODEL, Z_XR), lambda i: (l, 0, 0), pipeline_mode=once)],
        out_specs=[pl.BlockSpec((tm, Z_XR), lambda i: (i, 0)),
                   pl.BlockSpec((tm, D_MODEL), lambda i: (i, 0))],
        out_shape=[jax.ShapeDtypeStruct((m, Z_XR), f32), jax.ShapeDtypeStruct((m, D_MODEL), bf16)],
        scratch_shapes=[pltpu.VMEM((D_MODEL, Z_XR), bf16)],
        compiler_params=_params(("arbitrary",), _vmem_limit(blocks, D_MODEL * Z_XR * 6)),
        name="inproj_qkv",
    )(h, g, w)


def _rel_bias_of(dist, rel_ref, h):
    n = jnp.maximum(dist, 0)
    max_exact = N_BUCKETS // 2
    nf = jnp.maximum(n, 1).astype(f32)
    large = max_exact + (jnp.log(nf / max_exact) / math.log(MAX_DISTANCE / max_exact)
                         * (N_BUCKETS - max_exact)).astype(jnp.int32)
    large = jnp.minimum(large, N_BUCKETS - 1)
    bucket = jnp.where(n < max_exact, n, large)
    acc = jnp.zeros(dist.shape, f32)
    for b in range(N_BUCKETS):
        acc = jnp.where(bucket == b, rel_ref[b, h], acc)
    return jnp.where((dist >= 0) & (dist < WINDOW), acc, -jnp.inf)


def _bias_table_body(rel_ref, qk_ref, kq_ref):
    shape_qk = (WINDOW, 2 * WINDOW)
    dist_qk = (lax.broadcasted_iota(jnp.int32, shape_qk, 0) + WINDOW
               - lax.broadcasted_iota(jnp.int32, shape_qk, 1))
    shape_kq = (2 * WINDOW, WINDOW)
    dist_kq = (lax.broadcasted_iota(jnp.int32, shape_kq, 1) + WINDOW
               - lax.broadcasted_iota(jnp.int32, shape_kq, 0))
    for h in range(N_Q_HEADS):
        kv, g = divmod(h, Q_PER_KV)
        qk_ref[h] = _rel_bias_of(dist_qk, rel_ref, h)
        kq_ref[kv, :, g * WINDOW:(g + 1) * WINDOW] = _rel_bias_of(dist_kq, rel_ref, h) * INV_ATTN_SCALE


def _bias_tables(rel_bias):
    return pl.pallas_call(
        _bias_table_body,
        in_specs=[pl.BlockSpec(memory_space=pltpu.SMEM)],
        out_shape=[jax.ShapeDtypeStruct((N_Q_HEADS, WINDOW, 2 * WINDOW), f32),
                   jax.ShapeDtypeStruct((N_KV_HEADS, 2 * WINDOW, Q_PER_KV * WINDOW), f32)],
        name="bias_tables",
    )(rel_bias)


def _attn_block_t(q_rows, kpad_ref, vt_ref, r0, bias_ref, sinks_ref, l, gt_ref, lead_keys):
    raws, invs = [], []
    for kv in range(N_KV_HEADS):
        heads = range(kv * Q_PER_KV, (kv + 1) * Q_PER_KV)
        kwin = kpad_ref[pl.ds(r0, 2 * WINDOW), kv * HEAD_DIM:(kv + 1) * HEAD_DIM]
        q = jnp.concatenate([q_rows[:, h * HEAD_DIM:(h + 1) * HEAD_DIM] for h in heads], axis=0).astype(bf16)
        t = lax.dot_general(kwin, q, (((1,), (1,)), ((), ())), preferred_element_type=f32) + bias_ref[kv]
        if lead_keys:
            key = lax.broadcasted_iota(jnp.int32, t.shape, 0)
            t = jnp.where(key < lead_keys, -jnp.inf, t)
        sink = jnp.concatenate([jnp.full((1, WINDOW), sinks_ref[l, h] * INV_ATTN_SCALE, f32) for h in heads],
                               axis=1)
        m = jnp.maximum(jnp.max(t, axis=0, keepdims=True), sink)
        p = jnp.exp2((t - m) * EXP2_PER_T)
        denom = jnp.sum(p, axis=0, keepdims=True) + jnp.exp2((sink - m) * EXP2_PER_T)
        vt = vt_ref[kv * HEAD_DIM:(kv + 1) * HEAD_DIM, pl.ds(r0, 2 * WINDOW)]
        raws.append(jnp.dot(vt, p.astype(bf16), preferred_element_type=f32))
        invs.append(1.0 / denom)
    raw = jnp.concatenate(raws, axis=1)
    inv = jnp.concatenate(invs, axis=1)
    sq = jnp.sum(raw * raw, axis=0, keepdims=True) * (inv * inv)
    ssq = sq[:, 0:WINDOW]
    for h in range(1, N_Q_HEADS):
        ssq = ssq + sq[:, h * WINDOW:(h + 1) * WINDOW]
    r = lax.rsqrt(ssq * (1.0 / ATTN_W) + EPS)
    yt = raw * (inv * jnp.concatenate([r] * N_Q_HEADS, axis=1)) * gt_ref[...]
    return jnp.concatenate([yt[:, h * WINDOW:(h + 1) * WINDOW].T for h in range(N_Q_HEADS)], axis=1)


def _attn_prompt_body(q_ref, k_ref, v_ref, bias_ref, sinks_ref, gt_ref, o_ref, kpad_ref, vpad_ref, vt_ref,
                      *, t_len, l):
    pad_rows = kpad_ref.shape[0]
    kpad_ref[0:ATTN_LEAD, :] = jnp.zeros((ATTN_LEAD, KV_W), bf16)
    kpad_ref[ATTN_LEAD:pad_rows, :] = k_ref[...].astype(bf16)
    vpad_ref[0:ATTN_LEAD, :] = jnp.zeros((ATTN_LEAD, KV_W), f32)
    vpad_ref[ATTN_LEAD:pad_rows, :] = v_ref[...]

    def transpose_step(c, _):
        r = pl.multiple_of(c * WINDOW, WINDOW)
        chunk = vpad_ref[pl.ds(r, WINDOW), :]
        for kv in range(N_KV_HEADS):
            vt_ref[kv * HEAD_DIM:(kv + 1) * HEAD_DIM, pl.ds(r, WINDOW)] = (
                chunk[:, kv * HEAD_DIM:(kv + 1) * HEAD_DIM].T.astype(bf16))
        return 0
    lax.fori_loop(0, pad_rows // WINDOW, transpose_step, 0)

    n_first = 2 * WINDOW - ATTN_LEAD
    q0 = jnp.concatenate([jnp.zeros((WINDOW - n_first, ATTN_W), f32), q_ref[0:n_first, :]], axis=0)
    y0 = _attn_block_t(q0, kpad_ref, vt_ref, 0, bias_ref, sinks_ref, l, gt_ref, ATTN_LEAD)
    o_ref[0:n_first, :] = y0[WINDOW - n_first:, :].astype(o_ref.dtype)

    def block(j, lead_keys):
        r0 = j * WINDOW
        q0_row = j * WINDOW - (ATTN_LEAD - WINDOW)
        if not isinstance(j, int):
            r0, q0_row = pl.multiple_of(r0, WINDOW), pl.multiple_of(q0_row, BF16_ROWS)
        rows = pl.ds(q0_row, WINDOW)
        y = _attn_block_t(q_ref[rows, :], kpad_ref, vt_ref, r0, bias_ref, sinks_ref, l, gt_ref, lead_keys)
        o_ref[rows, :] = y.astype(o_ref.dtype)

    block(1, ATTN_LEAD - WINDOW)

    def step(j, _):
        block(j, 0)
        return 0
    lax.fori_loop(2, pad_rows // WINDOW - 1, step, 0, unroll=3)


def _attn_prompt(z, bias_kq, sinks, gt, l, *, n_batch, t_len):
    pad_rows = ATTN_LEAD + t_len
    assert pad_rows % WINDOW == 0 and (ATTN_LEAD - WINDOW) % BF16_ROWS == 0
    blocks = t_len * (ATTN_W + 2 * KV_W) * 4 + t_len * ATTN_W * 2 + bias_kq.size * 4 + HEAD_DIM * ATTN_W * 4
    scratch = pad_rows * KV_W * (2 + 4 + 2)
    return pl.pallas_call(
        functools.partial(_attn_prompt_body, t_len=t_len, l=l),
        grid=(n_batch,),
        in_specs=[pl.BlockSpec((t_len, ATTN_W), lambda b: (b, Z_Q // ATTN_W)),
                  pl.BlockSpec((t_len, KV_W), lambda b: (b, Z_K // KV_W)),
                  pl.BlockSpec((t_len, KV_W), lambda b: (b, Z_V // KV_W)),
                  pl.BlockSpec(bias_kq.shape, lambda b: (0, 0, 0)),
                  pl.BlockSpec(memory_space=pltpu.SMEM),
                  pl.BlockSpec((None, HEAD_DIM, ATTN_W), lambda b: (l, 0, 0))],
        out_specs=pl.BlockSpec((t_len, ATTN_W), lambda b: (b, 0)),
        out_shape=jax.ShapeDtypeStruct((n_batch * t_len, ATTN_W), bf16),
        scratch_shapes=[pltpu.VMEM((pad_rows, KV_W), bf16), pltpu.VMEM((pad_rows, KV_W), f32),
                        pltpu.VMEM((KV_W, pad_rows), bf16)],
        compiler_params=_params(("parallel",), _vmem_limit(blocks, scratch)),
        name="attn_prompt",
    )(z, z, z, bias_kq, sinks, gt)


def _attn_block(q_rows, kwin, vwin, bias_ref, sinks_ref, n_rows):
    outs = []
    for kv in range(N_KV_HEADS):
        heads = range(kv * Q_PER_KV, (kv + 1) * Q_PER_KV)
        k = kwin[:, kv * HEAD_DIM:(kv + 1) * HEAD_DIM]
        v = vwin[:, kv * HEAD_DIM:(kv + 1) * HEAD_DIM]
        q = jnp.concatenate([q_rows[:, h * HEAD_DIM:(h + 1) * HEAD_DIM] for h in heads], axis=0).astype(bf16)
        bias = jnp.concatenate([bias_ref[h * WINDOW:h * WINDOW + n_rows, :] for h in heads], axis=0)
        s = lax.dot_general(q, k, (((1,), (1,)), ((), ())), preferred_element_type=f32) * ATTN_SCALE + bias
        sink = jnp.concatenate([jnp.full((n_rows, 1), sinks_ref[h], f32) for h in heads], axis=0)
        m = jnp.maximum(jnp.max(s, axis=-1, keepdims=True), sink)
        p = jnp.exp(s - m)
        denom = jnp.sum(p, axis=-1, keepdims=True) + jnp.exp(sink - m)
        o = jnp.dot(p.astype(bf16), v, preferred_element_type=f32) / denom
        outs.extend(o[g * n_rows:(g + 1) * n_rows] for g in range(Q_PER_KV))
    return jnp.concatenate(outs, axis=1)


def _attn_sample_body(z_ref, ck_ref, cv_ref, bias_ref, sinks_ref, g_ref, o_ref, nk_ref, nv_ref, acc_ref,
                      *, n_batch, t_len):
    buf = ck_ref.shape[1]
    zeros = jnp.zeros((2 * WINDOW - buf - t_len, KV_W), f32)

    def step(b, _):
        r0 = pl.multiple_of(b * t_len, t_len)
        q_rows = z_ref[pl.ds(r0, t_len), Z_Q:Z_Q + ATTN_W]
        k_new = z_ref[pl.ds(r0, t_len), Z_K:Z_K + KV_W]
        v_new = z_ref[pl.ds(r0, t_len), Z_V:Z_V + KV_W]
        kwin = jnp.concatenate([ck_ref[b], k_new, zeros], axis=0).astype(bf16)
        vwin = jnp.concatenate([cv_ref[b], v_new, zeros], axis=0).astype(bf16)
        acc_ref[pl.ds(r0, t_len), :] = _attn_block(q_rows, kwin, vwin, bias_ref, sinks_ref, t_len)
        nk_ref[b, 0:buf - t_len, :] = ck_ref[b, t_len:buf, :]
        nk_ref[b, buf - t_len:buf, :] = k_new
        nv_ref[b, 0:buf - t_len, :] = cv_ref[b, t_len:buf, :]
        nv_ref[b, buf - t_len:buf, :] = v_new
        return 0
    lax.fori_loop(0, n_batch, step, 0, unroll=4)
    o_ref[...] = _rmsnorm(acc_ref[...], g_ref[...]).astype(o_ref.dtype)


def _attn_sample(z, ck, cv, bias_qk, sinks, g, *, n_batch, t_len):
    assert ck.shape[1] == WINDOW and t_len == SUBLANES
    rows = n_batch * t_len
    total = (z.size + 2 * ck.size + 2 * cv.size + bias_qk.size + rows * ATTN_W) * 4 + rows * ATTN_W * 2
    vmem = pl.BlockSpec(memory_space=pltpu.VMEM)
    return pl.pallas_call(
        functools.partial(_attn_sample_body, n_batch=n_batch, t_len=t_len),
        in_specs=[vmem, vmem, vmem, vmem, pl.BlockSpec(memory_space=pltpu.SMEM), vmem],
        out_specs=[vmem, vmem, vmem],
        out_shape=[jax.ShapeDtypeStruct((rows, ATTN_W), bf16),
                   jax.ShapeDtypeStruct(ck.shape, f32),
                   jax.ShapeDtypeStruct(cv.shape, f32)],
        scratch_shapes=[pltpu.VMEM((rows, ATTN_W), f32)],
        compiler_params=pltpu.CompilerParams(vmem_limit_bytes=_vmem_limit(0, total)),
        name="attn_sample",
    )(z, ck, cv, bias_qk, sinks, g)


def _lru_gates(xc, n, c0, wg_ref, ba_ref, bx_ref, lam_ref):
    lanes = pl.ds(c0, LRU_BLOCK)
    gates = jnp.dot(xc.astype(bf16), wg_ref[n], preferred_element_type=f32)
    gate_a = jax.nn.sigmoid(gates[:, :LRU_BLOCK] + ba_ref[:, lanes])
    gate_x = jax.nn.sigmoid(gates[:, LRU_BLOCK:] + bx_ref[:, lanes])
    log_a = -LRU_C * gate_a * jax.nn.softplus(-lam_ref[:, lanes])
    a = jnp.exp(log_a)
    y = -jnp.tanh(log_a) * (1.0 + a * a)
    mult = jnp.where(y > 0.0, y * lax.rsqrt(y), 0.0)
    return a, xc * gate_x, mult


def _tile_prefix(a, b, row):
    for d in (1, 2, 4):
        a_prev = pltpu.roll(a, d, axis=0)
        b_prev = pltpu.roll(b, d, axis=0)
        keep = row >= d
        b = jnp.where(keep, a * b_prev + b, b)
        a = jnp.where(keep, a * a_prev, a)
    return a, b


def _scan_tile(a, b, h_in, row):
    a, b = _tile_prefix(a, b, row)
    return a * h_in + b


def _last_row(h):
    return jnp.broadcast_to(h[SUBLANES - 1:SUBLANES, :], h.shape)


def _conv_taps(x, shifted, cw, cb):
    out = cb + shifted[CONV_W - 1] * cw[0:1]
    for j in range(1, CONV_W - 1):
        out = out + shifted[CONV_W - 1 - j] * cw[j:j + 1]
    return out + x * cw[CONV_W - 1:CONV_W]


def _rec_prompt_body(u_ref, wxl_ref, wxh_ref, wgl_ref, wgh_ref, prev_ref, h0_ref, cw_ref, cb_ref, wg_ref, ba_ref,
                     bx_ref, lam_ref, g_ref, o_ref, lru_ref, tail_ref, wb_ref, xprev_ref, a_ref, b_ref, gate_ref,
                     carry_ref, *, tc, n_chunks):
    c = pl.program_id(1)

    @pl.when((pl.program_id(0) == 0) & (c == 0))
    def _():
        for i, w_ref in enumerate((wxl_ref, wxh_ref, wgl_ref, wgh_ref)):
            wb_ref[:, i * LRU_HALF:(i + 1) * LRU_HALF] = w_ref[...].astype(bf16)

    @pl.when(c == 0)
    def _():
        xprev_ref[...] = prev_ref[...]
        for n in range(N_LRU_BLOCKS):
            carry_ref[n] = jnp.broadcast_to(h0_ref[:, n * LRU_BLOCK:(n + 1) * LRU_BLOCK], (SUBLANES, LRU_BLOCK))

    u = u_ref[...]
    pair = 2 * LRU_BLOCK
    n_pairs = LRU_W // pair

    def project(branch, p):
        c0 = branch * LRU_W + p * pair
        return jnp.dot(u, wb_ref[:, c0:c0 + pair], preferred_element_type=f32)

    def gates(p, x):
        lanes = slice(p * pair, (p + 1) * pair)
        big = jnp.concatenate([xprev_ref[:, lanes], x], axis=0)
        shifted = {k: pltpu.roll(big, k, axis=0)[SUBLANES:] for k in range(1, CONV_W)}
        xc = _conv_taps(x, shifted, cw_ref[:, lanes], cb_ref[:, lanes])
        for q in range(2):
            n = 2 * p + q
            sub = slice(q * LRU_BLOCK, (q + 1) * LRU_BLOCK)
            a, gated, mult = _lru_gates(xc[:, sub], n, n * LRU_BLOCK, wg_ref, ba_ref, bx_ref, lam_ref)
            b = gated * mult
            a_ref[n] = a
            b_ref[n] = b
            b_ref[n, 0:1, :] = jnp.where(c == 0, gated[0:1, :], b[0:1, :])
        return x[tc - SUBLANES:, :]

    def gelu_gate(p, gate):
        for q in range(2):
            gate_ref[2 * p + q] = jax.nn.gelu(gate[:, q * LRU_BLOCK:(q + 1) * LRU_BLOCK])

    steps = [(gates, 0, p) for p in range(n_pairs)] + [(gelu_gate, 1, p) for p in range(n_pairs)]
    tails = []
    ahead = n_pairs
    pending = [project(halves, p) for _, halves, p in steps[:ahead]]
    for i, (consume, _, p) in enumerate(steps):
        if i + ahead < len(steps):
            pending.append(project(steps[i + ahead][1], steps[i + ahead][2]))
        out = consume(p, pending.pop(0))
        if consume is gates:
            tails.append(out)
    tail = jnp.concatenate(tails, axis=1)
    xprev_ref[...] = tail
    tail_ref[...] = tail

    seg = tc // SUBLANES
    seg_rows = lambda r: pl.ds(r, SUBLANES, stride=seg)

    def local_scan(r, maps):
        out = []
        for n in range(N_LRU_BLOCKS):
            a = a_ref[n, seg_rows(r), :]
            a_cum = a * maps[2 * n]
            b_cum = a * maps[2 * n + 1] + b_ref[n, seg_rows(r), :]
            a_ref[n, seg_rows(r), :] = a_cum
            b_ref[n, seg_rows(r), :] = b_cum
            out += [a_cum, b_cum]
        return tuple(out)
    identity = (jnp.ones((SUBLANES, LRU_BLOCK), f32), jnp.zeros((SUBLANES, LRU_BLOCK), f32)) * N_LRU_BLOCKS
    seg_maps = lax.fori_loop(0, seg, local_scan, identity, unroll=2)

    row = lax.broadcasted_iota(jnp.int32, (SUBLANES, LRU_BLOCK), 0)
    h_in = []
    for n in range(N_LRU_BLOCKS):
        h_prev = carry_ref[n]
        a_cum, b_cum = _tile_prefix(seg_maps[2 * n], seg_maps[2 * n + 1], row)
        h_end = a_cum * h_prev + b_cum
        h_in.append(jnp.where(row == 0, h_prev, pltpu.roll(h_end, 1, axis=0)))
        carry_ref[n] = _last_row(h_end)

    def apply_scan(r, _):
        for n in range(N_LRU_BLOCKS):
            h = a_ref[n, seg_rows(r), :] * h_in[n] + b_ref[n, seg_rows(r), :]
            b_ref[n, seg_rows(r), :] = h * gate_ref[n, seg_rows(r), :]
        return 0
    lax.fori_loop(0, seg, apply_scan, 0, unroll=2)

    y = jnp.concatenate([b_ref[n] for n in range(N_LRU_BLOCKS)], axis=1)
    o_ref[...] = _rmsnorm(y, g_ref[...]).astype(o_ref.dtype)

    @pl.when(c == n_chunks - 1)
    def _():
        lru_ref[...] = jnp.concatenate([carry_ref[n] for n in range(N_LRU_BLOCKS)], axis=1)


def _rec_prompt(u, w_in, prev8, h0, cw, cb, wg, ba, bx, lam, g, l, *, n_batch, t_len, tc):
    n_chunks = t_len // tc
    once = pl.Buffered(1)
    row_vec = pl.BlockSpec((None, 1, LRU_W), lambda b, c: (l, 0, 0), pipeline_mode=once)
    w_half = lambda col: pl.BlockSpec((None, D_MODEL, LRU_HALF), lambda b, c: (l, 0, col // LRU_HALF),
                                      pipeline_mode=once)
    state = pl.BlockSpec((None, SUBLANES, LRU_W), lambda b, c: (b, 0, 0))
    blocks = tc * D_MODEL * 2 + tc * LRU_W * 2 + 3 * SUBLANES * LRU_W * 4
    scratch = (4 * D_MODEL * LRU_HALF * (4 + 2) + wg[0].size * 2 + (3 * tc + 2 * SUBLANES) * LRU_W * 4
               + 8 * tc * 2 * LRU_BLOCK * 4)
    by_block = pltpu.VMEM((N_LRU_BLOCKS, tc, LRU_BLOCK), f32)
    assert tc % SUBLANES == 0
    return pl.pallas_call(
        functools.partial(_rec_prompt_body, tc=tc, n_chunks=n_chunks),
        grid=(n_batch, n_chunks),
        in_specs=[pl.BlockSpec((tc, D_MODEL), lambda b, c: (b * n_chunks + c, 0)),
                  w_half(Z_XR), w_half(Z_XR + LRU_HALF), w_half(Z_GR), w_half(Z_GR + LRU_HALF),
                  state,
                  pl.BlockSpec((None, 1, LRU_W), lambda b, c: (b, 0, 0)),
                  pl.BlockSpec((None, CONV_W, LRU_W), lambda b, c: (l, 0, 0), pipeline_mode=once),
                  row_vec,
                  pl.BlockSpec((None,) + wg.shape[1:], lambda b, c: (l, 0, 0, 0), pipeline_mode=once),
                  row_vec, row_vec, row_vec, row_vec],
        out_specs=[pl.BlockSpec((tc, LRU_W), lambda b, c: (b * n_chunks + c, 0)), state, state],
        out_shape=[jax.ShapeDtypeStruct((n_batch * t_len, LRU_W), bf16),
                   jax.ShapeDtypeStruct((n_batch, SUBLANES, LRU_W), f32),
                   jax.ShapeDtypeStruct((n_batch, SUBLANES, LRU_W), f32)],
        scratch_shapes=[pltpu.VMEM((D_MODEL, 2 * LRU_W), bf16),
                        pltpu.VMEM((SUBLANES, LRU_W), f32), by_block, by_block, by_block,
                        pltpu.VMEM((N_LRU_BLOCKS, SUBLANES, LRU_BLOCK), f32)],
        compiler_params=_params(("arbitrary", "arbitrary"), _vmem_limit(blocks, scratch)),
        name="rec_prompt",
    )(u, w_in, w_in, w_in, w_in, prev8, h0, cw, cb, wg, ba, bx, lam, g)


def _rec_sample_body(z_ref, prev_ref, h0_ref, cw_ref, cb_ref, wg_ref, ba_ref, bx_ref, lam_ref, g_ref,
                     o_ref, lru_ref, xc_ref, a_ref, b_ref, *, n_batch):
    row = lax.broadcasted_iota(jnp.int32, (SUBLANES, LRU_W), 0)

    def conv_step(b, _):
        r = pl.multiple_of(b * SUBLANES, SUBLANES)
        x = z_ref[pl.ds(r, SUBLANES), Z_XR:Z_XR + LRU_W]
        prev = prev_ref[pl.ds(r, SUBLANES), :]
        shifted = {k: jnp.where(row >= k, pltpu.roll(x, k, axis=0), pltpu.roll(prev, k, axis=0))
                   for k in range(1, CONV_W)}
        xc_ref[pl.ds(r, SUBLANES), :] = _conv_taps(x, shifted, cw_ref[...], cb_ref[...])
        return 0
    lax.fori_loop(0, n_batch, conv_step, 0)

    def gates_step(n, _):
        c0 = pl.multiple_of(n * LRU_BLOCK, LRU_BLOCK)
        lanes = pl.ds(c0, LRU_BLOCK)
        a, gated, mult = _lru_gates(xc_ref[:, lanes], n, c0, wg_ref, ba_ref, bx_ref, lam_ref)
        a_ref[:, lanes] = a
        b_ref[:, lanes] = gated * mult
        return 0
    lax.fori_loop(0, N_LRU_BLOCKS, gates_step, 0)

    def scan_step(b, _):
        rows = pl.ds(pl.multiple_of(b * SUBLANES, SUBLANES), SUBLANES)
        h_in = jnp.broadcast_to(h0_ref[pl.ds(b, 1), :], (SUBLANES, LRU_W))
        hh = _scan_tile(a_ref[rows, :], b_ref[rows, :], h_in, row)
        b_ref[rows, :] = hh * jax.nn.gelu(z_ref[rows, Z_GR:Z_GR + LRU_W])
        lru_ref[pl.ds(b, 1), :] = hh[SUBLANES - 1:SUBLANES, :]
        return 0
    lax.fori_loop(0, n_batch, scan_step, 0)
    o_ref[...] = _rmsnorm(b_ref[...], g_ref[...]).astype(o_ref.dtype)


def _rec_sample(z, prev8, h0, cw, cb, wg, ba, bx, lam, g, *, n_batch, t_len):
    assert t_len == SUBLANES and PAST_LEN > 0
    rows = n_batch * t_len
    total = z.size * 4 + (5 * rows + 2 * n_batch) * LRU_W * 4 + wg.size * 2
    vmem = pl.BlockSpec(memory_space=pltpu.VMEM)
    return pl.pallas_call(
        functools.partial(_rec_sample_body, n_batch=n_batch),
        in_specs=[vmem] * 10,
        out_specs=[vmem, vmem],
        out_shape=[jax.ShapeDtypeStruct((rows, LRU_W), bf16),
                   jax.ShapeDtypeStruct((n_batch, LRU_W), f32)],
        scratch_shapes=[pltpu.VMEM((rows, LRU_W), f32)] * 3,
        compiler_params=pltpu.CompilerParams(vmem_limit_bytes=_vmem_limit(0, total)),
        name="rec_sample",
    )(z, prev8, h0, cw, cb, wg, ba, bx, lam, g)


def _mix_mlp_body(h_ref, attn_ref, rec_ref, wo_ref, g_ref, wu_ref, wd_ref, o_ref, u_ref):
    @pl.when(pl.program_id(1) == 0)
    def _():
        o_ref[...] = h_ref[...] + jnp.dot(attn_ref[...], wo_ref[0:ATTN_W, :], preferred_element_type=f32)
        o_ref[...] += jnp.dot(rec_ref[...], wo_ref[ATTN_W:ATTN_W + LRU_W, :], preferred_element_type=f32)
        u_ref[...] = _rmsnorm(o_ref[...], g_ref[...]).astype(bf16)
    hid = jnp.dot(u_ref[...], wu_ref[...], preferred_element_type=f32)
    hid = jnp.square(jnp.maximum(hid, 0.0)).astype(bf16)
    o_ref[...] += jnp.dot(hid, wd_ref[...], preferred_element_type=f32)


def _mix_mlp(h, attn, rec, wo, g, wu, wd, l, *, tm, tf):
    m = h.shape[0]
    once = pl.Buffered(1)
    blocks = 2 * tm * D_MODEL * 4 + tm * (ATTN_W + LRU_W) * 2 + 2 * D_MODEL * tf * 2
    scratch = wo[0].size * 2 + tm * D_MODEL * 2 + tm * tf * 6
    return pl.pallas_call(
        _mix_mlp_body,
        grid=(m // tm, D_FF // tf),
        in_specs=[pl.BlockSpec((tm, D_MODEL), lambda i, f: (i, 0)),
                  pl.BlockSpec((tm, ATTN_W), lambda i, f: (i, 0)),
                  pl.BlockSpec((tm, LRU_W), lambda i, f: (i, 0)),
                  pl.BlockSpec((None,) + wo.shape[1:], lambda i, f: (l, 0, 0), pipeline_mode=once),
                  pl.BlockSpec((None, 1, D_MODEL), lambda i, f: (l, 0, 0), pipeline_mode=once),
                  pl.BlockSpec((None, D_MODEL, tf), lambda i, f: (l, 0, f)),
                  pl.BlockSpec((None, tf, D_MODEL), lambda i, f: (l, f, 0))],
        out_specs=pl.BlockSpec((tm, D_MODEL), lambda i, f: (i, 0)),
        out_shape=jax.ShapeDtypeStruct((m, D_MODEL), f32),
        scratch_shapes=[pltpu.VMEM((tm, D_MODEL), bf16)],
        compiler_params=_params(("parallel", "arbitrary"), _vmem_limit(blocks, scratch)),
        name="mix_mlp",
    )(h, attn, rec, wo, g, wu, wd)


def _final_norm_body(h_ref, g_ref, o_ref):
    o_ref[0] = _rmsnorm(h_ref[0], g_ref[...])


def _final_norm(h, g, *, skip, tr):
    n_batch, t_len, _ = h.shape
    s_len = t_len - skip
    tiles = tr // SUBLANES
    skip_tiles = skip // SUBLANES
    h4 = h.reshape(n_batch, t_len // SUBLANES, SUBLANES, D_MODEL)
    out = pl.pallas_call(
        _final_norm_body,
        grid=(n_batch, s_len // tr),
        in_specs=[pl.BlockSpec((pl.Element(1), pl.Element(tiles), pl.Element(SUBLANES), pl.Element(D_MODEL)),
                               lambda b, r: (b, skip_tiles + r * tiles, 0, 0)),
                  pl.BlockSpec((1, D_MODEL), lambda b, r: (0, 0))],
        out_specs=pl.BlockSpec((1, tiles, SUBLANES, D_MODEL), lambda b, r: (b, r, 0, 0)),
        out_shape=jax.ShapeDtypeStruct((n_batch, s_len // SUBLANES, SUBLANES, D_MODEL), f32),
        compiler_params=_params(("parallel", "parallel"), _vmem_limit(2 * tr * D_MODEL * 4)),
        name="final_norm",
    )(h4, g)
    return out.reshape(n_batch, s_len, D_MODEL)


def _row_tile(m, candidates):
    for tm in candidates:
        if m % tm == 0:
            return tm
    raise ValueError(f"no row tile for {m} rows")


def kernel(x_prompt, x_sample, cache_k_win, cache_v_win, state_conv, state_lru, meta_tokens, norm_mix_g, w_in,
           conv_w, conv_b, w_gate_a, b_gate_a, w_gate_x, b_gate_x, lru_lambda, attn_sinks, rel_bias, attn_out_g,
           rec_out_g, w_out, norm_mlp_g, w_up, w_down, final_norm_g):
    n_p, s_p, _ = x_prompt.shape
    n_s, t_s, _ = x_sample.shape
    t_p = N_META + s_p
    buf = cache_k_win.shape[2]
    assert t_p % BF16_ROWS == 0 and buf == WINDOW

    w_out_b, w_up_b, w_down_b = w_out.astype(bf16), w_up.astype(bf16), w_down.astype(bf16)
    w_gates = jnp.concatenate([w_gate_a, w_gate_x], axis=-1).astype(bf16)
    rows3 = lambda p: p[:, None, :]
    g_mix, g_mlp, g_attn, g_rec = rows3(norm_mix_g), rows3(norm_mlp_g), rows3(attn_out_g), rows3(rec_out_g)
    cb3, ba3, bx3, lam3 = rows3(conv_b), rows3(b_gate_a), rows3(b_gate_x), rows3(lru_lambda)
    g_attn_t = jnp.broadcast_to(
        attn_out_g.reshape(DEPTH, N_Q_HEADS, HEAD_DIM).transpose(0, 2, 1)[..., None],
        (DEPTH, HEAD_DIM, N_Q_HEADS, WINDOW)).reshape(DEPTH, HEAD_DIM, ATTN_W)

    meta = jnp.broadcast_to(meta_tokens.astype(x_prompt.dtype)[None], (n_p, N_META, D_MODEL))
    hp = jnp.concatenate([meta, x_prompt], axis=1).reshape(n_p * t_p, D_MODEL)
    hs = x_sample.reshape(n_s * t_s, D_MODEL)

    bias_qk, bias_kq = _bias_tables(rel_bias)
    bias_qk = bias_qk.reshape(N_Q_HEADS * WINDOW, 2 * WINDOW)
    prev_p = jnp.zeros((n_p, SUBLANES, LRU_W), f32)
    h0_p = jnp.zeros((n_p, 1, LRU_W), f32)
    ck = cache_k_win.reshape(DEPTH, n_s, buf, KV_W)
    cv = cache_v_win.reshape(DEPTH, n_s, buf, KV_W)
    prev_s = jnp.pad(state_conv, ((0, 0), (0, 0), (SUBLANES - (CONV_W - 1), 0), (0, 0)))
    prev_s = prev_s.reshape(DEPTH, n_s * SUBLANES, LRU_W)

    tm_p = _row_tile(n_p * t_p, (688, 344))
    tm_s = n_s * t_s
    tc_p = _row_tile(t_p, (688, 344, 48, 16))
    tn = IN_W // 2

    kp_l, vp_l, cp_l, lp_l, ks_l, vs_l, cs_l, ls_l = ([] for _ in range(8))
    for l in range(DEPTH):
        z, u = _inproj_qkv(hp, g_mix, w_in, l, tm=tm_p)
        attn = _attn_prompt(z, bias_kq, attn_sinks, g_attn_t, l, n_batch=n_p, t_len=t_p)
        rec, lru, x_tail = _rec_prompt(u, w_in, prev_p, h0_p, conv_w, cb3, w_gates, ba3, bx3, lam3, g_rec, l,
                                       n_batch=n_p, t_len=t_p, tc=tc_p)
        hp = _mix_mlp(hp, attn, rec, w_out_b, g_mlp, w_up_b, w_down_b, l, tm=tm_p, tf=1024)
        z3 = z.reshape(n_p, t_p, Z_XR)
        kp_l.append(z3[:, t_p - WINDOW:, Z_K:Z_K + KV_W].reshape(n_p, WINDOW, N_KV_HEADS, HEAD_DIM))
        vp_l.append(z3[:, t_p - WINDOW:, Z_V:Z_V + KV_W].reshape(n_p, WINDOW, N_KV_HEADS, HEAD_DIM))
        cp_l.append(x_tail[:, SUBLANES - (CONV_W - 1):])
        lp_l.append(lru[:, 0])
        z = _inproj(hs, g_mix, w_in, l, tm=tm_s, tn=tn)
        attn, nk, nv = _attn_sample(z, ck[l], cv[l], bias_qk, attn_sinks[l], g_attn[l], n_batch=n_s, t_len=t_s)
        rec, lru = _rec_sample(z, prev_s[l], state_lru[l], conv_w[l], cb3[l], w_gates[l], ba3[l], bx3[l],
                               lam3[l], g_rec[l], n_batch=n_s, t_len=t_s)
        hs = _mix_mlp(hs, attn, rec, w_out_b, g_mlp, w_up_b, w_down_b, l, tm=tm_s, tf=1024)
        z3 = z.reshape(n_s, t_s, IN_W)
        ks_l.append(nk.reshape(n_s, buf, N_KV_HEADS, HEAD_DIM))
        vs_l.append(nv.reshape(n_s, buf, N_KV_HEADS, HEAD_DIM))
        cs_l.append(z3[:, t_s - (CONV_W - 1):, Z_XR:Z_XR + LRU_W])
        ls_l.append(lru)

    g_fin = final_norm_g[None, :]
    y_prompt = _final_norm(hp.reshape(n_p, t_p, D_MODEL), g_fin, skip=N_META, tr=256)
    y_sample = _final_norm(hs.reshape(1, n_s * t_s, D_MODEL), g_fin, skip=0, tr=n_s * t_s)
    y_sample = y_sample.reshape(n_s, t_s, D_MODEL)
    return (y_prompt, y_sample,
            jnp.stack(kp_l), jnp.stack(vp_l), jnp.stack(cp_l), jnp.stack(lp_l),
            jnp.stack(ks_l), jnp.stack(vs_l), jnp.stack(cs_l), jnp.stack(ls_l))
```

```python
import functools
import math

import jax
import jax.numpy as jnp
from jax import lax
from jax.experimental import pallas as pl
from jax.experimental.pallas import tpu as pltpu

f32 = jnp.float32
bf16 = jnp.bfloat16

D_MODEL = 2048
DEPTH = 4
PAST_LEN = 16384
HEAD_DIM = 128
N_Q_HEADS = 8
N_KV_HEADS = 2
Q_PER_KV = N_Q_HEADS // N_KV_HEADS
ATTN_W = N_Q_HEADS * HEAD_DIM
KV_W = N_KV_HEADS * HEAD_DIM
LRU_W = D_MODEL // 2
N_LRU_BLOCKS = 8
LRU_BLOCK = LRU_W // N_LRU_BLOCKS
CONV_W = 4
LRU_C = 8.0
IN_W = ATTN_W + 2 * KV_W + 2 * LRU_W
D_FF = 4 * D_MODEL
WINDOW = 128
N_BUCKETS = 32
MAX_DISTANCE = 128
N_META = 16
EPS = 1e-6
ATTN_SCALE = HEAD_DIM ** -0.5
INV_ATTN_SCALE = HEAD_DIM ** 0.5
EXP2_PER_T = math.log2(math.e) / INV_ATTN_SCALE

Z_Q = 0
Z_K = ATTN_W
Z_V = Z_K + KV_W
Z_XR = Z_V + KV_W
Z_GR = Z_XR + LRU_W
LRU_HALF = LRU_W // 2

ATTN_LEAD = WINDOW + (-N_META) % WINDOW

SUBLANES = 8
BF16_ROWS = 16
V7X_VMEM_BYTES = 64 * 1024 * 1024
VMEM_CAP_BYTES = V7X_VMEM_BYTES - 2 * 1024 * 1024


def _vmem_limit(pipelined_bytes, scratch_bytes=0):
    est = 2 * pipelined_bytes + scratch_bytes
    return int(min(VMEM_CAP_BYTES, est + est // 2 + (8 << 20)))


def _params(semantics, vmem_bytes):
    return pltpu.CompilerParams(dimension_semantics=semantics, vmem_limit_bytes=vmem_bytes)


def _rms_scale(x):
    return lax.rsqrt(jnp.mean(x * x, axis=-1, keepdims=True) + EPS)


def _rmsnorm(x, g):
    return x * _rms_scale(x) * g


def _inproj_body(h_ref, g_ref, w_ref, z_ref, u_ref):
    @pl.when(pl.program_id(1) == 0)
    def _():
        u_ref[...] = _rmsnorm(h_ref[...], g_ref[...]).astype(bf16)
    z_ref[...] = jnp.dot(u_ref[...], w_ref[...].astype(bf16), preferred_element_type=f32)


def _inproj(h, g, w, l, *, tm, tn):
    m = h.shape[0]
    blocks = tm * D_MODEL * 4 + D_MODEL * tn * 4 + tm * tn * 4 + D_MODEL * tn
    return pl.pallas_call(
        _inproj_body,
        grid=(m // tm, IN_W // tn),
        in_specs=[pl.BlockSpec((tm, D_MODEL), lambda i, j: (i, 0)),
                  pl.BlockSpec((None, 1, D_MODEL), lambda i, j: (l, 0, 0)),
                  pl.BlockSpec((None, D_MODEL, tn), lambda i, j: (l, 0, j))],
        out_specs=pl.BlockSpec((tm, tn), lambda i, j: (i, j)),
        out_shape=jax.ShapeDtypeStruct((m, IN_W), f32),
        scratch_shapes=[pltpu.VMEM((tm, D_MODEL), bf16)],
        compiler_params=_params(("parallel", "arbitrary"), _vmem_limit(blocks, tm * D_MODEL * 2)),
        name="inproj",
    )(h, g, w)


def _inproj_qkv_body(h_ref, g_ref, w_ref, z_ref, u_ref, wb_ref):
    @pl.when(pl.program_id(0) == 0)
    def _():
        wb_ref[...] = w_ref[...].astype(bf16)
    u_ref[...] = _rmsnorm(h_ref[...], g_ref[...]).astype(bf16)
    z_ref[...] = jnp.dot(u_ref[...], wb_ref[...], preferred_element_type=f32)


def _inproj_qkv(h, g, w, l, *, tm):
    m = h.shape[0]
    once = pl.Buffered(1)
    blocks = tm * D_MODEL * 4 + tm * Z_XR * 4 + tm * D_MODEL * 2
    return pl.pallas_call(
        _inproj_qkv_body,
        grid=(m // tm,),
        in_specs=[pl.BlockSpec((tm, D_MODEL), lambda i: (i, 0)),
                  pl.BlockSpec((None, 1, D_MODEL), lambda i: (l, 0, 0), pipeline_mode=once),
                  pl.BlockSpec((None, D_MODEL, Z_XR), lambda i: (l, 0, 0), pipeline_mode=once)],
        out_specs=[pl.BlockSpec((tm, Z_XR), lambda i: (i, 0)),
                   pl.BlockSpec((tm, D_MODEL), lambda i: (i, 0))],
        out_shape=[jax.ShapeDtypeStruct((m, Z_XR), f32), jax.ShapeDtypeStruct((m, D_MODEL), bf16)],
        scratch_shapes=[pltpu.VMEM((D_MODEL, Z_XR), bf16)],
        compiler_params=_params(("arbitrary",), _vmem_limit(blocks, D_MODEL * Z_XR * 6)),
        name="inproj_qkv",
    )(h, g, w)


def _rel_bias_of(dist, rel_ref, h):
    n = jnp.maximum(dist, 0)
    max_exact = N_BUCKETS // 2
    nf = jnp.maximum(n, 1).astype(f32)
    large = max_exact + (jnp.log(nf / max_exact) / math.log(MAX_DISTANCE / max_exact)
                         * (N_BUCKETS - max_exact)).astype(jnp.int32)
    large = jnp.minimum(large, N_BUCKETS - 1)
    bucket = jnp.where(n < max_exact, n, large)
    acc = jnp.zeros(dist.shape, f32)
    for b in range(N_BUCKETS):
        acc = jnp.where(bucket == b, rel_ref[b, h], acc)
    return jnp.where((dist >= 0) & (dist < WINDOW), acc, -jnp.inf)


def _bias_table_body(rel_ref, qk_ref, kq_ref):
    shape_qk = (WINDOW, 2 * WINDOW)
    dist_qk = (lax.broadcasted_iota(jnp.int32, shape_qk, 0) + WINDOW
               - lax.broadcasted_iota(jnp.int32, shape_qk, 1))
    shape_kq = (2 * WINDOW, WINDOW)
    dist_kq = (lax.broadcasted_iota(jnp.int32, shape_kq, 1) + WINDOW
               - lax.broadcasted_iota(jnp.int32, shape_kq, 0))
    for h in range(N_Q_HEADS):
        kv, g = divmod(h, Q_PER_KV)
        qk_ref[h] = _rel_bias_of(dist_qk, rel_ref, h)
        kq_ref[kv, :, g * WINDOW:(g + 1) * WINDOW] = _rel_bias_of(dist_kq, rel_ref, h) * INV_ATTN_SCALE


def _bias_tables(rel_bias):
    return pl.pallas_call(
        _bias_table_body,
        in_specs=[pl.BlockSpec(memory_space=pltpu.SMEM)],
        out_shape=[jax.ShapeDtypeStruct((N_Q_HEADS, WINDOW, 2 * WINDOW), f32),
                   jax.ShapeDtypeStruct((N_KV_HEADS, 2 * WINDOW, Q_PER_KV * WINDOW), f32)],
        name="bias_tables",
    )(rel_bias)


def _attn_block_t(q_rows, kpad_ref, vt_ref, r0, bias_ref, sinks_ref, l, gt_ref, lead_keys):
    raws, invs = [], []
    for kv in range(N_KV_HEADS):
        heads = range(kv * Q_PER_KV, (kv + 1) * Q_PER_KV)
        kwin = kpad_ref[pl.ds(r0, 2 * WINDOW), kv * HEAD_DIM:(kv + 1) * HEAD_DIM]
        q = jnp.concatenate([q_rows[:, h * HEAD_DIM:(h + 1) * HEAD_DIM] for h in heads], axis=0).astype(bf16)
        t = lax.dot_general(kwin, q, (((1,), (1,)), ((), ())), preferred_element_type=f32) + bias_ref[kv]
        if lead_keys:
            key = lax.broadcasted_iota(jnp.int32, t.shape, 0)
            t = jnp.where(key < lead_keys, -jnp.inf, t)
        sink = jnp.concatenate([jnp.full((1, WINDOW), sinks_ref[l, h] * INV_ATTN_SCALE, f32) for h in heads],
                               axis=1)
        m = jnp.maximum(jnp.max(t, axis=0, keepdims=True), sink)
        p = jnp.exp2((t - m) * EXP2_PER_T)
        denom = jnp.sum(p, axis=0, keepdims=True) + jnp.exp2((sink - m) * EXP2_PER_T)
        vt = vt_ref[kv * HEAD_DIM:(kv + 1) * HEAD_DIM, pl.ds(r0, 2 * WINDOW)]
        raws.append(jnp.dot(vt, p.astype(bf16), preferred_element_type=f32))
        invs.append(1.0 / denom)
    raw = jnp.concatenate(raws, axis=1)
    inv = jnp.concatenate(invs, axis=1)
    sq = jnp.sum(raw * raw, axis=0, keepdims=True) * (inv * inv)
    ssq = sq[:, 0:WINDOW]
    for h in range(1, N_Q_HEADS):
        ssq = ssq + sq[:, h * WINDOW:(h + 1) * WINDOW]
    r = lax.rsqrt(ssq * (1.0 / ATTN_W) + EPS)
    yt = raw * (inv * jnp.concatenate([r] * N_Q_HEADS, axis=1)) * gt_ref[...]
    return jnp.concatenate([yt[:, h * WINDOW:(h + 1) * WINDOW].T for h in range(N_Q_HEADS)], axis=1)


def _attn_prompt_body(q_ref, k_ref, v_ref, bias_ref, sinks_ref, gt_ref, o_ref, kpad_ref, vpad_ref, vt_ref,
                      *, t_len, l):
    pad_rows = kpad_ref.shape[0]
    kpad_ref[0:ATTN_LEAD, :] = jnp.zeros((ATTN_LEAD, KV_W), bf16)
    kpad_ref[ATTN_LEAD:pad_rows, :] = k_ref[...].astype(bf16)
    vpad_ref[0:ATTN_LEAD, :] = jnp.zeros((ATTN_LEAD, KV_W), f32)
    vpad_ref[ATTN_LEAD:pad_rows, :] = v_ref[...]

    def transpose_step(c, _):
        r = pl.multiple_of(c * WINDOW, WINDOW)
        chunk = vpad_ref[pl.ds(r, WINDOW), :]
        for kv in range(N_KV_HEADS):
            vt_ref[kv * HEAD_DIM:(kv + 1) * HEAD_DIM, pl.ds(r, WINDOW)] = (
                chunk[:, kv * HEAD_DIM:(kv + 1) * HEAD_DIM].T.astype(bf16))
        return 0
    lax.fori_loop(0, pad_rows // WINDOW, transpose_step, 0)

    n_first = 2 * WINDOW - ATTN_LEAD
    q0 = jnp.concatenate([jnp.zeros((WINDOW - n_first, ATTN_W), f32), q_ref[0:n_first, :]], axis=0)
    y0 = _attn_block_t(q0, kpad_ref, vt_ref, 0, bias_ref, sinks_ref, l, gt_ref, ATTN_LEAD)
    o_ref[0:n_first, :] = y0[WINDOW - n_first:, :].astype(o_ref.dtype)

    def block(j, lead_keys):
        r0 = j * WINDOW
        q0_row = j * WINDOW - (ATTN_LEAD - WINDOW)
        if not isinstance(j, int):
            r0, q0_row = pl.multiple_of(r0, WINDOW), pl.multiple_of(q0_row, BF16_ROWS)
        rows = pl.ds(q0_row, WINDOW)
        y = _attn_block_t(q_ref[rows, :], kpad_ref, vt_ref, r0, bias_ref, sinks_ref, l, gt_ref, lead_keys)
        o_ref[rows, :] = y.astype(o_ref.dtype)

    block(1, ATTN_LEAD - WINDOW)

    def step(j, _):
        block(j, 0)
        return 0
    lax.fori_loop(2, pad_rows // WINDOW - 1, step, 0, unroll=3)


def _attn_prompt(z, bias_kq, sinks, gt, l, *, n_batch, t_len):
    pad_rows = ATTN_LEAD + t_len
    assert pad_rows % WINDOW == 0 and (ATTN_LEAD - WINDOW) % BF16_ROWS == 0
    blocks = t_len * (ATTN_W + 2 * KV_W) * 4 + t_len * ATTN_W * 2 + bias_kq.size * 4 + HEAD_DIM * ATTN_W * 4
    scratch = pad_rows * KV_W * (2 + 4 + 2)
    return pl.pallas_call(
        functools.partial(_attn_prompt_body, t_len=t_len, l=l),
        grid=(n_batch,),
        in_specs=[pl.BlockSpec((t_len, ATTN_W), lambda b: (b, Z_Q // ATTN_W)),
                  pl.BlockSpec((t_len, KV_W), lambda b: (b, Z_K // KV_W)),
                  pl.BlockSpec((t_len, KV_W), lambda b: (b, Z_V // KV_W)),
                  pl.BlockSpec(bias_kq.shape, lambda b: (0, 0, 0)),
                  pl.BlockSpec(memory_space=pltpu.SMEM),
                  pl.BlockSpec((None, HEAD_DIM, ATTN_W), lambda b: (l, 0, 0))],
        out_specs=pl.BlockSpec((t_len, ATTN_W), lambda b: (b, 0)),
        out_shape=jax.ShapeDtypeStruct((n_batch * t_len, ATTN_W), bf16),
        scratch_shapes=[pltpu.VMEM((pad_rows, KV_W), bf16), pltpu.VMEM((pad_rows, KV_W), f32),
                        pltpu.VMEM((KV_W, pad_rows), bf16)],
        compiler_params=_params(("parallel",), _vmem_limit(blocks, scratch)),
        name="attn_prompt",
    )(z, z, z, bias_kq, sinks, gt)


def _attn_block(q_rows, kwin, vwin, bias_ref, sinks_ref, n_rows):
    outs = []
    for kv in range(N_KV_HEADS):
        heads = range(kv * Q_PER_KV, (kv + 1) * Q_PER_KV)
        k = kwin[:, kv * HEAD_DIM:(kv + 1) * HEAD_DIM]
        v = vwin[:, kv * HEAD_DIM:(kv + 1) * HEAD_DIM]
        q = jnp.concatenate([q_rows[:, h * HEAD_DIM:(h + 1) * HEAD_DIM] for h in heads], axis=0).astype(bf16)
        bias = jnp.concatenate([bias_ref[h * WINDOW:h * WINDOW + n_rows, :] for h in heads], axis=0)
        s = lax.dot_general(q, k, (((1,), (1,)), ((), ())), preferred_element_type=f32) * ATTN_SCALE + bias
        sink = jnp.concatenate([jnp.full((n_rows, 1), sinks_ref[h], f32) for h in heads], axis=0)
        m = jnp.maximum(jnp.max(s, axis=-1, keepdims=True), sink)
        p = jnp.exp(s - m)
        denom = jnp.sum(p, axis=-1, keepdims=True) + jnp.exp(sink - m)
        o = jnp.dot(p.astype(bf16), v, preferred_element_type=f32) / denom
        outs.extend(o[g * n_rows:(g + 1) * n_rows] for g in range(Q_PER_KV))
    return jnp.concatenate(outs, axis=1)


def _attn_sample_body(z_ref, ck_ref, cv_ref, bias_ref, sinks_ref, g_ref, o_ref, nk_ref, nv_ref, acc_ref,
                      *, n_batch, t_len):
    buf = ck_ref.shape[1]
    zeros = jnp.zeros((2 * WINDOW - buf - t_len, KV_W), f32)

    def step(b, _):
        r0 = pl.multiple_of(b * t_len, t_len)
        q_rows = z_ref[pl.ds(r0, t_len), Z_Q:Z_Q + ATTN_W]
        k_new = z_ref[pl.ds(r0, t_len), Z_K:Z_K + KV_W]
        v_new = z_ref[pl.ds(r0, t_len), Z_V:Z_V + KV_W]
        kwin = jnp.concatenate([ck_ref[b], k_new, zeros], axis=0).astype(bf16)
        vwin = jnp.concatenate([cv_ref[b], v_new, zeros], axis=0).astype(bf16)
        acc_ref[pl.ds(r0, t_len), :] = _attn_block(q_rows, kwin, vwin, bias_ref, sinks_ref, t_len)
        nk_ref[b, 0:buf - t_len, :] = ck_ref[b, t_len:buf, :]
        nk_ref[b, buf - t_len:buf, :] = k_new
        nv_ref[b, 0:buf - t_len, :] = cv_ref[b, t_len:buf, :]
        nv_ref[b, buf - t_len:buf, :] = v_new
        return 0
    lax.fori_loop(0, n_batch, step, 0, unroll=4)
    o_ref[...] = _rmsnorm(acc_ref[...], g_ref[...]).astype(o_ref.dtype)


def _attn_sample(z, ck, cv, bias_qk, sinks, g, *, n_batch, t_len):
    assert ck.shape[1] == WINDOW and t_len == SUBLANES
    rows = n_batch * t_len
    total = (z.size + 2 * ck.size + 2 * cv.size + bias_qk.size + rows * ATTN_W) * 4 + rows * ATTN_W * 2
    vmem = pl.BlockSpec(memory_space=pltpu.VMEM)
    return pl.pallas_call(
        functools.partial(_attn_sample_body, n_batch=n_batch, t_len=t_len),
        in_specs=[vmem, vmem, vmem, vmem, pl.BlockSpec(memory_space=pltpu.SMEM), vmem],
        out_specs=[vmem, vmem, vmem],
        out_shape=[jax.ShapeDtypeStruct((rows, ATTN_W), bf16),
                   jax.ShapeDtypeStruct(ck.shape, f32),
                   jax.ShapeDtypeStruct(cv.shape, f32)],
        scratch_shapes=[pltpu.VMEM((rows, ATTN_W), f32)],
        compiler_params=pltpu.CompilerParams(vmem_limit_bytes=_vmem_limit(0, total)),
        name="attn_sample",
    )(z, ck, cv, bias_qk, sinks, g)


def _lru_gates(xc, n, c0, wg_ref, ba_ref, bx_ref, lam_ref):
    lanes = pl.ds(c0, LRU_BLOCK)
    gates = jnp.dot(xc.astype(bf16), wg_ref[n], preferred_element_type=f32)
    gate_a = jax.nn.sigmoid(gates[:, :LRU_BLOCK] + ba_ref[:, lanes])
    gate_x = jax.nn.sigmoid(gates[:, LRU_BLOCK:] + bx_ref[:, lanes])
    log_a = -LRU_C * gate_a * jax.nn.softplus(-lam_ref[:, lanes])
    a = jnp.exp(log_a)
    y = -jnp.tanh(log_a) * (1.0 + a * a)
    mult = jnp.where(y > 0.0, y * lax.rsqrt(y), 0.0)
    return a, xc * gate_x, mult


def _tile_prefix(a, b, row):
    for d in (1, 2, 4):
        a_prev = pltpu.roll(a, d, axis=0)
        b_prev = pltpu.roll(b, d, axis=0)
        keep = row >= d
        b = jnp.where(keep, a * b_prev + b, b)
        a = jnp.where(keep, a * a_prev, a)
    return a, b


def _scan_tile(a, b, h_in, row):
    a, b = _tile_prefix(a, b, row)
    return a * h_in + b


def _last_row(h):
    return jnp.broadcast_to(h[SUBLANES - 1:SUBLANES, :], h.shape)


def _conv_taps(x, shifted, cw, cb):
    out = cb + shifted[CONV_W - 1] * cw[0:1]
    for j in range(1, CONV_W - 1):
        out = out + shifted[CONV_W - 1 - j] * cw[j:j + 1]
    return out + x * cw[CONV_W - 1:CONV_W]


def _rec_prompt_body(u_ref, wxl_ref, wxh_ref, wgl_ref, wgh_ref, prev_ref, h0_ref, cw_ref, cb_ref, wg_ref, ba_ref,
                     bx_ref, lam_ref, g_ref, o_ref, lru_ref, tail_ref, wb_ref, xprev_ref, a_ref, b_ref, gate_ref,
                     carry_ref, *, tc, n_chunks):
    c = pl.program_id(1)

    @pl.when((pl.program_id(0) == 0) & (c == 0))
    def _():
        for i, w_ref in enumerate((wxl_ref, wxh_ref, wgl_ref, wgh_ref)):
            wb_ref[:, i * LRU_HALF:(i + 1) * LRU_HALF] = w_ref[...].astype(bf16)

    @pl.when(c == 0)
    def _():
        xprev_ref[...] = prev_ref[...]
        for n in range(N_LRU_BLOCKS):
            carry_ref[n] = jnp.broadcast_to(h0_ref[:, n * LRU_BLOCK:(n + 1) * LRU_BLOCK], (SUBLANES, LRU_BLOCK))

    u = u_ref[...]
    pair = 2 * LRU_BLOCK
    n_pairs = LRU_W // pair

    def project(branch, p):
        c0 = branch * LRU_W + p * pair
        return jnp.dot(u, wb_ref[:, c0:c0 + pair], preferred_element_type=f32)

    def gates(p, x):
        lanes = slice(p * pair, (p + 1) * pair)
        big = jnp.concatenate([xprev_ref[:, lanes], x], axis=0)
        shifted = {k: pltpu.roll(big, k, axis=0)[SUBLANES:] for k in range(1, CONV_W)}
        xc = _conv_taps(x, shifted, cw_ref[:, lanes], cb_ref[:, lanes])
        for q in range(2):
            n = 2 * p + q
            sub = slice(q * LRU_BLOCK, (q + 1) * LRU_BLOCK)
            a, gated, mult = _lru_gates(xc[:, sub], n, n * LRU_BLOCK, wg_ref, ba_ref, bx_ref, lam_ref)
            b = gated * mult
            a_ref[n] = a
            b_ref[n] = b
            b_ref[n, 0:1, :] = jnp.where(c == 0, gated[0:1, :], b[0:1, :])
        return x[tc - SUBLANES:, :]

    def gelu_gate(p, gate):
        for q in range(2):
            gate_ref[2 * p + q] = jax.nn.gelu(gate[:, q * LRU_BLOCK:(q + 1) * LRU_BLOCK])

    steps = [(gates, 0, p) for p in range(n_pairs)] + [(gelu_gate, 1, p) for p in range(n_pairs)]
    tails = []
    ahead = n_pairs
    pending = [project(halves, p) for _, halves, p in steps[:ahead]]
    for i, (consume, _, p) in enumerate(steps):
        if i + ahead < len(steps):
            pending.append(project(steps[i + ahead][1], steps[i + ahead][2]))
        out = consume(p, pending.pop(0))
        if consume is gates:
            tails.append(out)
    tail = jnp.concatenate(tails, axis=1)
    xprev_ref[...] = tail
    tail_ref[...] = tail

    seg = tc // SUBLANES
    seg_rows = lambda r: pl.ds(r, SUBLANES, stride=seg)

    def local_scan(r, maps):
        out = []
        for n in range(N_LRU_BLOCKS):
            a = a_ref[n, seg_rows(r), :]
            a_cum = a * maps[2 * n]
            b_cum = a * maps[2 * n + 1] + b_ref[n, seg_rows(r), :]
            a_ref[n, seg_rows(r), :] = a_cum
            b_ref[n, seg_rows(r), :] = b_cum
            out += [a_cum, b_cum]
        return tuple(out)
    identity = (jnp.ones((SUBLANES, LRU_BLOCK), f32), jnp.zeros((SUBLANES, LRU_BLOCK), f32)) * N_LRU_BLOCKS
    seg_maps = lax.fori_loop(0, seg, local_scan, identity, unroll=2)

    row = lax.broadcasted_iota(jnp.int32, (SUBLANES, LRU_BLOCK), 0)
    h_in = []
    for n in range(N_LRU_BLOCKS):
        h_prev = carry_ref[n]
        a_cum, b_cum = _tile_prefix(seg_maps[2 * n], seg_maps[2 * n + 1], row)
        h_end = a_cum * h_prev + b_cum
        h_in.append(jnp.where(row == 0, h_prev, pltpu.roll(h_end, 1, axis=0)))
        carry_ref[n] = _last_row(h_end)

    def apply_scan(r, _):
        for n in range(N_LRU_BLOCKS):
            h = a_ref[n, seg_rows(r), :] * h_in[n] + b_ref[n, seg_rows(r), :]
            b_ref[n, seg_rows(r), :] = h * gate_ref[n, seg_rows(r), :]
        return 0
    lax.fori_loop(0, seg, apply_scan, 0, unroll=2)

    y = jnp.concatenate([b_ref[n] for n in range(N_LRU_BLOCKS)], axis=1)
    o_ref[...] = _rmsnorm(y, g_ref[...]).astype(o_ref.dtype)

    @pl.when(c == n_chunks - 1)
    def _():
        lru_ref[...] = jnp.concatenate([carry_ref[n] for n in range(N_LRU_BLOCKS)], axis=1)


def _rec_prompt(u, w_in, prev8, h0, cw, cb, wg, ba, bx, lam, g, l, *, n_batch, t_len, tc):
    n_chunks = t_len // tc
    once = pl.Buffered(1)
    row_vec = pl.BlockSpec((None, 1, LRU_W), lambda b, c: (l, 0, 0), pipeline_mode=once)
    w_half = lambda col: pl.BlockSpec((None, D_MODEL, LRU_HALF), lambda b, c: (l, 0, col // LRU_HALF),
                                      pipeline_mode=once)
    state = pl.BlockSpec((None, SUBLANES, LRU_W), lambda b, c: (b, 0, 0))
    blocks = tc * D_MODEL * 2 + tc * LRU_W * 2 + 3 * SUBLANES * LRU_W * 4
    scratch = (4 * D_MODEL * LRU_HALF * (4 + 2) + wg[0].size * 2 + (3 * tc + 2 * SUBLANES) * LRU_W * 4
               + 8 * tc * 2 * LRU_BLOCK * 4)
    by_block = pltpu.VMEM((N_LRU_BLOCKS, tc, LRU_BLOCK), f32)
    assert tc % SUBLANES == 0
    return pl.pallas_call(
        functools.partial(_rec_prompt_body, tc=tc, n_chunks=n_chunks),
        grid=(n_batch, n_chunks),
        in_specs=[pl.BlockSpec((tc, D_MODEL), lambda b, c: (b * n_chunks + c, 0)),
                  w_half(Z_XR), w_half(Z_XR + LRU_HALF), w_half(Z_GR), w_half(Z_GR + LRU_HALF),
                  state,
                  pl.BlockSpec((None, 1, LRU_W), lambda b, c: (b, 0, 0)),
                  pl.BlockSpec((None, CONV_W, LRU_W), lambda b, c: (l, 0, 0), pipeline_mode=once),
                  row_vec,
                  pl.BlockSpec((None,) + wg.shape[1:], lambda b, c: (l, 0, 0, 0), pipeline_mode=once),
                  row_vec, row_vec, row_vec, row_vec],
        out_specs=[pl.BlockSpec((tc, LRU_W), lambda b, c: (b * n_chunks + c, 0)), state, state],
        out_shape=[jax.ShapeDtypeStruct((n_batch * t_len, LRU_W), bf16),
                   jax.ShapeDtypeStruct((n_batch, SUBLANES, LRU_W), f32),
                   jax.ShapeDtypeStruct((n_batch, SUBLANES, LRU_W), f32)],
        scratch_shapes=[pltpu.VMEM((D_MODEL, 2 * LRU_W), bf16),
                        pltpu.VMEM((SUBLANES, LRU_W), f32), by_block, by_block, by_block,
                        pltpu.VMEM((N_LRU_BLOCKS, SUBLANES, LRU_BLOCK), f32)],
        compiler_params=_params(("arbitrary", "arbitrary"), _vmem_limit(blocks, scratch)),
        name="rec_prompt",
    )(u, w_in, w_in, w_in, w_in, prev8, h0, cw, cb, wg, ba, bx, lam, g)


def _rec_sample_body(z_ref, prev_ref, h0_ref, cw_ref, cb_ref, wg_ref, ba_ref, bx_ref, lam_ref, g_ref,
                     o_ref, lru_ref, xc_ref, a_ref, b_ref, *, n_batch):
    row = lax.broadcasted_iota(jnp.int32, (SUBLANES, LRU_W), 0)

    def conv_step(b, _):
        r = pl.multiple_of(b * SUBLANES, SUBLANES)
        x = z_ref[pl.ds(r, SUBLANES), Z_XR:Z_XR + LRU_W]
        prev = prev_ref[pl.ds(r, SUBLANES), :]
        shifted = {k: jnp.where(row >= k, pltpu.roll(x, k, axis=0), pltpu.roll(prev, k, axis=0))
                   for k in range(1, CONV_W)}
        xc_ref[pl.ds(r, SUBLANES), :] = _conv_taps(x, shifted, cw_ref[...], cb_ref[...])
        return 0
    lax.fori_loop(0, n_batch, conv_step, 0)

    def gates_step(n, _):
        c0 = pl.multiple_of(n * LRU_BLOCK, LRU_BLOCK)
        lanes = pl.ds(c0, LRU_BLOCK)
        a, gated, mult = _lru_gates(xc_ref[:, lanes], n, c0, wg_ref, ba_ref, bx_ref, lam_ref)
        a_ref[:, lanes] = a
        b_ref[:, lanes] = gated * mult
        return 0
    lax.fori_loop(0, N_LRU_BLOCKS, gates_step, 0)

    def scan_step(b, _):
        rows = pl.ds(pl.multiple_of(b * SUBLANES, SUBLANES), SUBLANES)
        h_in = jnp.broadcast_to(h0_ref[pl.ds(b, 1), :], (SUBLANES, LRU_W))
        hh = _scan_tile(a_ref[rows, :], b_ref[rows, :], h_in, row)
        b_ref[rows, :] = hh * jax.nn.gelu(z_ref[rows, Z_GR:Z_GR + LRU_W])
        lru_ref[pl.ds(b, 1), :] = hh[SUBLANES - 1:SUBLANES, :]
        return 0
    lax.fori_loop(0, n_batch, scan_step, 0)
    o_ref[...] = _rmsnorm(b_ref[...], g_ref[...]).astype(o_ref.dtype)


def _rec_sample(z, prev8, h0, cw, cb, wg, ba, bx, lam, g, *, n_batch, t_len):
    assert t_len == SUBLANES and PAST_LEN > 0
    rows = n_batch * t_len
    total = z.size * 4 + (5 * rows + 2 * n_batch) * LRU_W * 4 + wg.size * 2
    vmem = pl.BlockSpec(memory_space=pltpu.VMEM)
    return pl.pallas_call(
        functools.partial(_rec_sample_body, n_batch=n_batch),
        in_specs=[vmem] * 10,
        out_specs=[vmem, vmem],
        out_shape=[jax.ShapeDtypeStruct((rows, LRU_W), bf16),
                   jax.ShapeDtypeStruct((n_batch, LRU_W), f32)],
        scratch_shapes=[pltpu.VMEM((rows, LRU_W), f32)] * 3,
        compiler_params=pltpu.CompilerParams(vmem_limit_bytes=_vmem_limit(0, total)),
        name="rec_sample",
    )(z, prev8, h0, cw, cb, wg, ba, bx, lam, g)


def _mix_mlp_body(h_ref, attn_ref, rec_ref, wo_ref, g_ref, wu_ref, wd_ref, o_ref, u_ref):
    @pl.when(pl.program_id(1) == 0)
    def _():
        o_ref[...] = h_ref[...] + jnp.dot(attn_ref[...], wo_ref[0:ATTN_W, :], preferred_element_type=f32)
        o_ref[...] += jnp.dot(rec_ref[...], wo_ref[ATTN_W:ATTN_W + LRU_W, :], preferred_element_type=f32)
        u_ref[...] = _rmsnorm(o_ref[...], g_ref[...]).astype(bf16)
    hid = jnp.dot(u_ref[...], wu_ref[...], preferred_element_type=f32)
    hid = jnp.square(jnp.maximum(hid, 0.0)).astype(bf16)
    o_ref[...] += jnp.dot(hid, wd_ref[...], preferred_element_type=f32)


def _mix_mlp(h, attn, rec, wo, g, wu, wd, l, *, tm, tf):
    m = h.shape[0]
    once = pl.Buffered(1)
    blocks = 2 * tm * D_MODEL * 4 + tm * (ATTN_W + LRU_W) * 2 + 2 * D_MODEL * tf * 2
    scratch = wo[0].size * 2 + tm * D_MODEL * 2 + tm * tf * 6
    return pl.pallas_call(
        _mix_mlp_body,
        grid=(m // tm, D_FF // tf),
        in_specs=[pl.BlockSpec((tm, D_MODEL), lambda i, f: (i, 0)),
                  pl.BlockSpec((tm, ATTN_W), lambda i, f: (i, 0)),
                  pl.BlockSpec((tm, LRU_W), lambda i, f: (i, 0)),
                  pl.BlockSpec((None,) + wo.shape[1:], lambda i, f: (l, 0, 0), pipeline_mode=once),
                  pl.BlockSpec((None, 1, D_MODEL), lambda i, f: (l, 0, 0), pipeline_mode=once),
                  pl.BlockSpec((None, D_MODEL, tf), lambda i, f: (l, 0, f)),
                  pl.BlockSpec((None, tf, D_MODEL), lambda i, f: (l, f, 0))],
        out_specs=pl.BlockSpec((tm, D_MODEL), lambda i, f: (i, 0)),
        out_shape=jax.ShapeDtypeStruct((m, D_MODEL), f32),
        scratch_shapes=[pltpu.VMEM((tm, D_MODEL), bf16)],
        compiler_params=_params(("parallel", "arbitrary"), _vmem_limit(blocks, scratch)),
        name="mix_mlp",
    )(h, attn, rec, wo, g, wu, wd)


def _mix_mlp_cast_body(h_ref, attn_ref, rec_ref, wo_ref, g_ref, wu_ref, wd_ref,
                       o_ref, wob_ref, wub_ref, wdb_ref, u_ref):
    @pl.when(pl.program_id(1) == 0)
    def _():
        wob_ref[...] = wo_ref[...].astype(bf16)
        o_ref[...] = h_ref[...] + jnp.dot(attn_ref[...], wob_ref[0:ATTN_W, :], preferred_element_type=f32)
        o_ref[...] += jnp.dot(rec_ref[...], wob_ref[ATTN_W:ATTN_W + LRU_W, :], preferred_element_type=f32)
        u_ref[...] = _rmsnorm(o_ref[...], g_ref[...]).astype(bf16)
    wub_ref[...] = wu_ref[...].astype(bf16)
    wdb_ref[...] = wd_ref[...].astype(bf16)
    hid = jnp.dot(u_ref[...], wub_ref[...], preferred_element_type=f32)
    hid = jnp.square(jnp.maximum(hid, 0.0)).astype(bf16)
    o_ref[...] += jnp.dot(hid, wdb_ref[...], preferred_element_type=f32)


def _mix_mlp_cast(h, attn, rec, wo, g, wu, wd, l, *, tf):
    m = h.shape[0]
    once = pl.Buffered(1)
    blocks = 2 * D_MODEL * tf * (4 + 2)
    scratch = (2 * m * D_MODEL * 4 + m * (ATTN_W + LRU_W) * 2 + wo[0].size * (4 + 2) + m * D_MODEL * 2
               + m * tf * 6)
    return pl.pallas_call(
        _mix_mlp_cast_body,
        grid=(1, D_FF // tf),
        in_specs=[pl.BlockSpec((m, D_MODEL), lambda i, f: (0, 0), pipeline_mode=once),
                  pl.BlockSpec((m, ATTN_W), lambda i, f: (0, 0), pipeline_mode=once),
                  pl.BlockSpec((m, LRU_W), lambda i, f: (0, 0), pipeline_mode=once),
                  pl.BlockSpec((None,) + wo.shape[1:], lambda i, f: (l, 0, 0), pipeline_mode=once),
                  pl.BlockSpec((None, 1, D_MODEL), lambda i, f: (l, 0, 0), pipeline_mode=once),
                  pl.BlockSpec((None, D_MODEL, tf), lambda i, f: (l, 0, f)),
                  pl.BlockSpec((None, tf, D_MODEL), lambda i, f: (l, f, 0))],
        out_specs=[pl.BlockSpec((m, D_MODEL), lambda i, f: (0, 0)),
                   pl.BlockSpec(wo.shape[1:], lambda i, f: (0, 0)),
                   pl.BlockSpec((D_MODEL, tf), lambda i, f: (0, f)),
                   pl.BlockSpec((tf, D_MODEL), lambda i, f: (f, 0))],
        out_shape=[jax.ShapeDtypeStruct((m, D_MODEL), f32),
                   jax.ShapeDtypeStruct(wo.shape[1:], bf16),
                   jax.ShapeDtypeStruct(wu.shape[1:], bf16),
                   jax.ShapeDtypeStruct(wd.shape[1:], bf16)],
        scratch_shapes=[pltpu.VMEM((m, D_MODEL), bf16)],
        compiler_params=_params(("arbitrary", "arbitrary"), _vmem_limit(blocks, scratch)),
        name="mix_mlp_cast",
    )(h, attn, rec, wo, g, wu, wd)


def _final_norm_body(h_ref, g_ref, o_ref):
    o_ref[0] = _rmsnorm(h_ref[0], g_ref[...])


def _final_norm(h, g, *, skip, tr):
    n_batch, t_len, _ = h.shape
    s_len = t_len - skip
    tiles = tr // SUBLANES
    skip_tiles = skip // SUBLANES
    h4 = h.reshape(n_batch, t_len // SUBLANES, SUBLANES, D_MODEL)
    out = pl.pallas_call(
        _final_norm_body,
        grid=(n_batch, s_len // tr),
        in_specs=[pl.BlockSpec((pl.Element(1), pl.Element(tiles), pl.Element(SUBLANES), pl.Element(D_MODEL)),
                               lambda b, r: (b, skip_tiles + r * tiles, 0, 0)),
                  pl.BlockSpec((1, D_MODEL), lambda b, r: (0, 0))],
        out_specs=pl.BlockSpec((1, tiles, SUBLANES, D_MODEL), lambda b, r: (b, r, 0, 0)),
        out_shape=jax.ShapeDtypeStruct((n_batch, s_len // SUBLANES, SUBLANES, D_MODEL), f32),
        compiler_params=_params(("parallel", "parallel"), _vmem_limit(2 * tr * D_MODEL * 4)),
        name="final_norm",
    )(h4, g)
    return out.reshape(n_batch, s_len, D_MODEL)


def _row_tile(m, candidates):
    for tm in candidates:
        if m % tm == 0:
            return tm
    raise ValueError(f"no row tile for {m} rows")


def kernel(x_prompt, x_sample, cache_k_win, cache_v_win, state_conv, state_lru, meta_tokens, norm_mix_g, w_in,
           conv_w, conv_b, w_gate_a, b_gate_a, w_gate_x, b_gate_x, lru_lambda, attn_sinks, rel_bias, attn_out_g,
           rec_out_g, w_out, norm_mlp_g, w_up, w_down, final_norm_g):
    n_p, s_p, _ = x_prompt.shape
    n_s, t_s, _ = x_sample.shape
    t_p = N_META + s_p
    buf = cache_k_win.shape[2]
    assert t_p % BF16_ROWS == 0 and buf == WINDOW

    w_gates = jnp.concatenate([w_gate_a, w_gate_x], axis=-1).astype(bf16)
    rows3 = lambda p: p[:, None, :]
    g_mix, g_mlp, g_attn, g_rec = rows3(norm_mix_g), rows3(norm_mlp_g), rows3(attn_out_g), rows3(rec_out_g)
    cb3, ba3, bx3, lam3 = rows3(conv_b), rows3(b_gate_a), rows3(b_gate_x), rows3(lru_lambda)
    g_attn_t = jnp.broadcast_to(
        attn_out_g.reshape(DEPTH, N_Q_HEADS, HEAD_DIM).transpose(0, 2, 1)[..., None],
        (DEPTH, HEAD_DIM, N_Q_HEADS, WINDOW)).reshape(DEPTH, HEAD_DIM, ATTN_W)

    meta = jnp.broadcast_to(meta_tokens.astype(x_prompt.dtype)[None], (n_p, N_META, D_MODEL))
    hp = jnp.concatenate([meta, x_prompt], axis=1).reshape(n_p * t_p, D_MODEL)
    hs = x_sample.reshape(n_s * t_s, D_MODEL)

    bias_qk, bias_kq = _bias_tables(rel_bias)
    bias_qk = bias_qk.reshape(N_Q_HEADS * WINDOW, 2 * WINDOW)
    prev_p = jnp.zeros((n_p, SUBLANES, LRU_W), f32)
    h0_p = jnp.zeros((n_p, 1, LRU_W), f32)
    ck = cache_k_win.reshape(DEPTH, n_s, buf, KV_W)
    cv = cache_v_win.reshape(DEPTH, n_s, buf, KV_W)
    prev_s = jnp.pad(state_conv, ((0, 0), (0, 0), (SUBLANES - (CONV_W - 1), 0), (0, 0)))
    prev_s = prev_s.reshape(DEPTH, n_s * SUBLANES, LRU_W)

    tm_p = _row_tile(n_p * t_p, (688, 344))
    tm_s = n_s * t_s
    tc_p = _row_tile(t_p, (688, 344, 48, 16))
    tn = IN_W // 2

    kp_l, vp_l, cp_l, lp_l, ks_l, vs_l, cs_l, ls_l = ([] for _ in range(8))
    for l in range(DEPTH):
        z = _inproj(hs, g_mix, w_in, l, tm=tm_s, tn=tn)
        attn, nk, nv = _attn_sample(z, ck[l], cv[l], bias_qk, attn_sinks[l], g_attn[l], n_batch=n_s, t_len=t_s)
        rec, lru = _rec_sample(z, prev_s[l], state_lru[l], conv_w[l], cb3[l], w_gates[l], ba3[l], bx3[l],
                               lam3[l], g_rec[l], n_batch=n_s, t_len=t_s)
        hs, w_out_b, w_up_b, w_down_b = _mix_mlp_cast(hs, attn, rec, w_out, g_mlp, w_up, w_down, l, tf=512)
        z3 = z.reshape(n_s, t_s, IN_W)
        ks_l.append(nk.reshape(n_s, buf, N_KV_HEADS, HEAD_DIM))
        vs_l.append(nv.reshape(n_s, buf, N_KV_HEADS, HEAD_DIM))
        cs_l.append(z3[:, t_s - (CONV_W - 1):, Z_XR:Z_XR + LRU_W])
        ls_l.append(lru)
        z, u = _inproj_qkv(hp, g_mix, w_in, l, tm=tm_p)
        attn = _attn_prompt(z, bias_kq, attn_sinks, g_attn_t, l, n_batch=n_p, t_len=t_p)
        rec, lru, x_tail = _rec_prompt(u, w_in, prev_p, h0_p, conv_w, cb3, w_gates, ba3, bx3, lam3, g_rec, l,
                                       n_batch=n_p, t_len=t_p, tc=tc_p)
        hp = _mix_mlp(hp, attn, rec, w_out_b[None], g_mlp[l:l + 1], w_up_b[None], w_down_b[None], 0,
                      tm=tm_p, tf=1024)
        z3 = z.reshape(n_p, t_p, Z_XR)
        kp_l.append(z3[:, t_p - WINDOW:, Z_K:Z_K + KV_W].reshape(n_p, WINDOW, N_KV_HEADS, HEAD_DIM))
        vp_l.append(z3[:, t_p - WINDOW:, Z_V:Z_V + KV_W].reshape(n_p, WINDOW, N_KV_HEADS, HEAD_DIM))
        cp_l.append(x_tail[:, SUBLANES - (CONV_W - 1):])
        lp_l.append(lru[:, 0])

    g_fin = final_norm_g[None, :]
    y_prompt = _final_norm(hp.reshape(n_p, t_p, D_MODEL), g_fin, skip=N_META, tr=256)
    y_sample = _final_norm(hs.reshape(1, n_s * t_s, D_MODEL), g_fin, skip=0, tr=n_s * t_s)
    y_sample = y_sample.reshape(n_s, t_s, D_MODEL)
    return (y_prompt, y_sample,
            jnp.stack(kp_l), jnp.stack(vp_l), jnp.stack(cp_l), jnp.stack(lp_l),
            jnp.stack(ks_l), jnp.stack(vs_l), jnp.stack(cs_l), jnp.stack(ls_l))
```

```python
import functools
import math

import jax
import jax.numpy as jnp
from jax import lax
from jax.experimental import pallas as pl
from jax.experimental.pallas import tpu as pltpu

f32 = jnp.float32
bf16 = jnp.bfloat16

D_MODEL = 2048
DEPTH = 4
PAST_LEN = 16384
HEAD_DIM = 128
N_Q_HEADS = 8
N_KV_HEADS = 2
Q_PER_KV = N_Q_HEADS // N_KV_HEADS
ATTN_W = N_Q_HEADS * HEAD_DIM
KV_W = N_KV_HEADS * HEAD_DIM
LRU_W = D_MODEL // 2
N_LRU_BLOCKS = 8
LRU_BLOCK = LRU_W // N_LRU_BLOCKS
CONV_W = 4
LRU_C = 8.0
IN_W = ATTN_W + 2 * KV_W + 2 * LRU_W
D_FF = 4 * D_MODEL
WINDOW = 128
N_BUCKETS = 32
MAX_DISTANCE = 128
N_META = 16
EPS = 1e-6
ATTN_SCALE = HEAD_DIM ** -0.5
INV_ATTN_SCALE = HEAD_DIM ** 0.5
EXP2_PER_T = math.log2(math.e) / INV_ATTN_SCALE

Z_Q = 0
Z_K = ATTN_W
Z_V = Z_K + KV_W
Z_XR = Z_V + KV_W
Z_GR = Z_XR + LRU_W
LRU_HALF = LRU_W // 2

ATTN_LEAD = WINDOW + (-N_META) % WINDOW

SUBLANES = 8
BF16_ROWS = 16
V7X_VMEM_BYTES = 64 * 1024 * 1024
VMEM_CAP_BYTES = V7X_VMEM_BYTES - 2 * 1024 * 1024


def _vmem_limit(pipelined_bytes, scratch_bytes=0):
    est = 2 * pipelined_bytes + scratch_bytes
    return int(min(VMEM_CAP_BYTES, est + est // 2 + (8 << 20)))


def _params(semantics, vmem_bytes):
    return pltpu.CompilerParams(dimension_semantics=semantics, vmem_limit_bytes=vmem_bytes)


def _rms_scale(x):
    return lax.rsqrt(jnp.mean(x * x, axis=-1, keepdims=True) + EPS)


def _rmsnorm(x, g):
    return x * _rms_scale(x) * g


def _inproj_body(h_ref, g_ref, w_ref, z_ref, u_ref):
    @pl.when(pl.program_id(1) == 0)
    def _():
        u_ref[...] = _rmsnorm(h_ref[...], g_ref[...]).astype(bf16)
    z_ref[...] = jnp.dot(u_ref[...], w_ref[...].astype(bf16), preferred_element_type=f32)


def _inproj(h, g, w, l, *, tm, tn):
    m = h.shape[0]
    blocks = tm * D_MODEL * 4 + D_MODEL * tn * 4 + tm * tn * 4 + D_MODEL * tn
    return pl.pallas_call(
        _inproj_body,
        grid=(m // tm, IN_W // tn),
        in_specs=[pl.BlockSpec((tm, D_MODEL), lambda i, j: (i, 0)),
                  pl.BlockSpec((None, 1, D_MODEL), lambda i, j: (l, 0, 0)),
                  pl.BlockSpec((None, D_MODEL, tn), lambda i, j: (l, 0, j))],
        out_specs=pl.BlockSpec((tm, tn), lambda i, j: (i, j)),
        out_shape=jax.ShapeDtypeStruct((m, IN_W), f32),
        scratch_shapes=[pltpu.VMEM((tm, D_MODEL), bf16)],
        compiler_params=_params(("parallel", "arbitrary"), _vmem_limit(blocks, tm * D_MODEL * 2)),
        name="inproj",
    )(h, g, w)


def _inproj_qkv_body(h_ref, g_ref, w_ref, z_ref, u_ref, wb_ref):
    @pl.when(pl.program_id(0) == 0)
    def _():
        wb_ref[...] = w_ref[...].astype(bf16)
    u_ref[...] = _rmsnorm(h_ref[...], g_ref[...]).astype(bf16)
    z_ref[...] = jnp.dot(u_ref[...], wb_ref[...], preferred_element_type=f32)


def _inproj_qkv(h, g, w, l, *, tm):
    m = h.shape[0]
    once = pl.Buffered(1)
    blocks = tm * D_MODEL * 4 + tm * Z_XR * 4 + tm * D_MODEL * 2
    return pl.pallas_call(
        _inproj_qkv_body,
        grid=(m // tm,),
        in_specs=[pl.BlockSpec((tm, D_MODEL), lambda i: (i, 0)),
                  pl.BlockSpec((None, 1, D_MODEL), lambda i: (l, 0, 0), pipeline_mode=once),
                  pl.BlockSpec((None, D_MODEL, Z_XR), lambda i: (l, 0, 0), pipeline_mode=once)],
        out_specs=[pl.BlockSpec((tm, Z_XR), lambda i: (i, 0)),
                   pl.BlockSpec((tm, D_MODEL), lambda i: (i, 0))],
        out_shape=[jax.ShapeDtypeStruct((m, Z_XR), f32), jax.ShapeDtypeStruct((m, D_MODEL), bf16)],
        scratch_shapes=[pltpu.VMEM((D_MODEL, Z_XR), bf16)],
        compiler_params=_params(("arbitrary",), _vmem_limit(blocks, D_MODEL * Z_XR * 6)),
        name="inproj_qkv",
    )(h, g, w)


def _rel_bias_of(dist, rel_ref, h):
    n = jnp.maximum(dist, 0)
    max_exact = N_BUCKETS // 2
    nf = jnp.maximum(n, 1).astype(f32)
    large = max_exact + (jnp.log(nf / max_exact) / math.log(MAX_DISTANCE / max_exact)
                         * (N_BUCKETS - max_exact)).astype(jnp.int32)
    large = jnp.minimum(large, N_BUCKETS - 1)
    bucket = jnp.where(n < max_exact, n, large)
    acc = jnp.zeros(dist.shape, f32)
    for b in range(N_BUCKETS):
        acc = jnp.where(bucket == b, rel_ref[b, h], acc)
    return jnp.where((dist >= 0) & (dist < WINDOW), acc, -jnp.inf)


def _bias_table_body(rel_ref, qk_ref, kq_ref):
    shape_qk = (WINDOW, 2 * WINDOW)
    dist_qk = (lax.broadcasted_iota(jnp.int32, shape_qk, 0) + WINDOW
               - lax.broadcasted_iota(jnp.int32, shape_qk, 1))
    shape_kq = (2 * WINDOW, WINDOW)
    dist_kq = (lax.broadcasted_iota(jnp.int32, shape_kq, 1) + WINDOW
               - lax.broadcasted_iota(jnp.int32, shape_kq, 0))
    for h in range(N_Q_HEADS):
        kv, g = divmod(h, Q_PER_KV)
        qk_ref[h] = _rel_bias_of(dist_qk, rel_ref, h)
        kq_ref[kv, :, g * WINDOW:(g + 1) * WINDOW] = _rel_bias_of(dist_kq, rel_ref, h) * INV_ATTN_SCALE


def _bias_tables(rel_bias):
    return pl.pallas_call(
        _bias_table_body,
        in_specs=[pl.BlockSpec(memory_space=pltpu.SMEM)],
        out_shape=[jax.ShapeDtypeStruct((N_Q_HEADS, WINDOW, 2 * WINDOW), f32),
                   jax.ShapeDtypeStruct((N_KV_HEADS, 2 * WINDOW, Q_PER_KV * WINDOW), f32)],
        name="bias_tables",
    )(rel_bias)


def _attn_block_t(q_rows, kpad_ref, vt_ref, r0, bias_ref, sinks_ref, l, gt_ref, lead_keys):
    raws, invs = [], []
    for kv in range(N_KV_HEADS):
        heads = range(kv * Q_PER_KV, (kv + 1) * Q_PER_KV)
        kwin = kpad_ref[pl.ds(r0, 2 * WINDOW), kv * HEAD_DIM:(kv + 1) * HEAD_DIM]
        q = jnp.concatenate([q_rows[:, h * HEAD_DIM:(h + 1) * HEAD_DIM] for h in heads], axis=0).astype(bf16)
        t = lax.dot_general(kwin, q, (((1,), (1,)), ((), ())), preferred_element_type=f32) + bias_ref[kv]
        if lead_keys:
            key = lax.broadcasted_iota(jnp.int32, t.shape, 0)
            t = jnp.where(key < lead_keys, -jnp.inf, t)
        sink = jnp.concatenate([jnp.full((1, WINDOW), sinks_ref[l, h] * INV_ATTN_SCALE, f32) for h in heads],
                               axis=1)
        m = jnp.maximum(jnp.max(t, axis=0, keepdims=True), sink)
        p = jnp.exp2((t - m) * EXP2_PER_T)
        denom = jnp.sum(p, axis=0, keepdims=True) + jnp.exp2((sink - m) * EXP2_PER_T)
        vt = vt_ref[kv * HEAD_DIM:(kv + 1) * HEAD_DIM, pl.ds(r0, 2 * WINDOW)]
        raws.append(jnp.dot(vt, p.astype(bf16), preferred_element_type=f32))
        invs.append(1.0 / denom)
    raw = jnp.concatenate(raws, axis=1)
    inv = jnp.concatenate(invs, axis=1)
    sq = jnp.sum(raw * raw, axis=0, keepdims=True) * (inv * inv)
    ssq = sq[:, 0:WINDOW]
    for h in range(1, N_Q_HEADS):
        ssq = ssq + sq[:, h * WINDOW:(h + 1) * WINDOW]
    r = lax.rsqrt(ssq * (1.0 / ATTN_W) + EPS)
    yt = raw * (inv * jnp.concatenate([r] * N_Q_HEADS, axis=1)) * gt_ref[...]
    return jnp.concatenate([yt[:, h * WINDOW:(h + 1) * WINDOW].T for h in range(N_Q_HEADS)], axis=1)


def _attn_prompt_body(q_ref, k_ref, v_ref, bias_ref, sinks_ref, gt_ref, o_ref, kpad_ref, vpad_ref, vt_ref,
                      *, t_len, l):
    pad_rows = kpad_ref.shape[0]
    kpad_ref[0:ATTN_LEAD, :] = jnp.zeros((ATTN_LEAD, KV_W), bf16)
    kpad_ref[ATTN_LEAD:pad_rows, :] = k_ref[...].astype(bf16)
    vpad_ref[0:ATTN_LEAD, :] = jnp.zeros((ATTN_LEAD, KV_W), f32)
    vpad_ref[ATTN_LEAD:pad_rows, :] = v_ref[...]

    def transpose_step(c, _):
        r = pl.multiple_of(c * WINDOW, WINDOW)
        chunk = vpad_ref[pl.ds(r, WINDOW), :]
        for kv in range(N_KV_HEADS):
            vt_ref[kv * HEAD_DIM:(kv + 1) * HEAD_DIM, pl.ds(r, WINDOW)] = (
                chunk[:, kv * HEAD_DIM:(kv + 1) * HEAD_DIM].T.astype(bf16))
        return 0
    lax.fori_loop(0, pad_rows // WINDOW, transpose_step, 0)

    n_first = 2 * WINDOW - ATTN_LEAD
    q0 = jnp.concatenate([jnp.zeros((WINDOW - n_first, ATTN_W), f32), q_ref[0:n_first, :]], axis=0)
    y0 = _attn_block_t(q0, kpad_ref, vt_ref, 0, bias_ref, sinks_ref, l, gt_ref, ATTN_LEAD)
    o_ref[0:n_first, :] = y0[WINDOW - n_first:, :].astype(o_ref.dtype)

    def block(j, lead_keys):
        r0 = j * WINDOW
        q0_row = j * WINDOW - (ATTN_LEAD - WINDOW)
        if not isinstance(j, int):
            r0, q0_row = pl.multiple_of(r0, WINDOW), pl.multiple_of(q0_row, BF16_ROWS)
        rows = pl.ds(q0_row, WINDOW)
        y = _attn_block_t(q_ref[rows, :], kpad_ref, vt_ref, r0, bias_ref, sinks_ref, l, gt_ref, lead_keys)
        o_ref[rows, :] = y.astype(o_ref.dtype)

    block(1, ATTN_LEAD - WINDOW)

    def step(j, _):
        block(j, 0)
        return 0
    lax.fori_loop(2, pad_rows // WINDOW - 1, step, 0, unroll=3)


def _attn_prompt(z, bias_kq, sinks, gt, l, *, n_batch, t_len):
    pad_rows = ATTN_LEAD + t_len
    assert pad_rows % WINDOW == 0 and (ATTN_LEAD - WINDOW) % BF16_ROWS == 0
    blocks = t_len * (ATTN_W + 2 * KV_W) * 4 + t_len * ATTN_W * 2 + bias_kq.size * 4 + HEAD_DIM * ATTN_W * 4
    scratch = pad_rows * KV_W * (2 + 4 + 2)
    return pl.pallas_call(
        functools.partial(_attn_prompt_body, t_len=t_len, l=l),
        grid=(n_batch,),
        in_specs=[pl.BlockSpec((t_len, ATTN_W), lambda b: (b, Z_Q // ATTN_W)),
                  pl.BlockSpec((t_len, KV_W), lambda b: (b, Z_K // KV_W)),
                  pl.BlockSpec((t_len, KV_W), lambda b: (b, Z_V // KV_W)),
                  pl.BlockSpec(bias_kq.shape, lambda b: (0, 0, 0)),
                  pl.BlockSpec(memory_space=pltpu.SMEM),
                  pl.BlockSpec((None, HEAD_DIM, ATTN_W), lambda b: (l, 0, 0))],
        out_specs=pl.BlockSpec((t_len, ATTN_W), lambda b: (b, 0)),
        out_shape=jax.ShapeDtypeStruct((n_batch * t_len, ATTN_W), bf16),
        scratch_shapes=[pltpu.VMEM((pad_rows, KV_W), bf16), pltpu.VMEM((pad_rows, KV_W), f32),
                        pltpu.VMEM((KV_W, pad_rows), bf16)],
        compiler_params=_params(("parallel",), _vmem_limit(blocks, scratch)),
        name="attn_prompt",
    )(z, z, z, bias_kq, sinks, gt)


def _attn_block(q_rows, kwins, vwins, bias_ref, sinks_ref, l, n_rows):
    outs = []
    for kv in range(N_KV_HEADS):
        heads = range(kv * Q_PER_KV, (kv + 1) * Q_PER_KV)
        k, v = kwins[kv], vwins[kv]
        q = jnp.concatenate([q_rows[:, h * HEAD_DIM:(h + 1) * HEAD_DIM] for h in heads], axis=0).astype(bf16)
        bias = jnp.concatenate([bias_ref[h * WINDOW:h * WINDOW + n_rows, :] for h in heads], axis=0)
        s = lax.dot_general(q, k, (((1,), (1,)), ((), ())), preferred_element_type=f32) * ATTN_SCALE + bias
        sink = jnp.concatenate([jnp.full((n_rows, 1), sinks_ref[l, h], f32) for h in heads], axis=0)
        m = jnp.maximum(jnp.max(s, axis=-1, keepdims=True), sink)
        p = jnp.exp(s - m)
        denom = jnp.sum(p, axis=-1, keepdims=True) + jnp.exp(sink - m)
        o = jnp.dot(p.astype(bf16), v, preferred_element_type=f32) / denom
        outs.extend(o[g * n_rows:(g + 1) * n_rows] for g in range(Q_PER_KV))
    return jnp.concatenate(outs, axis=1)


def _attn_sample_body(z_ref, ck_ref, cv_ref, bias_ref, sinks_ref, g_ref, o_ref, nk_ref, nv_ref, acc_ref,
                      *, n_batch, t_len, buf, l):
    zeros = jnp.zeros((2 * WINDOW - buf - t_len, HEAD_DIM), f32)
    kept = N_KV_HEADS * (buf - t_len)

    def step(b, _):
        r0 = pl.multiple_of(b * t_len, t_len)
        q_rows = z_ref[pl.ds(r0, t_len), Z_Q:Z_Q + ATTN_W]
        wins = []
        for col, c_ref, n_ref in ((Z_K, ck_ref, nk_ref), (Z_V, cv_ref, nv_ref)):
            new = [z_ref[pl.ds(r0, t_len), col + kv * HEAD_DIM:col + (kv + 1) * HEAD_DIM]
                   for kv in range(N_KV_HEADS)]
            wins.append([jnp.concatenate([c_ref[b, pl.ds(kv, buf, stride=N_KV_HEADS), :], new[kv], zeros],
                                         axis=0).astype(bf16) for kv in range(N_KV_HEADS)])
            n_ref[b, 0:kept, :] = c_ref[b, N_KV_HEADS * t_len:N_KV_HEADS * buf, :]
            for kv in range(N_KV_HEADS):
                n_ref[b, pl.ds(kept + kv, t_len, stride=N_KV_HEADS), :] = new[kv]
        acc_ref[pl.ds(r0, t_len), :] = _attn_block(q_rows, wins[0], wins[1], bias_ref, sinks_ref, l, t_len)
        return 0
    lax.fori_loop(0, n_batch, step, 0, unroll=4)
    o_ref[...] = _rmsnorm(acc_ref[...], g_ref[...]).astype(o_ref.dtype)


def _attn_sample(z, ck, cv, bias_qk, sinks, g, l, *, n_batch, t_len):
    buf = ck.shape[2] // N_KV_HEADS
    assert buf == WINDOW and t_len == SUBLANES
    rows = n_batch * t_len
    whole = lambda a: pl.BlockSpec(a.shape, lambda i: (0,) * a.ndim)
    cache = pl.BlockSpec((None,) + ck.shape[1:], lambda i: (l, 0, 0, 0))
    blocks = (z.size + 4 * ck[0].size + bias_qk.size) * 4 + rows * ATTN_W * 2
    return pl.pallas_call(
        functools.partial(_attn_sample_body, n_batch=n_batch, t_len=t_len, buf=buf, l=l),
        grid=(1,),
        in_specs=[whole(z), cache, cache, whole(bias_qk), pl.BlockSpec(memory_space=pltpu.SMEM),
                  pl.BlockSpec((None, 1, ATTN_W), lambda i: (l, 0, 0))],
        out_specs=[pl.BlockSpec((rows, ATTN_W), lambda i: (0, 0)),
                   pl.BlockSpec(ck.shape[1:], lambda i: (0, 0, 0)),
                   pl.BlockSpec(cv.shape[1:], lambda i: (0, 0, 0))],
        out_shape=[jax.ShapeDtypeStruct((rows, ATTN_W), bf16),
                   jax.ShapeDtypeStruct(ck.shape[1:], f32),
                   jax.ShapeDtypeStruct(cv.shape[1:], f32)],
        scratch_shapes=[pltpu.VMEM((rows, ATTN_W), f32)],
        compiler_params=_params(("arbitrary",), _vmem_limit(blocks, rows * ATTN_W * 4)),
        name="attn_sample",
    )(z, ck, cv, bias_qk, sinks, g)


def _lru_gates(xc, n, c0, wg_ref, ba_ref, bx_ref, lam_ref):
    lanes = pl.ds(c0, LRU_BLOCK)
    gates = jnp.dot(xc.astype(bf16), wg_ref[n], preferred_element_type=f32)
    gate_a = jax.nn.sigmoid(gates[:, :LRU_BLOCK] + ba_ref[:, lanes])
    gate_x = jax.nn.sigmoid(gates[:, LRU_BLOCK:] + bx_ref[:, lanes])
    log_a = -LRU_C * gate_a * jax.nn.softplus(-lam_ref[:, lanes])
    a = jnp.exp(log_a)
    y = -jnp.tanh(log_a) * (1.0 + a * a)
    mult = jnp.where(y > 0.0, y * lax.rsqrt(y), 0.0)
    return a, xc * gate_x, mult


def _tile_prefix(a, b, row):
    for d in (1, 2, 4):
        a_prev = pltpu.roll(a, d, axis=0)
        b_prev = pltpu.roll(b, d, axis=0)
        keep = row >= d
        b = jnp.where(keep, a * b_prev + b, b)
        a = jnp.where(keep, a * a_prev, a)
    return a, b


def _scan_tile(a, b, h_in, row):
    a, b = _tile_prefix(a, b, row)
    return a * h_in + b


def _last_row(h):
    return jnp.broadcast_to(h[SUBLANES - 1:SUBLANES, :], h.shape)


def _conv_taps(x, shifted, cw, cb):
    out = cb + shifted[CONV_W - 1] * cw[0:1]
    for j in range(1, CONV_W - 1):
        out = out + shifted[CONV_W - 1 - j] * cw[j:j + 1]
    return out + x * cw[CONV_W - 1:CONV_W]


def _rec_prompt_body(u_ref, wxl_ref, wxh_ref, wgl_ref, wgh_ref, prev_ref, h0_ref, cw_ref, cb_ref, wg_ref, ba_ref,
                     bx_ref, lam_ref, g_ref, o_ref, lru_ref, tail_ref, wb_ref, xprev_ref, a_ref, b_ref, gate_ref,
                     carry_ref, *, tc, n_chunks):
    c = pl.program_id(1)

    @pl.when((pl.program_id(0) == 0) & (c == 0))
    def _():
        for i, w_ref in enumerate((wxl_ref, wxh_ref, wgl_ref, wgh_ref)):
            wb_ref[:, i * LRU_HALF:(i + 1) * LRU_HALF] = w_ref[...].astype(bf16)

    @pl.when(c == 0)
    def _():
        xprev_ref[...] = prev_ref[...]
        for n in range(N_LRU_BLOCKS):
            carry_ref[n] = jnp.broadcast_to(h0_ref[:, n * LRU_BLOCK:(n + 1) * LRU_BLOCK], (SUBLANES, LRU_BLOCK))

    u = u_ref[...]
    pair = 2 * LRU_BLOCK
    n_pairs = LRU_W // pair

    def project(branch, p):
        c0 = branch * LRU_W + p * pair
        return jnp.dot(u, wb_ref[:, c0:c0 + pair], preferred_element_type=f32)

    def gates(p, x):
        lanes = slice(p * pair, (p + 1) * pair)
        big = jnp.concatenate([xprev_ref[:, lanes], x], axis=0)
        shifted = {k: pltpu.roll(big, k, axis=0)[SUBLANES:] for k in range(1, CONV_W)}
        xc = _conv_taps(x, shifted, cw_ref[:, lanes], cb_ref[:, lanes])
        for q in range(2):
            n = 2 * p + q
            sub = slice(q * LRU_BLOCK, (q + 1) * LRU_BLOCK)
            a, gated, mult = _lru_gates(xc[:, sub], n, n * LRU_BLOCK, wg_ref, ba_ref, bx_ref, lam_ref)
            b = gated * mult
            a_ref[n] = a
            b_ref[n] = b
            b_ref[n, 0:1, :] = jnp.where(c == 0, gated[0:1, :], b[0:1, :])
        return x[tc - SUBLANES:, :]

    def gelu_gate(p, gate):
        for q in range(2):
            gate_ref[2 * p + q] = jax.nn.gelu(gate[:, q * LRU_BLOCK:(q + 1) * LRU_BLOCK])

    steps = [(gates, 0, p) for p in range(n_pairs)] + [(gelu_gate, 1, p) for p in range(n_pairs)]
    tails = []
    ahead = n_pairs
    pending = [project(halves, p) for _, halves, p in steps[:ahead]]
    for i, (consume, _, p) in enumerate(steps):
        if i + ahead < len(steps):
            pending.append(project(steps[i + ahead][1], steps[i + ahead][2]))
        out = consume(p, pending.pop(0))
        if consume is gates:
            tails.append(out)
    tail = jnp.concatenate(tails, axis=1)
    xprev_ref[...] = tail
    tail_ref[...] = tail

    seg = tc // SUBLANES
    seg_rows = lambda r: pl.ds(r, SUBLANES, stride=seg)

    def local_scan(r, maps):
        out = []
        for n in range(N_LRU_BLOCKS):
            a = a_ref[n, seg_rows(r), :]
            a_cum = a * maps[2 * n]
            b_cum = a * maps[2 * n + 1] + b_ref[n, seg_rows(r), :]
            a_ref[n, seg_rows(r), :] = a_cum
            b_ref[n, seg_rows(r), :] = b_cum
            out += [a_cum, b_cum]
        return tuple(out)
    identity = (jnp.ones((SUBLANES, LRU_BLOCK), f32), jnp.zeros((SUBLANES, LRU_BLOCK), f32)) * N_LRU_BLOCKS
    seg_maps = lax.fori_loop(0, seg, local_scan, identity, unroll=2)

    row = lax.broadcasted_iota(jnp.int32, (SUBLANES, LRU_BLOCK), 0)
    h_in = []
    for n in range(N_LRU_BLOCKS):
        h_prev = carry_ref[n]
        a_cum, b_cum = _tile_prefix(seg_maps[2 * n], seg_maps[2 * n + 1], row)
        h_end = a_cum * h_prev + b_cum
        h_in.append(jnp.where(row == 0, h_prev, pltpu.roll(h_end, 1, axis=0)))
        carry_ref[n] = _last_row(h_end)

    def apply_scan(r, _):
        for n in range(N_LRU_BLOCKS):
            h = a_ref[n, seg_rows(r), :] * h_in[n] + b_ref[n, seg_rows(r), :]
            b_ref[n, seg_rows(r), :] = h * gate_ref[n, seg_rows(r), :]
        return 0
    lax.fori_loop(0, seg, apply_scan, 0, unroll=2)

    y = jnp.concatenate([b_ref[n] for n in range(N_LRU_BLOCKS)], axis=1)
    o_ref[...] = _rmsnorm(y, g_ref[...]).astype(o_ref.dtype)

    @pl.when(c == n_chunks - 1)
    def _():
        lru_ref[...] = jnp.concatenate([carry_ref[n] for n in range(N_LRU_BLOCKS)], axis=1)


def _rec_prompt(u, w_in, prev8, h0, cw, cb, wg, ba, bx, lam, g, l, *, n_batch, t_len, tc):
    n_chunks = t_len // tc
    once = pl.Buffered(1)
    row_vec = pl.BlockSpec((None, 1, LRU_W), lambda b, c: (l, 0, 0), pipeline_mode=once)
    w_half = lambda col: pl.BlockSpec((None, D_MODEL, LRU_HALF), lambda b, c: (l, 0, col // LRU_HALF),
                                      pipeline_mode=once)
    state = pl.BlockSpec((None, SUBLANES, LRU_W), lambda b, c: (b, 0, 0))
    blocks = tc * D_MODEL * 2 + tc * LRU_W * 2 + 3 * SUBLANES * LRU_W * 4
    scratch = (4 * D_MODEL * LRU_HALF * (4 + 2) + wg[0].size * 2 + (3 * tc + 2 * SUBLANES) * LRU_W * 4
               + 8 * tc * 2 * LRU_BLOCK * 4)
    by_block = pltpu.VMEM((N_LRU_BLOCKS, tc, LRU_BLOCK), f32)
    assert tc % SUBLANES == 0
    return pl.pallas_call(
        functools.partial(_rec_prompt_body, tc=tc, n_chunks=n_chunks),
        grid=(n_batch, n_chunks),
        in_specs=[pl.BlockSpec((tc, D_MODEL), lambda b, c: (b * n_chunks + c, 0)),
                  w_half(Z_XR), w_half(Z_XR + LRU_HALF), w_half(Z_GR), w_half(Z_GR + LRU_HALF),
                  state,
                  pl.BlockSpec((None, 1, LRU_W), lambda b, c: (b, 0, 0)),
                  pl.BlockSpec((None, CONV_W, LRU_W), lambda b, c: (l, 0, 0), pipeline_mode=once),
                  row_vec,
                  pl.BlockSpec((None,) + wg.shape[1:], lambda b, c: (l, 0, 0, 0), pipeline_mode=once),
                  row_vec, row_vec, row_vec, row_vec],
        out_specs=[pl.BlockSpec((tc, LRU_W), lambda b, c: (b * n_chunks + c, 0)), state, state],
        out_shape=[jax.ShapeDtypeStruct((n_batch * t_len, LRU_W), bf16),
                   jax.ShapeDtypeStruct((n_batch, SUBLANES, LRU_W), f32),
                   jax.ShapeDtypeStruct((n_batch, SUBLANES, LRU_W), f32)],
        scratch_shapes=[pltpu.VMEM((D_MODEL, 2 * LRU_W), bf16),
                        pltpu.VMEM((SUBLANES, LRU_W), f32), by_block, by_block, by_block,
                        pltpu.VMEM((N_LRU_BLOCKS, SUBLANES, LRU_BLOCK), f32)],
        compiler_params=_params(("arbitrary", "arbitrary"), _vmem_limit(blocks, scratch)),
        name="rec_prompt",
    )(u, w_in, w_in, w_in, w_in, prev8, h0, cw, cb, wg, ba, bx, lam, g)


def _rec_sample_body(z_ref, prev_ref, h0_ref, cw_ref, cb_ref, wg_ref, ba_ref, bx_ref, lam_ref, g_ref,
                     o_ref, lru_ref, xc_ref, a_ref, b_ref, *, n_batch):
    row = lax.broadcasted_iota(jnp.int32, (SUBLANES, LRU_W), 0)

    def conv_step(b, _):
        r = pl.multiple_of(b * SUBLANES, SUBLANES)
        x = z_ref[pl.ds(r, SUBLANES), Z_XR:Z_XR + LRU_W]
        prev = prev_ref[pl.ds(r, SUBLANES), :]
        shifted = {k: jnp.where(row >= k, pltpu.roll(x, k, axis=0), pltpu.roll(prev, k, axis=0))
                   for k in range(1, CONV_W)}
        xc_ref[pl.ds(r, SUBLANES), :] = _conv_taps(x, shifted, cw_ref[...], cb_ref[...])
        return 0
    lax.fori_loop(0, n_batch, conv_step, 0)

    def gates_step(n, _):
        c0 = pl.multiple_of(n * LRU_BLOCK, LRU_BLOCK)
        lanes = pl.ds(c0, LRU_BLOCK)
        a, gated, mult = _lru_gates(xc_ref[:, lanes], n, c0, wg_ref, ba_ref, bx_ref, lam_ref)
        a_ref[:, lanes] = a
        b_ref[:, lanes] = gated * mult
        return 0
    lax.fori_loop(0, N_LRU_BLOCKS, gates_step, 0)

    def scan_step(b, _):
        rows = pl.ds(pl.multiple_of(b * SUBLANES, SUBLANES), SUBLANES)
        h_in = jnp.broadcast_to(h0_ref[pl.ds(b, 1), :], (SUBLANES, LRU_W))
        hh = _scan_tile(a_ref[rows, :], b_ref[rows, :], h_in, row)
        b_ref[rows, :] = hh * jax.nn.gelu(z_ref[rows, Z_GR:Z_GR + LRU_W])
        lru_ref[pl.ds(b, 1), :] = hh[SUBLANES - 1:SUBLANES, :]
        return 0
    lax.fori_loop(0, n_batch, scan_step, 0)
    o_ref[...] = _rmsnorm(b_ref[...], g_ref[...]).astype(o_ref.dtype)


def _rec_sample(z, prev8, h0, cw, cb, wg, ba, bx, lam, g, l, *, n_batch, t_len):
    assert t_len == SUBLANES and PAST_LEN > 0
    rows = n_batch * t_len
    layer = lambda a: pl.BlockSpec((None,) + a.shape[1:], lambda i: (l,) + (0,) * (a.ndim - 1))
    blocks = z.size * 4 + (2 * rows + 2 * n_batch) * LRU_W * 4 + wg[0].size * 2
    return pl.pallas_call(
        functools.partial(_rec_sample_body, n_batch=n_batch),
        grid=(1,),
        in_specs=[pl.BlockSpec(z.shape, lambda i: (0, 0))] + [layer(a) for a in (prev8, h0, cw, cb, wg, ba, bx, lam, g)],
        out_specs=[pl.BlockSpec((rows, LRU_W), lambda i: (0, 0)), pl.BlockSpec((n_batch, LRU_W), lambda i: (0, 0))],
        out_shape=[jax.ShapeDtypeStruct((rows, LRU_W), bf16),
                   jax.ShapeDtypeStruct((n_batch, LRU_W), f32)],
        scratch_shapes=[pltpu.VMEM((rows, LRU_W), f32)] * 3,
        compiler_params=_params(("arbitrary",), _vmem_limit(blocks, 3 * rows * LRU_W * 4)),
        name="rec_sample",
    )(z, prev8, h0, cw, cb, wg, ba, bx, lam, g)


def _mix_mlp_body(h_ref, attn_ref, rec_ref, wo_ref, g_ref, wu_ref, wd_ref, o_ref, u_ref):
    @pl.when(pl.program_id(1) == 0)
    def _():
        o_ref[...] = h_ref[...] + jnp.dot(attn_ref[...], wo_ref[0:ATTN_W, :], preferred_element_type=f32)
        o_ref[...] += jnp.dot(rec_ref[...], wo_ref[ATTN_W:ATTN_W + LRU_W, :], preferred_element_type=f32)
        u_ref[...] = _rmsnorm(o_ref[...], g_ref[...]).astype(bf16)
    hid = jnp.dot(u_ref[...], wu_ref[...], preferred_element_type=f32)
    hid = jnp.square(jnp.maximum(hid, 0.0)).astype(bf16)
    o_ref[...] += jnp.dot(hid, wd_ref[...], preferred_element_type=f32)


def _mix_mlp(h, attn, rec, wo, g, wu, wd, l, *, tm, tf):
    m = h.shape[0]
    once = pl.Buffered(1)
    blocks = 2 * tm * D_MODEL * 4 + tm * (ATTN_W + LRU_W) * 2 + 2 * D_MODEL * tf * 2
    scratch = wo[0].size * 2 + tm * D_MODEL * 2 + tm * tf * 6
    return pl.pallas_call(
        _mix_mlp_body,
        grid=(m // tm, D_FF // tf),
        in_specs=[pl.BlockSpec((tm, D_MODEL), lambda i, f: (i, 0)),
                  pl.BlockSpec((tm, ATTN_W), lambda i, f: (i, 0)),
                  pl.BlockSpec((tm, LRU_W), lambda i, f: (i, 0)),
                  pl.BlockSpec((None,) + wo.shape[1:], lambda i, f: (l, 0, 0), pipeline_mode=once),
                  pl.BlockSpec((None, 1, D_MODEL), lambda i, f: (l, 0, 0), pipeline_mode=once),
                  pl.BlockSpec((None, D_MODEL, tf), lambda i, f: (l, 0, f)),
                  pl.BlockSpec((None, tf, D_MODEL), lambda i, f: (l, f, 0))],
        out_specs=pl.BlockSpec((tm, D_MODEL), lambda i, f: (i, 0)),
        out_shape=jax.ShapeDtypeStruct((m, D_MODEL), f32),
        scratch_shapes=[pltpu.VMEM((tm, D_MODEL), bf16)],
        compiler_params=_params(("parallel", "arbitrary"), _vmem_limit(blocks, scratch)),
        name="mix_mlp",
    )(h, attn, rec, wo, g, wu, wd)


def _mix_mlp_cast_body(h_ref, attn_ref, rec_ref, wo_ref, g_ref, wu_ref, wd_ref,
                       o_ref, wob_ref, wub_ref, wdb_ref, u_ref):
    @pl.when(pl.program_id(1) == 0)
    def _():
        wob_ref[...] = wo_ref[...].astype(bf16)
        o_ref[...] = h_ref[...] + jnp.dot(attn_ref[...], wob_ref[0:ATTN_W, :], preferred_element_type=f32)
        o_ref[...] += jnp.dot(rec_ref[...], wob_ref[ATTN_W:ATTN_W + LRU_W, :], preferred_element_type=f32)
        u_ref[...] = _rmsnorm(o_ref[...], g_ref[...]).astype(bf16)
    wub_ref[...] = wu_ref[...].astype(bf16)
    wdb_ref[...] = wd_ref[...].astype(bf16)
    hid = jnp.dot(u_ref[...], wub_ref[...], preferred_element_type=f32)
    hid = jnp.square(jnp.maximum(hid, 0.0)).astype(bf16)
    o_ref[...] += jnp.dot(hid, wdb_ref[...], preferred_element_type=f32)


def _mix_mlp_cast(h, attn, rec, wo, g, wu, wd, l, *, tf):
    m = h.shape[0]
    once = pl.Buffered(1)
    blocks = 2 * D_MODEL * tf * (4 + 2)
    scratch = (2 * m * D_MODEL * 4 + m * (ATTN_W + LRU_W) * 2 + wo[0].size * (4 + 2) + m * D_MODEL * 2
               + m * tf * 6)
    return pl.pallas_call(
        _mix_mlp_cast_body,
        grid=(1, D_FF // tf),
        in_specs=[pl.BlockSpec((m, D_MODEL), lambda i, f: (0, 0), pipeline_mode=once),
                  pl.BlockSpec((m, ATTN_W), lambda i, f: (0, 0), pipeline_mode=once),
                  pl.BlockSpec((m, LRU_W), lambda i, f: (0, 0), pipeline_mode=once),
                  pl.BlockSpec((None,) + wo.shape[1:], lambda i, f: (l, 0, 0), pipeline_mode=once),
                  pl.BlockSpec((None, 1, D_MODEL), lambda i, f: (l, 0, 0), pipeline_mode=once),
                  pl.BlockSpec((None, D_MODEL, tf), lambda i, f: (l, 0, f)),
                  pl.BlockSpec((None, tf, D_MODEL), lambda i, f: (l, f, 0))],
        out_specs=[pl.BlockSpec((m, D_MODEL), lambda i, f: (0, 0)),
                   pl.BlockSpec(wo.shape[1:], lambda i, f: (0, 0)),
                   pl.BlockSpec((D_MODEL, tf), lambda i, f: (0, f)),
                   pl.BlockSpec((tf, D_MODEL), lambda i, f: (f, 0))],
        out_shape=[jax.ShapeDtypeStruct((m, D_MODEL), f32),
                   jax.ShapeDtypeStruct(wo.shape[1:], bf16),
                   jax.ShapeDtypeStruct(wu.shape[1:], bf16),
                   jax.ShapeDtypeStruct(wd.shape[1:], bf16)],
        scratch_shapes=[pltpu.VMEM((m, D_MODEL), bf16)],
        compiler_params=_params(("arbitrary", "arbitrary"), _vmem_limit(blocks, scratch)),
        name="mix_mlp_cast",
    )(h, attn, rec, wo, g, wu, wd)


def _final_norm_body(h_ref, g_ref, o_ref):
    o_ref[0] = _rmsnorm(h_ref[0], g_ref[...])


def _final_norm(h, g, *, skip, tr):
    n_batch, t_len, _ = h.shape
    s_len = t_len - skip
    tiles = tr // SUBLANES
    skip_tiles = skip // SUBLANES
    h4 = h.reshape(n_batch, t_len // SUBLANES, SUBLANES, D_MODEL)
    out = pl.pallas_call(
        _final_norm_body,
        grid=(n_batch, s_len // tr),
        in_specs=[pl.BlockSpec((pl.Element(1), pl.Element(tiles), pl.Element(SUBLANES), pl.Element(D_MODEL)),
                               lambda b, r: (b, skip_tiles + r * tiles, 0, 0)),
                  pl.BlockSpec((1, D_MODEL), lambda b, r: (0, 0))],
        out_specs=pl.BlockSpec((1, tiles, SUBLANES, D_MODEL), lambda b, r: (b, r, 0, 0)),
        out_shape=jax.ShapeDtypeStruct((n_batch, s_len // SUBLANES, SUBLANES, D_MODEL), f32),
        compiler_params=_params(("parallel", "parallel"), _vmem_limit(2 * tr * D_MODEL * 4)),
        name="final_norm",
    )(h4, g)
    return out.reshape(n_batch, s_len, D_MODEL)


def _row_tile(m, candidates):
    for tm in candidates:
        if m % tm == 0:
            return tm
    raise ValueError(f"no row tile for {m} rows")


def kernel(x_prompt, x_sample, cache_k_win, cache_v_win, state_conv, state_lru, meta_tokens, norm_mix_g, w_in,
           conv_w, conv_b, w_gate_a, b_gate_a, w_gate_x, b_gate_x, lru_lambda, attn_sinks, rel_bias, attn_out_g,
           rec_out_g, w_out, norm_mlp_g, w_up, w_down, final_norm_g):
    n_p, s_p, _ = x_prompt.shape
    n_s, t_s, _ = x_sample.shape
    t_p = N_META + s_p
    buf = cache_k_win.shape[2]
    assert t_p % BF16_ROWS == 0 and buf == WINDOW

    w_gates = jnp.concatenate([w_gate_a, w_gate_x], axis=-1).astype(bf16)
    rows3 = lambda p: p[:, None, :]
    g_mix, g_mlp, g_attn, g_rec = rows3(norm_mix_g), rows3(norm_mlp_g), rows3(attn_out_g), rows3(rec_out_g)
    cb3, ba3, bx3, lam3 = rows3(conv_b), rows3(b_gate_a), rows3(b_gate_x), rows3(lru_lambda)
    g_attn_t = jnp.broadcast_to(
        attn_out_g.reshape(DEPTH, N_Q_HEADS, HEAD_DIM).transpose(0, 2, 1)[..., None],
        (DEPTH, HEAD_DIM, N_Q_HEADS, WINDOW)).reshape(DEPTH, HEAD_DIM, ATTN_W)

    meta = jnp.broadcast_to(meta_tokens.astype(x_prompt.dtype)[None], (n_p, N_META, D_MODEL))
    hp = jnp.concatenate([meta, x_prompt], axis=1).reshape(n_p * t_p, D_MODEL)
    hs = x_sample.reshape(n_s * t_s, D_MODEL)

    bias_qk, bias_kq = _bias_tables(rel_bias)
    bias_qk = bias_qk.reshape(N_Q_HEADS * WINDOW, 2 * WINDOW)
    prev_p = jnp.zeros((n_p, SUBLANES, LRU_W), f32)
    h0_p = jnp.zeros((n_p, 1, LRU_W), f32)
    ck = cache_k_win.reshape(DEPTH, n_s, buf * N_KV_HEADS, HEAD_DIM)
    cv = cache_v_win.reshape(DEPTH, n_s, buf * N_KV_HEADS, HEAD_DIM)
    prev_s = jnp.pad(state_conv, ((0, 0), (0, 0), (SUBLANES - (CONV_W - 1), 0), (0, 0)))
    prev_s = prev_s.reshape(DEPTH, n_s * SUBLANES, LRU_W)

    tm_p = _row_tile(n_p * t_p, (688, 344))
    tm_s = n_s * t_s
    tc_p = _row_tile(t_p, (688, 344, 48, 16))
    tn = IN_W // 2

    kp_l, vp_l, cp_l, lp_l, ks_l, vs_l, cs_l, ls_l = ([] for _ in range(8))
    for l in range(DEPTH):
        z = _inproj(hs, g_mix, w_in, l, tm=tm_s, tn=tn)
        attn, nk, nv = _attn_sample(z, ck, cv, bias_qk, attn_sinks, g_attn, l, n_batch=n_s, t_len=t_s)
        rec, lru = _rec_sample(z, prev_s, state_lru, conv_w, cb3, w_gates, ba3, bx3, lam3, g_rec, l,
                               n_batch=n_s, t_len=t_s)
        hs, w_out_b, w_up_b, w_down_b = _mix_mlp_cast(hs, attn, rec, w_out, g_mlp, w_up, w_down, l, tf=512)
        z3 = z.reshape(n_s, t_s, IN_W)
        ks_l.append(nk)
        vs_l.append(nv)
        cs_l.append(z3[:, t_s - (CONV_W - 1):, Z_XR:Z_XR + LRU_W])
        ls_l.append(lru)
        z, u = _inproj_qkv(hp, g_mix, w_in, l, tm=tm_p)
        attn = _attn_prompt(z, bias_kq, attn_sinks, g_attn_t, l, n_batch=n_p, t_len=t_p)
        rec, lru, x_tail = _rec_prompt(u, w_in, prev_p, h0_p, conv_w, cb3, w_gates, ba3, bx3, lam3, g_rec, l,
                                       n_batch=n_p, t_len=t_p, tc=tc_p)
        hp = _mix_mlp(hp, attn, rec, w_out_b[None], g_mlp[l:l + 1], w_up_b[None], w_down_b[None], 0,
                      tm=tm_p, tf=1024)
        z3 = z.reshape(n_p, t_p, Z_XR)
        kp_l.append(z3[:, t_p - WINDOW:, Z_K:Z_K + KV_W].reshape(n_p, WINDOW, N_KV_HEADS, HEAD_DIM))
        vp_l.append(z3[:, t_p - WINDOW:, Z_V:Z_V + KV_W].reshape(n_p, WINDOW, N_KV_HEADS, HEAD_DIM))
        cp_l.append(x_tail[:, SUBLANES - (CONV_W - 1):])
        lp_l.append(lru[:, 0])

    g_fin = final_norm_g[None, :]
    y_prompt = _final_norm(hp.reshape(n_p, t_p, D_MODEL), g_fin, skip=N_META, tr=256)
    y_sample = _final_norm(hs.reshape(1, n_s * t_s, D_MODEL), g_fin, skip=0, tr=n_s * t_s)
    y_sample = y_sample.reshape(n_s, t_s, D_MODEL)
    cache_shape = (DEPTH, n_s, buf, N_KV_HEADS, HEAD_DIM)
    return (y_prompt, y_sample,
            jnp.stack(kp_l), jnp.stack(vp_l), jnp.stack(cp_l), jnp.stack(lp_l),
            jnp.stack(ks_l).reshape(cache_shape), jnp.stack(vs_l).reshape(cache_shape),
            jnp.stack(cs_l), jnp.stack(ls_l))
```

```python
import functools
import math

import jax
import jax.numpy as jnp
from jax import lax
from jax.experimental import pallas as pl
from jax.experimental.pallas import tpu as pltpu

f32 = jnp.float32
bf16 = jnp.bfloat16

D_MODEL = 2048
DEPTH = 4
PAST_LEN = 16384
HEAD_DIM = 128
N_Q_HEADS = 8
N_KV_HEADS = 2
Q_PER_KV = N_Q_HEADS // N_KV_HEADS
ATTN_W = N_Q_HEADS * HEAD_DIM
KV_W = N_KV_HEADS * HEAD_DIM
LRU_W = D_MODEL // 2
N_LRU_BLOCKS = 8
LRU_BLOCK = LRU_W // N_LRU_BLOCKS
CONV_W = 4
LRU_C = 8.0
IN_W = ATTN_W + 2 * KV_W + 2 * LRU_W
D_FF = 4 * D_MODEL
WINDOW = 128
N_BUCKETS = 32
MAX_DISTANCE = 128
N_META = 16
EPS = 1e-6
ATTN_SCALE = HEAD_DIM ** -0.5
INV_ATTN_SCALE = HEAD_DIM ** 0.5
EXP2_PER_T = math.log2(math.e) / INV_ATTN_SCALE

Z_Q = 0
Z_K = ATTN_W
Z_V = Z_K + KV_W
Z_XR = Z_V + KV_W
Z_GR = Z_XR + LRU_W
LRU_HALF = LRU_W // 2

ATTN_LEAD = WINDOW + (-N_META) % WINDOW

SUBLANES = 8
BF16_ROWS = 16
V7X_VMEM_BYTES = 64 * 1024 * 1024
VMEM_CAP_BYTES = V7X_VMEM_BYTES - 2 * 1024 * 1024


def _vmem_limit(pipelined_bytes, scratch_bytes=0):
    est = 2 * pipelined_bytes + scratch_bytes
    return int(min(VMEM_CAP_BYTES, est + est // 2 + (8 << 20)))


def _params(semantics, vmem_bytes):
    return pltpu.CompilerParams(dimension_semantics=semantics, vmem_limit_bytes=vmem_bytes)


def _rms_scale(x):
    return lax.rsqrt(jnp.mean(x * x, axis=-1, keepdims=True) + EPS)


def _rmsnorm(x, g):
    return x * _rms_scale(x) * g


def _inproj_body(h_ref, g_ref, w_ref, z_ref, u_ref):
    @pl.when(pl.program_id(1) == 0)
    def _():
        u_ref[...] = _rmsnorm(h_ref[...], g_ref[...]).astype(bf16)
    z_ref[...] = jnp.dot(u_ref[...], w_ref[...].astype(bf16), preferred_element_type=f32)


def _inproj(h, g, w, l, *, tm, tn):
    m = h.shape[0]
    blocks = tm * D_MODEL * 4 + D_MODEL * tn * 4 + tm * tn * 4 + D_MODEL * tn
    return pl.pallas_call(
        _inproj_body,
        grid=(m // tm, IN_W // tn),
        in_specs=[pl.BlockSpec((tm, D_MODEL), lambda i, j: (i, 0)),
                  pl.BlockSpec((None, 1, D_MODEL), lambda i, j: (l, 0, 0)),
                  pl.BlockSpec((None, D_MODEL, tn), lambda i, j: (l, 0, j))],
        out_specs=pl.BlockSpec((tm, tn), lambda i, j: (i, j)),
        out_shape=jax.ShapeDtypeStruct((m, IN_W), f32),
        scratch_shapes=[pltpu.VMEM((tm, D_MODEL), bf16)],
        compiler_params=_params(("parallel", "arbitrary"), _vmem_limit(blocks, tm * D_MODEL * 2)),
        name="inproj",
    )(h, g, w)


def _inproj_qkv_body(h_ref, g_ref, w_ref, z_ref, u_ref, wb_ref):
    @pl.when(pl.program_id(0) == 0)
    def _():
        wb_ref[...] = w_ref[...].astype(bf16)
    u_ref[...] = _rmsnorm(h_ref[...], g_ref[...]).astype(bf16)
    z_ref[...] = jnp.dot(u_ref[...], wb_ref[...], preferred_element_type=f32)


def _inproj_qkv(h, g, w, l, *, tm):
    m = h.shape[0]
    once = pl.Buffered(1)
    blocks = tm * D_MODEL * 4 + tm * Z_XR * 4 + tm * D_MODEL * 2
    return pl.pallas_call(
        _inproj_qkv_body,
        grid=(m // tm,),
        in_specs=[pl.BlockSpec((tm, D_MODEL), lambda i: (i, 0)),
                  pl.BlockSpec((None, 1, D_MODEL), lambda i: (l, 0, 0), pipeline_mode=once),
                  pl.BlockSpec((None, D_MODEL, Z_XR), lambda i: (l, 0, 0), pipeline_mode=once)],
        out_specs=[pl.BlockSpec((tm, Z_XR), lambda i: (i, 0)),
                   pl.BlockSpec((tm, D_MODEL), lambda i: (i, 0))],
        out_shape=[jax.ShapeDtypeStruct((m, Z_XR), f32), jax.ShapeDtypeStruct((m, D_MODEL), bf16)],
        scratch_shapes=[pltpu.VMEM((D_MODEL, Z_XR), bf16)],
        compiler_params=_params(("arbitrary",), _vmem_limit(blocks, D_MODEL * Z_XR * 6)),
        name="inproj_qkv",
    )(h, g, w)


def _rel_bias_of(dist, rel_ref, h):
    n = jnp.maximum(dist, 0)
    max_exact = N_BUCKETS // 2
    nf = jnp.maximum(n, 1).astype(f32)
    large = max_exact + (jnp.log(nf / max_exact) / math.log(MAX_DISTANCE / max_exact)
                         * (N_BUCKETS - max_exact)).astype(jnp.int32)
    large = jnp.minimum(large, N_BUCKETS - 1)
    bucket = jnp.where(n < max_exact, n, large)
    acc = jnp.zeros(dist.shape, f32)
    for b in range(N_BUCKETS):
        acc = jnp.where(bucket == b, rel_ref[b, h], acc)
    return jnp.where((dist >= 0) & (dist < WINDOW), acc, -jnp.inf)


def _bias_table_body(rel_ref, qk_ref, kq_ref):
    shape_qk = (WINDOW, 2 * WINDOW)
    dist_qk = (lax.broadcasted_iota(jnp.int32, shape_qk, 0) + WINDOW
               - lax.broadcasted_iota(jnp.int32, shape_qk, 1))
    shape_kq = (2 * WINDOW, WINDOW)
    dist_kq = (lax.broadcasted_iota(jnp.int32, shape_kq, 1) + WINDOW
               - lax.broadcasted_iota(jnp.int32, shape_kq, 0))
    for h in range(N_Q_HEADS):
        kv, g = divmod(h, Q_PER_KV)
        qk_ref[h] = _rel_bias_of(dist_qk, rel_ref, h)
        kq_ref[kv, :, g * WINDOW:(g + 1) * WINDOW] = _rel_bias_of(dist_kq, rel_ref, h) * INV_ATTN_SCALE


def _bias_tables(rel_bias):
    return pl.pallas_call(
        _bias_table_body,
        in_specs=[pl.BlockSpec(memory_space=pltpu.SMEM)],
        out_shape=[jax.ShapeDtypeStruct((N_Q_HEADS, WINDOW, 2 * WINDOW), f32),
                   jax.ShapeDtypeStruct((N_KV_HEADS, 2 * WINDOW, Q_PER_KV * WINDOW), f32)],
        name="bias_tables",
    )(rel_bias)


def _attn_blocks_t(blocks, kpad_ref, vt_ref, bias_ref, sinks_ref, l, gt_ref):
    kv_heads = [range(kv * Q_PER_KV, (kv + 1) * Q_PER_KV) for kv in range(N_KV_HEADS)]
    sink = [jnp.concatenate([jnp.full((1, WINDOW), sinks_ref[l, h] * INV_ATTN_SCALE, f32) for h in heads], axis=1)
            for heads in kv_heads]
    chains = [(i, kv) for i in range(len(blocks)) for kv in range(N_KV_HEADS)]
    scores = {}
    for i, kv in chains:
        q_rows, r0, _ = blocks[i]
        kwin = kpad_ref[pl.ds(r0, 2 * WINDOW), kv * HEAD_DIM:(kv + 1) * HEAD_DIM]
        q = jnp.concatenate([q_rows[:, h * HEAD_DIM:(h + 1) * HEAD_DIM] for h in kv_heads[kv]],
                            axis=0).astype(bf16)
        scores[i, kv] = lax.dot_general(kwin, q, (((1,), (1,)), ((), ())), preferred_element_type=f32)
    probs, invs = {}, {}
    for i, kv in chains:
        t = scores[i, kv] + bias_ref[kv]
        lead_keys = blocks[i][2]
        if lead_keys:
            key = lax.broadcasted_iota(jnp.int32, t.shape, 0)
            t = jnp.where(key < lead_keys, -jnp.inf, t)
        m = jnp.maximum(jnp.max(t, axis=0, keepdims=True), sink[kv])
        p = jnp.exp2((t - m) * EXP2_PER_T)
        invs[i, kv] = 1.0 / (jnp.sum(p, axis=0, keepdims=True) + jnp.exp2((sink[kv] - m) * EXP2_PER_T))
        probs[i, kv] = p.astype(bf16)
    raws = {}
    for i, kv in chains:
        vt = vt_ref[kv * HEAD_DIM:(kv + 1) * HEAD_DIM, pl.ds(blocks[i][1], 2 * WINDOW)]
        raws[i, kv] = jnp.dot(vt, probs[i, kv], preferred_element_type=f32)
    outs = []
    for i in range(len(blocks)):
        raw = jnp.concatenate([raws[i, kv] for kv in range(N_KV_HEADS)], axis=1)
        inv = jnp.concatenate([invs[i, kv] for kv in range(N_KV_HEADS)], axis=1)
        sq = jnp.sum(raw * raw, axis=0, keepdims=True) * (inv * inv)
        ssq = sq[:, 0:WINDOW]
        for h in range(1, N_Q_HEADS):
            ssq = ssq + sq[:, h * WINDOW:(h + 1) * WINDOW]
        r = lax.rsqrt(ssq * (1.0 / ATTN_W) + EPS)
        yt = raw * (inv * jnp.concatenate([r] * N_Q_HEADS, axis=1)) * gt_ref[...]
        outs.append(jnp.concatenate([yt[:, h * WINDOW:(h + 1) * WINDOW].T for h in range(N_Q_HEADS)], axis=1))
    return outs


ATTN_GROUP = 5


def _attn_prompt_body(q_ref, k_ref, v_ref, bias_ref, sinks_ref, gt_ref, o_ref, kpad_ref, vpad_ref, vt_ref,
                      *, t_len, l):
    pad_rows = kpad_ref.shape[0]
    kpad_ref[0:ATTN_LEAD, :] = jnp.zeros((ATTN_LEAD, KV_W), bf16)
    kpad_ref[ATTN_LEAD:pad_rows, :] = k_ref[...].astype(bf16)
    vpad_ref[0:ATTN_LEAD, :] = jnp.zeros((ATTN_LEAD, KV_W), f32)
    vpad_ref[ATTN_LEAD:pad_rows, :] = v_ref[...]

    def transpose_step(c, _):
        r = pl.multiple_of(c * WINDOW, WINDOW)
        chunk = vpad_ref[pl.ds(r, WINDOW), :]
        for kv in range(N_KV_HEADS):
            vt_ref[kv * HEAD_DIM:(kv + 1) * HEAD_DIM, pl.ds(r, WINDOW)] = (
                chunk[:, kv * HEAD_DIM:(kv + 1) * HEAD_DIM].T.astype(bf16))
        return 0
    lax.fori_loop(0, pad_rows // WINDOW, transpose_step, 0, unroll=3)

    def q_rows_of(j):
        start = j * WINDOW - (ATTN_LEAD - WINDOW)
        return pl.ds(start if isinstance(j, int) else pl.multiple_of(start, BF16_ROWS), WINDOW)

    n_first = 2 * WINDOW - ATTN_LEAD
    q0 = jnp.concatenate([jnp.zeros((WINDOW - n_first, ATTN_W), f32), q_ref[0:n_first, :]], axis=0)
    y0, y1 = _attn_blocks_t([(q0, 0, ATTN_LEAD), (q_ref[q_rows_of(1), :], WINDOW, ATTN_LEAD - WINDOW)],
                            kpad_ref, vt_ref, bias_ref, sinks_ref, l, gt_ref)
    o_ref[0:n_first, :] = y0[WINDOW - n_first:, :].astype(o_ref.dtype)
    o_ref[q_rows_of(1), :] = y1.astype(o_ref.dtype)

    def step(i, _):
        js = [2 + i * ATTN_GROUP + g for g in range(ATTN_GROUP)]
        ys = _attn_blocks_t([(q_ref[q_rows_of(j), :], pl.multiple_of(j * WINDOW, WINDOW), 0) for j in js],
                            kpad_ref, vt_ref, bias_ref, sinks_ref, l, gt_ref)
        for j, y in zip(js, ys):
            o_ref[q_rows_of(j), :] = y.astype(o_ref.dtype)
        return 0
    n_blocks = pad_rows // WINDOW - 1
    assert (n_blocks - 2) % ATTN_GROUP == 0
    lax.fori_loop(0, (n_blocks - 2) // ATTN_GROUP, step, 0)


def _attn_prompt(z, bias_kq, sinks, gt, l, *, n_batch, t_len):
    pad_rows = ATTN_LEAD + t_len
    assert pad_rows % WINDOW == 0 and (ATTN_LEAD - WINDOW) % BF16_ROWS == 0
    blocks = t_len * (ATTN_W + 2 * KV_W) * 4 + t_len * ATTN_W * 2 + bias_kq.size * 4 + HEAD_DIM * ATTN_W * 4
    scratch = pad_rows * KV_W * (2 + 4 + 2)
    return pl.pallas_call(
        functools.partial(_attn_prompt_body, t_len=t_len, l=l),
        grid=(n_batch,),
        in_specs=[pl.BlockSpec((t_len, ATTN_W), lambda b: (b, Z_Q // ATTN_W)),
                  pl.BlockSpec((t_len, KV_W), lambda b: (b, Z_K // KV_W)),
                  pl.BlockSpec((t_len, KV_W), lambda b: (b, Z_V // KV_W)),
                  pl.BlockSpec(bias_kq.shape, lambda b: (0, 0, 0)),
                  pl.BlockSpec(memory_space=pltpu.SMEM),
                  pl.BlockSpec((None, HEAD_DIM, ATTN_W), lambda b: (l, 0, 0))],
        out_specs=pl.BlockSpec((t_len, ATTN_W), lambda b: (b, 0)),
        out_shape=jax.ShapeDtypeStruct((n_batch * t_len, ATTN_W), bf16),
        scratch_shapes=[pltpu.VMEM((pad_rows, KV_W), bf16), pltpu.VMEM((pad_rows, KV_W), f32),
                        pltpu.VMEM((KV_W, pad_rows), bf16)],
        compiler_params=_params(("parallel",), _vmem_limit(blocks, scratch)),
        name="attn_prompt",
    )(z, z, z, bias_kq, sinks, gt)


DECODE_GROUP = 8


def _attn_sample_body(z_ref, ck_ref, cv_ref, bias_ref, sinks_ref, g_ref, o_ref, nk_ref, nv_ref, acc_ref,
                      *, n_batch, t_len, buf, l):
    zeros = jnp.zeros((2 * WINDOW - buf - t_len, HEAD_DIM), f32)
    kept = N_KV_HEADS * (buf - t_len)
    kv_heads = [range(kv * Q_PER_KV, (kv + 1) * Q_PER_KV) for kv in range(N_KV_HEADS)]
    bias = [jnp.concatenate([bias_ref[h * WINDOW:h * WINDOW + t_len, :] for h in heads], axis=0)
            for heads in kv_heads]
    sink = [jnp.concatenate([jnp.full((t_len, 1), sinks_ref[l, h], f32) for h in heads], axis=0)
            for heads in kv_heads]

    def window(c_ref, n_ref, b, r0, col, kv):
        new = z_ref[pl.ds(r0, t_len), col + kv * HEAD_DIM:col + (kv + 1) * HEAD_DIM]
        n_ref[b, pl.ds(kept + kv, t_len, stride=N_KV_HEADS), :] = new
        old = c_ref[b, pl.ds(kv, buf, stride=N_KV_HEADS), :]
        return jnp.concatenate([old, new, zeros], axis=0).astype(bf16)

    def group_step(i, _):
        seqs = [i * DECODE_GROUP + j for j in range(DECODE_GROUP)]
        rows = [pl.multiple_of(b * t_len, t_len) for b in seqs]
        scores = {}
        for j, (b, r0) in enumerate(zip(seqs, rows)):
            nk_ref[b, 0:kept, :] = ck_ref[b, N_KV_HEADS * t_len:N_KV_HEADS * buf, :]
            nv_ref[b, 0:kept, :] = cv_ref[b, N_KV_HEADS * t_len:N_KV_HEADS * buf, :]
            for kv, heads in enumerate(kv_heads):
                q = jnp.concatenate([z_ref[pl.ds(r0, t_len), Z_Q + h * HEAD_DIM:Z_Q + (h + 1) * HEAD_DIM]
                                     for h in heads], axis=0).astype(bf16)
                k = window(ck_ref, nk_ref, b, r0, Z_K, kv)
                scores[j, kv] = lax.dot_general(q, k, (((1,), (1,)), ((), ())), preferred_element_type=f32)
        probs, denoms = {}, {}
        for key, s in scores.items():
            kv = key[1]
            s = s * ATTN_SCALE + bias[kv]
            m = jnp.maximum(jnp.max(s, axis=-1, keepdims=True), sink[kv])
            p = jnp.exp(s - m)
            denoms[key] = jnp.sum(p, axis=-1, keepdims=True) + jnp.exp(sink[kv] - m)
            probs[key] = p.astype(bf16)
        for j, (b, r0) in enumerate(zip(seqs, rows)):
            outs = []
            for kv in range(N_KV_HEADS):
                v = window(cv_ref, nv_ref, b, r0, Z_V, kv)
                o = jnp.dot(probs[j, kv], v, preferred_element_type=f32) / denoms[j, kv]
                outs.extend(o[g * t_len:(g + 1) * t_len] for g in range(Q_PER_KV))
            acc_ref[pl.ds(r0, t_len), :] = jnp.concatenate(outs, axis=1)
        return 0
    lax.fori_loop(0, n_batch // DECODE_GROUP, group_step, 0)
    o_ref[...] = _rmsnorm(acc_ref[...], g_ref[...]).astype(o_ref.dtype)


def _attn_sample(z, ck, cv, bias_qk, sinks, g, l, *, n_batch, t_len):
    buf = ck.shape[2] // N_KV_HEADS
    assert buf == WINDOW and t_len == SUBLANES and n_batch % DECODE_GROUP == 0
    rows = n_batch * t_len
    whole =lambda a: pl.BlockSpec(a.shape, lambda i: (0,) * a.ndim)
    cache = pl.BlockSpec((None,) + ck.shape[1:], lambda i: (l, 0, 0, 0))
    blocks = (z.size + 4 * ck[0].size + bias_qk.size) * 4 + rows * ATTN_W * 2
    return pl.pallas_call(
        functools.partial(_attn_sample_body, n_batch=n_batch, t_len=t_len, buf=buf, l=l),
        grid=(1,),
        in_specs=[whole(z), cache, cache, whole(bias_qk), pl.BlockSpec(memory_space=pltpu.SMEM),
                  pl.BlockSpec((None, 1, ATTN_W), lambda i: (l, 0, 0))],
        out_specs=[pl.BlockSpec((rows, ATTN_W), lambda i: (0, 0)),
                   pl.BlockSpec(ck.shape[1:], lambda i: (0, 0, 0)),
                   pl.BlockSpec(cv.shape[1:], lambda i: (0, 0, 0))],
        out_shape=[jax.ShapeDtypeStruct((rows, ATTN_W), bf16),
                   jax.ShapeDtypeStruct(ck.shape[1:], f32),
                   jax.ShapeDtypeStruct(cv.shape[1:], f32)],
        scratch_shapes=[pltpu.VMEM((rows, ATTN_W), f32)],
        compiler_params=_params(("arbitrary",), _vmem_limit(blocks, rows * ATTN_W * 4)),
        name="attn_sample",
    )(z, ck, cv, bias_qk, sinks, g)


def _lru_gates(xc, n, c0, wg_ref, ba_ref, bx_ref, lam_ref):
    lanes = pl.ds(c0, LRU_BLOCK)
    gates = jnp.dot(xc.astype(bf16), wg_ref[n], preferred_element_type=f32)
    gate_a = jax.nn.sigmoid(gates[:, :LRU_BLOCK] + ba_ref[:, lanes])
    gate_x = jax.nn.sigmoid(gates[:, LRU_BLOCK:] + bx_ref[:, lanes])
    log_a = -LRU_C * gate_a * jax.nn.softplus(-lam_ref[:, lanes])
    a = jnp.exp(log_a)
    y = -jnp.tanh(log_a) * (1.0 + a * a)
    mult = jnp.where(y > 0.0, y * lax.rsqrt(y), 0.0)
    return a, xc * gate_x, mult


def _tile_prefix(a, b, row):
    for d in (1, 2, 4):
        a_prev = pltpu.roll(a, d, axis=0)
        b_prev = pltpu.roll(b, d, axis=0)
        keep = row >= d
        b = jnp.where(keep, a * b_prev + b, b)
        a = jnp.where(keep, a * a_prev, a)
    return a, b


def _scan_tile(a, b, h_in, row):
    a, b = _tile_prefix(a, b, row)
    return a * h_in + b


def _last_row(h):
    return jnp.broadcast_to(h[SUBLANES - 1:SUBLANES, :], h.shape)


def _conv_taps(x, shifted, cw, cb):
    out = cb + shifted[CONV_W - 1] * cw[0:1]
    for j in range(1, CONV_W - 1):
        out = out + shifted[CONV_W - 1 - j] * cw[j:j + 1]
    return out + x * cw[CONV_W - 1:CONV_W]


def _rec_prompt_body(u_ref, wxl_ref, wxh_ref, wgl_ref, wgh_ref, prev_ref, h0_ref, cw_ref, cb_ref, wg_ref, ba_ref,
                     bx_ref, lam_ref, g_ref, o_ref, lru_ref, tail_ref, wb_ref, xprev_ref, a_ref, b_ref, gate_ref,
                     carry_ref, *, tc, n_chunks):
    c = pl.program_id(1)

    @pl.when((pl.program_id(0) == 0) & (c == 0))
    def _():
        for i, w_ref in enumerate((wxl_ref, wxh_ref, wgl_ref, wgh_ref)):
            wb_ref[:, i * LRU_HALF:(i + 1) * LRU_HALF] = w_ref[...].astype(bf16)

    @pl.when(c == 0)
    def _():
        xprev_ref[...] = prev_ref[...]
        for n in range(N_LRU_BLOCKS):
            carry_ref[n] = jnp.broadcast_to(h0_ref[:, n * LRU_BLOCK:(n + 1) * LRU_BLOCK], (SUBLANES, LRU_BLOCK))

    u = u_ref[...]
    pair = 2 * LRU_BLOCK
    n_pairs = LRU_W // pair

    def project(branch, p):
        c0 = branch * LRU_W + p * pair
        return jnp.dot(u, wb_ref[:, c0:c0 + pair], preferred_element_type=f32)

    def gates(p, x):
        lanes = slice(p * pair, (p + 1) * pair)
        big = jnp.concatenate([xprev_ref[:, lanes], x], axis=0)
        shifted = {k: pltpu.roll(big, k, axis=0)[SUBLANES:] for k in range(1, CONV_W)}
        xc = _conv_taps(x, shifted, cw_ref[:, lanes], cb_ref[:, lanes])
        for q in range(2):
            n = 2 * p + q
            sub = slice(q * LRU_BLOCK, (q + 1) * LRU_BLOCK)
            a, gated, mult = _lru_gates(xc[:, sub], n, n * LRU_BLOCK, wg_ref, ba_ref, bx_ref, lam_ref)
            b = gated * mult
            a_ref[n] = a
            b_ref[n] = b
            b_ref[n, 0:1, :] = jnp.where(c == 0, gated[0:1, :], b[0:1, :])
        return x[tc - SUBLANES:, :]

    def gelu_gate(p, gate):
        for q in range(2):
            gate_ref[2 * p + q] = jax.nn.gelu(gate[:, q * LRU_BLOCK:(q + 1) * LRU_BLOCK])

    steps = [(gates, 0, p) for p in range(n_pairs)] + [(gelu_gate, 1, p) for p in range(n_pairs)]
    tails = []
    ahead = n_pairs
    pending = [project(halves, p) for _, halves, p in steps[:ahead]]
    for i, (consume, _, p) in enumerate(steps):
        if i + ahead < len(steps):
            pending.append(project(steps[i + ahead][1], steps[i + ahead][2]))
        out = consume(p, pending.pop(0))
        if consume is gates:
            tails.append(out)
    tail = jnp.concatenate(tails, axis=1)
    xprev_ref[...] = tail
    tail_ref[...] = tail

    seg = tc // SUBLANES
    seg_rows = lambda r: pl.ds(r, SUBLANES, stride=seg)

    def local_scan(r, maps):
        out = []
        for n in range(N_LRU_BLOCKS):
            a = a_ref[n, seg_rows(r), :]
            a_cum = a * maps[2 * n]
            b_cum = a * maps[2 * n + 1] + b_ref[n, seg_rows(r), :]
            a_ref[n, seg_rows(r), :] = a_cum
            b_ref[n, seg_rows(r), :] = b_cum
            out += [a_cum, b_cum]
        return tuple(out)
    identity = (jnp.ones((SUBLANES, LRU_BLOCK), f32), jnp.zeros((SUBLANES, LRU_BLOCK), f32)) * N_LRU_BLOCKS
    seg_maps = lax.fori_loop(0, seg, local_scan, identity, unroll=2)

    row = lax.broadcasted_iota(jnp.int32, (SUBLANES, LRU_BLOCK), 0)
    h_in = []
    for n in range(N_LRU_BLOCKS):
        h_prev = carry_ref[n]
        a_cum, b_cum = _tile_prefix(seg_maps[2 * n], seg_maps[2 * n + 1], row)
        h_end = a_cum * h_prev + b_cum
        h_in.append(jnp.where(row == 0, h_prev, pltpu.roll(h_end, 1, axis=0)))
        carry_ref[n] = _last_row(h_end)

    def apply_scan(r, _):
        for n in range(N_LRU_BLOCKS):
            h = a_ref[n, seg_rows(r), :] * h_in[n] + b_ref[n, seg_rows(r), :]
            b_ref[n, seg_rows(r), :] = h * gate_ref[n, seg_rows(r), :]
        return 0
    lax.fori_loop(0, seg, apply_scan, 0, unroll=2)

    y = jnp.concatenate([b_ref[n] for n in range(N_LRU_BLOCKS)], axis=1)
    o_ref[...] = _rmsnorm(y, g_ref[...]).astype(o_ref.dtype)

    @pl.when(c == n_chunks - 1)
    def _():
        lru_ref[...] = jnp.concatenate([carry_ref[n] for n in range(N_LRU_BLOCKS)], axis=1)


def _rec_prompt(u, w_in, prev8, h0, cw, cb, wg, ba, bx, lam, g, l, *, n_batch, t_len, tc):
    n_chunks = t_len // tc
    once = pl.Buffered(1)
    row_vec = pl.BlockSpec((None, 1, LRU_W), lambda b, c: (l, 0, 0), pipeline_mode=once)
    w_half = lambda col: pl.BlockSpec((None, D_MODEL, LRU_HALF), lambda b, c: (l, 0, col // LRU_HALF),
                                      pipeline_mode=once)
    state = pl.BlockSpec((None, SUBLANES, LRU_W), lambda b, c: (b, 0, 0))
    blocks = tc * D_MODEL * 2 + tc * LRU_W * 2 + 3 * SUBLANES * LRU_W * 4
    scratch = (4 * D_MODEL * LRU_HALF * (4 + 2) + wg[0].size * 2 + (3 * tc + 2 * SUBLANES) * LRU_W * 4
               + 8 * tc * 2 * LRU_BLOCK * 4)
    by_block = pltpu.VMEM((N_LRU_BLOCKS, tc, LRU_BLOCK), f32)
    assert tc % SUBLANES == 0
    return pl.pallas_call(
        functools.partial(_rec_prompt_body, tc=tc, n_chunks=n_chunks),
        grid=(n_batch, n_chunks),
        in_specs=[pl.BlockSpec((tc, D_MODEL), lambda b, c: (b * n_chunks + c, 0)),
                  w_half(Z_XR), w_half(Z_XR + LRU_HALF), w_half(Z_GR), w_half(Z_GR + LRU_HALF),
                  state,
                  pl.BlockSpec((None, 1, LRU_W), lambda b, c: (b, 0, 0)),
                  pl.BlockSpec((None, CONV_W, LRU_W), lambda b, c: (l, 0, 0), pipeline_mode=once),
                  row_vec,
                  pl.BlockSpec((None,) + wg.shape[1:], lambda b, c: (l, 0, 0, 0), pipeline_mode=once),
                  row_vec, row_vec, row_vec, row_vec],
        out_specs=[pl.BlockSpec((tc, LRU_W), lambda b, c: (b * n_chunks + c, 0)), state, state],
        out_shape=[jax.ShapeDtypeStruct((n_batch * t_len, LRU_W), bf16),
                   jax.ShapeDtypeStruct((n_batch, SUBLANES, LRU_W), f32),
                   jax.ShapeDtypeStruct((n_batch, SUBLANES, LRU_W), f32)],
        scratch_shapes=[pltpu.VMEM((D_MODEL, 2 * LRU_W), bf16),
                        pltpu.VMEM((SUBLANES, LRU_W), f32), by_block, by_block, by_block,
                        pltpu.VMEM((N_LRU_BLOCKS, SUBLANES, LRU_BLOCK), f32)],
        compiler_params=_params(("arbitrary", "arbitrary"), _vmem_limit(blocks, scratch)),
        name="rec_prompt",
    )(u, w_in, w_in, w_in, w_in, prev8, h0, cw, cb, wg, ba, bx, lam, g)


def _rec_sample_body(z_ref, prev_ref, h0_ref, cw_ref, cb_ref, wg_ref, ba_ref, bx_ref, lam_ref, g_ref,
                     o_ref, lru_ref, xc_ref, a_ref, b_ref, *, n_batch):
    row = lax.broadcasted_iota(jnp.int32, (SUBLANES, LRU_W), 0)

    def conv_step(b, _):
        r = pl.multiple_of(b * SUBLANES, SUBLANES)
        x = z_ref[pl.ds(r, SUBLANES), Z_XR:Z_XR + LRU_W]
        prev = prev_ref[pl.ds(r, SUBLANES), :]
        shifted = {k: jnp.where(row >= k, pltpu.roll(x, k, axis=0), pltpu.roll(prev, k, axis=0))
                   for k in range(1, CONV_W)}
        xc_ref[pl.ds(r, SUBLANES), :] = _conv_taps(x, shifted, cw_ref[...], cb_ref[...])
        return 0
    lax.fori_loop(0, n_batch, conv_step, 0)

    def gates_step(n, _):
        c0 = pl.multiple_of(n * LRU_BLOCK, LRU_BLOCK)
        lanes = pl.ds(c0, LRU_BLOCK)
        a, gated, mult = _lru_gates(xc_ref[:, lanes], n, c0, wg_ref, ba_ref, bx_ref, lam_ref)
        a_ref[:, lanes] = a
        b_ref[:, lanes] = gated * mult
        return 0
    lax.fori_loop(0, N_LRU_BLOCKS, gates_step, 0)

    def scan_step(b, _):
        rows = pl.ds(pl.multiple_of(b * SUBLANES, SUBLANES), SUBLANES)
        h_in = jnp.broadcast_to(h0_ref[pl.ds(b, 1), :], (SUBLANES, LRU_W))
        hh = _scan_tile(a_ref[rows, :], b_ref[rows, :], h_in, row)
        b_ref[rows, :] = hh * jax.nn.gelu(z_ref[rows, Z_GR:Z_GR + LRU_W])
        lru_ref[pl.ds(b, 1), :] = hh[SUBLANES - 1:SUBLANES, :]
        return 0
    lax.fori_loop(0, n_batch, scan_step, 0)
    o_ref[...] = _rmsnorm(b_ref[...], g_ref[...]).astype(o_ref.dtype)


def _rec_sample(z, prev8, h0, cw, cb, wg, ba, bx, lam, g, l, *, n_batch, t_len):
    assert t_len == SUBLANES and PAST_LEN > 0
    rows = n_batch * t_len
    layer = lambda a: pl.BlockSpec((None,) + a.shape[1:], lambda i: (l,) + (0,) * (a.ndim - 1))
    blocks = z.size * 4 + (2 * rows + 2 * n_batch) * LRU_W * 4 + wg[0].size * 2
    return pl.pallas_call(
        functools.partial(_rec_sample_body, n_batch=n_batch),
        grid=(1,),
        in_specs=[pl.BlockSpec(z.shape, lambda i: (0, 0))] + [layer(a) for a in (prev8, h0, cw, cb, wg, ba, bx, lam, g)],
        out_specs=[pl.BlockSpec((rows, LRU_W), lambda i: (0, 0)), pl.BlockSpec((n_batch, LRU_W), lambda i: (0, 0))],
        out_shape=[jax.ShapeDtypeStruct((rows, LRU_W), bf16),
                   jax.ShapeDtypeStruct((n_batch, LRU_W), f32)],
        scratch_shapes=[pltpu.VMEM((rows, LRU_W), f32)] * 3,
        compiler_params=_params(("arbitrary",), _vmem_limit(blocks, 3 * rows * LRU_W * 4)),
        name="rec_sample",
    )(z, prev8, h0, cw, cb, wg, ba, bx, lam, g)


def _mix_mlp_body(h_ref, attn_ref, rec_ref, wo_ref, g_ref, wu_ref, wd_ref, o_ref, u_ref):
    @pl.when(pl.program_id(1) == 0)
    def _():
        o_ref[...] = h_ref[...] + jnp.dot(attn_ref[...], wo_ref[0:ATTN_W, :], preferred_element_type=f32)
        o_ref[...] += jnp.dot(rec_ref[...], wo_ref[ATTN_W:ATTN_W + LRU_W, :], preferred_element_type=f32)
        u_ref[...] = _rmsnorm(o_ref[...], g_ref[...]).astype(bf16)
    hid = jnp.dot(u_ref[...], wu_ref[...], preferred_element_type=f32)
    hid = jnp.square(jnp.maximum(hid, 0.0)).astype(bf16)
    o_ref[...] += jnp.dot(hid, wd_ref[...], preferred_element_type=f32)


def _mix_mlp(h, attn, rec, wo, g, wu, wd, l, *, tm, tf):
    m = h.shape[0]
    once = pl.Buffered(1)
    blocks = 2 * tm * D_MODEL * 4 + tm * (ATTN_W + LRU_W) * 2 + 2 * D_MODEL * tf * 2
    scratch = wo[0].size * 2 + tm * D_MODEL * 2 + tm * tf * 6
    return pl.pallas_call(
        _mix_mlp_body,
        grid=(m // tm, D_FF // tf),
        in_specs=[pl.BlockSpec((tm, D_MODEL), lambda i, f: (i, 0)),
                  pl.BlockSpec((tm, ATTN_W), lambda i, f: (i, 0)),
                  pl.BlockSpec((tm, LRU_W), lambda i, f: (i, 0)),
                  pl.BlockSpec((None,) + wo.shape[1:], lambda i, f: (l, 0, 0), pipeline_mode=once),
                  pl.BlockSpec((None, 1, D_MODEL), lambda i, f: (l, 0, 0), pipeline_mode=once),
                  pl.BlockSpec((None, D_MODEL, tf), lambda i, f: (l, 0, f)),
                  pl.BlockSpec((None, tf, D_MODEL), lambda i, f: (l, f, 0))],
        out_specs=pl.BlockSpec((tm, D_MODEL), lambda i, f: (i, 0)),
        out_shape=jax.ShapeDtypeStruct((m, D_MODEL), f32),
        scratch_shapes=[pltpu.VMEM((tm, D_MODEL), bf16)],
        compiler_params=_params(("parallel", "arbitrary"), _vmem_limit(blocks, scratch)),
        name="mix_mlp",
    )(h, attn, rec, wo, g, wu, wd)


def _mix_mlp_cast_body(h_ref, attn_ref, rec_ref, wo_ref, g_ref, wu_ref, wd_ref,
                       o_ref, wob_ref, wub_ref, wdb_ref, u_ref):
    @pl.when(pl.program_id(1) == 0)
    def _():
        wob_ref[...] = wo_ref[...].astype(bf16)
        o_ref[...] = h_ref[...] + jnp.dot(attn_ref[...], wob_ref[0:ATTN_W, :], preferred_element_type=f32)
        o_ref[...] += jnp.dot(rec_ref[...], wob_ref[ATTN_W:ATTN_W + LRU_W, :], preferred_element_type=f32)
        u_ref[...] = _rmsnorm(o_ref[...], g_ref[...]).astype(bf16)
    wub_ref[...] = wu_ref[...].astype(bf16)
    wdb_ref[...] = wd_ref[...].astype(bf16)
    hid = jnp.dot(u_ref[...], wub_ref[...], preferred_element_type=f32)
    hid = jnp.square(jnp.maximum(hid, 0.0)).astype(bf16)
    o_ref[...] += jnp.dot(hid, wdb_ref[...], preferred_element_type=f32)


def _mix_mlp_cast(h, attn, rec, wo, g, wu, wd, l, *, tf):
    m = h.shape[0]
    once = pl.Buffered(1)
    blocks = 2 * D_MODEL * tf * (4 + 2)
    scratch = (2 * m * D_MODEL * 4 + m * (ATTN_W + LRU_W) * 2 + wo[0].size * (4 + 2) + m * D_MODEL * 2
               + m * tf * 6)
    return pl.pallas_call(
        _mix_mlp_cast_body,
        grid=(1, D_FF // tf),
        in_specs=[pl.BlockSpec((m, D_MODEL), lambda i, f: (0, 0), pipeline_mode=once),
                  pl.BlockSpec((m, ATTN_W), lambda i, f: (0, 0), pipeline_mode=once),
                  pl.BlockSpec((m, LRU_W), lambda i, f: (0, 0), pipeline_mode=once),
                  pl.BlockSpec((None,) + wo.shape[1:], lambda i, f: (l, 0, 0), pipeline_mode=once),
                  pl.BlockSpec((None, 1, D_MODEL), lambda i, f: (l, 0, 0), pipeline_mode=once),
                  pl.BlockSpec((None, D_MODEL, tf), lambda i, f: (l, 0, f)),
                  pl.BlockSpec((None, tf, D_MODEL), lambda i, f: (l, f, 0))],
        out_specs=[pl.BlockSpec((m, D_MODEL), lambda i, f: (0, 0)),
                   pl.BlockSpec(wo.shape[1:], lambda i, f: (0, 0)),
                   pl.BlockSpec((D_MODEL, tf), lambda i, f: (0, f)),
                   pl.BlockSpec((tf, D_MODEL), lambda i, f: (f, 0))],
        out_shape=[jax.ShapeDtypeStruct((m, D_MODEL), f32),
                   jax.ShapeDtypeStruct(wo.shape[1:], bf16),
                   jax.ShapeDtypeStruct(wu.shape[1:], bf16),
                   jax.ShapeDtypeStruct(wd.shape[1:], bf16)],
        scratch_shapes=[pltpu.VMEM((m, D_MODEL), bf16)],
        compiler_params=_params(("arbitrary", "arbitrary"), _vmem_limit(blocks, scratch)),
        name="mix_mlp_cast",
    )(h, attn, rec, wo, g, wu, wd)


def _final_norm_body(h_ref, g_ref, o_ref):
    o_ref[0] = _rmsnorm(h_ref[0], g_ref[...])


def _final_norm(h, g, *, skip, tr):
    n_batch, t_len, _ = h.shape
    s_len = t_len - skip
    tiles = tr // SUBLANES
    skip_tiles = skip // SUBLANES
    h4 = h.reshape(n_batch, t_len // SUBLANES, SUBLANES, D_MODEL)
    out = pl.pallas_call(
        _final_norm_body,
        grid=(n_batch, s_len // tr),
        in_specs=[pl.BlockSpec((pl.Element(1), pl.Element(tiles), pl.Element(SUBLANES), pl.Element(D_MODEL)),
                               lambda b, r: (b, skip_tiles + r * tiles, 0, 0)),
                  pl.BlockSpec((1, D_MODEL), lambda b, r: (0, 0))],
        out_specs=pl.BlockSpec((1, tiles, SUBLANES, D_MODEL), lambda b, r: (b, r, 0, 0)),
        out_shape=jax.ShapeDtypeStruct((n_batch, s_len // SUBLANES, SUBLANES, D_MODEL), f32),
        compiler_params=_params(("parallel", "parallel"), _vmem_limit(2 * tr * D_MODEL * 4)),
        name="final_norm",
    )(h4, g)
    return out.reshape(n_batch, s_len, D_MODEL)


def _row_tile(m, candidates):
    for tm in candidates:
        if m % tm == 0:
            return tm
    raise ValueError(f"no row tile for {m} rows")


def kernel(x_prompt, x_sample, cache_k_win, cache_v_win, state_conv, state_lru, meta_tokens, norm_mix_g, w_in,
           conv_w, conv_b, w_gate_a, b_gate_a, w_gate_x, b_gate_x, lru_lambda, attn_sinks, rel_bias, attn_out_g,
           rec_out_g, w_out, norm_mlp_g, w_up, w_down, final_norm_g):
    n_p, s_p, _ = x_prompt.shape
    n_s, t_s, _ = x_sample.shape
    t_p = N_META + s_p
    buf = cache_k_win.shape[2]
    assert t_p % BF16_ROWS == 0 and buf == WINDOW

    w_gates = jnp.concatenate([w_gate_a, w_gate_x], axis=-1).astype(bf16)
    rows3 = lambda p: p[:, None, :]
    g_mix, g_mlp, g_attn, g_rec = rows3(norm_mix_g), rows3(norm_mlp_g), rows3(attn_out_g), rows3(rec_out_g)
    cb3, ba3, bx3, lam3 = rows3(conv_b), rows3(b_gate_a), rows3(b_gate_x), rows3(lru_lambda)
    g_attn_t = jnp.broadcast_to(
        attn_out_g.reshape(DEPTH, N_Q_HEADS, HEAD_DIM).transpose(0, 2, 1)[..., None],
        (DEPTH, HEAD_DIM, N_Q_HEADS, WINDOW)).reshape(DEPTH, HEAD_DIM, ATTN_W)

    meta = jnp.broadcast_to(meta_tokens.astype(x_prompt.dtype)[None], (n_p, N_META, D_MODEL))
    hp = jnp.concatenate([meta, x_prompt], axis=1).reshape(n_p * t_p, D_MODEL)
    hs = x_sample.reshape(n_s * t_s, D_MODEL)

    bias_qk, bias_kq = _bias_tables(rel_bias)
    bias_qk = bias_qk.reshape(N_Q_HEADS * WINDOW, 2 * WINDOW)
    prev_p = jnp.zeros((n_p, SUBLANES, LRU_W), f32)
    h0_p = jnp.zeros((n_p, 1, LRU_W), f32)
    ck = cache_k_win.reshape(DEPTH, n_s, buf * N_KV_HEADS, HEAD_DIM)
    cv = cache_v_win.reshape(DEPTH, n_s, buf * N_KV_HEADS, HEAD_DIM)
    prev_s = jnp.pad(state_conv, ((0, 0), (0, 0), (SUBLANES - (CONV_W - 1), 0), (0, 0)))
    prev_s = prev_s.reshape(DEPTH, n_s * SUBLANES, LRU_W)

    tm_p = _row_tile(n_p * t_p, (688, 344))
    tm_s = n_s * t_s
    tc_p = _row_tile(t_p, (688, 344, 48, 16))
    tn = IN_W // 2

    kp_l, vp_l, cp_l, lp_l, ks_l, vs_l, cs_l, ls_l = ([] for _ in range(8))
    for l in range(DEPTH):
        z = _inproj(hs, g_mix, w_in, l, tm=tm_s, tn=tn)
        attn, nk, nv = _attn_sample(z, ck, cv, bias_qk, attn_sinks, g_attn, l, n_batch=n_s, t_len=t_s)
        rec, lru = _rec_sample(z, prev_s, state_lru, conv_w, cb3, w_gates, ba3, bx3, lam3, g_rec, l,
                               n_batch=n_s, t_len=t_s)
        hs, w_out_b, w_up_b, w_down_b = _mix_mlp_cast(hs, attn, rec, w_out, g_mlp, w_up, w_down, l, tf=512)
        z3 = z.reshape(n_s, t_s, IN_W)
        ks_l.append(nk)
        vs_l.append(nv)
        cs_l.append(z3[:, t_s - (CONV_W - 1):, Z_XR:Z_XR + LRU_W])
        ls_l.append(lru)
        z, u = _inproj_qkv(hp, g_mix, w_in, l, tm=tm_p)
        attn = _attn_prompt(z, bias_kq, attn_sinks, g_attn_t, l, n_batch=n_p, t_len=t_p)
        rec, lru, x_tail = _rec_prompt(u, w_in, prev_p, h0_p, conv_w, cb3, w_gates, ba3, bx3, lam3, g_rec, l,
                                       n_batch=n_p, t_len=t_p, tc=tc_p)
        hp = _mix_mlp(hp, attn, rec, w_out_b[None], g_mlp[l:l + 1], w_up_b[None], w_down_b[None], 0,
                      tm=tm_p, tf=1024)
        z3 = z.reshape(n_p, t_p, Z_XR)
        kp_l.append(z3[:, t_p - WINDOW:, Z_K:Z_K + KV_W].reshape(n_p, WINDOW, N_KV_HEADS, HEAD_DIM))
        vp_l.append(z3[:, t_p - WINDOW:, Z_V:Z_V + KV_W].reshape(n_p, WINDOW, N_KV_HEADS, HEAD_DIM))
        cp_l.append(x_tail[:, SUBLANES - (CONV_W - 1):])
        lp_l.append(lru[:, 0])

    g_fin = final_norm_g[None, :]
    y_prompt = _final_norm(hp.reshape(n_p, t_p, D_MODEL), g_fin, skip=N_META, tr=256)
    y_sample = _final_norm(hs.reshape(1, n_s * t_s, D_MODEL), g_fin, skip=0, tr=n_s * t_s)
    y_sample = y_sample.reshape(n_s, t_s, D_MODEL)
    cache_shape = (DEPTH, n_s, buf, N_KV_HEADS, HEAD_DIM)
    return (y_prompt, y_sample,
            jnp.stack(kp_l), jnp.stack(vp_l), jnp.stack(cp_l), jnp.stack(lp_l),
            jnp.stack(ks_l).reshape(cache_shape), jnp.stack(vs_l).reshape(cache_shape),
            jnp.stack(cs_l), jnp.stack(ls_l))
```

```python
import functools
import math

import jax
import jax.numpy as jnp
from jax import lax
from jax.experimental import pallas as pl
from jax.experimental.pallas import tpu as pltpu

f32 = jnp.float32
bf16 = jnp.bfloat16

D_MODEL = 2048
DEPTH = 4
PAST_LEN = 16384
HEAD_DIM = 128
N_Q_HEADS = 8
N_KV_HEADS = 2
Q_PER_KV = N_Q_HEADS // N_KV_HEADS
ATTN_W = N_Q_HEADS * HEAD_DIM
KV_W = N_KV_HEADS * HEAD_DIM
LRU_W = D_MODEL // 2
N_LRU_BLOCKS = 8
LRU_BLOCK = LRU_W // N_LRU_BLOCKS
CONV_W = 4
LRU_C = 8.0
IN_W = ATTN_W + 2 * KV_W + 2 * LRU_W
D_FF = 4 * D_MODEL
WINDOW = 128
N_BUCKETS = 32
MAX_DISTANCE = 128
N_META = 16
EPS = 1e-6
ATTN_SCALE = HEAD_DIM ** -0.5
INV_ATTN_SCALE = HEAD_DIM ** 0.5
EXP2_PER_T = math.log2(math.e) / INV_ATTN_SCALE

Z_Q = 0
Z_K = ATTN_W
Z_V = Z_K + KV_W
Z_XR = Z_V + KV_W
Z_GR = Z_XR + LRU_W
LRU_HALF = LRU_W // 2

ATTN_LEAD = WINDOW + (-N_META) % WINDOW

SUBLANES = 8
BF16_ROWS = 16
V7X_VMEM_BYTES = 64 * 1024 * 1024
VMEM_CAP_BYTES = V7X_VMEM_BYTES - 2 * 1024 * 1024


def _vmem_limit(pipelined_bytes, scratch_bytes=0):
    est = 2 * pipelined_bytes + scratch_bytes
    return int(min(VMEM_CAP_BYTES, est + est // 2 + (8 << 20)))


def _params(semantics, vmem_bytes):
    return pltpu.CompilerParams(dimension_semantics=semantics, vmem_limit_bytes=vmem_bytes)


def _rms_scale(x):
    return lax.rsqrt(jnp.mean(x * x, axis=-1, keepdims=True) + EPS)


def _rmsnorm(x, g):
    return x * _rms_scale(x) * g


def _inproj_cast_body(h_ref, g_ref, w_ref, z_ref, wb_ref, u_ref):
    @pl.when(pl.program_id(0) == 0)
    def _():
        u_ref[...] = _rmsnorm(h_ref[...], g_ref[...]).astype(bf16)
    wb_ref[...] = w_ref[...].astype(bf16)
    z_ref[...] = jnp.dot(u_ref[...], wb_ref[...], preferred_element_type=f32)


def _inproj_cast(h, g, w, l, *, tn):
    m = h.shape[0]
    blocks = D_MODEL * tn * (4 + 2) + m * tn * 4
    return pl.pallas_call(
        _inproj_cast_body,
        grid=(IN_W // tn,),
        in_specs=[pl.BlockSpec((m, D_MODEL), lambda j: (0, 0)),
                  pl.BlockSpec((None, 1, D_MODEL), lambda j: (l, 0, 0)),
                  pl.BlockSpec((None, D_MODEL, tn), lambda j: (l, 0, j))],
        out_specs=[pl.BlockSpec((m, tn), lambda j: (0, j)),
                   pl.BlockSpec((D_MODEL, tn), lambda j: (0, j))],
        out_shape=[jax.ShapeDtypeStruct((m, IN_W), f32), jax.ShapeDtypeStruct(w.shape[1:], bf16)],
        scratch_shapes=[pltpu.VMEM((m, D_MODEL), bf16)],
        compiler_params=_params(("arbitrary",), _vmem_limit(blocks, m * D_MODEL * (2 * 4 + 2))),
        name="inproj_cast",
    )(h, g, w)


def _rel_bias_of(dist, rel_ref, h):
    n = jnp.maximum(dist, 0)
    max_exact = N_BUCKETS // 2
    nf = jnp.maximum(n, 1).astype(f32)
    large = max_exact + (jnp.log(nf / max_exact) / math.log(MAX_DISTANCE / max_exact)
                         * (N_BUCKETS - max_exact)).astype(jnp.int32)
    large = jnp.minimum(large, N_BUCKETS - 1)
    bucket = jnp.where(n < max_exact, n, large)
    acc = jnp.zeros(dist.shape, f32)
    for b in range(N_BUCKETS):
        acc = jnp.where(bucket == b, rel_ref[b, h], acc)
    return jnp.where((dist >= 0) & (dist < WINDOW), acc, -jnp.inf)


def _bias_table_body(rel_ref, qk_ref, kq_ref):
    shape_qk = (WINDOW, 2 * WINDOW)
    dist_qk = (lax.broadcasted_iota(jnp.int32, shape_qk, 0) + WINDOW
               - lax.broadcasted_iota(jnp.int32, shape_qk, 1))
    shape_kq = (2 * WINDOW, WINDOW)
    dist_kq = (lax.broadcasted_iota(jnp.int32, shape_kq, 1) + WINDOW
               - lax.broadcasted_iota(jnp.int32, shape_kq, 0))
    for h in range(N_Q_HEADS):
        kv, g = divmod(h, Q_PER_KV)
        qk_ref[h] = _rel_bias_of(dist_qk, rel_ref, h)
        kq_ref[kv, :, g * WINDOW:(g + 1) * WINDOW] = _rel_bias_of(dist_kq, rel_ref, h) * INV_ATTN_SCALE


def _bias_tables(rel_bias):
    return pl.pallas_call(
        _bias_table_body,
        in_specs=[pl.BlockSpec(memory_space=pltpu.SMEM)],
        out_shape=[jax.ShapeDtypeStruct((N_Q_HEADS, WINDOW, 2 * WINDOW), f32),
                   jax.ShapeDtypeStruct((N_KV_HEADS, 2 * WINDOW, Q_PER_KV * WINDOW), f32)],
        name="bias_tables",
    )(rel_bias)


def _attn_blocks_t(blocks, kpad_ref, vt_ref, bias_ref, sinks_ref, l, gt_ref):
    kv_heads = [range(kv * Q_PER_KV, (kv + 1) * Q_PER_KV) for kv in range(N_KV_HEADS)]
    sink = [jnp.concatenate([jnp.full((1, WINDOW), sinks_ref[l, h] * INV_ATTN_SCALE, f32) for h in heads], axis=1)
            for heads in kv_heads]
    chains = [(i, kv) for i in range(len(blocks)) for kv in range(N_KV_HEADS)]
    scores = {}
    for i, kv in chains:
        q_rows, r0, _ = blocks[i]
        kwin = kpad_ref[pl.ds(r0, 2 * WINDOW), kv * HEAD_DIM:(kv + 1) * HEAD_DIM]
        q = jnp.concatenate([q_rows[:, h * HEAD_DIM:(h + 1) * HEAD_DIM] for h in kv_heads[kv]],
                            axis=0).astype(bf16)
        scores[i, kv] = lax.dot_general(kwin, q, (((1,), (1,)), ((), ())), preferred_element_type=f32)
    probs, invs = {}, {}
    for i, kv in chains:
        t = scores[i, kv] + bias_ref[kv]
        lead_keys = blocks[i][2]
        if lead_keys:
            key = lax.broadcasted_iota(jnp.int32, t.shape, 0)
            t = jnp.where(key < lead_keys, -jnp.inf, t)
        m = jnp.maximum(jnp.max(t, axis=0, keepdims=True), sink[kv])
        p = jnp.exp2((t - m) * EXP2_PER_T)
        invs[i, kv] = 1.0 / (jnp.sum(p, axis=0, keepdims=True) + jnp.exp2((sink[kv] - m) * EXP2_PER_T))
        probs[i, kv] = p.astype(bf16)
    raws = {}
    for i, kv in chains:
        vt = vt_ref[kv * HEAD_DIM:(kv + 1) * HEAD_DIM, pl.ds(blocks[i][1], 2 * WINDOW)]
        raws[i, kv] = jnp.dot(vt, probs[i, kv], preferred_element_type=f32)
    outs = []
    for i in range(len(blocks)):
        raw = jnp.concatenate([raws[i, kv] for kv in range(N_KV_HEADS)], axis=1)
        inv = jnp.concatenate([invs[i, kv] for kv in range(N_KV_HEADS)], axis=1)
        sq = jnp.sum(raw * raw, axis=0, keepdims=True) * (inv * inv)
        ssq = sq[:, 0:WINDOW]
        for h in range(1, N_Q_HEADS):
            ssq = ssq + sq[:, h * WINDOW:(h + 1) * WINDOW]
        r = lax.rsqrt(ssq * (1.0 / ATTN_W) + EPS)
        yt = raw * (inv * jnp.concatenate([r] * N_Q_HEADS, axis=1)) * gt_ref[...]
        outs.append(jnp.concatenate([yt[:, h * WINDOW:(h + 1) * WINDOW].T for h in range(N_Q_HEADS)], axis=1))
    return outs


ATTN_GROUP = 5


def _attn_prompt_body(q_ref, k_ref, v_ref, bias_ref, sinks_ref, gt_ref, o_ref, kpad_ref, vpad_ref, vt_ref,
                      *, t_len, l):
    pad_rows = kpad_ref.shape[0]
    kpad_ref[0:ATTN_LEAD, :] = jnp.zeros((ATTN_LEAD, KV_W), bf16)
    kpad_ref[ATTN_LEAD:pad_rows, :] = k_ref[...].astype(bf16)
    vpad_ref[0:ATTN_LEAD, :] = jnp.zeros((ATTN_LEAD, KV_W), f32)
    vpad_ref[ATTN_LEAD:pad_rows, :] = v_ref[...]

    def transpose_step(c, _):
        r = pl.multiple_of(c * WINDOW, WINDOW)
        chunk = vpad_ref[pl.ds(r, WINDOW), :]
        for kv in range(N_KV_HEADS):
            vt_ref[kv * HEAD_DIM:(kv + 1) * HEAD_DIM, pl.ds(r, WINDOW)] = (
                chunk[:, kv * HEAD_DIM:(kv + 1) * HEAD_DIM].T.astype(bf16))
        return 0
    lax.fori_loop(0, pad_rows // WINDOW, transpose_step, 0, unroll=3)

    def q_rows_of(j):
        start = j * WINDOW - (ATTN_LEAD - WINDOW)
        return pl.ds(start if isinstance(j, int) else pl.multiple_of(start, BF16_ROWS), WINDOW)

    n_first = 2 * WINDOW - ATTN_LEAD
    q0 = jnp.concatenate([jnp.zeros((WINDOW - n_first, ATTN_W), f32), q_ref[0:n_first, :]], axis=0)
    y0, y1 = _attn_blocks_t([(q0, 0, ATTN_LEAD), (q_ref[q_rows_of(1), :], WINDOW, ATTN_LEAD - WINDOW)],
                            kpad_ref, vt_ref, bias_ref, sinks_ref, l, gt_ref)
    o_ref[0:n_first, :] = y0[WINDOW - n_first:, :].astype(o_ref.dtype)
    o_ref[q_rows_of(1), :] = y1.astype(o_ref.dtype)

    def step(i, _):
        js = [2 + i * ATTN_GROUP + g for g in range(ATTN_GROUP)]
        ys = _attn_blocks_t([(q_ref[q_rows_of(j), :], pl.multiple_of(j * WINDOW, WINDOW), 0) for j in js],
                            kpad_ref, vt_ref, bias_ref, sinks_ref, l, gt_ref)
        for j, y in zip(js, ys):
            o_ref[q_rows_of(j), :] = y.astype(o_ref.dtype)
        return 0
    n_blocks = pad_rows // WINDOW - 1
    assert (n_blocks - 2) % ATTN_GROUP == 0
    lax.fori_loop(0, (n_blocks - 2) // ATTN_GROUP, step, 0)


def _attn_prompt(z, bias_kq, sinks, gt, l, *, n_batch, t_len):
    pad_rows = ATTN_LEAD + t_len
    assert pad_rows % WINDOW == 0 and (ATTN_LEAD - WINDOW) % BF16_ROWS == 0
    blocks = t_len * (ATTN_W + 2 * KV_W) * 4 + t_len * ATTN_W * 2 + bias_kq.size * 4 + HEAD_DIM * ATTN_W * 4
    scratch = pad_rows * KV_W * (2 + 4 + 2)
    return pl.pallas_call(
        functools.partial(_attn_prompt_body, t_len=t_len, l=l),
        grid=(n_batch,),
        in_specs=[pl.BlockSpec((t_len, ATTN_W), lambda b: (b, Z_Q // ATTN_W)),
                  pl.BlockSpec((t_len, KV_W), lambda b: (b, Z_K // KV_W)),
                  pl.BlockSpec((t_len, KV_W), lambda b: (b, Z_V // KV_W)),
                  pl.BlockSpec(bias_kq.shape, lambda b: (0, 0, 0)),
                  pl.BlockSpec(memory_space=pltpu.SMEM),
                  pl.BlockSpec((None, HEAD_DIM, ATTN_W), lambda b: (l, 0, 0))],
        out_specs=pl.BlockSpec((t_len, ATTN_W), lambda b: (b, 0)),
        out_shape=jax.ShapeDtypeStruct((n_batch * t_len, ATTN_W), bf16),
        scratch_shapes=[pltpu.VMEM((pad_rows, KV_W), bf16), pltpu.VMEM((pad_rows, KV_W), f32),
                        pltpu.VMEM((KV_W, pad_rows), bf16)],
        compiler_params=_params(("parallel",), _vmem_limit(blocks, scratch)),
        name="attn_prompt",
    )(z, z, z, bias_kq, sinks, gt)


DECODE_GROUP = 8


def _attn_sample_body(z_ref, ck_ref, cv_ref, bias_ref, sinks_ref, g_ref, o_ref, nk_ref, nv_ref, acc_ref,
                      *, n_batch, t_len, buf, l):
    zeros = jnp.zeros((2 * WINDOW - buf - t_len, HEAD_DIM), f32)
    kept = N_KV_HEADS * (buf - t_len)
    kv_heads = [range(kv * Q_PER_KV, (kv + 1) * Q_PER_KV) for kv in range(N_KV_HEADS)]
    bias = [jnp.concatenate([bias_ref[h * WINDOW:h * WINDOW + t_len, :] for h in heads], axis=0)
            for heads in kv_heads]
    sink = [jnp.concatenate([jnp.full((t_len, 1), sinks_ref[l, h], f32) for h in heads], axis=0)
            for heads in kv_heads]

    def window(c_ref, n_ref, b, r0, col, kv):
        new = z_ref[pl.ds(r0, t_len), col + kv * HEAD_DIM:col + (kv + 1) * HEAD_DIM]
        n_ref[b, pl.ds(kept + kv, t_len, stride=N_KV_HEADS), :] = new
        old = c_ref[b, pl.ds(kv, buf, stride=N_KV_HEADS), :]
        return jnp.concatenate([old, new, zeros], axis=0).astype(bf16)

    def group_step(i, _):
        seqs = [i * DECODE_GROUP + j for j in range(DECODE_GROUP)]
        rows = [pl.multiple_of(b * t_len, t_len) for b in seqs]
        scores = {}
        for j, (b, r0) in enumerate(zip(seqs, rows)):
            nk_ref[b, 0:kept, :] = ck_ref[b, N_KV_HEADS * t_len:N_KV_HEADS * buf, :]
            nv_ref[b, 0:kept, :] = cv_ref[b, N_KV_HEADS * t_len:N_KV_HEADS * buf, :]
            for kv, heads in enumerate(kv_heads):
                q = jnp.concatenate([z_ref[pl.ds(r0, t_len), Z_Q + h * HEAD_DIM:Z_Q + (h + 1) * HEAD_DIM]
                                     for h in heads], axis=0).astype(bf16)
                k = window(ck_ref, nk_ref, b, r0, Z_K, kv)
                scores[j, kv] = lax.dot_general(q, k, (((1,), (1,)), ((), ())), preferred_element_type=f32)
        probs, denoms = {}, {}
        for key, s in scores.items():
            kv = key[1]
            s = s * ATTN_SCALE + bias[kv]
            m = jnp.maximum(jnp.max(s, axis=-1, keepdims=True), sink[kv])
            p = jnp.exp(s - m)
            denoms[key] = jnp.sum(p, axis=-1, keepdims=True) + jnp.exp(sink[kv] - m)
            probs[key] = p.astype(bf16)
        for j, (b, r0) in enumerate(zip(seqs, rows)):
            outs = []
            for kv in range(N_KV_HEADS):
                v = window(cv_ref, nv_ref, b, r0, Z_V, kv)
                o = jnp.dot(probs[j, kv], v, preferred_element_type=f32) / denoms[j, kv]
                outs.extend(o[g * t_len:(g + 1) * t_len] for g in range(Q_PER_KV))
            acc_ref[pl.ds(r0, t_len), :] = jnp.concatenate(outs, axis=1)
        return 0
    lax.fori_loop(0, n_batch // DECODE_GROUP, group_step, 0)
    o_ref[...] = _rmsnorm(acc_ref[...], g_ref[...]).astype(o_ref.dtype)


def _attn_sample(z, ck, cv, bias_qk, sinks, g, l, *, n_batch, t_len):
    buf = ck.shape[2] // N_KV_HEADS
    assert buf == WINDOW and t_len == SUBLANES and n_batch % DECODE_GROUP == 0
    rows = n_batch * t_len
    whole =lambda a: pl.BlockSpec(a.shape, lambda i: (0,) * a.ndim)
    cache = pl.BlockSpec((None,) + ck.shape[1:], lambda i: (l, 0, 0, 0))
    blocks = (z.size + 4 * ck[0].size + bias_qk.size) * 4 + rows * ATTN_W * 2
    return pl.pallas_call(
        functools.partial(_attn_sample_body, n_batch=n_batch, t_len=t_len, buf=buf, l=l),
        grid=(1,),
        in_specs=[whole(z), cache, cache, whole(bias_qk), pl.BlockSpec(memory_space=pltpu.SMEM),
                  pl.BlockSpec((None, 1, ATTN_W), lambda i: (l, 0, 0))],
        out_specs=[pl.BlockSpec((rows, ATTN_W), lambda i: (0, 0)),
                   pl.BlockSpec(ck.shape[1:], lambda i: (0, 0, 0)),
                   pl.BlockSpec(cv.shape[1:], lambda i: (0, 0, 0))],
        out_shape=[jax.ShapeDtypeStruct((rows, ATTN_W), bf16),
                   jax.ShapeDtypeStruct(ck.shape[1:], f32),
                   jax.ShapeDtypeStruct(cv.shape[1:], f32)],
        scratch_shapes=[pltpu.VMEM((rows, ATTN_W), f32)],
        compiler_params=_params(("arbitrary",), _vmem_limit(blocks, rows * ATTN_W * 4)),
        name="attn_sample",
    )(z, ck, cv, bias_qk, sinks, g)


def _lru_gates(xc, n, c0, wg_ref, ba_ref, bx_ref, lam_ref):
    lanes = pl.ds(c0, LRU_BLOCK)
    gates = jnp.dot(xc.astype(bf16), wg_ref[n], preferred_element_type=f32)
    gate_a = jax.nn.sigmoid(gates[:, :LRU_BLOCK] + ba_ref[:, lanes])
    gate_x = jax.nn.sigmoid(gates[:, LRU_BLOCK:] + bx_ref[:, lanes])
    log_a = -LRU_C * gate_a * jax.nn.softplus(-lam_ref[:, lanes])
    a = jnp.exp(log_a)
    y = -jnp.tanh(log_a) * (1.0 + a * a)
    mult = jnp.where(y > 0.0, y * lax.rsqrt(y), 0.0)
    return a, xc * gate_x, mult


def _tile_prefix(a, b, row):
    for d in (1, 2, 4):
        a_prev = pltpu.roll(a, d, axis=0)
        b_prev = pltpu.roll(b, d, axis=0)
        keep = row >= d
        b = jnp.where(keep, a * b_prev + b, b)
        a = jnp.where(keep, a * a_prev, a)
    return a, b


def _scan_tile(a, b, h_in, row):
    a, b = _tile_prefix(a, b, row)
    return a * h_in + b


def _last_row(h):
    return jnp.broadcast_to(h[SUBLANES - 1:SUBLANES, :], h.shape)


def _conv_taps(x, shifted, cw, cb):
    out = cb + shifted[CONV_W - 1] * cw[0:1]
    for j in range(1, CONV_W - 1):
        out = out + shifted[CONV_W - 1 - j] * cw[j:j + 1]
    return out + x * cw[CONV_W - 1:CONV_W]


REC_PROJECTIONS_AHEAD = 5


def REC_ISSUE(n_pairs, n_z):
    return [(0, p) for p in range(n_pairs)] + [(1, p) for p in range(n_pairs)] + [(2, p) for p in range(n_z)]


def REC_CONSUME(n_pairs, n_z):
    order = [step for p in range(n_pairs) for step in ((0, p), (1, p))]
    return order + [(2, p) for p in range(n_z)]


def _rec_prompt_body(h_ref, gmix_ref, wq_ref, wxl_ref, wxh_ref, wgl_ref, wgh_ref, prev_ref, h0_ref, cw_ref, cb_ref,
                     wg_ref, ba_ref, bx_ref, lam_ref, g_ref, z_ref, o_ref, lru_ref, tail_ref, u_ref, xprev_ref,
                     a_ref, b_ref, gate_ref, carry_ref, *, tc, n_chunks):
    c = pl.program_id(1)

    @pl.when(c == 0)
    def _():
        xprev_ref[...] = prev_ref[...]
        for n in range(N_LRU_BLOCKS):
            carry_ref[n] = jnp.broadcast_to(h0_ref[:, n * LRU_BLOCK:(n + 1) * LRU_BLOCK], (SUBLANES, LRU_BLOCK))

    u_ref[...] = _rmsnorm(h_ref[...], gmix_ref[...]).astype(bf16)
    u = u_ref[...]
    pair = 2 * LRU_BLOCK
    n_pairs = LRU_W // pair
    pairs_per_half = LRU_HALF // pair
    branch_refs = ((wxl_ref, wxh_ref), (wgl_ref, wgh_ref))

    def project(branch, p):
        if branch == 2:
            w = wq_ref[:, p * pair:(p + 1) * pair]
        else:
            q = p % pairs_per_half
            w = branch_refs[branch][p // pairs_per_half][:, q * pair:(q + 1) * pair]
        return jnp.dot(u, w, preferred_element_type=f32)

    def store_z(p, z):
        z_ref[:, p * pair:(p + 1) * pair] = z

    def gates(p, x):
        lanes = slice(p * pair, (p + 1) * pair)
        big = jnp.concatenate([xprev_ref[:, lanes], x], axis=0)
        shifted = {k: pltpu.roll(big, k, axis=0)[SUBLANES:] for k in range(1, CONV_W)}
        xc = _conv_taps(x, shifted, cw_ref[:, lanes], cb_ref[:, lanes])
        for q in range(2):
            n = 2 * p + q
            sub = slice(q * LRU_BLOCK, (q + 1) * LRU_BLOCK)
            a, gated, mult = _lru_gates(xc[:, sub], n, n * LRU_BLOCK, wg_ref, ba_ref, bx_ref, lam_ref)
            b = gated * mult
            a_ref[n] = a
            b_ref[n] = b
            b_ref[n, 0:1, :] = jnp.where(c == 0, gated[0:1, :], b[0:1, :])
        return x[tc - SUBLANES:, :]

    def gelu_gate(p, gate):
        for q in range(2):
            gate_ref[2 * p + q] = jax.nn.gelu(gate[:, q * LRU_BLOCK:(q + 1) * LRU_BLOCK])

    n_z = Z_XR // pair
    issue = REC_ISSUE(n_pairs, n_z)
    consumers = [((gates, gelu_gate, store_z)[br], br, p) for br, p in REC_CONSUME(n_pairs, n_z)]
    projected, tails = {}, []
    for consume, br, p in consumers:
        while issue and (len(projected) < REC_PROJECTIONS_AHEAD or (br, p) not in projected):
            key = issue.pop(0)
            projected[key] = project(*key)
        out = consume(p, projected.pop((br, p)))
        if consume is gates:
            tails.append(out)
    tail = jnp.concatenate(tails, axis=1)
    xprev_ref[...] = tail
    tail_ref[...] = tail

    seg = tc // SUBLANES
    seg_rows = lambda r: pl.ds(r, SUBLANES, stride=seg)

    def local_scan(r, maps):
        out = []
        for n in range(N_LRU_BLOCKS):
            a = a_ref[n, seg_rows(r), :]
            a_cum = a * maps[2 * n]
            b_cum = a * maps[2 * n + 1] + b_ref[n, seg_rows(r), :]
            a_ref[n, seg_rows(r), :] = a_cum
            b_ref[n, seg_rows(r), :] = b_cum
            out += [a_cum, b_cum]
        return tuple(out)
    identity = (jnp.ones((SUBLANES, LRU_BLOCK), f32), jnp.zeros((SUBLANES, LRU_BLOCK), f32)) * N_LRU_BLOCKS
    seg_maps = lax.fori_loop(0, seg, local_scan, identity, unroll=2)

    row = lax.broadcasted_iota(jnp.int32, (SUBLANES, LRU_BLOCK), 0)
    h_in = []
    for n in range(N_LRU_BLOCKS):
        h_prev = carry_ref[n]
        a_cum, b_cum = _tile_prefix(seg_maps[2 * n], seg_maps[2 * n + 1], row)
        h_end = a_cum * h_prev + b_cum
        h_in.append(jnp.where(row == 0, h_prev, pltpu.roll(h_end, 1, axis=0)))
        carry_ref[n] = _last_row(h_end)

    def apply_scan(r, _):
        for n in range(N_LRU_BLOCKS):
            h = a_ref[n, seg_rows(r), :] * h_in[n] + b_ref[n, seg_rows(r), :]
            b_ref[n, seg_rows(r), :] = h * gate_ref[n, seg_rows(r), :]
        return 0
    lax.fori_loop(0, seg, apply_scan, 0, unroll=2)

    y = jnp.concatenate([b_ref[n] for n in range(N_LRU_BLOCKS)], axis=1)
    o_ref[...] = _rmsnorm(y, g_ref[...]).astype(o_ref.dtype)

    @pl.when(c == n_chunks - 1)
    def _():
        lru_ref[...] = jnp.concatenate([carry_ref[n] for n in range(N_LRU_BLOCKS)], axis=1)


def _rec_prompt(h, g_mix, w_in, prev8, h0, cw, cb, wg, ba, bx, lam, g, l, *, n_batch, t_len, tc):
    n_chunks = t_len // tc
    once = pl.Buffered(1)
    row_vec = pl.BlockSpec((None, 1, LRU_W), lambda b, c: (l, 0, 0), pipeline_mode=once)
    w_half = lambda col: pl.BlockSpec((D_MODEL, LRU_HALF), lambda b, c: (0, col // LRU_HALF), pipeline_mode=once)
    state = pl.BlockSpec((None, SUBLANES, LRU_W), lambda b, c: (b, 0, 0))
    rows = lambda width: pl.BlockSpec((tc, width), lambda b, c: (b * n_chunks + c, 0))
    blocks = tc * D_MODEL * 4 + tc * Z_XR * 4 + tc * LRU_W * 2 + 3 * SUBLANES * LRU_W * 4
    scratch = (D_MODEL * IN_W * 2 + wg[0].size * 2 + tc * D_MODEL * 2 + (3 * tc + 2 * SUBLANES) * LRU_W * 4
               + 8 * tc * 2 * LRU_BLOCK * 4)
    by_block = pltpu.VMEM((N_LRU_BLOCKS, tc, LRU_BLOCK), f32)
    assert tc % SUBLANES == 0
    return pl.pallas_call(
        functools.partial(_rec_prompt_body, tc=tc, n_chunks=n_chunks),
        grid=(n_batch, n_chunks),
        in_specs=[rows(D_MODEL),
                  pl.BlockSpec((None, 1, D_MODEL), lambda b, c: (l, 0, 0), pipeline_mode=once),
                  pl.BlockSpec((D_MODEL, Z_XR), lambda b, c: (0, 0), pipeline_mode=once),
                  w_half(Z_XR), w_half(Z_XR + LRU_HALF), w_half(Z_GR), w_half(Z_GR + LRU_HALF),
                  state,
                  pl.BlockSpec((None, 1, LRU_W), lambda b, c: (b, 0, 0)),
                  pl.BlockSpec((None, CONV_W, LRU_W), lambda b, c: (l, 0, 0), pipeline_mode=once),
                  row_vec,
                  pl.BlockSpec((None,) + wg.shape[1:], lambda b, c: (l, 0, 0, 0), pipeline_mode=once),
                  row_vec, row_vec, row_vec, row_vec],
        out_specs=[rows(Z_XR), rows(LRU_W), state, state],
        out_shape=[jax.ShapeDtypeStruct((n_batch * t_len, Z_XR), f32),
                   jax.ShapeDtypeStruct((n_batch * t_len, LRU_W), bf16),
                   jax.ShapeDtypeStruct((n_batch, SUBLANES, LRU_W), f32),
                   jax.ShapeDtypeStruct((n_batch, SUBLANES, LRU_W), f32)],
        scratch_shapes=[pltpu.VMEM((tc, D_MODEL), bf16),
                        pltpu.VMEM((SUBLANES, LRU_W), f32), by_block, by_block, by_block,
                        pltpu.VMEM((N_LRU_BLOCKS, SUBLANES, LRU_BLOCK), f32)],
        compiler_params=_params(("parallel", "arbitrary"), _vmem_limit(blocks, scratch)),
        name="rec_prompt",
    )(h, g_mix, w_in, w_in, w_in, w_in, w_in, prev8, h0, cw, cb, wg, ba, bx, lam, g)


def _rec_sample_body(z_ref, prev_ref, h0_ref, cw_ref, cb_ref, wg_ref, ba_ref, bx_ref, lam_ref, g_ref,
                     o_ref, lru_ref, xc_ref, a_ref, b_ref, *, n_batch):
    row = lax.broadcasted_iota(jnp.int32, (SUBLANES, LRU_W), 0)

    def conv_step(b, _):
        r = pl.multiple_of(b * SUBLANES, SUBLANES)
        x = z_ref[pl.ds(r, SUBLANES), Z_XR:Z_XR + LRU_W]
        prev = prev_ref[pl.ds(r, SUBLANES), :]
        shifted = {k: jnp.where(row >= k, pltpu.roll(x, k, axis=0), pltpu.roll(prev, k, axis=0))
                   for k in range(1, CONV_W)}
        xc_ref[pl.ds(r, SUBLANES), :] = _conv_taps(x, shifted, cw_ref[...], cb_ref[...])
        return 0
    lax.fori_loop(0, n_batch, conv_step, 0)

    def gates_step(n, _):
        c0 = pl.multiple_of(n * LRU_BLOCK, LRU_BLOCK)
        lanes = pl.ds(c0, LRU_BLOCK)
        a, gated, mult = _lru_gates(xc_ref[:, lanes], n, c0, wg_ref, ba_ref, bx_ref, lam_ref)
        a_ref[:, lanes] = a
        b_ref[:, lanes] = gated * mult
        return 0
    lax.fori_loop(0, N_LRU_BLOCKS, gates_step, 0)

    def scan_step(b, _):
        rows = pl.ds(pl.multiple_of(b * SUBLANES, SUBLANES), SUBLANES)
        h_in = jnp.broadcast_to(h0_ref[pl.ds(b, 1), :], (SUBLANES, LRU_W))
        hh = _scan_tile(a_ref[rows, :], b_ref[rows, :], h_in, row)
        b_ref[rows, :] = hh * jax.nn.gelu(z_ref[rows, Z_GR:Z_GR + LRU_W])
        lru_ref[pl.ds(b, 1), :] = hh[SUBLANES - 1:SUBLANES, :]
        return 0
    lax.fori_loop(0, n_batch, scan_step, 0)
    o_ref[...] = _rmsnorm(b_ref[...], g_ref[...]).astype(o_ref.dtype)


def _rec_sample(z, prev8, h0, cw, cb, wg, ba, bx, lam, g, l, *, n_batch, t_len):
    assert t_len == SUBLANES and PAST_LEN > 0
    rows = n_batch * t_len
    layer = lambda a: pl.BlockSpec((None,) + a.shape[1:], lambda i: (l,) + (0,) * (a.ndim - 1))
    blocks = z.size * 4 + (2 * rows + 2 * n_batch) * LRU_W * 4 + wg[0].size * 2
    return pl.pallas_call(
        functools.partial(_rec_sample_body, n_batch=n_batch),
        grid=(1,),
        in_specs=[pl.BlockSpec(z.shape, lambda i: (0, 0))] + [layer(a) for a in (prev8, h0, cw, cb, wg, ba, bx, lam, g)],
        out_specs=[pl.BlockSpec((rows, LRU_W), lambda i: (0, 0)), pl.BlockSpec((n_batch, LRU_W), lambda i: (0, 0))],
        out_shape=[jax.ShapeDtypeStruct((rows, LRU_W), bf16),
                   jax.ShapeDtypeStruct((n_batch, LRU_W), f32)],
        scratch_shapes=[pltpu.VMEM((rows, LRU_W), f32)] * 3,
        compiler_params=_params(("arbitrary",), _vmem_limit(blocks, 3 * rows * LRU_W * 4)),
        name="rec_sample",
    )(z, prev8, h0, cw, cb, wg, ba, bx, lam, g)


def _mix_mlp_body(h_ref, attn_ref, rec_ref, wo_ref, g_ref, wu_ref, wd_ref, o_ref, u_ref):
    @pl.when(pl.program_id(1) == 0)
    def _():
        o_ref[...] = h_ref[...] + jnp.dot(attn_ref[...], wo_ref[0:ATTN_W, :], preferred_element_type=f32)
        o_ref[...] += jnp.dot(rec_ref[...], wo_ref[ATTN_W:ATTN_W + LRU_W, :], preferred_element_type=f32)
        u_ref[...] = _rmsnorm(o_ref[...], g_ref[...]).astype(bf16)
    hid = jnp.dot(u_ref[...], wu_ref[...], preferred_element_type=f32)
    hid = jnp.square(jnp.maximum(hid, 0.0)).astype(bf16)
    o_ref[...] += jnp.dot(hid, wd_ref[...], preferred_element_type=f32)


def _mix_mlp(h, attn, rec, wo, g, wu, wd, l, *, tm, tf):
    m = h.shape[0]
    once = pl.Buffered(1)
    blocks = 2 * tm * D_MODEL * 4 + tm * (ATTN_W + LRU_W) * 2 + 2 * D_MODEL * tf * 2
    scratch = wo[0].size * 2 + tm * D_MODEL * 2 + tm * tf * 6
    return pl.pallas_call(
        _mix_mlp_body,
        grid=(m // tm, D_FF // tf),
        in_specs=[pl.BlockSpec((tm, D_MODEL), lambda i, f: (i, 0)),
                  pl.BlockSpec((tm, ATTN_W), lambda i, f: (i, 0)),
                  pl.BlockSpec((tm, LRU_W), lambda i, f: (i, 0)),
                  pl.BlockSpec((None,) + wo.shape[1:], lambda i, f: (l, 0, 0), pipeline_mode=once),
                  pl.BlockSpec((None, 1, D_MODEL), lambda i, f: (l, 0, 0), pipeline_mode=once),
                  pl.BlockSpec((None, D_MODEL, tf), lambda i, f: (l, 0, f)),
                  pl.BlockSpec((None, tf, D_MODEL), lambda i, f: (l, f, 0))],
        out_specs=pl.BlockSpec((tm, D_MODEL), lambda i, f: (i, 0)),
        out_shape=jax.ShapeDtypeStruct((m, D_MODEL), f32),
        scratch_shapes=[pltpu.VMEM((tm, D_MODEL), bf16)],
        compiler_params=_params(("parallel", "arbitrary"), _vmem_limit(blocks, scratch)),
        name="mix_mlp",
    )(h, attn, rec, wo, g, wu, wd)


def _mix_mlp_cast_body(h_ref, attn_ref, rec_ref, wo_ref, g_ref, wu_ref, wd_ref,
                       o_ref, wob_ref, wub_ref, wdb_ref, u_ref):
    @pl.when(pl.program_id(1) == 0)
    def _():
        wob_ref[...] = wo_ref[...].astype(bf16)
        o_ref[...] = h_ref[...] + jnp.dot(attn_ref[...], wob_ref[0:ATTN_W, :], preferred_element_type=f32)
        o_ref[...] += jnp.dot(rec_ref[...], wob_ref[ATTN_W:ATTN_W + LRU_W, :], preferred_element_type=f32)
        u_ref[...] = _rmsnorm(o_ref[...], g_ref[...]).astype(bf16)
    wub_ref[...] = wu_ref[...].astype(bf16)
    wdb_ref[...] = wd_ref[...].astype(bf16)
    hid = jnp.dot(u_ref[...], wub_ref[...], preferred_element_type=f32)
    hid = jnp.square(jnp.maximum(hid, 0.0)).astype(bf16)
    o_ref[...] += jnp.dot(hid, wdb_ref[...], preferred_element_type=f32)


def _mix_mlp_cast(h, attn, rec, wo, g, wu, wd, l, *, tf):
    m = h.shape[0]
    once = pl.Buffered(1)
    blocks = 2 * D_MODEL * tf * (4 + 2)
    scratch = (2 * m * D_MODEL * 4 + m * (ATTN_W + LRU_W) * 2 + wo[0].size * (4 + 2) + m * D_MODEL * 2
               + m * tf * 6)
    return pl.pallas_call(
        _mix_mlp_cast_body,
        grid=(1, D_FF // tf),
        in_specs=[pl.BlockSpec((m, D_MODEL), lambda i, f: (0, 0), pipeline_mode=once),
                  pl.BlockSpec((m, ATTN_W), lambda i, f: (0, 0), pipeline_mode=once),
                  pl.BlockSpec((m, LRU_W), lambda i, f: (0, 0), pipeline_mode=once),
                  pl.BlockSpec((None,) + wo.shape[1:], lambda i, f: (l, 0, 0), pipeline_mode=once),
                  pl.BlockSpec((None, 1, D_MODEL), lambda i, f: (l, 0, 0), pipeline_mode=once),
                  pl.BlockSpec((None, D_MODEL, tf), lambda i, f: (l, 0, f)),
                  pl.BlockSpec((None, tf, D_MODEL), lambda i, f: (l, f, 0))],
        out_specs=[pl.BlockSpec((m, D_MODEL), lambda i, f: (0, 0)),
                   pl.BlockSpec(wo.shape[1:], lambda i, f: (0, 0)),
                   pl.BlockSpec((D_MODEL, tf), lambda i, f: (0, f)),
                   pl.BlockSpec((tf, D_MODEL), lambda i, f: (f, 0))],
        out_shape=[jax.ShapeDtypeStruct((m, D_MODEL), f32),
                   jax.ShapeDtypeStruct(wo.shape[1:], bf16),
                   jax.ShapeDtypeStruct(wu.shape[1:], bf16),
                   jax.ShapeDtypeStruct(wd.shape[1:], bf16)],
        scratch_shapes=[pltpu.VMEM((m, D_MODEL), bf16)],
        compiler_params=_params(("arbitrary", "arbitrary"), _vmem_limit(blocks, scratch)),
        name="mix_mlp_cast",
    )(h, attn, rec, wo, g, wu, wd)


def _final_norm_body(h_ref, g_ref, o_ref):
    o_ref[0] = _rmsnorm(h_ref[0], g_ref[...])


def _final_norm(h, g, *, skip, tr):
    n_batch, t_len, _ = h.shape
    s_len = t_len - skip
    tiles = tr // SUBLANES
    skip_tiles = skip // SUBLANES
    h4 = h.reshape(n_batch, t_len // SUBLANES, SUBLANES, D_MODEL)
    out = pl.pallas_call(
        _final_norm_body,
        grid=(n_batch, s_len // tr),
        in_specs=[pl.BlockSpec((pl.Element(1), pl.Element(tiles), pl.Element(SUBLANES), pl.Element(D_MODEL)),
                               lambda b, r: (b, skip_tiles + r * tiles, 0, 0)),
                  pl.BlockSpec((1, D_MODEL), lambda b, r: (0, 0))],
        out_specs=pl.BlockSpec((1, tiles, SUBLANES, D_MODEL), lambda b, r: (b, r, 0, 0)),
        out_shape=jax.ShapeDtypeStruct((n_batch, s_len // SUBLANES, SUBLANES, D_MODEL), f32),
        compiler_params=_params(("parallel", "parallel"), _vmem_limit(2 * tr * D_MODEL * 4)),
        name="final_norm",
    )(h4, g)
    return out.reshape(n_batch, s_len, D_MODEL)


def _row_tile(m, candidates):
    for tm in candidates:
        if m % tm == 0:
            return tm
    raise ValueError(f"no row tile for {m} rows")


def kernel(x_prompt, x_sample, cache_k_win, cache_v_win, state_conv, state_lru, meta_tokens, norm_mix_g, w_in,
           conv_w, conv_b, w_gate_a, b_gate_a, w_gate_x, b_gate_x, lru_lambda, attn_sinks, rel_bias, attn_out_g,
           rec_out_g, w_out, norm_mlp_g, w_up, w_down, final_norm_g):
    n_p, s_p, _ = x_prompt.shape
    n_s, t_s, _ = x_sample.shape
    t_p = N_META + s_p
    buf = cache_k_win.shape[2]
    assert t_p % BF16_ROWS == 0 and buf == WINDOW

    w_gates = jnp.concatenate([w_gate_a, w_gate_x], axis=-1).astype(bf16)
    rows3 = lambda p: p[:, None, :]
    g_mix, g_mlp, g_attn, g_rec = rows3(norm_mix_g), rows3(norm_mlp_g), rows3(attn_out_g), rows3(rec_out_g)
    cb3, ba3, bx3, lam3 = rows3(conv_b), rows3(b_gate_a), rows3(b_gate_x), rows3(lru_lambda)
    g_attn_t = jnp.broadcast_to(
        attn_out_g.reshape(DEPTH, N_Q_HEADS, HEAD_DIM).transpose(0, 2, 1)[..., None],
        (DEPTH, HEAD_DIM, N_Q_HEADS, WINDOW)).reshape(DEPTH, HEAD_DIM, ATTN_W)

    meta = jnp.broadcast_to(meta_tokens.astype(x_prompt.dtype)[None], (n_p, N_META, D_MODEL))
    hp = jnp.concatenate([meta, x_prompt], axis=1).reshape(n_p * t_p, D_MODEL)
    hs = x_sample.reshape(n_s * t_s, D_MODEL)

    bias_qk, bias_kq = _bias_tables(rel_bias)
    bias_qk = bias_qk.reshape(N_Q_HEADS * WINDOW, 2 * WINDOW)
    prev_p = jnp.zeros((n_p, SUBLANES, LRU_W), f32)
    h0_p = jnp.zeros((n_p, 1, LRU_W), f32)
    ck = cache_k_win.reshape(DEPTH, n_s, buf * N_KV_HEADS, HEAD_DIM)
    cv = cache_v_win.reshape(DEPTH, n_s, buf * N_KV_HEADS, HEAD_DIM)
    prev_s = jnp.pad(state_conv, ((0, 0), (0, 0), (SUBLANES - (CONV_W - 1), 0), (0, 0)))
    prev_s = prev_s.reshape(DEPTH, n_s * SUBLANES, LRU_W)

    tm_p = _row_tile(n_p * t_p, (688, 344))
    tm_s = n_s * t_s
    tc_p = _row_tile(t_p, (688, 344, 48, 16))
    tn = IN_W // 2

    kp_l, vp_l, cp_l, lp_l, ks_l, vs_l, cs_l, ls_l = ([] for _ in range(8))
    for l in range(DEPTH):
        z, w_in_b = _inproj_cast(hs, g_mix, w_in, l, tn=tn)
        attn, nk, nv = _attn_sample(z, ck, cv, bias_qk, attn_sinks, g_attn, l, n_batch=n_s, t_len=t_s)
        rec, lru = _rec_sample(z, prev_s, state_lru, conv_w, cb3, w_gates, ba3, bx3, lam3, g_rec, l,
                               n_batch=n_s, t_len=t_s)
        hs, w_out_b, w_up_b, w_down_b = _mix_mlp_cast(hs, attn, rec, w_out, g_mlp, w_up, w_down, l, tf=512)
        z3 = z.reshape(n_s, t_s, IN_W)
        ks_l.append(nk)
        vs_l.append(nv)
        cs_l.append(z3[:, t_s - (CONV_W - 1):, Z_XR:Z_XR + LRU_W])
        ls_l.append(lru)
        z, rec, lru, x_tail = _rec_prompt(hp, g_mix, w_in_b, prev_p, h0_p, conv_w, cb3, w_gates, ba3, bx3, lam3,
                                          g_rec, l, n_batch=n_p, t_len=t_p, tc=tc_p)
        attn = _attn_prompt(z, bias_kq, attn_sinks, g_attn_t, l, n_batch=n_p, t_len=t_p)
        hp = _mix_mlp(hp, attn, rec, w_out_b[None], g_mlp[l:l + 1], w_up_b[None], w_down_b[None], 0,
                      tm=tm_p, tf=1024)
        z3 = z.reshape(n_p, t_p, Z_XR)
        kp_l.append(z3[:, t_p - WINDOW:, Z_K:Z_K + KV_W].reshape(n_p, WINDOW, N_KV_HEADS, HEAD_DIM))
        vp_l.append(z3[:, t_p - WINDOW:, Z_V:Z_V + KV_W].reshape(n_p, WINDOW, N_KV_HEADS, HEAD_DIM))
        cp_l.append(x_tail[:, SUBLANES - (CONV_W - 1):])
        lp_l.append(lru[:, 0])

    g_fin = final_norm_g[None, :]
    y_prompt = _final_norm(hp.reshape(n_p, t_p, D_MODEL), g_fin, skip=N_META, tr=256)
    y_sample = _final_norm(hs.reshape(1, n_s * t_s, D_MODEL), g_fin, skip=0, tr=n_s * t_s)
    y_sample = y_sample.reshape(n_s, t_s, D_MODEL)
    cache_shape = (DEPTH, n_s, buf, N_KV_HEADS, HEAD_DIM)
    return (y_prompt, y_sample,
            jnp.stack(kp_l), jnp.stack(vp_l), jnp.stack(cp_l), jnp.stack(lp_l),
            jnp.stack(ks_l).reshape(cache_shape), jnp.stack(vs_l).reshape(cache_shape),
            jnp.stack(cs_l), jnp.stack(ls_l))
```

```python
import functools
import math

import jax
import jax.numpy as jnp
from jax import lax
from jax.experimental import pallas as pl
from jax.experimental.pallas import tpu as pltpu

f32 = jnp.float32
bf16 = jnp.bfloat16

D_MODEL = 2048
DEPTH = 4
PAST_LEN = 16384
HEAD_DIM = 128
N_Q_HEADS = 8
N_KV_HEADS = 2
Q_PER_KV = N_Q_HEADS // N_KV_HEADS
ATTN_W = N_Q_HEADS * HEAD_DIM
KV_W = N_KV_HEADS * HEAD_DIM
LRU_W = D_MODEL // 2
N_LRU_BLOCKS = 8
LRU_BLOCK = LRU_W // N_LRU_BLOCKS
CONV_W = 4
LRU_C = 8.0
IN_W = ATTN_W + 2 * KV_W + 2 * LRU_W
D_FF = 4 * D_MODEL
WINDOW = 128
N_BUCKETS = 32
MAX_DISTANCE = 128
N_META = 16
EPS = 1e-6
ATTN_SCALE = HEAD_DIM ** -0.5
INV_ATTN_SCALE = HEAD_DIM ** 0.5
EXP2_PER_T = math.log2(math.e) / INV_ATTN_SCALE

Z_Q = 0
Z_K = ATTN_W
Z_V = Z_K + KV_W
Z_XR = Z_V + KV_W
Z_GR = Z_XR + LRU_W
LRU_HALF = LRU_W // 2

ATTN_LEAD = WINDOW + (-N_META) % WINDOW

SUBLANES = 8
BF16_ROWS = 16
V7X_VMEM_BYTES = 64 * 1024 * 1024
VMEM_CAP_BYTES = V7X_VMEM_BYTES - 2 * 1024 * 1024


def _vmem_limit(pipelined_bytes, scratch_bytes=0):
    est = 2 * pipelined_bytes + scratch_bytes
    return int(min(VMEM_CAP_BYTES, est + est // 2 + (8 << 20)))


def _params(semantics, vmem_bytes):
    return pltpu.CompilerParams(dimension_semantics=semantics, vmem_limit_bytes=vmem_bytes)


def _rms_scale(x):
    return lax.rsqrt(jnp.mean(x * x, axis=-1, keepdims=True) + EPS)


def _rmsnorm(x, g):
    return x * _rms_scale(x) * g


def _inproj_cast_body(h_ref, g_ref, w_ref, z_ref, wb_ref, u_ref):
    @pl.when(pl.program_id(0) == 0)
    def _():
        u_ref[...] = _rmsnorm(h_ref[...], g_ref[...]).astype(bf16)
    wb_ref[...] = w_ref[...].astype(bf16)
    z_ref[...] = jnp.dot(u_ref[...], wb_ref[...], preferred_element_type=f32)


def _inproj_cast(h, g, w, l, *, tn):
    m = h.shape[0]
    blocks = D_MODEL * tn * (4 + 2) + m * tn * 4
    return pl.pallas_call(
        _inproj_cast_body,
        grid=(IN_W // tn,),
        in_specs=[pl.BlockSpec((m, D_MODEL), lambda j: (0, 0)),
                  pl.BlockSpec((None, 1, D_MODEL), lambda j: (l, 0, 0)),
                  pl.BlockSpec((None, D_MODEL, tn), lambda j: (l, 0, j))],
        out_specs=[pl.BlockSpec((m, tn), lambda j: (0, j)),
                   pl.BlockSpec((D_MODEL, tn), lambda j: (0, j))],
        out_shape=[jax.ShapeDtypeStruct((m, IN_W), f32), jax.ShapeDtypeStruct(w.shape[1:], bf16)],
        scratch_shapes=[pltpu.VMEM((m, D_MODEL), bf16)],
        compiler_params=_params(("arbitrary",), _vmem_limit(blocks, m * D_MODEL * (2 * 4 + 2))),
        name="inproj_cast",
    )(h, g, w)


def _rel_bias_of(dist, rel_ref, h):
    n = jnp.maximum(dist, 0)
    max_exact = N_BUCKETS // 2
    nf = jnp.maximum(n, 1).astype(f32)
    large = max_exact + (jnp.log(nf / max_exact) / math.log(MAX_DISTANCE / max_exact)
                         * (N_BUCKETS - max_exact)).astype(jnp.int32)
    large = jnp.minimum(large, N_BUCKETS - 1)
    bucket = jnp.where(n < max_exact, n, large)
    acc = jnp.zeros(dist.shape, f32)
    for b in range(N_BUCKETS):
        acc = jnp.where(bucket == b, rel_ref[b, h], acc)
    return jnp.where((dist >= 0) & (dist < WINDOW), acc, -jnp.inf)


def _bias_table_body(rel_ref, qk_ref, kq_ref):
    shape_qk = (WINDOW, 2 * WINDOW)
    dist_qk = (lax.broadcasted_iota(jnp.int32, shape_qk, 0) + WINDOW
               - lax.broadcasted_iota(jnp.int32, shape_qk, 1))
    shape_kq = (2 * WINDOW, WINDOW)
    dist_kq = (lax.broadcasted_iota(jnp.int32, shape_kq, 1) + WINDOW
               - lax.broadcasted_iota(jnp.int32, shape_kq, 0))
    for h in range(N_Q_HEADS):
        kv, g = divmod(h, Q_PER_KV)
        qk_ref[h] = _rel_bias_of(dist_qk, rel_ref, h)
        kq_ref[kv, :, g * WINDOW:(g + 1) * WINDOW] = _rel_bias_of(dist_kq, rel_ref, h) * INV_ATTN_SCALE


def _bias_tables(rel_bias):
    return pl.pallas_call(
        _bias_table_body,
        in_specs=[pl.BlockSpec(memory_space=pltpu.SMEM)],
        out_shape=[jax.ShapeDtypeStruct((N_Q_HEADS, WINDOW, 2 * WINDOW), f32),
                   jax.ShapeDtypeStruct((N_KV_HEADS, 2 * WINDOW, Q_PER_KV * WINDOW), f32)],
        name="bias_tables",
    )(rel_bias)


def _attn_blocks_t(blocks, kpad_ref, vt_ref, bias_ref, sinks_ref, l, gt_ref):
    kv_heads = [range(kv * Q_PER_KV, (kv + 1) * Q_PER_KV) for kv in range(N_KV_HEADS)]
    sink = [jnp.concatenate([jnp.full((1, WINDOW), sinks_ref[l, h] * INV_ATTN_SCALE, f32) for h in heads], axis=1)
            for heads in kv_heads]
    chains = [(i, kv) for i in range(len(blocks)) for kv in range(N_KV_HEADS)]
    scores = {}
    for i, kv in chains:
        q_rows, r0, _ = blocks[i]
        kwin = kpad_ref[pl.ds(r0, 2 * WINDOW), kv * HEAD_DIM:(kv + 1) * HEAD_DIM]
        q = jnp.concatenate([q_rows[:, h * HEAD_DIM:(h + 1) * HEAD_DIM] for h in kv_heads[kv]],
                            axis=0).astype(bf16)
        scores[i, kv] = lax.dot_general(kwin, q, (((1,), (1,)), ((), ())), preferred_element_type=f32)
    probs, invs = {}, {}
    for i, kv in chains:
        t = scores[i, kv] + bias_ref[kv]
        lead_keys = blocks[i][2]
        if lead_keys:
            key = lax.broadcasted_iota(jnp.int32, t.shape, 0)
            t = jnp.where(key < lead_keys, -jnp.inf, t)
        m = jnp.maximum(jnp.max(t, axis=0, keepdims=True), sink[kv])
        p = jnp.exp2((t - m) * EXP2_PER_T)
        invs[i, kv] = 1.0 / (jnp.sum(p, axis=0, keepdims=True) + jnp.exp2((sink[kv] - m) * EXP2_PER_T))
        probs[i, kv] = p.astype(bf16)
    raws = {}
    for i, kv in chains:
        vt = vt_ref[kv * HEAD_DIM:(kv + 1) * HEAD_DIM, pl.ds(blocks[i][1], 2 * WINDOW)]
        raws[i, kv] = jnp.dot(vt, probs[i, kv], preferred_element_type=f32)
    outs = []
    for i in range(len(blocks)):
        raw = jnp.concatenate([raws[i, kv] for kv in range(N_KV_HEADS)], axis=1)
        inv = jnp.concatenate([invs[i, kv] for kv in range(N_KV_HEADS)], axis=1)
        sq = jnp.sum(raw * raw, axis=0, keepdims=True) * (inv * inv)
        ssq = sq[:, 0:WINDOW]
        for h in range(1, N_Q_HEADS):
            ssq = ssq + sq[:, h * WINDOW:(h + 1) * WINDOW]
        r = lax.rsqrt(ssq * (1.0 / ATTN_W) + EPS)
        yt = raw * (inv * jnp.concatenate([r] * N_Q_HEADS, axis=1)) * gt_ref[...]
        outs.append(jnp.concatenate([yt[:, h * WINDOW:(h + 1) * WINDOW].T for h in range(N_Q_HEADS)], axis=1))
    return outs


ATTN_GROUP = 5


def _attn_prompt_body(q_ref, k_ref, v_ref, bias_ref, sinks_ref, gt_ref, o_ref, kpad_ref, vpad_ref, vt_ref,
                      *, t_len, l):
    pad_rows = kpad_ref.shape[0]
    kpad_ref[0:ATTN_LEAD, :] = jnp.zeros((ATTN_LEAD, KV_W), bf16)
    kpad_ref[ATTN_LEAD:pad_rows, :] = k_ref[...].astype(bf16)
    vpad_ref[0:ATTN_LEAD, :] = jnp.zeros((ATTN_LEAD, KV_W), f32)
    vpad_ref[ATTN_LEAD:pad_rows, :] = v_ref[...]

    def transpose_step(c, _):
        r = pl.multiple_of(c * WINDOW, WINDOW)
        chunk = vpad_ref[pl.ds(r, WINDOW), :]
        for kv in range(N_KV_HEADS):
            vt_ref[kv * HEAD_DIM:(kv + 1) * HEAD_DIM, pl.ds(r, WINDOW)] = (
                chunk[:, kv * HEAD_DIM:(kv + 1) * HEAD_DIM].T.astype(bf16))
        return 0
    lax.fori_loop(0, pad_rows // WINDOW, transpose_step, 0, unroll=3)

    def q_rows_of(j):
        start = j * WINDOW - (ATTN_LEAD - WINDOW)
        return pl.ds(start if isinstance(j, int) else pl.multiple_of(start, BF16_ROWS), WINDOW)

    n_first = 2 * WINDOW - ATTN_LEAD
    q0 = jnp.concatenate([jnp.zeros((WINDOW - n_first, ATTN_W), f32), q_ref[0:n_first, :]], axis=0)
    y0, y1 = _attn_blocks_t([(q0, 0, ATTN_LEAD), (q_ref[q_rows_of(1), :], WINDOW, ATTN_LEAD - WINDOW)],
                            kpad_ref, vt_ref, bias_ref, sinks_ref, l, gt_ref)
    o_ref[0:n_first, :] = y0[WINDOW - n_first:, :].astype(o_ref.dtype)
    o_ref[q_rows_of(1), :] = y1.astype(o_ref.dtype)

    def step(i, _):
        js = [2 + i * ATTN_GROUP + g for g in range(ATTN_GROUP)]
        ys = _attn_blocks_t([(q_ref[q_rows_of(j), :], pl.multiple_of(j * WINDOW, WINDOW), 0) for j in js],
                            kpad_ref, vt_ref, bias_ref, sinks_ref, l, gt_ref)
        for j, y in zip(js, ys):
            o_ref[q_rows_of(j), :] = y.astype(o_ref.dtype)
        return 0
    n_blocks = pad_rows // WINDOW - 1
    assert (n_blocks - 2) % ATTN_GROUP == 0
    lax.fori_loop(0, (n_blocks - 2) // ATTN_GROUP, step, 0)


def _attn_prompt(z, bias_kq, sinks, gt, l, *, n_batch, t_len):
    pad_rows = ATTN_LEAD + t_len
    assert pad_rows % WINDOW == 0 and (ATTN_LEAD - WINDOW) % BF16_ROWS == 0
    blocks = t_len * (ATTN_W + 2 * KV_W) * 4 + t_len * ATTN_W * 2 + bias_kq.size * 4 + HEAD_DIM * ATTN_W * 4
    scratch = pad_rows * KV_W * (2 + 4 + 2)
    return pl.pallas_call(
        functools.partial(_attn_prompt_body, t_len=t_len, l=l),
        grid=(n_batch,),
        in_specs=[pl.BlockSpec((t_len, ATTN_W), lambda b: (b, Z_Q // ATTN_W)),
                  pl.BlockSpec((t_len, KV_W), lambda b: (b, Z_K // KV_W)),
                  pl.BlockSpec((t_len, KV_W), lambda b: (b, Z_V // KV_W)),
                  pl.BlockSpec(bias_kq.shape, lambda b: (0, 0, 0)),
                  pl.BlockSpec(memory_space=pltpu.SMEM),
                  pl.BlockSpec((None, HEAD_DIM, ATTN_W), lambda b: (l, 0, 0))],
        out_specs=pl.BlockSpec((t_len, ATTN_W), lambda b: (b, 0)),
        out_shape=jax.ShapeDtypeStruct((n_batch * t_len, ATTN_W), bf16),
        scratch_shapes=[pltpu.VMEM((pad_rows, KV_W), bf16), pltpu.VMEM((pad_rows, KV_W), f32),
                        pltpu.VMEM((KV_W, pad_rows), bf16)],
        compiler_params=_params(("parallel",), _vmem_limit(blocks, scratch)),
        name="attn_prompt",
    )(z, z, z, bias_kq, sinks, gt)


DECODE_GROUP = 8


def _attn_sample_body(z_ref, ck_ref, cv_ref, bias_ref, sinks_ref, g_ref, o_ref, nk_ref, nv_ref, acc_ref,
                      *, n_batch, t_len, buf, l):
    zeros = jnp.zeros((2 * WINDOW - buf - t_len, HEAD_DIM), f32)
    kept = N_KV_HEADS * (buf - t_len)
    kv_heads = [range(kv * Q_PER_KV, (kv + 1) * Q_PER_KV) for kv in range(N_KV_HEADS)]
    bias = [jnp.concatenate([bias_ref[h * WINDOW:h * WINDOW + t_len, :] for h in heads], axis=0)
            for heads in kv_heads]
    sink = [jnp.concatenate([jnp.full((t_len, 1), sinks_ref[l, h], f32) for h in heads], axis=0)
            for heads in kv_heads]

    def window(c_ref, n_ref, b, r0, col, kv):
        new = z_ref[pl.ds(r0, t_len), col + kv * HEAD_DIM:col + (kv + 1) * HEAD_DIM]
        n_ref[b, pl.ds(kept + kv, t_len, stride=N_KV_HEADS), :] = new
        old = c_ref[b, pl.ds(kv, buf, stride=N_KV_HEADS), :]
        return jnp.concatenate([old, new, zeros], axis=0).astype(bf16)

    def group_step(i, _):
        seqs = [i * DECODE_GROUP + j for j in range(DECODE_GROUP)]
        rows = [pl.multiple_of(b * t_len, t_len) for b in seqs]
        scores = {}
        for j, (b, r0) in enumerate(zip(seqs, rows)):
            nk_ref[b, 0:kept, :] = ck_ref[b, N_KV_HEADS * t_len:N_KV_HEADS * buf, :]
            nv_ref[b, 0:kept, :] = cv_ref[b, N_KV_HEADS * t_len:N_KV_HEADS * buf, :]
            for kv, heads in enumerate(kv_heads):
                q = jnp.concatenate([z_ref[pl.ds(r0, t_len), Z_Q + h * HEAD_DIM:Z_Q + (h + 1) * HEAD_DIM]
                                     for h in heads], axis=0).astype(bf16)
                k = window(ck_ref, nk_ref, b, r0, Z_K, kv)
                scores[j, kv] = lax.dot_general(q, k, (((1,), (1,)), ((), ())), preferred_element_type=f32)
        probs, denoms = {}, {}
        for key, s in scores.items():
            kv = key[1]
            s = s * ATTN_SCALE + bias[kv]
            m = jnp.maximum(jnp.max(s, axis=-1, keepdims=True), sink[kv])
            p = jnp.exp(s - m)
            denoms[key] = jnp.sum(p, axis=-1, keepdims=True) + jnp.exp(sink[kv] - m)
            probs[key] = p.astype(bf16)
        for j, (b, r0) in enumerate(zip(seqs, rows)):
            outs = []
            for kv in range(N_KV_HEADS):
                v = window(cv_ref, nv_ref, b, r0, Z_V, kv)
                o = jnp.dot(probs[j, kv], v, preferred_element_type=f32) / denoms[j, kv]
                outs.extend(o[g * t_len:(g + 1) * t_len] for g in range(Q_PER_KV))
            acc_ref[pl.ds(r0, t_len), :] = jnp.concatenate(outs, axis=1)
        return 0
    lax.fori_loop(0, n_batch // DECODE_GROUP, group_step, 0)
    o_ref[...] = _rmsnorm(acc_ref[...], g_ref[...]).astype(o_ref.dtype)


def _attn_sample(z, ck, cv, bias_qk, sinks, g, l, *, n_batch, t_len):
    buf = ck.shape[2] // N_KV_HEADS
    assert buf == WINDOW and t_len == SUBLANES and n_batch % DECODE_GROUP == 0
    rows = n_batch * t_len
    whole =lambda a: pl.BlockSpec(a.shape, lambda i: (0,) * a.ndim)
    cache = pl.BlockSpec((None,) + ck.shape[1:], lambda i: (l, 0, 0, 0))
    blocks = (z.size + 4 * ck[0].size + bias_qk.size) * 4 + rows * ATTN_W * 2
    return pl.pallas_call(
        functools.partial(_attn_sample_body, n_batch=n_batch, t_len=t_len, buf=buf, l=l),
        grid=(1,),
        in_specs=[whole(z), cache, cache, whole(bias_qk), pl.BlockSpec(memory_space=pltpu.SMEM),
                  pl.BlockSpec((None, 1, ATTN_W), lambda i: (l, 0, 0))],
        out_specs=[pl.BlockSpec((rows, ATTN_W), lambda i: (0, 0)),
                   pl.BlockSpec(ck.shape[1:], lambda i: (0, 0, 0)),
                   pl.BlockSpec(cv.shape[1:], lambda i: (0, 0, 0))],
        out_shape=[jax.ShapeDtypeStruct((rows, ATTN_W), bf16),
                   jax.ShapeDtypeStruct(ck.shape[1:], f32),
                   jax.ShapeDtypeStruct(cv.shape[1:], f32)],
        scratch_shapes=[pltpu.VMEM((rows, ATTN_W), f32)],
        compiler_params=_params(("arbitrary",), _vmem_limit(blocks, rows * ATTN_W * 4)),
        name="attn_sample",
    )(z, ck, cv, bias_qk, sinks, g)


def _lru_gates(xc, n, c0, wg_ref, ba_ref, bx_ref, lam_ref):
    lanes = pl.ds(c0, LRU_BLOCK)
    gates = jnp.dot(xc.astype(bf16), wg_ref[n], preferred_element_type=f32)
    gate_a = jax.nn.sigmoid(gates[:, :LRU_BLOCK] + ba_ref[:, lanes])
    gate_x = jax.nn.sigmoid(gates[:, LRU_BLOCK:] + bx_ref[:, lanes])
    log_a = -LRU_C * gate_a * jax.nn.softplus(-lam_ref[:, lanes])
    a = jnp.exp(log_a)
    y = -jnp.tanh(log_a) * (1.0 + a * a)
    mult = jnp.where(y > 0.0, y * lax.rsqrt(y), 0.0)
    return a, xc * gate_x, mult


def _tile_prefix(a, b, row):
    for d in (1, 2, 4):
        a_prev = pltpu.roll(a, d, axis=0)
        b_prev = pltpu.roll(b, d, axis=0)
        keep = row >= d
        b = jnp.where(keep, a * b_prev + b, b)
        a = jnp.where(keep, a * a_prev, a)
    return a, b


def _scan_tile(a, b, h_in, row):
    a, b = _tile_prefix(a, b, row)
    return a * h_in + b


def _last_row(h):
    return jnp.broadcast_to(h[SUBLANES - 1:SUBLANES, :], h.shape)


def _conv_taps(x, shifted, cw, cb):
    out = cb + shifted[CONV_W - 1] * cw[0:1]
    for j in range(1, CONV_W - 1):
        out = out + shifted[CONV_W - 1 - j] * cw[j:j + 1]
    return out + x * cw[CONV_W - 1:CONV_W]


REC_PROJECTIONS_AHEAD = 5


def _projection_issue_order(n_pairs, n_z):
    return [(0, p) for p in range(n_pairs)] + [(1, p) for p in range(n_pairs)] + [(2, p) for p in range(n_z)]


def _projection_consume_order(n_pairs, n_z):
    order = [step for p in range(n_pairs) for step in ((0, p), (1, p))]
    return order + [(2, p) for p in range(n_z)]


def _rec_prompt_body(h_ref, gmix_ref, wq_ref, wxl_ref, wxh_ref, wgl_ref, wgh_ref, prev_ref, h0_ref, cw_ref, cb_ref,
                     wg_ref, ba_ref, bx_ref, lam_ref, g_ref, z_ref, o_ref, lru_ref, tail_ref, u_ref, xprev_ref,
                     a_ref, b_ref, gate_ref, carry_ref, *, tc, n_chunks):
    c = pl.program_id(1)

    @pl.when(c == 0)
    def _():
        xprev_ref[...] = prev_ref[...]
        for n in range(N_LRU_BLOCKS):
            carry_ref[n] = jnp.broadcast_to(h0_ref[:, n * LRU_BLOCK:(n + 1) * LRU_BLOCK], (SUBLANES, LRU_BLOCK))

    u_ref[...] = _rmsnorm(h_ref[...], gmix_ref[...]).astype(bf16)
    u = u_ref[...]
    pair = 2 * LRU_BLOCK
    n_pairs = LRU_W // pair
    pairs_per_half = LRU_HALF // pair
    branch_refs = ((wxl_ref, wxh_ref), (wgl_ref, wgh_ref))

    def project(branch, p):
        if branch == 2:
            w = wq_ref[:, p * pair:(p + 1) * pair]
        else:
            q = p % pairs_per_half
            w = branch_refs[branch][p // pairs_per_half][:, q * pair:(q + 1) * pair]
        return jnp.dot(u, w, preferred_element_type=f32)

    def store_z(p, z):
        z_ref[:, p * pair:(p + 1) * pair] = z

    def gates(p, x):
        lanes = slice(p * pair, (p + 1) * pair)
        big = jnp.concatenate([xprev_ref[:, lanes], x], axis=0)
        shifted = {k: pltpu.roll(big, k, axis=0)[SUBLANES:] for k in range(1, CONV_W)}
        xc = _conv_taps(x, shifted, cw_ref[:, lanes], cb_ref[:, lanes])
        for q in range(2):
            n = 2 * p + q
            sub = slice(q * LRU_BLOCK, (q + 1) * LRU_BLOCK)
            a, gated, mult = _lru_gates(xc[:, sub], n, n * LRU_BLOCK, wg_ref, ba_ref, bx_ref, lam_ref)
            b = gated * mult
            a_ref[n] = a
            b_ref[n] = b
            b_ref[n, 0:1, :] = jnp.where(c == 0, gated[0:1, :], b[0:1, :])
        return x[tc - SUBLANES:, :]

    def gelu_gate(p, gate):
        for q in range(2):
            gate_ref[2 * p + q] = jax.nn.gelu(gate[:, q * LRU_BLOCK:(q + 1) * LRU_BLOCK])

    n_z = Z_XR // pair
    issue = _projection_issue_order(n_pairs, n_z)
    consumers = [((gates, gelu_gate, store_z)[br], br, p) for br, p in _projection_consume_order(n_pairs, n_z)]
    projected, tails = {}, []
    for consume, br, p in consumers:
        while issue and (len(projected) < REC_PROJECTIONS_AHEAD or (br, p) not in projected):
            key = issue.pop(0)
            projected[key] = project(*key)
        out = consume(p, projected.pop((br, p)))
        if consume is gates:
            tails.append(out)
    tail = jnp.concatenate(tails, axis=1)
    xprev_ref[...] = tail
    tail_ref[...] = tail

    seg = tc // SUBLANES
    seg_rows = lambda r: pl.ds(r, SUBLANES, stride=seg)

    def local_scan(r, maps):
        out = []
        for n in range(N_LRU_BLOCKS):
            a = a_ref[n, seg_rows(r), :]
            a_cum = a * maps[2 * n]
            b_cum = a * maps[2 * n + 1] + b_ref[n, seg_rows(r), :]
            a_ref[n, seg_rows(r), :] = a_cum
            b_ref[n, seg_rows(r), :] = b_cum
            out += [a_cum, b_cum]
        return tuple(out)
    identity = (jnp.ones((SUBLANES, LRU_BLOCK), f32), jnp.zeros((SUBLANES, LRU_BLOCK), f32)) * N_LRU_BLOCKS
    seg_maps = lax.fori_loop(0, seg, local_scan, identity, unroll=2)

    row = lax.broadcasted_iota(jnp.int32, (SUBLANES, LRU_BLOCK), 0)
    h_in = []
    for n in range(N_LRU_BLOCKS):
        h_prev = carry_ref[n]
        a_cum, b_cum = _tile_prefix(seg_maps[2 * n], seg_maps[2 * n + 1], row)
        h_end = a_cum * h_prev + b_cum
        h_in.append(jnp.where(row == 0, h_prev, pltpu.roll(h_end, 1, axis=0)))
        carry_ref[n] = _last_row(h_end)

    def apply_scan(r, _):
        for n in range(N_LRU_BLOCKS):
            h = a_ref[n, seg_rows(r), :] * h_in[n] + b_ref[n, seg_rows(r), :]
            b_ref[n, seg_rows(r), :] = h * gate_ref[n, seg_rows(r), :]
        return 0
    lax.fori_loop(0, seg, apply_scan, 0, unroll=2)

    y = jnp.concatenate([b_ref[n] for n in range(N_LRU_BLOCKS)], axis=1)
    o_ref[...] = _rmsnorm(y, g_ref[...]).astype(o_ref.dtype)

    @pl.when(c == n_chunks - 1)
    def _():
        lru_ref[...] = jnp.concatenate([carry_ref[n] for n in range(N_LRU_BLOCKS)], axis=1)


def _rec_prompt(h, g_mix, w_in, prev8, h0, cw, cb, wg, ba, bx, lam, g, l, *, n_batch, t_len, tc):
    n_chunks = t_len // tc
    once = pl.Buffered(1)
    row_vec = pl.BlockSpec((None, 1, LRU_W), lambda b, c: (l, 0, 0), pipeline_mode=once)
    w_half = lambda col: pl.BlockSpec((D_MODEL, LRU_HALF), lambda b, c: (0, col // LRU_HALF), pipeline_mode=once)
    state = pl.BlockSpec((None, SUBLANES, LRU_W), lambda b, c: (b, 0, 0))
    rows = lambda width: pl.BlockSpec((tc, width), lambda b, c: (b * n_chunks + c, 0))
    blocks = tc * D_MODEL * 4 + tc * Z_XR * 4 + tc * LRU_W * 2 + 3 * SUBLANES * LRU_W * 4
    scratch = (D_MODEL * IN_W * 2 + wg[0].size * 2 + tc * D_MODEL * 2 + (3 * tc + 2 * SUBLANES) * LRU_W * 4
               + 8 * tc * 2 * LRU_BLOCK * 4)
    by_block = pltpu.VMEM((N_LRU_BLOCKS, tc, LRU_BLOCK), f32)
    assert tc % SUBLANES == 0
    return pl.pallas_call(
        functools.partial(_rec_prompt_body, tc=tc, n_chunks=n_chunks),
        grid=(n_batch, n_chunks),
        in_specs=[rows(D_MODEL),
                  pl.BlockSpec((None, 1, D_MODEL), lambda b, c: (l, 0, 0), pipeline_mode=once),
                  pl.BlockSpec((D_MODEL, Z_XR), lambda b, c: (0, 0), pipeline_mode=once),
                  w_half(Z_XR), w_half(Z_XR + LRU_HALF), w_half(Z_GR), w_half(Z_GR + LRU_HALF),
                  state,
                  pl.BlockSpec((None, 1, LRU_W), lambda b, c: (b, 0, 0)),
                  pl.BlockSpec((None, CONV_W, LRU_W), lambda b, c: (l, 0, 0), pipeline_mode=once),
                  row_vec,
                  pl.BlockSpec((None,) + wg.shape[1:], lambda b, c: (l, 0, 0, 0), pipeline_mode=once),
                  row_vec, row_vec, row_vec, row_vec],
        out_specs=[rows(Z_XR), rows(LRU_W), state, state],
        out_shape=[jax.ShapeDtypeStruct((n_batch * t_len, Z_XR), f32),
                   jax.ShapeDtypeStruct((n_batch * t_len, LRU_W), bf16),
                   jax.ShapeDtypeStruct((n_batch, SUBLANES, LRU_W), f32),
                   jax.ShapeDtypeStruct((n_batch, SUBLANES, LRU_W), f32)],
        scratch_shapes=[pltpu.VMEM((tc, D_MODEL), bf16),
                        pltpu.VMEM((SUBLANES, LRU_W), f32), by_block, by_block, by_block,
                        pltpu.VMEM((N_LRU_BLOCKS, SUBLANES, LRU_BLOCK), f32)],
        compiler_params=_params(("parallel", "arbitrary"), _vmem_limit(blocks, scratch)),
        name="rec_prompt",
    )(h, g_mix, w_in, w_in, w_in, w_in, w_in, prev8, h0, cw, cb, wg, ba, bx, lam, g)


def _rec_sample_body(z_ref, prev_ref, h0_ref, cw_ref, cb_ref, wg_ref, ba_ref, bx_ref, lam_ref, g_ref,
                     o_ref, lru_ref, xc_ref, a_ref, b_ref, *, n_batch):
    row = lax.broadcasted_iota(jnp.int32, (SUBLANES, LRU_W), 0)

    def conv_step(b, _):
        r = pl.multiple_of(b * SUBLANES, SUBLANES)
        x = z_ref[pl.ds(r, SUBLANES), Z_XR:Z_XR + LRU_W]
        prev = prev_ref[pl.ds(r, SUBLANES), :]
        shifted = {k: jnp.where(row >= k, pltpu.roll(x, k, axis=0), pltpu.roll(prev, k, axis=0))
                   for k in range(1, CONV_W)}
        xc_ref[pl.ds(r, SUBLANES), :] = _conv_taps(x, shifted, cw_ref[...], cb_ref[...])
        return 0
    lax.fori_loop(0, n_batch, conv_step, 0)

    def gates_step(n, _):
        c0 = pl.multiple_of(n * LRU_BLOCK, LRU_BLOCK)
        lanes = pl.ds(c0, LRU_BLOCK)
        a, gated, mult = _lru_gates(xc_ref[:, lanes], n, c0, wg_ref, ba_ref, bx_ref, lam_ref)
        a_ref[:, lanes] = a
        b_ref[:, lanes] = gated * mult
        return 0
    lax.fori_loop(0, N_LRU_BLOCKS, gates_step, 0)

    def scan_step(b, _):
        rows = pl.ds(pl.multiple_of(b * SUBLANES, SUBLANES), SUBLANES)
        h_in = jnp.broadcast_to(h0_ref[pl.ds(b, 1), :], (SUBLANES, LRU_W))
        hh = _scan_tile(a_ref[rows, :], b_ref[rows, :], h_in, row)
        b_ref[rows, :] = hh * jax.nn.gelu(z_ref[rows, Z_GR:Z_GR + LRU_W])
        lru_ref[pl.ds(b, 1), :] = hh[SUBLANES - 1:SUBLANES, :]
        return 0
    lax.fori_loop(0, n_batch, scan_step, 0)
    o_ref[...] = _rmsnorm(b_ref[...], g_ref[...]).astype(o_ref.dtype)


def _rec_sample(z, prev8, h0, cw, cb, wg, ba, bx, lam, g, l, *, n_batch, t_len):
    assert t_len == SUBLANES and PAST_LEN > 0
    rows = n_batch * t_len
    layer = lambda a: pl.BlockSpec((None,) + a.shape[1:], lambda i: (l,) + (0,) * (a.ndim - 1))
    blocks = z.size * 4 + (2 * rows + 2 * n_batch) * LRU_W * 4 + wg[0].size * 2
    return pl.pallas_call(
        functools.partial(_rec_sample_body, n_batch=n_batch),
        grid=(1,),
        in_specs=[pl.BlockSpec(z.shape, lambda i: (0, 0))] + [layer(a) for a in (prev8, h0, cw, cb, wg, ba, bx, lam, g)],
        out_specs=[pl.BlockSpec((rows, LRU_W), lambda i: (0, 0)), pl.BlockSpec((n_batch, LRU_W), lambda i: (0, 0))],
        out_shape=[jax.ShapeDtypeStruct((rows, LRU_W), bf16),
                   jax.ShapeDtypeStruct((n_batch, LRU_W), f32)],
        scratch_shapes=[pltpu.VMEM((rows, LRU_W), f32)] * 3,
        compiler_params=_params(("arbitrary",), _vmem_limit(blocks, 3 * rows * LRU_W * 4)),
        name="rec_sample",
    )(z, prev8, h0, cw, cb, wg, ba, bx, lam, g)


def _mix_mlp_body(h_ref, attn_ref, rec_ref, wo_ref, g_ref, wu_ref, wd_ref, o_ref, u_ref):
    @pl.when(pl.program_id(1) == 0)
    def _():
        o_ref[...] = h_ref[...] + jnp.dot(attn_ref[...], wo_ref[0:ATTN_W, :], preferred_element_type=f32)
        o_ref[...] += jnp.dot(rec_ref[...], wo_ref[ATTN_W:ATTN_W + LRU_W, :], preferred_element_type=f32)
        u_ref[...] = _rmsnorm(o_ref[...], g_ref[...]).astype(bf16)
    hid = jnp.dot(u_ref[...], wu_ref[...], preferred_element_type=f32)
    hid = jnp.square(jnp.maximum(hid, 0.0)).astype(bf16)
    o_ref[...] += jnp.dot(hid, wd_ref[...], preferred_element_type=f32)


def _mix_mlp(h, attn, rec, wo, g, wu, wd, l, *, tm, tf):
    m = h.shape[0]
    once = pl.Buffered(1)
    blocks = 2 * tm * D_MODEL * 4 + tm * (ATTN_W + LRU_W) * 2 + 2 * D_MODEL * tf * 2
    scratch = wo[0].size * 2 + tm * D_MODEL * 2 + tm * tf * 6
    return pl.pallas_call(
        _mix_mlp_body,
        grid=(m // tm, D_FF // tf),
        in_specs=[pl.BlockSpec((tm, D_MODEL), lambda i, f: (i, 0)),
                  pl.BlockSpec((tm, ATTN_W), lambda i, f: (i, 0)),
                  pl.BlockSpec((tm, LRU_W), lambda i, f: (i, 0)),
                  pl.BlockSpec((None,) + wo.shape[1:], lambda i, f: (l, 0, 0), pipeline_mode=once),
                  pl.BlockSpec((None, 1, D_MODEL), lambda i, f: (l, 0, 0), pipeline_mode=once),
                  pl.BlockSpec((None, D_MODEL, tf), lambda i, f: (l, 0, f)),
                  pl.BlockSpec((None, tf, D_MODEL), lambda i, f: (l, f, 0))],
        out_specs=pl.BlockSpec((tm, D_MODEL), lambda i, f: (i, 0)),
        out_shape=jax.ShapeDtypeStruct((m, D_MODEL), f32),
        scratch_shapes=[pltpu.VMEM((tm, D_MODEL), bf16)],
        compiler_params=_params(("parallel", "arbitrary"), _vmem_limit(blocks, scratch)),
        name="mix_mlp",
    )(h, attn, rec, wo, g, wu, wd)


def _mix_mlp_cast_body(h_ref, attn_ref, rec_ref, wo_ref, g_ref, wu_ref, wd_ref,
                       o_ref, wob_ref, wub_ref, wdb_ref, u_ref):
    @pl.when(pl.program_id(1) == 0)
    def _():
        wob_ref[...] = wo_ref[...].astype(bf16)
        o_ref[...] = h_ref[...] + jnp.dot(attn_ref[...], wob_ref[0:ATTN_W, :], preferred_element_type=f32)
        o_ref[...] += jnp.dot(rec_ref[...], wob_ref[ATTN_W:ATTN_W + LRU_W, :], preferred_element_type=f32)
        u_ref[...] = _rmsnorm(o_ref[...], g_ref[...]).astype(bf16)
    wub_ref[...] = wu_ref[...].astype(bf16)
    wdb_ref[...] = wd_ref[...].astype(bf16)
    hid = jnp.dot(u_ref[...], wub_ref[...], preferred_element_type=f32)
    hid = jnp.square(jnp.maximum(hid, 0.0)).astype(bf16)
    o_ref[...] += jnp.dot(hid, wdb_ref[...], preferred_element_type=f32)


def _mix_mlp_cast(h, attn, rec, wo, g, wu, wd, l, *, tf):
    m = h.shape[0]
    once = pl.Buffered(1)
    blocks = 2 * D_MODEL * tf * (4 + 2)
    scratch = (2 * m * D_MODEL * 4 + m * (ATTN_W + LRU_W) * 2 + wo[0].size * (4 + 2) + m * D_MODEL * 2
               + m * tf * 6)
    return pl.pallas_call(
        _mix_mlp_cast_body,
        grid=(1, D_FF // tf),
        in_specs=[pl.BlockSpec((m, D_MODEL), lambda i, f: (0, 0), pipeline_mode=once),
                  pl.BlockSpec((m, ATTN_W), lambda i, f: (0, 0), pipeline_mode=once),
                  pl.BlockSpec((m, LRU_W), lambda i, f: (0, 0), pipeline_mode=once),
                  pl.BlockSpec((None,) + wo.shape[1:], lambda i, f: (l, 0, 0), pipeline_mode=once),
                  pl.BlockSpec((None, 1, D_MODEL), lambda i, f: (l, 0, 0), pipeline_mode=once),
                  pl.BlockSpec((None, D_MODEL, tf), lambda i, f: (l, 0, f)),
                  pl.BlockSpec((None, tf, D_MODEL), lambda i, f: (l, f, 0))],
        out_specs=[pl.BlockSpec((m, D_MODEL), lambda i, f: (0, 0)),
                   pl.BlockSpec(wo.shape[1:], lambda i, f: (0, 0)),
                   pl.BlockSpec((D_MODEL, tf), lambda i, f: (0, f)),
                   pl.BlockSpec((tf, D_MODEL), lambda i, f: (f, 0))],
        out_shape=[jax.ShapeDtypeStruct((m, D_MODEL), f32),
                   jax.ShapeDtypeStruct(wo.shape[1:], bf16),
                   jax.ShapeDtypeStruct(wu.shape[1:], bf16),
                   jax.ShapeDtypeStruct(wd.shape[1:], bf16)],
        scratch_shapes=[pltpu.VMEM((m, D_MODEL), bf16)],
        compiler_params=_params(("arbitrary", "arbitrary"), _vmem_limit(blocks, scratch)),
        name="mix_mlp_cast",
    )(h, attn, rec, wo, g, wu, wd)


def _final_norm_body(h_ref, g_ref, o_ref):
    o_ref[0] = _rmsnorm(h_ref[0], g_ref[...])


def _final_norm(h, g, *, skip, tr):
    n_batch, t_len, _ = h.shape
    s_len = t_len - skip
    tiles = tr // SUBLANES
    skip_tiles = skip // SUBLANES
    h4 = h.reshape(n_batch, t_len // SUBLANES, SUBLANES, D_MODEL)
    out = pl.pallas_call(
        _final_norm_body,
        grid=(n_batch, s_len // tr),
        in_specs=[pl.BlockSpec((pl.Element(1), pl.Element(tiles), pl.Element(SUBLANES), pl.Element(D_MODEL)),
                               lambda b, r: (b, skip_tiles + r * tiles, 0, 0)),
                  pl.BlockSpec((1, D_MODEL), lambda b, r: (0, 0))],
        out_specs=pl.BlockSpec((1, tiles, SUBLANES, D_MODEL), lambda b, r: (b, r, 0, 0)),
        out_shape=jax.ShapeDtypeStruct((n_batch, s_len // SUBLANES, SUBLANES, D_MODEL), f32),
        compiler_params=_params(("parallel", "parallel"), _vmem_limit(2 * tr * D_MODEL * 4)),
        name="final_norm",
    )(h4, g)
    return out.reshape(n_batch, s_len, D_MODEL)


def _row_tile(m, candidates):
    for tm in candidates:
        if m % tm == 0:
            return tm
    raise ValueError(f"no row tile for {m} rows")


def kernel(x_prompt, x_sample, cache_k_win, cache_v_win, state_conv, state_lru, meta_tokens, norm_mix_g, w_in,
           conv_w, conv_b, w_gate_a, b_gate_a, w_gate_x, b_gate_x, lru_lambda, attn_sinks, rel_bias, attn_out_g,
           rec_out_g, w_out, norm_mlp_g, w_up, w_down, final_norm_g):
    n_p, s_p, _ = x_prompt.shape
    n_s, t_s, _ = x_sample.shape
    t_p = N_META + s_p
    buf = cache_k_win.shape[2]
    assert t_p % BF16_ROWS == 0 and buf == WINDOW

    w_gates = jnp.concatenate([w_gate_a, w_gate_x], axis=-1).astype(bf16)
    rows3 = lambda p: p[:, None, :]
    g_mix, g_mlp, g_attn, g_rec = rows3(norm_mix_g), rows3(norm_mlp_g), rows3(attn_out_g), rows3(rec_out_g)
    cb3, ba3, bx3, lam3 = rows3(conv_b), rows3(b_gate_a), rows3(b_gate_x), rows3(lru_lambda)
    g_attn_t = jnp.broadcast_to(
        attn_out_g.reshape(DEPTH, N_Q_HEADS, HEAD_DIM).transpose(0, 2, 1)[..., None],
        (DEPTH, HEAD_DIM, N_Q_HEADS, WINDOW)).reshape(DEPTH, HEAD_DIM, ATTN_W)

    meta = jnp.broadcast_to(meta_tokens.astype(x_prompt.dtype)[None], (n_p, N_META, D_MODEL))
    hp = jnp.concatenate([meta, x_prompt], axis=1).reshape(n_p * t_p, D_MODEL)
    hs = x_sample.reshape(n_s * t_s, D_MODEL)

    bias_qk, bias_kq = _bias_tables(rel_bias)
    bias_qk = bias_qk.reshape(N_Q_HEADS * WINDOW, 2 * WINDOW)
    prev_p = jnp.zeros((n_p, SUBLANES, LRU_W), f32)
    h0_p = jnp.zeros((n_p, 1, LRU_W), f32)
    ck = cache_k_win.reshape(DEPTH, n_s, buf * N_KV_HEADS, HEAD_DIM)
    cv = cache_v_win.reshape(DEPTH, n_s, buf * N_KV_HEADS, HEAD_DIM)
    prev_s = jnp.pad(state_conv, ((0, 0), (0, 0), (SUBLANES - (CONV_W - 1), 0), (0, 0)))
    prev_s = prev_s.reshape(DEPTH, n_s * SUBLANES, LRU_W)

    tm_p = _row_tile(n_p * t_p, (688, 344))
    tc_p = _row_tile(t_p, (688, 344, 48, 16))
    tn = IN_W // 2

    kp_l, vp_l, cp_l, lp_l, ks_l, vs_l, cs_l, ls_l = ([] for _ in range(8))
    for l in range(DEPTH):
        z, w_in_b = _inproj_cast(hs, g_mix, w_in, l, tn=tn)
        attn, nk, nv = _attn_sample(z, ck, cv, bias_qk, attn_sinks, g_attn, l, n_batch=n_s, t_len=t_s)
        rec, lru = _rec_sample(z, prev_s, state_lru, conv_w, cb3, w_gates, ba3, bx3, lam3, g_rec, l,
                               n_batch=n_s, t_len=t_s)
        hs, w_out_b, w_up_b, w_down_b = _mix_mlp_cast(hs, attn, rec, w_out, g_mlp, w_up, w_down, l, tf=512)
        z3 = z.reshape(n_s, t_s, IN_W)
        ks_l.append(nk)
        vs_l.append(nv)
        cs_l.append(z3[:, t_s - (CONV_W - 1):, Z_XR:Z_XR + LRU_W])
        ls_l.append(lru)
        z, rec, lru, x_tail = _rec_prompt(hp, g_mix, w_in_b, prev_p, h0_p, conv_w, cb3, w_gates, ba3, bx3, lam3,
                                          g_rec, l, n_batch=n_p, t_len=t_p, tc=tc_p)
        attn = _attn_prompt(z, bias_kq, attn_sinks, g_attn_t, l, n_batch=n_p, t_len=t_p)
        hp = _mix_mlp(hp, attn, rec, w_out_b[None], g_mlp[l:l + 1], w_up_b[None], w_down_b[None], 0,
                      tm=tm_p, tf=1024)
        z3 = z.reshape(n_p, t_p, Z_XR)
        kp_l.append(z3[:, t_p - WINDOW:, Z_K:Z_K + KV_W].reshape(n_p, WINDOW, N_KV_HEADS, HEAD_DIM))
        vp_l.append(z3[:, t_p - WINDOW:, Z_V:Z_V + KV_W].reshape(n_p, WINDOW, N_KV_HEADS, HEAD_DIM))
        cp_l.append(x_tail[:, SUBLANES - (CONV_W - 1):])
        lp_l.append(lru[:, 0])

    g_fin = final_norm_g[None, :]
    y_prompt = _final_norm(hp.reshape(n_p, t_p, D_MODEL), g_fin, skip=N_META, tr=512)
    y_sample = _final_norm(hs.reshape(1, n_s * t_s, D_MODEL), g_fin, skip=0, tr=n_s * t_s)
    y_sample = y_sample.reshape(n_s, t_s, D_MODEL)
    cache_shape = (DEPTH, n_s, buf, N_KV_HEADS, HEAD_DIM)
    return (y_prompt, y_sample,
            jnp.stack(kp_l), jnp.stack(vp_l), jnp.stack(cp_l), jnp.stack(lp_l),
            jnp.stack(ks_l).reshape(cache_shape), jnp.stack(vs_l).reshape(cache_shape),
            jnp.stack(cs_l), jnp.stack(ls_l))
```

```python
import functools
import math

import jax
import jax.numpy as jnp
from jax import lax
from jax.experimental import pallas as pl
from jax.experimental.pallas import tpu as pltpu

f32 = jnp.float32
bf16 = jnp.bfloat16

D_MODEL = 2048
DEPTH = 4
PAST_LEN = 16384
HEAD_DIM = 128
N_Q_HEADS = 8
N_KV_HEADS = 2
Q_PER_KV = N_Q_HEADS // N_KV_HEADS
ATTN_W = N_Q_HEADS * HEAD_DIM
KV_W = N_KV_HEADS * HEAD_DIM
LRU_W = D_MODEL // 2
N_LRU_BLOCKS = 8
LRU_BLOCK = LRU_W // N_LRU_BLOCKS
CONV_W = 4
LRU_C = 8.0
IN_W = ATTN_W + 2 * KV_W + 2 * LRU_W
D_FF = 4 * D_MODEL
WINDOW = 128
N_BUCKETS = 32
MAX_DISTANCE = 128
N_META = 16
EPS = 1e-6
ATTN_SCALE = HEAD_DIM ** -0.5
INV_ATTN_SCALE = HEAD_DIM ** 0.5
EXP2_PER_T = math.log2(math.e) / INV_ATTN_SCALE

Z_Q = 0
Z_K = ATTN_W
Z_V = Z_K + KV_W
Z_XR = Z_V + KV_W
Z_GR = Z_XR + LRU_W
LRU_HALF = LRU_W // 2

ATTN_LEAD = WINDOW + (-N_META) % WINDOW

SUBLANES = 8
BF16_ROWS = 16
V7X_VMEM_BYTES = 64 * 1024 * 1024
VMEM_CAP_BYTES = V7X_VMEM_BYTES - 2 * 1024 * 1024


def _vmem_limit(pipelined_bytes, scratch_bytes=0):
    est = 2 * pipelined_bytes + scratch_bytes
    return int(min(VMEM_CAP_BYTES, est + est // 2 + (8 << 20)))


def _params(semantics, vmem_bytes):
    return pltpu.CompilerParams(dimension_semantics=semantics, vmem_limit_bytes=vmem_bytes)


def _rms_scale(x):
    return lax.rsqrt(jnp.mean(x * x, axis=-1, keepdims=True) + EPS)


def _rmsnorm(x, g):
    return x * _rms_scale(x) * g


def _inproj_cast_body(h_ref, g_ref, w_ref, z_ref, wb_ref, u_ref):
    @pl.when(pl.program_id(0) == 0)
    def _():
        u_ref[...] = _rmsnorm(h_ref[...], g_ref[...]).astype(bf16)
    wb_ref[...] = w_ref[...].astype(bf16)
    z_ref[...] = jnp.dot(u_ref[...], wb_ref[...], preferred_element_type=f32)


def _inproj_cast(h, g, w, l, *, tn):
    m = h.shape[0]
    blocks = D_MODEL * tn * (4 + 2) + m * tn * 4
    return pl.pallas_call(
        _inproj_cast_body,
        grid=(IN_W // tn,),
        in_specs=[pl.BlockSpec((m, D_MODEL), lambda j: (0, 0)),
                  pl.BlockSpec((None, 1, D_MODEL), lambda j: (l, 0, 0)),
                  pl.BlockSpec((None, D_MODEL, tn), lambda j: (l, 0, j))],
        out_specs=[pl.BlockSpec((m, tn), lambda j: (0, j)),
                   pl.BlockSpec((D_MODEL, tn), lambda j: (0, j))],
        out_shape=[jax.ShapeDtypeStruct((m, IN_W), f32), jax.ShapeDtypeStruct(w.shape[1:], bf16)],
        scratch_shapes=[pltpu.VMEM((m, D_MODEL), bf16)],
        compiler_params=_params(("arbitrary",), _vmem_limit(blocks, m * D_MODEL * (2 * 4 + 2))),
        name="inproj_cast",
    )(h, g, w)


def _rel_bias_of(dist, rel_ref, h):
    n = jnp.maximum(dist, 0)
    max_exact = N_BUCKETS // 2
    nf = jnp.maximum(n, 1).astype(f32)
    large = max_exact + (jnp.log(nf / max_exact) / math.log(MAX_DISTANCE / max_exact)
                         * (N_BUCKETS - max_exact)).astype(jnp.int32)
    large = jnp.minimum(large, N_BUCKETS - 1)
    bucket = jnp.where(n < max_exact, n, large)
    acc = jnp.zeros(dist.shape, f32)
    for b in range(N_BUCKETS):
        acc = jnp.where(bucket == b, rel_ref[b, h], acc)
    return jnp.where((dist >= 0) & (dist < WINDOW), acc, -jnp.inf)


def _bias_table_body(rel_ref, qk_ref, kq_ref):
    shape_qk = (WINDOW, 2 * WINDOW)
    dist_qk = (lax.broadcasted_iota(jnp.int32, shape_qk, 0) + WINDOW
               - lax.broadcasted_iota(jnp.int32, shape_qk, 1))
    shape_kq = (2 * WINDOW, WINDOW)
    dist_kq = (lax.broadcasted_iota(jnp.int32, shape_kq, 1) + WINDOW
               - lax.broadcasted_iota(jnp.int32, shape_kq, 0))
    for h in range(N_Q_HEADS):
        kv, g = divmod(h, Q_PER_KV)
        qk_ref[h] = _rel_bias_of(dist_qk, rel_ref, h)
        kq_ref[kv, :, g * WINDOW:(g + 1) * WINDOW] = _rel_bias_of(dist_kq, rel_ref, h) * INV_ATTN_SCALE


def _bias_tables(rel_bias):
    return pl.pallas_call(
        _bias_table_body,
        in_specs=[pl.BlockSpec(memory_space=pltpu.SMEM)],
        out_shape=[jax.ShapeDtypeStruct((N_Q_HEADS, WINDOW, 2 * WINDOW), f32),
                   jax.ShapeDtypeStruct((N_KV_HEADS, 2 * WINDOW, Q_PER_KV * WINDOW), f32)],
        name="bias_tables",
    )(rel_bias)


def _attn_blocks_t(blocks, kpad_ref, vt_ref, bias_ref, sinks_ref, l, gt_ref):
    kv_heads = [range(kv * Q_PER_KV, (kv + 1) * Q_PER_KV) for kv in range(N_KV_HEADS)]
    sink = [jnp.concatenate([jnp.full((1, WINDOW), sinks_ref[l, h] * INV_ATTN_SCALE, f32) for h in heads], axis=1)
            for heads in kv_heads]
    chains = [(i, kv) for i in range(len(blocks)) for kv in range(N_KV_HEADS)]
    scores = {}
    for i, kv in chains:
        q_rows, r0, _ = blocks[i]
        kwin = kpad_ref[pl.ds(r0, 2 * WINDOW), kv * HEAD_DIM:(kv + 1) * HEAD_DIM]
        q = jnp.concatenate([q_rows[:, h * HEAD_DIM:(h + 1) * HEAD_DIM] for h in kv_heads[kv]],
                            axis=0).astype(bf16)
        scores[i, kv] = lax.dot_general(kwin, q, (((1,), (1,)), ((), ())), preferred_element_type=f32)
    probs, invs = {}, {}
    for i, kv in chains:
        t = scores[i, kv] + bias_ref[kv]
        lead_keys = blocks[i][2]
        if lead_keys:
            key = lax.broadcasted_iota(jnp.int32, t.shape, 0)
            t = jnp.where(key < lead_keys, -jnp.inf, t)
        m = jnp.maximum(jnp.max(t, axis=0, keepdims=True), sink[kv])
        p = jnp.exp2((t - m) * EXP2_PER_T)
        invs[i, kv] = 1.0 / (jnp.sum(p, axis=0, keepdims=True) + jnp.exp2((sink[kv] - m) * EXP2_PER_T))
        probs[i, kv] = p.astype(bf16)
    raws = {}
    for i, kv in chains:
        vt = vt_ref[kv * HEAD_DIM:(kv + 1) * HEAD_DIM, pl.ds(blocks[i][1], 2 * WINDOW)]
        raws[i, kv] = jnp.dot(vt, probs[i, kv], preferred_element_type=f32)
    outs = []
    for i in range(len(blocks)):
        raw = jnp.concatenate([raws[i, kv] for kv in range(N_KV_HEADS)], axis=1)
        inv = jnp.concatenate([invs[i, kv] for kv in range(N_KV_HEADS)], axis=1)
        sq = jnp.sum(raw * raw, axis=0, keepdims=True) * (inv * inv)
        ssq = sq[:, 0:WINDOW]
        for h in range(1, N_Q_HEADS):
            ssq = ssq + sq[:, h * WINDOW:(h + 1) * WINDOW]
        r = lax.rsqrt(ssq * (1.0 / ATTN_W) + EPS)
        yt = raw * (inv * jnp.concatenate([r] * N_Q_HEADS, axis=1)) * gt_ref[...]
        outs.append(jnp.concatenate([yt[:, h * WINDOW:(h + 1) * WINDOW].T for h in range(N_Q_HEADS)], axis=1))
    return outs


ATTN_GROUP = 5


def _attn_prompt_body(q_ref, k_ref, v_ref, bias_ref, sinks_ref, gt_ref, o_ref, kpad_ref, vpad_ref, vt_ref,
                      *, t_len, l):
    pad_rows = kpad_ref.shape[0]
    kpad_ref[0:ATTN_LEAD, :] = jnp.zeros((ATTN_LEAD, KV_W), bf16)
    kpad_ref[ATTN_LEAD:pad_rows, :] = k_ref[...].astype(bf16)
    vpad_ref[0:ATTN_LEAD, :] = jnp.zeros((ATTN_LEAD, KV_W), f32)
    vpad_ref[ATTN_LEAD:pad_rows, :] = v_ref[...]

    def transpose_step(c, _):
        r = pl.multiple_of(c * WINDOW, WINDOW)
        chunk = vpad_ref[pl.ds(r, WINDOW), :]
        for kv in range(N_KV_HEADS):
            vt_ref[kv * HEAD_DIM:(kv + 1) * HEAD_DIM, pl.ds(r, WINDOW)] = (
                chunk[:, kv * HEAD_DIM:(kv + 1) * HEAD_DIM].T.astype(bf16))
        return 0
    lax.fori_loop(0, pad_rows // WINDOW, transpose_step, 0, unroll=3)

    def q_rows_of(j):
        start = j * WINDOW - (ATTN_LEAD - WINDOW)
        return pl.ds(start if isinstance(j, int) else pl.multiple_of(start, BF16_ROWS), WINDOW)

    n_first = 2 * WINDOW - ATTN_LEAD
    q0 = jnp.concatenate([jnp.zeros((WINDOW - n_first, ATTN_W), f32), q_ref[0:n_first, :]], axis=0)
    y0, y1 = _attn_blocks_t([(q0, 0, ATTN_LEAD), (q_ref[q_rows_of(1), :], WINDOW, ATTN_LEAD - WINDOW)],
                            kpad_ref, vt_ref, bias_ref, sinks_ref, l, gt_ref)
    o_ref[0:n_first, :] = y0[WINDOW - n_first:, :].astype(o_ref.dtype)
    o_ref[q_rows_of(1), :] = y1.astype(o_ref.dtype)

    def step(i, _):
        js = [2 + i * ATTN_GROUP + g for g in range(ATTN_GROUP)]
        ys = _attn_blocks_t([(q_ref[q_rows_of(j), :], pl.multiple_of(j * WINDOW, WINDOW), 0) for j in js],
                            kpad_ref, vt_ref, bias_ref, sinks_ref, l, gt_ref)
        for j, y in zip(js, ys):
            o_ref[q_rows_of(j), :] = y.astype(o_ref.dtype)
        return 0
    n_blocks = pad_rows // WINDOW - 1
    assert (n_blocks - 2) % ATTN_GROUP == 0
    lax.fori_loop(0, (n_blocks - 2) // ATTN_GROUP, step, 0)


def _attn_prompt(q, k, v, bias_kq, sinks, gt, l, *, n_batch, t_len):
    pad_rows = ATTN_LEAD + t_len
    assert pad_rows % WINDOW == 0 and (ATTN_LEAD - WINDOW) % BF16_ROWS == 0
    blocks = t_len * (ATTN_W + 2 * KV_W) * 4 + t_len * ATTN_W * 2 + bias_kq.size * 4 + HEAD_DIM * ATTN_W * 4
    scratch = pad_rows * KV_W * (2 + 4 + 2)
    return pl.pallas_call(
        functools.partial(_attn_prompt_body, t_len=t_len, l=l),
        grid=(n_batch,),
        in_specs=[pl.BlockSpec((t_len, ATTN_W), lambda b: (b, 0)),
                  pl.BlockSpec((t_len, KV_W), lambda b: (b, 0)),
                  pl.BlockSpec((t_len, KV_W), lambda b: (b, 0)),
                  pl.BlockSpec(bias_kq.shape, lambda b: (0, 0, 0)),
                  pl.BlockSpec(memory_space=pltpu.SMEM),
                  pl.BlockSpec((None, HEAD_DIM, ATTN_W), lambda b: (l, 0, 0))],
        out_specs=pl.BlockSpec((t_len, ATTN_W), lambda b: (b, 0)),
        out_shape=jax.ShapeDtypeStruct((n_batch * t_len, ATTN_W), bf16),
        scratch_shapes=[pltpu.VMEM((pad_rows, KV_W), bf16), pltpu.VMEM((pad_rows, KV_W), f32),
                        pltpu.VMEM((KV_W, pad_rows), bf16)],
        compiler_params=_params(("parallel",), _vmem_limit(blocks, scratch)),
        name="attn_prompt",
    )(q, k, v, bias_kq, sinks, gt)


DECODE_GROUP = 8


def _attn_sample_body(z_ref, ck_ref, cv_ref, bias_ref, sinks_ref, g_ref, o_ref, nk_ref, nv_ref, acc_ref,
                      *, n_batch, t_len, buf, l):
    zeros = jnp.zeros((2 * WINDOW - buf - t_len, HEAD_DIM), f32)
    kept = N_KV_HEADS * (buf - t_len)
    kv_heads = [range(kv * Q_PER_KV, (kv + 1) * Q_PER_KV) for kv in range(N_KV_HEADS)]
    bias = [jnp.concatenate([bias_ref[h * WINDOW:h * WINDOW + t_len, :] for h in heads], axis=0)
            for heads in kv_heads]
    sink = [jnp.concatenate([jnp.full((t_len, 1), sinks_ref[l, h], f32) for h in heads], axis=0)
            for heads in kv_heads]

    def window(c_ref, n_ref, b, r0, col, kv):
        new = z_ref[pl.ds(r0, t_len), col + kv * HEAD_DIM:col + (kv + 1) * HEAD_DIM]
        n_ref[b, pl.ds(kept + kv, t_len, stride=N_KV_HEADS), :] = new
        old = c_ref[b, pl.ds(kv, buf, stride=N_KV_HEADS), :]
        return jnp.concatenate([old, new, zeros], axis=0).astype(bf16)

    def group_step(i, _):
        seqs = [i * DECODE_GROUP + j for j in range(DECODE_GROUP)]
        rows = [pl.multiple_of(b * t_len, t_len) for b in seqs]
        scores = {}
        for j, (b, r0) in enumerate(zip(seqs, rows)):
            nk_ref[b, 0:kept, :] = ck_ref[b, N_KV_HEADS * t_len:N_KV_HEADS * buf, :]
            nv_ref[b, 0:kept, :] = cv_ref[b, N_KV_HEADS * t_len:N_KV_HEADS * buf, :]
            for kv, heads in enumerate(kv_heads):
                q = jnp.concatenate([z_ref[pl.ds(r0, t_len), Z_Q + h * HEAD_DIM:Z_Q + (h + 1) * HEAD_DIM]
                                     for h in heads], axis=0).astype(bf16)
                k = window(ck_ref, nk_ref, b, r0, Z_K, kv)
                scores[j, kv] = lax.dot_general(q, k, (((1,), (1,)), ((), ())), preferred_element_type=f32)
        probs, denoms = {}, {}
        for key, s in scores.items():
            kv = key[1]
            s = s * ATTN_SCALE + bias[kv]
            m = jnp.maximum(jnp.max(s, axis=-1, keepdims=True), sink[kv])
            p = jnp.exp(s - m)
            denoms[key] = jnp.sum(p, axis=-1, keepdims=True) + jnp.exp(sink[kv] - m)
            probs[key] = p.astype(bf16)
        for j, (b, r0) in enumerate(zip(seqs, rows)):
            outs = []
            for kv in range(N_KV_HEADS):
                v = window(cv_ref, nv_ref, b, r0, Z_V, kv)
                o = jnp.dot(probs[j, kv], v, preferred_element_type=f32) / denoms[j, kv]
                outs.extend(o[g * t_len:(g + 1) * t_len] for g in range(Q_PER_KV))
            acc_ref[pl.ds(r0, t_len), :] = jnp.concatenate(outs, axis=1)
        return 0
    lax.fori_loop(0, n_batch // DECODE_GROUP, group_step, 0)
    o_ref[...] = _rmsnorm(acc_ref[...], g_ref[...]).astype(o_ref.dtype)


def _attn_sample(z, ck, cv, bias_qk, sinks, g, l, *, n_batch, t_len):
    buf = ck.shape[2] // N_KV_HEADS
    assert buf == WINDOW and t_len == SUBLANES and n_batch % DECODE_GROUP == 0
    rows = n_batch * t_len
    whole =lambda a: pl.BlockSpec(a.shape, lambda i: (0,) * a.ndim)
    cache = pl.BlockSpec((None,) + ck.shape[1:], lambda i: (l, 0, 0, 0))
    blocks = (z.size + 4 * ck[0].size + bias_qk.size) * 4 + rows * ATTN_W * 2
    return pl.pallas_call(
        functools.partial(_attn_sample_body, n_batch=n_batch, t_len=t_len, buf=buf, l=l),
        grid=(1,),
        in_specs=[whole(z), cache, cache, whole(bias_qk), pl.BlockSpec(memory_space=pltpu.SMEM),
                  pl.BlockSpec((None, 1, ATTN_W), lambda i: (l, 0, 0))],
        out_specs=[pl.BlockSpec((rows, ATTN_W), lambda i: (0, 0)),
                   pl.BlockSpec(ck.shape[1:], lambda i: (0, 0, 0)),
                   pl.BlockSpec(cv.shape[1:], lambda i: (0, 0, 0))],
        out_shape=[jax.ShapeDtypeStruct((rows, ATTN_W), bf16),
                   jax.ShapeDtypeStruct(ck.shape[1:], f32),
                   jax.ShapeDtypeStruct(cv.shape[1:], f32)],
        scratch_shapes=[pltpu.VMEM((rows, ATTN_W), f32)],
        compiler_params=_params(("arbitrary",), _vmem_limit(blocks, rows * ATTN_W * 4)),
        name="attn_sample",
    )(z, ck, cv, bias_qk, sinks, g)


def _lru_gates(xc, n, c0, wg_ref, ba_ref, bx_ref, lam_ref):
    lanes = pl.ds(c0, LRU_BLOCK)
    gates = jnp.dot(xc.astype(bf16), wg_ref[n], preferred_element_type=f32)
    gate_a = jax.nn.sigmoid(gates[:, :LRU_BLOCK] + ba_ref[:, lanes])
    gate_x = jax.nn.sigmoid(gates[:, LRU_BLOCK:] + bx_ref[:, lanes])
    log_a = -LRU_C * gate_a * jax.nn.softplus(-lam_ref[:, lanes])
    a = jnp.exp(log_a)
    y = -jnp.tanh(log_a) * (1.0 + a * a)
    mult = jnp.where(y > 0.0, y * lax.rsqrt(y), 0.0)
    return a, xc * gate_x, mult


def _tile_prefix(a, b, row):
    for d in (1, 2, 4):
        a_prev = pltpu.roll(a, d, axis=0)
        b_prev = pltpu.roll(b, d, axis=0)
        keep = row >= d
        b = jnp.where(keep, a * b_prev + b, b)
        a = jnp.where(keep, a * a_prev, a)
    return a, b


def _scan_tile(a, b, h_in, row):
    a, b = _tile_prefix(a, b, row)
    return a * h_in + b


def _last_row(h):
    return jnp.broadcast_to(h[SUBLANES - 1:SUBLANES, :], h.shape)


def _conv_taps(x, shifted, cw, cb):
    out = cb + shifted[CONV_W - 1] * cw[0:1]
    for j in range(1, CONV_W - 1):
        out = out + shifted[CONV_W - 1 - j] * cw[j:j + 1]
    return out + x * cw[CONV_W - 1:CONV_W]


REC_PROJECTIONS_AHEAD = 5


def _projection_issue_order(n_pairs, n_z):
    return [(0, p) for p in range(n_pairs)] + [(1, p) for p in range(n_pairs)] + [(2, p) for p in range(n_z)]


def _projection_consume_order(n_pairs, n_z):
    order = [step for p in range(n_pairs) for step in ((0, p), (1, p))]
    return order + [(2, p) for p in range(n_z)]


def _rec_prompt_body(h_ref, gmix_ref, wq_ref, wxl_ref, wxh_ref, wgl_ref, wgh_ref, prev_ref, h0_ref, cw_ref, cb_ref,
                     wg_ref, ba_ref, bx_ref, lam_ref, g_ref, zq_ref, zk_ref, zv_ref, o_ref, lru_ref, tail_ref, u_ref,
                     xprev_ref,
                     a_ref, b_ref, gate_ref, carry_ref, *, tc, n_chunks):
    c = pl.program_id(1)

    @pl.when(c == 0)
    def _():
        xprev_ref[...] = prev_ref[...]
        for n in range(N_LRU_BLOCKS):
            carry_ref[n] = jnp.broadcast_to(h0_ref[:, n * LRU_BLOCK:(n + 1) * LRU_BLOCK], (SUBLANES, LRU_BLOCK))

    u_ref[...] = _rmsnorm(h_ref[...], gmix_ref[...]).astype(bf16)
    u = u_ref[...]
    pair = 2 * LRU_BLOCK
    n_pairs = LRU_W // pair
    pairs_per_half = LRU_HALF // pair
    branch_refs = ((wxl_ref, wxh_ref), (wgl_ref, wgh_ref))

    def project(branch, p):
        if branch == 2:
            w = wq_ref[:, p * pair:(p + 1) * pair]
        else:
            q = p % pairs_per_half
            w = branch_refs[branch][p // pairs_per_half][:, q * pair:(q + 1) * pair]
        return jnp.dot(u, w, preferred_element_type=f32)

    def store_z(p, z):
        c0 = p * pair
        if c0 < Z_K:
            zq_ref[:, c0:c0 + pair] = z
        else:
            (zk_ref if c0 < Z_V else zv_ref)[...] = z

    def gates(p, x):
        lanes = slice(p * pair, (p + 1) * pair)
        big = jnp.concatenate([xprev_ref[:, lanes], x], axis=0)
        shifted = {k: pltpu.roll(big, k, axis=0)[SUBLANES:] for k in range(1, CONV_W)}
        xc = _conv_taps(x, shifted, cw_ref[:, lanes], cb_ref[:, lanes])
        for q in range(2):
            n = 2 * p + q
            sub = slice(q * LRU_BLOCK, (q + 1) * LRU_BLOCK)
            a, gated, mult = _lru_gates(xc[:, sub], n, n * LRU_BLOCK, wg_ref, ba_ref, bx_ref, lam_ref)
            b = gated * mult
            a_ref[n] = a
            b_ref[n] = b
            b_ref[n, 0:1, :] = jnp.where(c == 0, gated[0:1, :], b[0:1, :])
        return x[tc - SUBLANES:, :]

    def gelu_gate(p, gate):
        for q in range(2):
            gate_ref[2 * p + q] = jax.nn.gelu(gate[:, q * LRU_BLOCK:(q + 1) * LRU_BLOCK])

    n_z = Z_XR // pair
    issue = _projection_issue_order(n_pairs, n_z)
    consumers = [((gates, gelu_gate, store_z)[br], br, p) for br, p in _projection_consume_order(n_pairs, n_z)]
    projected, tails = {}, []
    for consume, br, p in consumers:
        while issue and (len(projected) < REC_PROJECTIONS_AHEAD or (br, p) not in projected):
            key = issue.pop(0)
            projected[key] = project(*key)
        out = consume(p, projected.pop((br, p)))
        if consume is gates:
            tails.append(out)
    tail = jnp.concatenate(tails, axis=1)
    xprev_ref[...] = tail
    tail_ref[...] = tail

    seg = tc // SUBLANES
    seg_rows = lambda r: pl.ds(r, SUBLANES, stride=seg)

    def local_scan(r, maps):
        out = []
        for n in range(N_LRU_BLOCKS):
            a = a_ref[n, seg_rows(r), :]
            a_cum = a * maps[2 * n]
            b_cum = a * maps[2 * n + 1] + b_ref[n, seg_rows(r), :]
            a_ref[n, seg_rows(r), :] = a_cum
            b_ref[n, seg_rows(r), :] = b_cum
            out += [a_cum, b_cum]
        return tuple(out)
    identity = (jnp.ones((SUBLANES, LRU_BLOCK), f32), jnp.zeros((SUBLANES, LRU_BLOCK), f32)) * N_LRU_BLOCKS
    seg_maps = lax.fori_loop(0, seg, local_scan, identity, unroll=2)

    row = lax.broadcasted_iota(jnp.int32, (SUBLANES, LRU_BLOCK), 0)
    h_in = []
    for n in range(N_LRU_BLOCKS):
        h_prev = carry_ref[n]
        a_cum, b_cum = _tile_prefix(seg_maps[2 * n], seg_maps[2 * n + 1], row)
        h_end = a_cum * h_prev + b_cum
        h_in.append(jnp.where(row == 0, h_prev, pltpu.roll(h_end, 1, axis=0)))
        carry_ref[n] = _last_row(h_end)

    def apply_scan(r, _):
        for n in range(N_LRU_BLOCKS):
            h = a_ref[n, seg_rows(r), :] * h_in[n] + b_ref[n, seg_rows(r), :]
            b_ref[n, seg_rows(r), :] = h * gate_ref[n, seg_rows(r), :]
        return 0
    lax.fori_loop(0, seg, apply_scan, 0, unroll=2)

    y = jnp.concatenate([b_ref[n] for n in range(N_LRU_BLOCKS)], axis=1)
    o_ref[...] = _rmsnorm(y, g_ref[...]).astype(o_ref.dtype)

    @pl.when(c == n_chunks - 1)
    def _():
        lru_ref[...] = jnp.concatenate([carry_ref[n] for n in range(N_LRU_BLOCKS)], axis=1)


def _rec_prompt(h, g_mix, w_in, prev8, h0, cw, cb, wg, ba, bx, lam, g, l, *, n_batch, t_len, tc):
    n_chunks = t_len // tc
    once = pl.Buffered(1)
    row_vec = pl.BlockSpec((None, 1, LRU_W), lambda b, c: (l, 0, 0), pipeline_mode=once)
    w_half = lambda col: pl.BlockSpec((D_MODEL, LRU_HALF), lambda b, c: (0, col // LRU_HALF), pipeline_mode=once)
    state = pl.BlockSpec((None, SUBLANES, LRU_W), lambda b, c: (b, 0, 0))
    rows = lambda width: pl.BlockSpec((tc, width), lambda b, c: (b * n_chunks + c, 0))
    blocks = tc * D_MODEL * 4 + tc * Z_XR * 4 + tc * LRU_W * 2 + 3 * SUBLANES * LRU_W * 4
    scratch = (D_MODEL * IN_W * 2 + wg[0].size * 2 + tc * D_MODEL * 2 + (3 * tc + 2 * SUBLANES) * LRU_W * 4
               + 8 * tc * 2 * LRU_BLOCK * 4)
    by_block = pltpu.VMEM((N_LRU_BLOCKS, tc, LRU_BLOCK), f32)
    assert tc % SUBLANES == 0
    return pl.pallas_call(
        functools.partial(_rec_prompt_body, tc=tc, n_chunks=n_chunks),
        grid=(n_batch, n_chunks),
        in_specs=[rows(D_MODEL),
                  pl.BlockSpec((None, 1, D_MODEL), lambda b, c: (l, 0, 0), pipeline_mode=once),
                  pl.BlockSpec((D_MODEL, Z_XR), lambda b, c: (0, 0), pipeline_mode=once),
                  w_half(Z_XR), w_half(Z_XR + LRU_HALF), w_half(Z_GR), w_half(Z_GR + LRU_HALF),
                  state,
                  pl.BlockSpec((None, 1, LRU_W), lambda b, c: (b, 0, 0)),
                  pl.BlockSpec((None, CONV_W, LRU_W), lambda b, c: (l, 0, 0), pipeline_mode=once),
                  row_vec,
                  pl.BlockSpec((None,) + wg.shape[1:], lambda b, c: (l, 0, 0, 0), pipeline_mode=once),
                  row_vec, row_vec, row_vec, row_vec],
        out_specs=[rows(ATTN_W), rows(KV_W), rows(KV_W), rows(LRU_W), state, state],
        out_shape=[jax.ShapeDtypeStruct((n_batch * t_len, ATTN_W), f32),
                   jax.ShapeDtypeStruct((n_batch * t_len, KV_W), f32),
                   jax.ShapeDtypeStruct((n_batch * t_len, KV_W), f32),
                   jax.ShapeDtypeStruct((n_batch * t_len, LRU_W), bf16),
                   jax.ShapeDtypeStruct((n_batch, SUBLANES, LRU_W), f32),
                   jax.ShapeDtypeStruct((n_batch, SUBLANES, LRU_W), f32)],
        scratch_shapes=[pltpu.VMEM((tc, D_MODEL), bf16),
                        pltpu.VMEM((SUBLANES, LRU_W), f32), by_block, by_block, by_block,
                        pltpu.VMEM((N_LRU_BLOCKS, SUBLANES, LRU_BLOCK), f32)],
        compiler_params=_params(("parallel", "arbitrary"), _vmem_limit(blocks, scratch)),
        name="rec_prompt",
    )(h, g_mix, w_in, w_in, w_in, w_in, w_in, prev8, h0, cw, cb, wg, ba, bx, lam, g)


def _rec_sample_body(z_ref, prev_ref, h0_ref, cw_ref, cb_ref, wg_ref, ba_ref, bx_ref, lam_ref, g_ref,
                     o_ref, lru_ref, xc_ref, a_ref, b_ref, *, n_batch):
    row = lax.broadcasted_iota(jnp.int32, (SUBLANES, LRU_W), 0)

    def conv_step(b, _):
        r = pl.multiple_of(b * SUBLANES, SUBLANES)
        x = z_ref[pl.ds(r, SUBLANES), Z_XR:Z_XR + LRU_W]
        prev = prev_ref[pl.ds(r, SUBLANES), :]
        shifted = {k: jnp.where(row >= k, pltpu.roll(x, k, axis=0), pltpu.roll(prev, k, axis=0))
                   for k in range(1, CONV_W)}
        xc_ref[pl.ds(r, SUBLANES), :] = _conv_taps(x, shifted, cw_ref[...], cb_ref[...])
        return 0
    lax.fori_loop(0, n_batch, conv_step, 0)

    def gates_step(n, _):
        c0 = pl.multiple_of(n * LRU_BLOCK, LRU_BLOCK)
        lanes = pl.ds(c0, LRU_BLOCK)
        a, gated, mult = _lru_gates(xc_ref[:, lanes], n, c0, wg_ref, ba_ref, bx_ref, lam_ref)
        a_ref[:, lanes] = a
        b_ref[:, lanes] = gated * mult
        return 0
    lax.fori_loop(0, N_LRU_BLOCKS, gates_step, 0)

    def scan_step(b, _):
        rows = pl.ds(pl.multiple_of(b * SUBLANES, SUBLANES), SUBLANES)
        h_in = jnp.broadcast_to(h0_ref[pl.ds(b, 1), :], (SUBLANES, LRU_W))
        hh = _scan_tile(a_ref[rows, :], b_ref[rows, :], h_in, row)
        b_ref[rows, :] = hh * jax.nn.gelu(z_ref[rows, Z_GR:Z_GR + LRU_W])
        lru_ref[pl.ds(b, 1), :] = hh[SUBLANES - 1:SUBLANES, :]
        return 0
    lax.fori_loop(0, n_batch, scan_step, 0)
    o_ref[...] = _rmsnorm(b_ref[...], g_ref[...]).astype(o_ref.dtype)


def _rec_sample(z, prev8, h0, cw, cb, wg, ba, bx, lam, g, l, *, n_batch, t_len):
    assert t_len == SUBLANES and PAST_LEN > 0
    rows = n_batch * t_len
    layer = lambda a: pl.BlockSpec((None,) + a.shape[1:], lambda i: (l,) + (0,) * (a.ndim - 1))
    blocks = z.size * 4 + (2 * rows + 2 * n_batch) * LRU_W * 4 + wg[0].size * 2
    return pl.pallas_call(
        functools.partial(_rec_sample_body, n_batch=n_batch),
        grid=(1,),
        in_specs=[pl.BlockSpec(z.shape, lambda i: (0, 0))] + [layer(a) for a in (prev8, h0, cw, cb, wg, ba, bx, lam, g)],
        out_specs=[pl.BlockSpec((rows, LRU_W), lambda i: (0, 0)), pl.BlockSpec((n_batch, LRU_W), lambda i: (0, 0))],
        out_shape=[jax.ShapeDtypeStruct((rows, LRU_W), bf16),
                   jax.ShapeDtypeStruct((n_batch, LRU_W), f32)],
        scratch_shapes=[pltpu.VMEM((rows, LRU_W), f32)] * 3,
        compiler_params=_params(("arbitrary",), _vmem_limit(blocks, 3 * rows * LRU_W * 4)),
        name="rec_sample",
    )(z, prev8, h0, cw, cb, wg, ba, bx, lam, g)


def _mix_mlp_body(h_ref, attn_ref, rec_ref, wo_ref, g_ref, wu_ref, wd_ref, o_ref, u_ref):
    @pl.when(pl.program_id(1) == 0)
    def _():
        o_ref[...] = h_ref[...] + jnp.dot(attn_ref[...], wo_ref[0:ATTN_W, :], preferred_element_type=f32)
        o_ref[...] += jnp.dot(rec_ref[...], wo_ref[ATTN_W:ATTN_W + LRU_W, :], preferred_element_type=f32)
        u_ref[...] = _rmsnorm(o_ref[...], g_ref[...]).astype(bf16)
    hid = jnp.dot(u_ref[...], wu_ref[...], preferred_element_type=f32)
    hid = jnp.square(jnp.maximum(hid, 0.0)).astype(bf16)
    o_ref[...] += jnp.dot(hid, wd_ref[...], preferred_element_type=f32)


def _mix_mlp(h, attn, rec, wo, g, wu, wd, l, *, tm, tf):
    m = h.shape[0]
    once = pl.Buffered(1)
    blocks = 2 * tm * D_MODEL * 4 + tm * (ATTN_W + LRU_W) * 2 + 2 * D_MODEL * tf * 2
    scratch = wo[0].size * 2 + tm * D_MODEL * 2 + tm * tf * 6
    return pl.pallas_call(
        _mix_mlp_body,
        grid=(m // tm, D_FF // tf),
        in_specs=[pl.BlockSpec((tm, D_MODEL), lambda i, f: (i, 0)),
                  pl.BlockSpec((tm, ATTN_W), lambda i, f: (i, 0)),
                  pl.BlockSpec((tm, LRU_W), lambda i, f: (i, 0)),
                  pl.BlockSpec((None,) + wo.shape[1:], lambda i, f: (l, 0, 0), pipeline_mode=once),
                  pl.BlockSpec((None, 1, D_MODEL), lambda i, f: (l, 0, 0), pipeline_mode=once),
                  pl.BlockSpec((None, D_MODEL, tf), lambda i, f: (l, 0, f)),
                  pl.BlockSpec((None, tf, D_MODEL), lambda i, f: (l, f, 0))],
        out_specs=pl.BlockSpec((tm, D_MODEL), lambda i, f: (i, 0)),
        out_shape=jax.ShapeDtypeStruct((m, D_MODEL), f32),
        scratch_shapes=[pltpu.VMEM((tm, D_MODEL), bf16)],
        compiler_params=_params(("parallel", "arbitrary"), _vmem_limit(blocks, scratch)),
        name="mix_mlp",
    )(h, attn, rec, wo, g, wu, wd)


def _mix_mlp_cast_body(h_ref, attn_ref, rec_ref, wo_ref, g_ref, wu_ref, wd_ref,
                       o_ref, wob_ref, wub_ref, wdb_ref, u_ref):
    @pl.when(pl.program_id(1) == 0)
    def _():
        wob_ref[...] = wo_ref[...].astype(bf16)
        o_ref[...] = h_ref[...] + jnp.dot(attn_ref[...], wob_ref[0:ATTN_W, :], preferred_element_type=f32)
        o_ref[...] += jnp.dot(rec_ref[...], wob_ref[ATTN_W:ATTN_W + LRU_W, :], preferred_element_type=f32)
        u_ref[...] = _rmsnorm(o_ref[...], g_ref[...]).astype(bf16)
    wub_ref[...] = wu_ref[...].astype(bf16)
    wdb_ref[...] = wd_ref[...].astype(bf16)
    hid = jnp.dot(u_ref[...], wub_ref[...], preferred_element_type=f32)
    hid = jnp.square(jnp.maximum(hid, 0.0)).astype(bf16)
    o_ref[...] += jnp.dot(hid, wdb_ref[...], preferred_element_type=f32)


def _mix_mlp_cast(h, attn, rec, wo, g, wu, wd, l, *, tf):
    m = h.shape[0]
    once = pl.Buffered(1)
    blocks = 2 * D_MODEL * tf * (4 + 2)
    scratch = (2 * m * D_MODEL * 4 + m * (ATTN_W + LRU_W) * 2 + wo[0].size * (4 + 2) + m * D_MODEL * 2
               + m * tf * 6)
    return pl.pallas_call(
        _mix_mlp_cast_body,
        grid=(1, D_FF // tf),
        in_specs=[pl.BlockSpec((m, D_MODEL), lambda i, f: (0, 0), pipeline_mode=once),
                  pl.BlockSpec((m, ATTN_W), lambda i, f: (0, 0), pipeline_mode=once),
                  pl.BlockSpec((m, LRU_W), lambda i, f: (0, 0), pipeline_mode=once),
                  pl.BlockSpec((None,) + wo.shape[1:], lambda i, f: (l, 0, 0), pipeline_mode=once),
                  pl.BlockSpec((None, 1, D_MODEL), lambda i, f: (l, 0, 0), pipeline_mode=once),
                  pl.BlockSpec((None, D_MODEL, tf), lambda i, f: (l, 0, f)),
                  pl.BlockSpec((None, tf, D_MODEL), lambda i, f: (l, f, 0))],
        out_specs=[pl.BlockSpec((m, D_MODEL), lambda i, f: (0, 0)),
                   pl.BlockSpec(wo.shape[1:], lambda i, f: (0, 0)),
                   pl.BlockSpec((D_MODEL, tf), lambda i, f: (0, f)),
                   pl.BlockSpec((tf, D_MODEL), lambda i, f: (f, 0))],
        out_shape=[jax.ShapeDtypeStruct((m, D_MODEL), f32),
                   jax.ShapeDtypeStruct(wo.shape[1:], bf16),
                   jax.ShapeDtypeStruct(wu.shape[1:], bf16),
                   jax.ShapeDtypeStruct(wd.shape[1:], bf16)],
        scratch_shapes=[pltpu.VMEM((m, D_MODEL), bf16)],
        compiler_params=_params(("arbitrary", "arbitrary"), _vmem_limit(blocks, scratch)),
        name="mix_mlp_cast",
    )(h, attn, rec, wo, g, wu, wd)


def _final_norm_body(h_ref, g_ref, o_ref):
    o_ref[0] = _rmsnorm(h_ref[0], g_ref[...])


def _final_norm(h, g, *, skip, tr):
    n_batch, t_len, _ = h.shape
    s_len = t_len - skip
    tiles = tr // SUBLANES
    skip_tiles = skip // SUBLANES
    h4 = h.reshape(n_batch, t_len // SUBLANES, SUBLANES, D_MODEL)
    out = pl.pallas_call(
        _final_norm_body,
        grid=(n_batch, s_len // tr),
        in_specs=[pl.BlockSpec((pl.Element(1), pl.Element(tiles), pl.Element(SUBLANES), pl.Element(D_MODEL)),
                               lambda b, r: (b, skip_tiles + r * tiles, 0, 0)),
                  pl.BlockSpec((1, D_MODEL), lambda b, r: (0, 0))],
        out_specs=pl.BlockSpec((1, tiles, SUBLANES, D_MODEL), lambda b, r: (b, r, 0, 0)),
        out_shape=jax.ShapeDtypeStruct((n_batch, s_len // SUBLANES, SUBLANES, D_MODEL), f32),
        compiler_params=_params(("parallel", "parallel"), _vmem_limit(2 * tr * D_MODEL * 4)),
        name="final_norm",
    )(h4, g)
    return out.reshape(n_batch, s_len, D_MODEL)


def _row_tile(m, candidates):
    for tm in candidates:
        if m % tm == 0:
            return tm
    raise ValueError(f"no row tile for {m} rows")


def kernel(x_prompt, x_sample, cache_k_win, cache_v_win, state_conv, state_lru, meta_tokens, norm_mix_g, w_in,
           conv_w, conv_b, w_gate_a, b_gate_a, w_gate_x, b_gate_x, lru_lambda, attn_sinks, rel_bias, attn_out_g,
           rec_out_g, w_out, norm_mlp_g, w_up, w_down, final_norm_g):
    n_p, s_p, _ = x_prompt.shape
    n_s, t_s, _ = x_sample.shape
    t_p = N_META + s_p
    buf = cache_k_win.shape[2]
    assert t_p % BF16_ROWS == 0 and buf == WINDOW

    w_gates = jnp.concatenate([w_gate_a, w_gate_x], axis=-1).astype(bf16)
    rows3 = lambda p: p[:, None, :]
    g_mix, g_mlp, g_attn, g_rec = rows3(norm_mix_g), rows3(norm_mlp_g), rows3(attn_out_g), rows3(rec_out_g)
    cb3, ba3, bx3, lam3 = rows3(conv_b), rows3(b_gate_a), rows3(b_gate_x), rows3(lru_lambda)
    g_attn_t = jnp.broadcast_to(
        attn_out_g.reshape(DEPTH, N_Q_HEADS, HEAD_DIM).transpose(0, 2, 1)[..., None],
        (DEPTH, HEAD_DIM, N_Q_HEADS, WINDOW)).reshape(DEPTH, HEAD_DIM, ATTN_W)

    meta = jnp.broadcast_to(meta_tokens.astype(x_prompt.dtype)[None], (n_p, N_META, D_MODEL))
    hp = jnp.concatenate([meta, x_prompt], axis=1).reshape(n_p * t_p, D_MODEL)
    hs = x_sample.reshape(n_s * t_s, D_MODEL)

    bias_qk, bias_kq = _bias_tables(rel_bias)
    bias_qk = bias_qk.reshape(N_Q_HEADS * WINDOW, 2 * WINDOW)
    prev_p = jnp.zeros((n_p, SUBLANES, LRU_W), f32)
    h0_p = jnp.zeros((n_p, 1, LRU_W), f32)
    ck = cache_k_win.reshape(DEPTH, n_s, buf * N_KV_HEADS, HEAD_DIM)
    cv = cache_v_win.reshape(DEPTH, n_s, buf * N_KV_HEADS, HEAD_DIM)
    prev_s = jnp.pad(state_conv, ((0, 0), (0, 0), (SUBLANES - (CONV_W - 1), 0), (0, 0)))
    prev_s = prev_s.reshape(DEPTH, n_s * SUBLANES, LRU_W)

    tm_p = _row_tile(n_p * t_p, (688, 344))
    tc_p = _row_tile(t_p, (688, 344, 48, 16))
    tn = IN_W // 2

    kp_l, vp_l, cp_l, lp_l, ks_l, vs_l, cs_l, ls_l = ([] for _ in range(8))
    for l in range(DEPTH):
        z, w_in_b = _inproj_cast(hs, g_mix, w_in, l, tn=tn)
        attn, nk, nv = _attn_sample(z, ck, cv, bias_qk, attn_sinks, g_attn, l, n_batch=n_s, t_len=t_s)
        rec, lru = _rec_sample(z, prev_s, state_lru, conv_w, cb3, w_gates, ba3, bx3, lam3, g_rec, l,
                               n_batch=n_s, t_len=t_s)
        hs, w_out_b, w_up_b, w_down_b = _mix_mlp_cast(hs, attn, rec, w_out, g_mlp, w_up, w_down, l, tf=512)
        z3 = z.reshape(n_s, t_s, IN_W)
        ks_l.append(nk)
        vs_l.append(nv)
        cs_l.append(z3[:, t_s - (CONV_W - 1):, Z_XR:Z_XR + LRU_W])
        ls_l.append(lru)
        q, k, v, rec, lru, x_tail = _rec_prompt(hp, g_mix, w_in_b, prev_p, h0_p, conv_w, cb3, w_gates, ba3, bx3,
                                                lam3, g_rec, l, n_batch=n_p, t_len=t_p, tc=tc_p)
        attn = _attn_prompt(q, k, v, bias_kq, attn_sinks, g_attn_t, l, n_batch=n_p, t_len=t_p)
        hp = _mix_mlp(hp, attn, rec, w_out_b[None], g_mlp[l:l + 1], w_up_b[None], w_down_b[None], 0,
                      tm=tm_p, tf=1024)
        last_window = lambda a: a.reshape(n_p, t_p, N_KV_HEADS, HEAD_DIM)[:, t_p - WINDOW:]
        kp_l.append(last_window(k))
        vp_l.append(last_window(v))
        cp_l.append(x_tail[:, SUBLANES - (CONV_W - 1):])
        lp_l.append(lru[:, 0])

    g_fin = final_norm_g[None, :]
    y_prompt = _final_norm(hp.reshape(n_p, t_p, D_MODEL), g_fin, skip=N_META, tr=512)
    y_sample = _final_norm(hs.reshape(1, n_s * t_s, D_MODEL), g_fin, skip=0, tr=n_s * t_s)
    y_sample = y_sample.reshape(n_s, t_s, D_MODEL)
    cache_shape = (DEPTH, n_s, buf, N_KV_HEADS, HEAD_DIM)
    return (y_prompt, y_sample,
            jnp.stack(kp_l), jnp.stack(vp_l), jnp.stack(cp_l), jnp.stack(lp_l),
            jnp.stack(ks_l).reshape(cache_shape), jnp.stack(vs_l).reshape(cache_shape),
            jnp.stack(cs_l), jnp.stack(ls_l))
```

```python
import functools
import math

import jax
import jax.numpy as jnp
from jax import lax
from jax.experimental import pallas as pl
from jax.experimental.pallas import tpu as pltpu

f32 = jnp.float32
bf16 = jnp.bfloat16

D_MODEL = 2048
DEPTH = 4
PAST_LEN = 16384
HEAD_DIM = 128
N_Q_HEADS = 8
N_KV_HEADS = 2
Q_PER_KV = N_Q_HEADS // N_KV_HEADS
ATTN_W = N_Q_HEADS * HEAD_DIM
KV_W = N_KV_HEADS * HEAD_DIM
LRU_W = D_MODEL // 2
N_LRU_BLOCKS = 8
LRU_BLOCK = LRU_W // N_LRU_BLOCKS
CONV_W = 4
LRU_C = 8.0
IN_W = ATTN_W + 2 * KV_W + 2 * LRU_W
D_FF = 4 * D_MODEL
WINDOW = 128
N_BUCKETS = 32
MAX_DISTANCE = 128
N_META = 16
EPS = 1e-6
ATTN_SCALE = HEAD_DIM ** -0.5
INV_ATTN_SCALE = HEAD_DIM ** 0.5
EXP2_PER_T = math.log2(math.e) / INV_ATTN_SCALE

Z_Q = 0
Z_K = ATTN_W
Z_V = Z_K + KV_W
Z_XR = Z_V + KV_W
Z_GR = Z_XR + LRU_W
LRU_HALF = LRU_W // 2

ATTN_LEAD = WINDOW + (-N_META) % WINDOW

SUBLANES = 8
BF16_ROWS = 16
V7X_VMEM_BYTES = 64 * 1024 * 1024
VMEM_CAP_BYTES = V7X_VMEM_BYTES - 2 * 1024 * 1024


def _vmem_limit(pipelined_bytes, scratch_bytes=0):
    est = 2 * pipelined_bytes + scratch_bytes
    return int(min(VMEM_CAP_BYTES, est + est // 2 + (8 << 20)))


def _params(semantics, vmem_bytes):
    return pltpu.CompilerParams(dimension_semantics=semantics, vmem_limit_bytes=vmem_bytes)


def _rms_scale(x):
    return lax.rsqrt(jnp.mean(x * x, axis=-1, keepdims=True) + EPS)


def _rmsnorm(x, g):
    return x * _rms_scale(x) * g


def _inproj_cast_body(h_ref, g_ref, w_ref, z_ref, wb_ref, u_ref):
    @pl.when(pl.program_id(0) == 0)
    def _():
        u_ref[...] = _rmsnorm(h_ref[...], g_ref[...]).astype(bf16)
    wb_ref[...] = w_ref[...].astype(bf16)
    z_ref[...] = jnp.dot(u_ref[...], wb_ref[...], preferred_element_type=f32)


def _inproj_cast(h, g, w, l, *, tn):
    m = h.shape[0]
    blocks = D_MODEL * tn * (4 + 2) + m * tn * 4
    return pl.pallas_call(
        _inproj_cast_body,
        grid=(IN_W // tn,),
        in_specs=[pl.BlockSpec((m, D_MODEL), lambda j: (0, 0)),
                  pl.BlockSpec((None, 1, D_MODEL), lambda j: (l, 0, 0)),
                  pl.BlockSpec((None, D_MODEL, tn), lambda j: (l, 0, j))],
        out_specs=[pl.BlockSpec((m, tn), lambda j: (0, j)),
                   pl.BlockSpec((D_MODEL, tn), lambda j: (0, j))],
        out_shape=[jax.ShapeDtypeStruct((m, IN_W), f32), jax.ShapeDtypeStruct(w.shape[1:], bf16)],
        scratch_shapes=[pltpu.VMEM((m, D_MODEL), bf16)],
        compiler_params=_params(("arbitrary",), _vmem_limit(blocks, m * D_MODEL * (2 * 4 + 2))),
        name="inproj_cast",
    )(h, g, w)


def _rel_bias_of(dist, rel_ref, h):
    n = jnp.maximum(dist, 0)
    max_exact = N_BUCKETS // 2
    nf = jnp.maximum(n, 1).astype(f32)
    large = max_exact + (jnp.log(nf / max_exact) / math.log(MAX_DISTANCE / max_exact)
                         * (N_BUCKETS - max_exact)).astype(jnp.int32)
    large = jnp.minimum(large, N_BUCKETS - 1)
    bucket = jnp.where(n < max_exact, n, large)
    acc = jnp.zeros(dist.shape, f32)
    for b in range(N_BUCKETS):
        acc = jnp.where(bucket == b, rel_ref[b, h], acc)
    return jnp.where((dist >= 0) & (dist < WINDOW), acc, -jnp.inf)


def _bias_table_body(rel_ref, qk_ref, kq_ref):
    shape_qk = (WINDOW, 2 * WINDOW)
    dist_qk = (lax.broadcasted_iota(jnp.int32, shape_qk, 0) + WINDOW
               - lax.broadcasted_iota(jnp.int32, shape_qk, 1))
    shape_kq = (2 * WINDOW, WINDOW)
    dist_kq = (lax.broadcasted_iota(jnp.int32, shape_kq, 1) + WINDOW
               - lax.broadcasted_iota(jnp.int32, shape_kq, 0))
    for h in range(N_Q_HEADS):
        kv, g = divmod(h, Q_PER_KV)
        qk_ref[h] = _rel_bias_of(dist_qk, rel_ref, h)
        kq_ref[kv, :, g * WINDOW:(g + 1) * WINDOW] = _rel_bias_of(dist_kq, rel_ref, h) * INV_ATTN_SCALE


def _bias_tables(rel_bias):
    return pl.pallas_call(
        _bias_table_body,
        in_specs=[pl.BlockSpec(memory_space=pltpu.SMEM)],
        out_shape=[jax.ShapeDtypeStruct((N_Q_HEADS, WINDOW, 2 * WINDOW), f32),
                   jax.ShapeDtypeStruct((N_KV_HEADS, 2 * WINDOW, Q_PER_KV * WINDOW), f32)],
        name="bias_tables",
    )(rel_bias)


def _attn_blocks_t(blocks, kpad_ref, vt_ref, bias_ref, sinks_ref, l, gt_ref):
    kv_heads = [range(kv * Q_PER_KV, (kv + 1) * Q_PER_KV) for kv in range(N_KV_HEADS)]
    sink = [jnp.concatenate([jnp.full((1, WINDOW), sinks_ref[l, h] * INV_ATTN_SCALE, f32) for h in heads], axis=1)
            for heads in kv_heads]
    chains = [(i, kv) for i in range(len(blocks)) for kv in range(N_KV_HEADS)]
    scores = {}
    for i, kv in chains:
        q_rows, r0, _ = blocks[i]
        kwin = kpad_ref[pl.ds(r0, 2 * WINDOW), kv * HEAD_DIM:(kv + 1) * HEAD_DIM]
        q = jnp.concatenate([q_rows[:, h * HEAD_DIM:(h + 1) * HEAD_DIM] for h in kv_heads[kv]],
                            axis=0).astype(bf16)
        scores[i, kv] = lax.dot_general(kwin, q, (((1,), (1,)), ((), ())), preferred_element_type=f32)
    probs, invs = {}, {}
    for i, kv in chains:
        t = scores[i, kv] + bias_ref[kv]
        lead_keys = blocks[i][2]
        if lead_keys:
            key = lax.broadcasted_iota(jnp.int32, t.shape, 0)
            t = jnp.where(key < lead_keys, -jnp.inf, t)
        m = jnp.maximum(jnp.max(t, axis=0, keepdims=True), sink[kv])
        p = jnp.exp2((t - m) * EXP2_PER_T)
        invs[i, kv] = 1.0 / (jnp.sum(p, axis=0, keepdims=True) + jnp.exp2((sink[kv] - m) * EXP2_PER_T))
        probs[i, kv] = p.astype(bf16)
    raws = {}
    for i, kv in chains:
        vt = vt_ref[kv * HEAD_DIM:(kv + 1) * HEAD_DIM, pl.ds(blocks[i][1], 2 * WINDOW)]
        raws[i, kv] = jnp.dot(vt, probs[i, kv], preferred_element_type=f32)
    outs = []
    for i in range(len(blocks)):
        raw = jnp.concatenate([raws[i, kv] for kv in range(N_KV_HEADS)], axis=1)
        inv = jnp.concatenate([invs[i, kv] for kv in range(N_KV_HEADS)], axis=1)
        sq = jnp.sum(raw * raw, axis=0, keepdims=True) * (inv * inv)
        ssq = sq[:, 0:WINDOW]
        for h in range(1, N_Q_HEADS):
            ssq = ssq + sq[:, h * WINDOW:(h + 1) * WINDOW]
        r = lax.rsqrt(ssq * (1.0 / ATTN_W) + EPS)
        yt = raw * (inv * jnp.concatenate([r] * N_Q_HEADS, axis=1)) * gt_ref[...]
        outs.append(jnp.concatenate([yt[:, h * WINDOW:(h + 1) * WINDOW].T for h in range(N_Q_HEADS)], axis=1))
    return outs


ATTN_GROUP = 3


def _attn_prompt_body(q_ref, k_ref, v_ref, bias_ref, sinks_ref, gt_ref, o_ref, kpad_ref, vpad_ref, vt_ref,
                      *, t_len, l):
    pad_rows = kpad_ref.shape[0]
    kpad_ref[0:ATTN_LEAD, :] = jnp.zeros((ATTN_LEAD, KV_W), bf16)
    kpad_ref[ATTN_LEAD:pad_rows, :] = k_ref[...].astype(bf16)
    vpad_ref[0:ATTN_LEAD, :] = jnp.zeros((ATTN_LEAD, KV_W), f32)
    vpad_ref[ATTN_LEAD:pad_rows, :] = v_ref[...]

    def transpose_step(c, _):
        r = pl.multiple_of(c * WINDOW, WINDOW)
        chunk = vpad_ref[pl.ds(r, WINDOW), :]
        for kv in range(N_KV_HEADS):
            vt_ref[kv * HEAD_DIM:(kv + 1) * HEAD_DIM, pl.ds(r, WINDOW)] = (
                chunk[:, kv * HEAD_DIM:(kv + 1) * HEAD_DIM].T.astype(bf16))
        return 0
    lax.fori_loop(0, pad_rows // WINDOW, transpose_step, 0, unroll=3)

    def q_rows_of(j):
        start = j * WINDOW - (ATTN_LEAD - WINDOW)
        return pl.ds(start if isinstance(j, int) else pl.multiple_of(start, BF16_ROWS), WINDOW)

    n_first = 2 * WINDOW - ATTN_LEAD
    q0 = jnp.concatenate([jnp.zeros((WINDOW - n_first, ATTN_W), f32), q_ref[0:n_first, :]], axis=0)
    y0, y1 = _attn_blocks_t([(q0, 0, ATTN_LEAD), (q_ref[q_rows_of(1), :], WINDOW, ATTN_LEAD - WINDOW)],
                            kpad_ref, vt_ref, bias_ref, sinks_ref, l, gt_ref)
    o_ref[0:n_first, :] = y0[WINDOW - n_first:, :].astype(o_ref.dtype)
    o_ref[q_rows_of(1), :] = y1.astype(o_ref.dtype)

    def step(i, _):
        js = [2 + i * ATTN_GROUP + g for g in range(ATTN_GROUP)]
        ys = _attn_blocks_t([(q_ref[q_rows_of(j), :], pl.multiple_of(j * WINDOW, WINDOW), 0) for j in js],
                            kpad_ref, vt_ref, bias_ref, sinks_ref, l, gt_ref)
        for j, y in zip(js, ys):
            o_ref[q_rows_of(j), :] = y.astype(o_ref.dtype)
        return 0
    n_blocks = pad_rows // WINDOW - 1
    assert (n_blocks - 2) % ATTN_GROUP == 0
    lax.fori_loop(0, (n_blocks - 2) // ATTN_GROUP, step, 0)


def _attn_prompt(z, bias_kq, sinks, gt, l, *, n_batch, t_len):
    pad_rows = ATTN_LEAD + t_len
    assert pad_rows % WINDOW == 0 and (ATTN_LEAD - WINDOW) % BF16_ROWS == 0
    blocks = t_len * (ATTN_W + 2 * KV_W) * 4 + t_len * ATTN_W * 2 + bias_kq.size * 4 + HEAD_DIM * ATTN_W * 4
    scratch = pad_rows * KV_W * (2 + 4 + 2)
    return pl.pallas_call(
        functools.partial(_attn_prompt_body, t_len=t_len, l=l),
        grid=(n_batch,),
        in_specs=[pl.BlockSpec((t_len, ATTN_W), lambda b: (b, Z_Q // ATTN_W)),
                  pl.BlockSpec((t_len, KV_W), lambda b: (b, Z_K // KV_W)),
                  pl.BlockSpec((t_len, KV_W), lambda b: (b, Z_V // KV_W)),
                  pl.BlockSpec(bias_kq.shape, lambda b: (0, 0, 0)),
                  pl.BlockSpec(memory_space=pltpu.SMEM),
                  pl.BlockSpec((None, HEAD_DIM, ATTN_W), lambda b: (l, 0, 0))],
        out_specs=pl.BlockSpec((t_len, ATTN_W), lambda b: (b, 0)),
        out_shape=jax.ShapeDtypeStruct((n_batch * t_len, ATTN_W), bf16),
        scratch_shapes=[pltpu.VMEM((pad_rows, KV_W), bf16), pltpu.VMEM((pad_rows, KV_W), f32),
                        pltpu.VMEM((KV_W, pad_rows), bf16)],
        compiler_params=_params(("parallel",), _vmem_limit(blocks, scratch)),
        name="attn_prompt",
    )(z, z, z, bias_kq, sinks, gt)


DECODE_GROUP = 8


def _attn_sample_body(z_ref, ck_ref, cv_ref, bias_ref, sinks_ref, g_ref, o_ref, nk_ref, nv_ref, acc_ref,
                      *, n_batch, t_len, buf, l):
    zeros = jnp.zeros((2 * WINDOW - buf - t_len, HEAD_DIM), f32)
    kept = N_KV_HEADS * (buf - t_len)
    kv_heads = [range(kv * Q_PER_KV, (kv + 1) * Q_PER_KV) for kv in range(N_KV_HEADS)]
    bias = [jnp.concatenate([bias_ref[h * WINDOW:h * WINDOW + t_len, :] for h in heads], axis=0)
            for heads in kv_heads]
    sink = [jnp.concatenate([jnp.full((t_len, 1), sinks_ref[l, h], f32) for h in heads], axis=0)
            for heads in kv_heads]

    def window(c_ref, n_ref, b, r0, col, kv):
        new = z_ref[pl.ds(r0, t_len), col + kv * HEAD_DIM:col + (kv + 1) * HEAD_DIM]
        n_ref[b, pl.ds(kept + kv, t_len, stride=N_KV_HEADS), :] = new
        old = c_ref[b, pl.ds(kv, buf, stride=N_KV_HEADS), :]
        return jnp.concatenate([old, new, zeros], axis=0).astype(bf16)

    def group_step(i, _):
        seqs = [i * DECODE_GROUP + j for j in range(DECODE_GROUP)]
        rows = [pl.multiple_of(b * t_len, t_len) for b in seqs]
        scores = {}
        for j, (b, r0) in enumerate(zip(seqs, rows)):
            nk_ref[b, 0:kept, :] = ck_ref[b, N_KV_HEADS * t_len:N_KV_HEADS * buf, :]
            nv_ref[b, 0:kept, :] = cv_ref[b, N_KV_HEADS * t_len:N_KV_HEADS * buf, :]
            for kv, heads in enumerate(kv_heads):
                q = jnp.concatenate([z_ref[pl.ds(r0, t_len), Z_Q + h * HEAD_DIM:Z_Q + (h + 1) * HEAD_DIM]
                                     for h in heads], axis=0).astype(bf16)
                k = window(ck_ref, nk_ref, b, r0, Z_K, kv)
                scores[j, kv] = lax.dot_general(q, k, (((1,), (1,)), ((), ())), preferred_element_type=f32)
        probs, denoms = {}, {}
        for key, s in scores.items():
            kv = key[1]
            s = s * ATTN_SCALE + bias[kv]
            m = jnp.maximum(jnp.max(s, axis=-1, keepdims=True), sink[kv])
            p = jnp.exp(s - m)
            denoms[key] = jnp.sum(p, axis=-1, keepdims=True) + jnp.exp(sink[kv] - m)
            probs[key] = p.astype(bf16)
        for j, (b, r0) in enumerate(zip(seqs, rows)):
            outs = []
            for kv in range(N_KV_HEADS):
                v = window(cv_ref, nv_ref, b, r0, Z_V, kv)
                o = jnp.dot(probs[j, kv], v, preferred_element_type=f32) / denoms[j, kv]
                outs.extend(o[g * t_len:(g + 1) * t_len] for g in range(Q_PER_KV))
            acc_ref[pl.ds(r0, t_len), :] = jnp.concatenate(outs, axis=1)
        return 0
    lax.fori_loop(0, n_batch // DECODE_GROUP, group_step, 0)
    o_ref[...] = _rmsnorm(acc_ref[...], g_ref[...]).astype(o_ref.dtype)


def _attn_sample(z, ck, cv, bias_qk, sinks, g, l, *, n_batch, t_len):
    buf = ck.shape[2] // N_KV_HEADS
    assert buf == WINDOW and t_len == SUBLANES and n_batch % DECODE_GROUP == 0
    rows = n_batch * t_len
    whole =lambda a: pl.BlockSpec(a.shape, lambda i: (0,) * a.ndim)
    cache = pl.BlockSpec((None,) + ck.shape[1:], lambda i: (l, 0, 0, 0))
    blocks = (z.size + 4 * ck[0].size + bias_qk.size) * 4 + rows * ATTN_W * 2
    return pl.pallas_call(
        functools.partial(_attn_sample_body, n_batch=n_batch, t_len=t_len, buf=buf, l=l),
        grid=(1,),
        in_specs=[whole(z), cache, cache, whole(bias_qk), pl.BlockSpec(memory_space=pltpu.SMEM),
                  pl.BlockSpec((None, 1, ATTN_W), lambda i: (l, 0, 0))],
        out_specs=[pl.BlockSpec((rows, ATTN_W), lambda i: (0, 0)),
                   pl.BlockSpec(ck.shape[1:], lambda i: (0, 0, 0)),
                   pl.BlockSpec(cv.shape[1:], lambda i: (0, 0, 0))],
        out_shape=[jax.ShapeDtypeStruct((rows, ATTN_W), bf16),
                   jax.ShapeDtypeStruct(ck.shape[1:], f32),
                   jax.ShapeDtypeStruct(cv.shape[1:], f32)],
        scratch_shapes=[pltpu.VMEM((rows, ATTN_W), f32)],
        compiler_params=_params(("arbitrary",), _vmem_limit(blocks, rows * ATTN_W * 4)),
        name="attn_sample",
    )(z, ck, cv, bias_qk, sinks, g)


def _lru_gates(xc, n, c0, wg_ref, ba_ref, bx_ref, lam_ref):
    lanes = pl.ds(c0, LRU_BLOCK)
    gates = jnp.dot(xc.astype(bf16), wg_ref[n], preferred_element_type=f32)
    gate_a = jax.nn.sigmoid(gates[:, :LRU_BLOCK] + ba_ref[:, lanes])
    gate_x = jax.nn.sigmoid(gates[:, LRU_BLOCK:] + bx_ref[:, lanes])
    log_a = -LRU_C * gate_a * jax.nn.softplus(-lam_ref[:, lanes])
    a = jnp.exp(log_a)
    y = -jnp.tanh(log_a) * (1.0 + a * a)
    mult = jnp.where(y > 0.0, y * lax.rsqrt(y), 0.0)
    return a, xc * gate_x, mult


def _tile_prefix(a, b, row):
    for d in (1, 2, 4):
        a_prev = pltpu.roll(a, d, axis=0)
        b_prev = pltpu.roll(b, d, axis=0)
        keep = row >= d
        b = jnp.where(keep, a * b_prev + b, b)
        a = jnp.where(keep, a * a_prev, a)
    return a, b


def _scan_tile(a, b, h_in, row):
    a, b = _tile_prefix(a, b, row)
    return a * h_in + b


def _last_row(h):
    return jnp.broadcast_to(h[SUBLANES - 1:SUBLANES, :], h.shape)


def _conv_taps(x, shifted, cw, cb):
    out = cb + shifted[CONV_W - 1] * cw[0:1]
    for j in range(1, CONV_W - 1):
        out = out + shifted[CONV_W - 1 - j] * cw[j:j + 1]
    return out + x * cw[CONV_W - 1:CONV_W]


REC_PROJECTIONS_AHEAD = 5


def _projection_issue_order(n_pairs, n_z):
    return [(0, p) for p in range(n_pairs)] + [(1, p) for p in range(n_pairs)] + [(2, p) for p in range(n_z)]


def _projection_consume_order(n_pairs, n_z):
    order = [step for p in range(n_pairs) for step in ((0, p), (1, p))]
    return order + [(2, p) for p in range(n_z)]


def _rec_prompt_body(h_ref, gmix_ref, wq_ref, wxl_ref, wxh_ref, wgl_ref, wgh_ref, prev_ref, h0_ref, cw_ref, cb_ref,
                     wg_ref, ba_ref, bx_ref, lam_ref, g_ref, z_ref, o_ref, lru_ref, tail_ref, u_ref, xprev_ref,
                     a_ref, b_ref, gate_ref, carry_ref, *, tc, n_chunks):
    c = pl.program_id(1)

    @pl.when(c == 0)
    def _():
        xprev_ref[...] = prev_ref[...]
        for n in range(N_LRU_BLOCKS):
            carry_ref[n] = jnp.broadcast_to(h0_ref[:, n * LRU_BLOCK:(n + 1) * LRU_BLOCK], (SUBLANES, LRU_BLOCK))

    u_ref[...] = _rmsnorm(h_ref[...], gmix_ref[...]).astype(bf16)
    u = u_ref[...]
    pair = 2 * LRU_BLOCK
    n_pairs = LRU_W // pair
    pairs_per_half = LRU_HALF // pair
    branch_refs = ((wxl_ref, wxh_ref), (wgl_ref, wgh_ref))

    def project(branch, p):
        if branch == 2:
            w = wq_ref[:, p * pair:(p + 1) * pair]
        else:
            q = p % pairs_per_half
            w = branch_refs[branch][p // pairs_per_half][:, q * pair:(q + 1) * pair]
        return jnp.dot(u, w, preferred_element_type=f32)

    def store_z(p, z):
        z_ref[:, p * pair:(p + 1) * pair] = z

    def gates(p, x):
        lanes = slice(p * pair, (p + 1) * pair)
        big = jnp.concatenate([xprev_ref[:, lanes], x], axis=0)
        shifted = {k: pltpu.roll(big, k, axis=0)[SUBLANES:] for k in range(1, CONV_W)}
        xc = _conv_taps(x, shifted, cw_ref[:, lanes], cb_ref[:, lanes])
        for q in range(2):
            n = 2 * p + q
            sub = slice(q * LRU_BLOCK, (q + 1) * LRU_BLOCK)
            a, gated, mult = _lru_gates(xc[:, sub], n, n * LRU_BLOCK, wg_ref, ba_ref, bx_ref, lam_ref)
            b = gated * mult
            a_ref[n] = a
            b_ref[n] = b
            b_ref[n, 0:1, :] = jnp.where(c == 0, gated[0:1, :], b[0:1, :])
        return x[tc - SUBLANES:, :]

    def gelu_gate(p, gate):
        for q in range(2):
            gate_ref[2 * p + q] = jax.nn.gelu(gate[:, q * LRU_BLOCK:(q + 1) * LRU_BLOCK])

    n_z = Z_XR // pair
    issue = _projection_issue_order(n_pairs, n_z)
    consumers = [((gates, gelu_gate, store_z)[br], br, p) for br, p in _projection_consume_order(n_pairs, n_z)]
    projected, tails = {}, []
    for consume, br, p in consumers:
        while issue and (len(projected) < REC_PROJECTIONS_AHEAD or (br, p) not in projected):
            key = issue.pop(0)
            projected[key] = project(*key)
        out = consume(p, projected.pop((br, p)))
        if consume is gates:
            tails.append(out)
    tail = jnp.concatenate(tails, axis=1)
    xprev_ref[...] = tail
    tail_ref[...] = tail

    seg = tc // SUBLANES
    seg_rows = lambda r: pl.ds(r, SUBLANES, stride=seg)

    def local_scan(r, maps):
        out = []
        for n in range(N_LRU_BLOCKS):
            a = a_ref[n, seg_rows(r), :]
            a_cum = a * maps[2 * n]
            b_cum = a * maps[2 * n + 1] + b_ref[n, seg_rows(r), :]
            a_ref[n, seg_rows(r), :] = a_cum
            b_ref[n, seg_rows(r), :] = b_cum
            out += [a_cum, b_cum]
        return tuple(out)
    identity = (jnp.ones((SUBLANES, LRU_BLOCK), f32), jnp.zeros((SUBLANES, LRU_BLOCK), f32)) * N_LRU_BLOCKS
    seg_maps = lax.fori_loop(0, seg, local_scan, identity, unroll=2)

    row = lax.broadcasted_iota(jnp.int32, (SUBLANES, LRU_BLOCK), 0)
    h_in = []
    for n in range(N_LRU_BLOCKS):
        h_prev = carry_ref[n]
        a_cum, b_cum = _tile_prefix(seg_maps[2 * n], seg_maps[2 * n + 1], row)
        h_end = a_cum * h_prev + b_cum
        h_in.append(jnp.where(row == 0, h_prev, pltpu.roll(h_end, 1, axis=0)))
        carry_ref[n] = _last_row(h_end)

    def apply_scan(r, _):
        for n in range(N_LRU_BLOCKS):
            h = a_ref[n, seg_rows(r), :] * h_in[n] + b_ref[n, seg_rows(r), :]
            b_ref[n, seg_rows(r), :] = h * gate_ref[n, seg_rows(r), :]
        return 0
    lax.fori_loop(0, seg, apply_scan, 0, unroll=2)

    y = jnp.concatenate([b_ref[n] for n in range(N_LRU_BLOCKS)], axis=1)
    o_ref[...] = _rmsnorm(y, g_ref[...]).astype(o_ref.dtype)

    @pl.when(c == n_chunks - 1)
    def _():
        lru_ref[...] = jnp.concatenate([carry_ref[n] for n in range(N_LRU_BLOCKS)], axis=1)


def _rec_prompt(h, g_mix, w_in, prev8, h0, cw, cb, wg, ba, bx, lam, g, l, *, n_batch, t_len, tc):
    n_chunks = t_len // tc
    once = pl.Buffered(1)
    row_vec = pl.BlockSpec((None, 1, LRU_W), lambda b, c: (l, 0, 0), pipeline_mode=once)
    w_half = lambda col: pl.BlockSpec((D_MODEL, LRU_HALF), lambda b, c: (0, col // LRU_HALF), pipeline_mode=once)
    state = pl.BlockSpec((None, SUBLANES, LRU_W), lambda b, c: (b, 0, 0))
    rows = lambda width: pl.BlockSpec((tc, width), lambda b, c: (b * n_chunks + c, 0))
    blocks = tc * D_MODEL * 4 + tc * Z_XR * 4 + tc * LRU_W * 2 + 3 * SUBLANES * LRU_W * 4
    scratch = (D_MODEL * IN_W * 2 + wg[0].size * 2 + tc * D_MODEL * 2 + (3 * tc + 2 * SUBLANES) * LRU_W * 4
               + 8 * tc * 2 * LRU_BLOCK * 4)
    by_block = pltpu.VMEM((N_LRU_BLOCKS, tc, LRU_BLOCK), f32)
    assert tc % SUBLANES == 0
    return pl.pallas_call(
        functools.partial(_rec_prompt_body, tc=tc, n_chunks=n_chunks),
        grid=(n_batch, n_chunks),
        in_specs=[rows(D_MODEL),
                  pl.BlockSpec((None, 1, D_MODEL), lambda b, c: (l, 0, 0), pipeline_mode=once),
                  pl.BlockSpec((D_MODEL, Z_XR), lambda b, c: (0, 0), pipeline_mode=once),
                  w_half(Z_XR), w_half(Z_XR + LRU_HALF), w_half(Z_GR), w_half(Z_GR + LRU_HALF),
                  state,
                  pl.BlockSpec((None, 1, LRU_W), lambda b, c: (b, 0, 0)),
                  pl.BlockSpec((None, CONV_W, LRU_W), lambda b, c: (l, 0, 0), pipeline_mode=once),
                  row_vec,
                  pl.BlockSpec((None,) + wg.shape[1:], lambda b, c: (l, 0, 0, 0), pipeline_mode=once),
                  row_vec, row_vec, row_vec, row_vec],
        out_specs=[rows(Z_XR), rows(LRU_W), state, state],
        out_shape=[jax.ShapeDtypeStruct((n_batch * t_len, Z_XR), f32),
                   jax.ShapeDtypeStruct((n_batch * t_len, LRU_W), bf16),
                   jax.ShapeDtypeStruct((n_batch, SUBLANES, LRU_W), f32),
                   jax.ShapeDtypeStruct((n_batch, SUBLANES, LRU_W), f32)],
        scratch_shapes=[pltpu.VMEM((tc, D_MODEL), bf16),
                        pltpu.VMEM((SUBLANES, LRU_W), f32), by_block, by_block, by_block,
                        pltpu.VMEM((N_LRU_BLOCKS, SUBLANES, LRU_BLOCK), f32)],
        compiler_params=_params(("parallel", "arbitrary"), _vmem_limit(blocks, scratch)),
        name="rec_prompt",
    )(h, g_mix, w_in, w_in, w_in, w_in, w_in, prev8, h0, cw, cb, wg, ba, bx, lam, g)


def _rec_sample_body(z_ref, prev_ref, h0_ref, cw_ref, cb_ref, wg_ref, ba_ref, bx_ref, lam_ref, g_ref,
                     o_ref, lru_ref, xc_ref, a_ref, b_ref, *, n_batch):
    row = lax.broadcasted_iota(jnp.int32, (SUBLANES, LRU_W), 0)

    def conv_step(b, _):
        r = pl.multiple_of(b * SUBLANES, SUBLANES)
        x = z_ref[pl.ds(r, SUBLANES), Z_XR:Z_XR + LRU_W]
        prev = prev_ref[pl.ds(r, SUBLANES), :]
        shifted = {k: jnp.where(row >= k, pltpu.roll(x, k, axis=0), pltpu.roll(prev, k, axis=0))
                   for k in range(1, CONV_W)}
        xc_ref[pl.ds(r, SUBLANES), :] = _conv_taps(x, shifted, cw_ref[...], cb_ref[...])
        return 0
    lax.fori_loop(0, n_batch, conv_step, 0)

    def gates_step(n, _):
        c0 = pl.multiple_of(n * LRU_BLOCK, LRU_BLOCK)
        lanes = pl.ds(c0, LRU_BLOCK)
        a, gated, mult = _lru_gates(xc_ref[:, lanes], n, c0, wg_ref, ba_ref, bx_ref, lam_ref)
        a_ref[:, lanes] = a
        b_ref[:, lanes] = gated * mult
        return 0
    lax.fori_loop(0, N_LRU_BLOCKS, gates_step, 0)

    def scan_step(b, _):
        rows = pl.ds(pl.multiple_of(b * SUBLANES, SUBLANES), SUBLANES)
        h_in = jnp.broadcast_to(h0_ref[pl.ds(b, 1), :], (SUBLANES, LRU_W))
        hh = _scan_tile(a_ref[rows, :], b_ref[rows, :], h_in, row)
        b_ref[rows, :] = hh * jax.nn.gelu(z_ref[rows, Z_GR:Z_GR + LRU_W])
        lru_ref[pl.ds(b, 1), :] = hh[SUBLANES - 1:SUBLANES, :]
        return 0
    lax.fori_loop(0, n_batch, scan_step, 0)
    o_ref[...] = _rmsnorm(b_ref[...], g_ref[...]).astype(o_ref.dtype)


def _rec_sample(z, prev8, h0, cw, cb, wg, ba, bx, lam, g, l, *, n_batch, t_len):
    assert t_len == SUBLANES and PAST_LEN > 0
    rows = n_batch * t_len
    layer = lambda a: pl.BlockSpec((None,) + a.shape[1:], lambda i: (l,) + (0,) * (a.ndim - 1))
    blocks = z.size * 4 + (2 * rows + 2 * n_batch) * LRU_W * 4 + wg[0].size * 2
    return pl.pallas_call(
        functools.partial(_rec_sample_body, n_batch=n_batch),
        grid=(1,),
        in_specs=[pl.BlockSpec(z.shape, lambda i: (0, 0))] + [layer(a) for a in (prev8, h0, cw, cb, wg, ba, bx, lam, g)],
        out_specs=[pl.BlockSpec((rows, LRU_W), lambda i: (0, 0)), pl.BlockSpec((n_batch, LRU_W), lambda i: (0, 0))],
        out_shape=[jax.ShapeDtypeStruct((rows, LRU_W), bf16),
                   jax.ShapeDtypeStruct((n_batch, LRU_W), f32)],
        scratch_shapes=[pltpu.VMEM((rows, LRU_W), f32)] * 3,
        compiler_params=_params(("arbitrary",), _vmem_limit(blocks, 3 * rows * LRU_W * 4)),
        name="rec_sample",
    )(z, prev8, h0, cw, cb, wg, ba, bx, lam, g)


def _mix_mlp_body(h_ref, attn_ref, rec_ref, wo_ref, g_ref, wu_ref, wd_ref, o_ref, u_ref):
    @pl.when(pl.program_id(1) == 0)
    def _():
        o_ref[...] = h_ref[...] + jnp.dot(attn_ref[...], wo_ref[0:ATTN_W, :], preferred_element_type=f32)
        o_ref[...] += jnp.dot(rec_ref[...], wo_ref[ATTN_W:ATTN_W + LRU_W, :], preferred_element_type=f32)
        u_ref[...] = _rmsnorm(o_ref[...], g_ref[...]).astype(bf16)
    hid = jnp.dot(u_ref[...], wu_ref[...], preferred_element_type=f32)
    hid = jnp.square(jnp.maximum(hid, 0.0)).astype(bf16)
    o_ref[...] += jnp.dot(hid, wd_ref[...], preferred_element_type=f32)


def _mix_mlp(h, attn, rec, wo, g, wu, wd, l, *, tm, tf):
    m = h.shape[0]
    once = pl.Buffered(1)
    blocks = 2 * tm * D_MODEL * 4 + tm * (ATTN_W + LRU_W) * 2 + 2 * D_MODEL * tf * 2
    scratch = wo[0].size * 2 + tm * D_MODEL * 2 + tm * tf * 6
    return pl.pallas_call(
        _mix_mlp_body,
        grid=(m // tm, D_FF // tf),
        in_specs=[pl.BlockSpec((tm, D_MODEL), lambda i, f: (i, 0)),
                  pl.BlockSpec((tm, ATTN_W), lambda i, f: (i, 0)),
                  pl.BlockSpec((tm, LRU_W), lambda i, f: (i, 0)),
                  pl.BlockSpec((None,) + wo.shape[1:], lambda i, f: (l, 0, 0), pipeline_mode=once),
                  pl.BlockSpec((None, 1, D_MODEL), lambda i, f: (l, 0, 0), pipeline_mode=once),
                  pl.BlockSpec((None, D_MODEL, tf), lambda i, f: (l, 0, f)),
                  pl.BlockSpec((None, tf, D_MODEL), lambda i, f: (l, f, 0))],
        out_specs=pl.BlockSpec((tm, D_MODEL), lambda i, f: (i, 0)),
        out_shape=jax.ShapeDtypeStruct((m, D_MODEL), f32),
        scratch_shapes=[pltpu.VMEM((tm, D_MODEL), bf16)],
        compiler_params=_params(("parallel", "arbitrary"), _vmem_limit(blocks, scratch)),
        name="mix_mlp",
    )(h, attn, rec, wo, g, wu, wd)


def _mix_mlp_cast_body(h_ref, attn_ref, rec_ref, wo_ref, g_ref, wu_ref, wd_ref,
                       o_ref, wob_ref, wub_ref, wdb_ref, u_ref):
    @pl.when(pl.program_id(1) == 0)
    def _():
        wob_ref[...] = wo_ref[...].astype(bf16)
        o_ref[...] = h_ref[...] + jnp.dot(attn_ref[...], wob_ref[0:ATTN_W, :], preferred_element_type=f32)
        o_ref[...] += jnp.dot(rec_ref[...], wob_ref[ATTN_W:ATTN_W + LRU_W, :], preferred_element_type=f32)
        u_ref[...] = _rmsnorm(o_ref[...], g_ref[...]).astype(bf16)
    wub_ref[...] = wu_ref[...].astype(bf16)
    wdb_ref[...] = wd_ref[...].astype(bf16)
    hid = jnp.dot(u_ref[...], wub_ref[...], preferred_element_type=f32)
    hid = jnp.square(jnp.maximum(hid, 0.0)).astype(bf16)
    o_ref[...] += jnp.dot(hid, wdb_ref[...], preferred_element_type=f32)


def _mix_mlp_cast(h, attn, rec, wo, g, wu, wd, l, *, tf):
    m = h.shape[0]
    once = pl.Buffered(1)
    blocks = 2 * D_MODEL * tf * (4 + 2)
    scratch = (2 * m * D_MODEL * 4 + m * (ATTN_W + LRU_W) * 2 + wo[0].size * (4 + 2) + m * D_MODEL * 2
               + m * tf * 6)
    return pl.pallas_call(
        _mix_mlp_cast_body,
        grid=(1, D_FF // tf),
        in_specs=[pl.BlockSpec((m, D_MODEL), lambda i, f: (0, 0), pipeline_mode=once),
                  pl.BlockSpec((m, ATTN_W), lambda i, f: (0, 0), pipeline_mode=once),
                  pl.BlockSpec((m, LRU_W), lambda i, f: (0, 0), pipeline_mode=once),
                  pl.BlockSpec((None,) + wo.shape[1:], lambda i, f: (l, 0, 0), pipeline_mode=once),
                  pl.BlockSpec((None, 1, D_MODEL), lambda i, f: (l, 0, 0), pipeline_mode=once),
                  pl.BlockSpec((None, D_MODEL, tf), lambda i, f: (l, 0, f)),
                  pl.BlockSpec((None, tf, D_MODEL), lambda i, f: (l, f, 0))],
        out_specs=[pl.BlockSpec((m, D_MODEL), lambda i, f: (0, 0)),
                   pl.BlockSpec(wo.shape[1:], lambda i, f: (0, 0)),
                   pl.BlockSpec((D_MODEL, tf), lambda i, f: (0, f)),
                   pl.BlockSpec((tf, D_MODEL), lambda i, f: (f, 0))],
        out_shape=[jax.ShapeDtypeStruct((m, D_MODEL), f32),
                   jax.ShapeDtypeStruct(wo.shape[1:], bf16),
                   jax.ShapeDtypeStruct(wu.shape[1:], bf16),
                   jax.ShapeDtypeStruct(wd.shape[1:], bf16)],
        scratch_shapes=[pltpu.VMEM((m, D_MODEL), bf16)],
        compiler_params=_params(("arbitrary", "arbitrary"), _vmem_limit(blocks, scratch)),
        name="mix_mlp_cast",
    )(h, attn, rec, wo, g, wu, wd)


def _final_norm_body(h_ref, g_ref, o_ref):
    o_ref[0] = _rmsnorm(h_ref[0], g_ref[...])


def _final_norm(h, g, *, skip, tr):
    n_batch, t_len, _ = h.shape
    s_len = t_len - skip
    tiles = tr // SUBLANES
    skip_tiles = skip // SUBLANES
    h4 = h.reshape(n_batch, t_len // SUBLANES, SUBLANES, D_MODEL)
    out = pl.pallas_call(
        _final_norm_body,
        grid=(n_batch, s_len // tr),
        in_specs=[pl.BlockSpec((pl.Element(1), pl.Element(tiles), pl.Element(SUBLANES), pl.Element(D_MODEL)),
                               lambda b, r: (b, skip_tiles + r * tiles, 0, 0)),
                  pl.BlockSpec((1, D_MODEL), lambda b, r: (0, 0))],
        out_specs=pl.BlockSpec((1, tiles, SUBLANES, D_MODEL), lambda b, r: (b, r, 0, 0)),
        out_shape=jax.ShapeDtypeStruct((n_batch, s_len // SUBLANES, SUBLANES, D_MODEL), f32),
        compiler_params=_params(("parallel", "parallel"), _vmem_limit(2 * tr * D_MODEL * 4)),
        name="final_norm",
    )(h4, g)
    return out.reshape(n_batch, s_len, D_MODEL)


def _row_tile(m, candidates):
    for tm in candidates:
        if m % tm == 0:
            return tm
    raise ValueError(f"no row tile for {m} rows")


def kernel(x_prompt, x_sample, cache_k_win, cache_v_win, state_conv, state_lru, meta_tokens, norm_mix_g, w_in,
           conv_w, conv_b, w_gate_a, b_gate_a, w_gate_x, b_gate_x, lru_lambda, attn_sinks, rel_bias, attn_out_g,
           rec_out_g, w_out, norm_mlp_g, w_up, w_down, final_norm_g):
    n_p, s_p, _ = x_prompt.shape
    n_s, t_s, _ = x_sample.shape
    t_p = N_META + s_p
    buf = cache_k_win.shape[2]
    assert t_p % BF16_ROWS == 0 and buf == WINDOW

    w_gates = jnp.concatenate([w_gate_a, w_gate_x], axis=-1).astype(bf16)
    rows3 = lambda p: p[:, None, :]
    g_mix, g_mlp, g_attn, g_rec = rows3(norm_mix_g), rows3(norm_mlp_g), rows3(attn_out_g), rows3(rec_out_g)
    cb3, ba3, bx3, lam3 = rows3(conv_b), rows3(b_gate_a), rows3(b_gate_x), rows3(lru_lambda)
    g_attn_t = jnp.broadcast_to(
        attn_out_g.reshape(DEPTH, N_Q_HEADS, HEAD_DIM).transpose(0, 2, 1)[..., None],
        (DEPTH, HEAD_DIM, N_Q_HEADS, WINDOW)).reshape(DEPTH, HEAD_DIM, ATTN_W)

    meta = jnp.broadcast_to(meta_tokens.astype(x_prompt.dtype)[None], (n_p, N_META, D_MODEL))
    hp = jnp.concatenate([meta, x_prompt], axis=1).reshape(n_p * t_p, D_MODEL)
    hs = x_sample.reshape(n_s * t_s, D_MODEL)

    bias_qk, bias_kq = _bias_tables(rel_bias)
    bias_qk = bias_qk.reshape(N_Q_HEADS * WINDOW, 2 * WINDOW)
    prev_p = jnp.zeros((n_p, SUBLANES, LRU_W), f32)
    h0_p = jnp.zeros((n_p, 1, LRU_W), f32)
    ck = cache_k_win.reshape(DEPTH, n_s, buf * N_KV_HEADS, HEAD_DIM)
    cv = cache_v_win.reshape(DEPTH, n_s, buf * N_KV_HEADS, HEAD_DIM)
    prev_s = jnp.pad(state_conv, ((0, 0), (0, 0), (SUBLANES - (CONV_W - 1), 0), (0, 0)))
    prev_s = prev_s.reshape(DEPTH, n_s * SUBLANES, LRU_W)

    tm_p = _row_tile(n_p * t_p, (688, 344))
    tc_p = _row_tile(t_p, (688, 344, 48, 16))
    tn = IN_W // 2

    kp_l, vp_l, cp_l, lp_l, ks_l, vs_l, cs_l, ls_l = ([] for _ in range(8))
    for l in range(DEPTH):
        z, w_in_b = _inproj_cast(hs, g_mix, w_in, l, tn=tn)
        attn, nk, nv = _attn_sample(z, ck, cv, bias_qk, attn_sinks, g_attn, l, n_batch=n_s, t_len=t_s)
        rec, lru = _rec_sample(z, prev_s, state_lru, conv_w, cb3, w_gates, ba3, bx3, lam3, g_rec, l,
                               n_batch=n_s, t_len=t_s)
        hs, w_out_b, w_up_b, w_down_b = _mix_mlp_cast(hs, attn, rec, w_out, g_mlp, w_up, w_down, l, tf=512)
        z3 = z.reshape(n_s, t_s, IN_W)
        ks_l.append(nk)
        vs_l.append(nv)
        cs_l.append(z3[:, t_s - (CONV_W - 1):, Z_XR:Z_XR + LRU_W])
        ls_l.append(lru)
        z, rec, lru, x_tail = _rec_prompt(hp, g_mix, w_in_b, prev_p, h0_p, conv_w, cb3, w_gates, ba3, bx3, lam3,
                                          g_rec, l, n_batch=n_p, t_len=t_p, tc=tc_p)
        attn = _attn_prompt(z, bias_kq, attn_sinks, g_attn_t, l, n_batch=n_p, t_len=t_p)
        hp = _mix_mlp(hp, attn, rec, w_out_b[None], g_mlp[l:l + 1], w_up_b[None], w_down_b[None], 0,
                      tm=tm_p, tf=1024)
        z3 = z.reshape(n_p, t_p, Z_XR)
        kp_l.append(z3[:, t_p - WINDOW:, Z_K:Z_K + KV_W].reshape(n_p, WINDOW, N_KV_HEADS, HEAD_DIM))
        vp_l.append(z3[:, t_p - WINDOW:, Z_V:Z_V + KV_W].reshape(n_p, WINDOW, N_KV_HEADS, HEAD_DIM))
        cp_l.append(x_tail[:, SUBLANES - (CONV_W - 1):])
        lp_l.append(lru[:, 0])

    g_fin = final_norm_g[None, :]
    y_prompt = _final_norm(hp.reshape(n_p, t_p, D_MODEL), g_fin, skip=N_META, tr=512)
    y_sample = _final_norm(hs.reshape(1, n_s * t_s, D_MODEL), g_fin, skip=0, tr=n_s * t_s)
    y_sample = y_sample.reshape(n_s, t_s, D_MODEL)
    cache_shape = (DEPTH, n_s, buf, N_KV_HEADS, HEAD_DIM)
    return (y_prompt, y_sample,
            jnp.stack(kp_l), jnp.stack(vp_l), jnp.stack(cp_l), jnp.stack(lp_l),
            jnp.stack(ks_l).reshape(cache_shape), jnp.stack(vs_l).reshape(cache_shape),
            jnp.stack(cs_l), jnp.stack(ls_l))
```

```python
import functools
import math

import jax
import jax.numpy as jnp
from jax import lax
from jax.experimental import pallas as pl
from jax.experimental.pallas import tpu as pltpu

f32 = jnp.float32
bf16 = jnp.bfloat16

D_MODEL = 2048
DEPTH = 4
PAST_LEN = 16384
HEAD_DIM = 128
N_Q_HEADS = 8
N_KV_HEADS = 2
Q_PER_KV = N_Q_HEADS // N_KV_HEADS
ATTN_W = N_Q_HEADS * HEAD_DIM
KV_W = N_KV_HEADS * HEAD_DIM
LRU_W = D_MODEL // 2
N_LRU_BLOCKS = 8
LRU_BLOCK = LRU_W // N_LRU_BLOCKS
CONV_W = 4
LRU_C = 8.0
IN_W = ATTN_W + 2 * KV_W + 2 * LRU_W
D_FF = 4 * D_MODEL
WINDOW = 128
N_BUCKETS = 32
MAX_DISTANCE = 128
N_META = 16
EPS = 1e-6
ATTN_SCALE = HEAD_DIM ** -0.5
INV_ATTN_SCALE = HEAD_DIM ** 0.5
EXP2_PER_T = math.log2(math.e) / INV_ATTN_SCALE

Z_Q = 0
Z_K = ATTN_W
Z_V = Z_K + KV_W
Z_XR = Z_V + KV_W
Z_GR = Z_XR + LRU_W
LRU_HALF = LRU_W // 2

ATTN_LEAD = WINDOW + (-N_META) % WINDOW

SUBLANES = 8
BF16_ROWS = 16
V7X_VMEM_BYTES = 64 * 1024 * 1024
VMEM_CAP_BYTES = V7X_VMEM_BYTES - 2 * 1024 * 1024


def _vmem_limit(pipelined_bytes, scratch_bytes=0):
    est = 2 * pipelined_bytes + scratch_bytes
    return int(min(VMEM_CAP_BYTES, est + est // 2 + (8 << 20)))


def _params(semantics, vmem_bytes):
    return pltpu.CompilerParams(dimension_semantics=semantics, vmem_limit_bytes=vmem_bytes)


def _rms_scale(x):
    return lax.rsqrt(jnp.mean(x * x, axis=-1, keepdims=True) + EPS)


def _rmsnorm(x, g):
    return x * _rms_scale(x) * g


def _inproj_cast_body(h_ref, g_ref, w_ref, z_ref, wb_ref, u_ref):
    @pl.when(pl.program_id(0) == 0)
    def _():
        u_ref[...] = _rmsnorm(h_ref[...], g_ref[...]).astype(bf16)
    wb_ref[...] = w_ref[...].astype(bf16)
    z_ref[...] = jnp.dot(u_ref[...], wb_ref[...], preferred_element_type=f32)


def _inproj_cast(h, g, w, l, *, tn):
    m = h.shape[0]
    blocks = D_MODEL * tn * (4 + 2) + m * tn * 4
    return pl.pallas_call(
        _inproj_cast_body,
        grid=(IN_W // tn,),
        in_specs=[pl.BlockSpec((m, D_MODEL), lambda j: (0, 0)),
                  pl.BlockSpec((None, 1, D_MODEL), lambda j: (l, 0, 0)),
                  pl.BlockSpec((None, D_MODEL, tn), lambda j: (l, 0, j))],
        out_specs=[pl.BlockSpec((m, tn), lambda j: (0, j)),
                   pl.BlockSpec((D_MODEL, tn), lambda j: (0, j))],
        out_shape=[jax.ShapeDtypeStruct((m, IN_W), f32), jax.ShapeDtypeStruct(w.shape[1:], bf16)],
        scratch_shapes=[pltpu.VMEM((m, D_MODEL), bf16)],
        compiler_params=_params(("arbitrary",), _vmem_limit(blocks, m * D_MODEL * (2 * 4 + 2))),
        name="inproj_cast",
    )(h, g, w)


def _rel_bias_of(dist, rel_ref, h):
    n = jnp.maximum(dist, 0)
    max_exact = N_BUCKETS // 2
    nf = jnp.maximum(n, 1).astype(f32)
    large = max_exact + (jnp.log(nf / max_exact) / math.log(MAX_DISTANCE / max_exact)
                         * (N_BUCKETS - max_exact)).astype(jnp.int32)
    large = jnp.minimum(large, N_BUCKETS - 1)
    bucket = jnp.where(n < max_exact, n, large)
    acc = jnp.zeros(dist.shape, f32)
    for b in range(N_BUCKETS):
        acc = jnp.where(bucket == b, rel_ref[b, h], acc)
    return jnp.where((dist >= 0) & (dist < WINDOW), acc, -jnp.inf)


def _bias_table_body(rel_ref, qk_ref, kq_ref):
    shape_qk = (WINDOW, 2 * WINDOW)
    dist_qk = (lax.broadcasted_iota(jnp.int32, shape_qk, 0) + WINDOW
               - lax.broadcasted_iota(jnp.int32, shape_qk, 1))
    shape_kq = (2 * WINDOW, WINDOW)
    dist_kq = (lax.broadcasted_iota(jnp.int32, shape_kq, 1) + WINDOW
               - lax.broadcasted_iota(jnp.int32, shape_kq, 0))
    for h in range(N_Q_HEADS):
        kv, g = divmod(h, Q_PER_KV)
        qk_ref[h] = _rel_bias_of(dist_qk, rel_ref, h)
        kq_ref[kv, :, g * WINDOW:(g + 1) * WINDOW] = _rel_bias_of(dist_kq, rel_ref, h) * INV_ATTN_SCALE


def _bias_tables(rel_bias):
    return pl.pallas_call(
        _bias_table_body,
        in_specs=[pl.BlockSpec(memory_space=pltpu.SMEM)],
        out_shape=[jax.ShapeDtypeStruct((N_Q_HEADS, WINDOW, 2 * WINDOW), f32),
                   jax.ShapeDtypeStruct((N_KV_HEADS, 2 * WINDOW, Q_PER_KV * WINDOW), f32)],
        name="bias_tables",
    )(rel_bias)


def _attn_blocks_t(blocks, kpad_ref, vt_ref, bias_ref, sinks_ref, l, gt_ref):
    kv_heads = [range(kv * Q_PER_KV, (kv + 1) * Q_PER_KV) for kv in range(N_KV_HEADS)]
    sink = [jnp.concatenate([jnp.full((1, WINDOW), sinks_ref[l, h] * INV_ATTN_SCALE, f32) for h in heads], axis=1)
            for heads in kv_heads]
    chains = [(i, kv) for i in range(len(blocks)) for kv in range(N_KV_HEADS)]
    scores = {}
    for i, kv in chains:
        q_rows, r0, _ = blocks[i]
        kwin = kpad_ref[pl.ds(r0, 2 * WINDOW), kv * HEAD_DIM:(kv + 1) * HEAD_DIM]
        q = jnp.concatenate([q_rows[:, h * HEAD_DIM:(h + 1) * HEAD_DIM] for h in kv_heads[kv]],
                            axis=0).astype(bf16)
        scores[i, kv] = lax.dot_general(kwin, q, (((1,), (1,)), ((), ())), preferred_element_type=f32)
    probs, invs = {}, {}
    for i, kv in chains:
        t = scores[i, kv] + bias_ref[kv]
        lead_keys = blocks[i][2]
        if lead_keys:
            key = lax.broadcasted_iota(jnp.int32, t.shape, 0)
            t = jnp.where(key < lead_keys, -jnp.inf, t)
        m = jnp.maximum(jnp.max(t, axis=0, keepdims=True), sink[kv])
        p = jnp.exp2((t - m) * EXP2_PER_T)
        invs[i, kv] = 1.0 / (jnp.sum(p, axis=0, keepdims=True) + jnp.exp2((sink[kv] - m) * EXP2_PER_T))
        probs[i, kv] = p.astype(bf16)
    raws = {}
    for i, kv in chains:
        vt = vt_ref[kv * HEAD_DIM:(kv + 1) * HEAD_DIM, pl.ds(blocks[i][1], 2 * WINDOW)]
        raws[i, kv] = jnp.dot(vt, probs[i, kv], preferred_element_type=f32)
    outs = []
    for i in range(len(blocks)):
        raw = jnp.concatenate([raws[i, kv] for kv in range(N_KV_HEADS)], axis=1)
        inv = jnp.concatenate([invs[i, kv] for kv in range(N_KV_HEADS)], axis=1)
        sq = jnp.sum(raw * raw, axis=0, keepdims=True) * (inv * inv)
        ssq = sq[:, 0:WINDOW]
        for h in range(1, N_Q_HEADS):
            ssq = ssq + sq[:, h * WINDOW:(h + 1) * WINDOW]
        r = lax.rsqrt(ssq * (1.0 / ATTN_W) + EPS)
        yt = raw * (inv * jnp.concatenate([r] * N_Q_HEADS, axis=1)) * gt_ref[...]
        outs.append(jnp.concatenate([yt[:, h * WINDOW:(h + 1) * WINDOW].T for h in range(N_Q_HEADS)], axis=1))
    return outs


ATTN_GROUP = 5


def _attn_prompt_body(q_ref, k_ref, v_ref, bias_ref, sinks_ref, gt_ref, o_ref, kpad_ref, vpad_ref, vt_ref,
                      *, t_len, l):
    pad_rows = kpad_ref.shape[0]
    kpad_ref[0:ATTN_LEAD, :] = jnp.zeros((ATTN_LEAD, KV_W), bf16)
    kpad_ref[ATTN_LEAD:pad_rows, :] = k_ref[...].astype(bf16)
    vpad_ref[0:ATTN_LEAD, :] = jnp.zeros((ATTN_LEAD, KV_W), f32)
    vpad_ref[ATTN_LEAD:pad_rows, :] = v_ref[...]

    def transpose_step(c, _):
        r = pl.multiple_of(c * WINDOW, WINDOW)
        chunk = vpad_ref[pl.ds(r, WINDOW), :]
        for kv in range(N_KV_HEADS):
            vt_ref[kv * HEAD_DIM:(kv + 1) * HEAD_DIM, pl.ds(r, WINDOW)] = (
                chunk[:, kv * HEAD_DIM:(kv + 1) * HEAD_DIM].T.astype(bf16))
        return 0
    lax.fori_loop(0, pad_rows // WINDOW, transpose_step, 0, unroll=3)

    def q_rows_of(j):
        start = j * WINDOW - (ATTN_LEAD - WINDOW)
        return pl.ds(start if isinstance(j, int) else pl.multiple_of(start, BF16_ROWS), WINDOW)

    n_first = 2 * WINDOW - ATTN_LEAD
    q0 = jnp.concatenate([jnp.zeros((WINDOW - n_first, ATTN_W), f32), q_ref[0:n_first, :]], axis=0)
    y0, y1 = _attn_blocks_t([(q0, 0, ATTN_LEAD), (q_ref[q_rows_of(1), :], WINDOW, ATTN_LEAD - WINDOW)],
                            kpad_ref, vt_ref, bias_ref, sinks_ref, l, gt_ref)
    o_ref[0:n_first, :] = y0[WINDOW - n_first:, :].astype(o_ref.dtype)
    o_ref[q_rows_of(1), :] = y1.astype(o_ref.dtype)

    def step(i, _):
        js = [2 + i * ATTN_GROUP + g for g in range(ATTN_GROUP)]
        ys = _attn_blocks_t([(q_ref[q_rows_of(j), :], pl.multiple_of(j * WINDOW, WINDOW), 0) for j in js],
                            kpad_ref, vt_ref, bias_ref, sinks_ref, l, gt_ref)
        for j, y in zip(js, ys):
            o_ref[q_rows_of(j), :] = y.astype(o_ref.dtype)
        return 0
    n_blocks = pad_rows // WINDOW - 1
    assert (n_blocks - 2) % ATTN_GROUP == 0
    lax.fori_loop(0, (n_blocks - 2) // ATTN_GROUP, step, 0)


def _attn_prompt(z, bias_kq, sinks, gt, l, *, n_batch, t_len):
    pad_rows = ATTN_LEAD + t_len
    assert pad_rows % WINDOW == 0 and (ATTN_LEAD - WINDOW) % BF16_ROWS == 0
    blocks = t_len * (ATTN_W + 2 * KV_W) * 4 + t_len * ATTN_W * 2 + bias_kq.size * 4 + HEAD_DIM * ATTN_W * 4
    scratch = pad_rows * KV_W * (2 + 4 + 2)
    return pl.pallas_call(
        functools.partial(_attn_prompt_body, t_len=t_len, l=l),
        grid=(n_batch,),
        in_specs=[pl.BlockSpec((t_len, ATTN_W), lambda b: (b, Z_Q // ATTN_W)),
                  pl.BlockSpec((t_len, KV_W), lambda b: (b, Z_K // KV_W)),
                  pl.BlockSpec((t_len, KV_W), lambda b: (b, Z_V // KV_W)),
                  pl.BlockSpec(bias_kq.shape, lambda b: (0, 0, 0)),
                  pl.BlockSpec(memory_space=pltpu.SMEM),
                  pl.BlockSpec((None, HEAD_DIM, ATTN_W), lambda b: (l, 0, 0))],
        out_specs=pl.BlockSpec((t_len, ATTN_W), lambda b: (b, 0)),
        out_shape=jax.ShapeDtypeStruct((n_batch * t_len, ATTN_W), bf16),
        scratch_shapes=[pltpu.VMEM((pad_rows, KV_W), bf16), pltpu.VMEM((pad_rows, KV_W), f32),
                        pltpu.VMEM((KV_W, pad_rows), bf16)],
        compiler_params=_params(("parallel",), _vmem_limit(blocks, scratch)),
        name="attn_prompt",
    )(z, z, z, bias_kq, sinks, gt)


DECODE_GROUP = 8


def _attn_sample_body(z_ref, ck_ref, cv_ref, bias_ref, sinks_ref, g_ref, o_ref, nk_ref, nv_ref, acc_ref,
                      *, n_batch, t_len, buf, l):
    zeros = jnp.zeros((2 * WINDOW - buf - t_len, HEAD_DIM), f32)
    kept = N_KV_HEADS * (buf - t_len)
    kv_heads = [range(kv * Q_PER_KV, (kv + 1) * Q_PER_KV) for kv in range(N_KV_HEADS)]
    bias = [jnp.concatenate([bias_ref[h * WINDOW:h * WINDOW + t_len, :] for h in heads], axis=0)
            for heads in kv_heads]
    sink = [jnp.concatenate([jnp.full((t_len, 1), sinks_ref[l, h], f32) for h in heads], axis=0)
            for heads in kv_heads]

    def window(c_ref, n_ref, b, r0, col, kv):
        new = z_ref[pl.ds(r0, t_len), col + kv * HEAD_DIM:col + (kv + 1) * HEAD_DIM]
        n_ref[b, pl.ds(kept + kv, t_len, stride=N_KV_HEADS), :] = new
        old = c_ref[b, pl.ds(kv, buf, stride=N_KV_HEADS), :]
        return jnp.concatenate([old, new, zeros], axis=0).astype(bf16)

    def group_step(i, _):
        seqs = [i * DECODE_GROUP + j for j in range(DECODE_GROUP)]
        rows = [pl.multiple_of(b * t_len, t_len) for b in seqs]
        scores = {}
        for j, (b, r0) in enumerate(zip(seqs, rows)):
            nk_ref[b, 0:kept, :] = ck_ref[b, N_KV_HEADS * t_len:N_KV_HEADS * buf, :]
            nv_ref[b, 0:kept, :] = cv_ref[b, N_KV_HEADS * t_len:N_KV_HEADS * buf, :]
            for kv, heads in enumerate(kv_heads):
                q = jnp.concatenate([z_ref[pl.ds(r0, t_len), Z_Q + h * HEAD_DIM:Z_Q + (h + 1) * HEAD_DIM]
                                     for h in heads], axis=0).astype(bf16)
                k = window(ck_ref, nk_ref, b, r0, Z_K, kv)
                scores[j, kv] = lax.dot_general(q, k, (((1,), (1,)), ((), ())), preferred_element_type=f32)
        probs, denoms = {}, {}
        for key, s in scores.items():
            kv = key[1]
            s = s * ATTN_SCALE + bias[kv]
            m = jnp.maximum(jnp.max(s, axis=-1, keepdims=True), sink[kv])
            p = jnp.exp(s - m)
            denoms[key] = jnp.sum(p, axis=-1, keepdims=True) + jnp.exp(sink[kv] - m)
            probs[key] = p.astype(bf16)
        for j, (b, r0) in enumerate(zip(seqs, rows)):
            outs = []
            for kv in range(N_KV_HEADS):
                v = window(cv_ref, nv_ref, b, r0, Z_V, kv)
                o = jnp.dot(probs[j, kv], v, preferred_element_type=f32) / denoms[j, kv]
                outs.extend(o[g * t_len:(g + 1) * t_len] for g in range(Q_PER_KV))
            acc_ref[pl.ds(r0, t_len), :] = jnp.concatenate(outs, axis=1)
        return 0
    lax.fori_loop(0, n_batch // DECODE_GROUP, group_step, 0)
    o_ref[...] = _rmsnorm(acc_ref[...], g_ref[...]).astype(o_ref.dtype)


def _attn_sample(z, ck, cv, bias_qk, sinks, g, l, *, n_batch, t_len):
    buf = ck.shape[2] // N_KV_HEADS
    assert buf == WINDOW and t_len == SUBLANES and n_batch % DECODE_GROUP == 0
    rows = n_batch * t_len
    whole =lambda a: pl.BlockSpec(a.shape, lambda i: (0,) * a.ndim)
    cache = pl.BlockSpec((None,) + ck.shape[1:], lambda i: (l, 0, 0, 0))
    blocks = (z.size + 4 * ck[0].size + bias_qk.size) * 4 + rows * ATTN_W * 2
    return pl.pallas_call(
        functools.partial(_attn_sample_body, n_batch=n_batch, t_len=t_len, buf=buf, l=l),
        grid=(1,),
        in_specs=[whole(z), cache, cache, whole(bias_qk), pl.BlockSpec(memory_space=pltpu.SMEM),
                  pl.BlockSpec((None, 1, ATTN_W), lambda i: (l, 0, 0))],
        out_specs=[pl.BlockSpec((rows, ATTN_W), lambda i: (0, 0)),
                   pl.BlockSpec(ck.shape[1:], lambda i: (0, 0, 0)),
                   pl.BlockSpec(cv.shape[1:], lambda i: (0, 0, 0))],
        out_shape=[jax.ShapeDtypeStruct((rows, ATTN_W), bf16),
                   jax.ShapeDtypeStruct(ck.shape[1:], f32),
                   jax.ShapeDtypeStruct(cv.shape[1:], f32)],
        scratch_shapes=[pltpu.VMEM((rows, ATTN_W), f32)],
        compiler_params=_params(("arbitrary",), _vmem_limit(blocks, rows * ATTN_W * 4)),
        name="attn_sample",
    )(z, ck, cv, bias_qk, sinks, g)


def _lru_gates(xc, n, c0, wg_ref, ba_ref, bx_ref, lam_ref):
    lanes = pl.ds(c0, LRU_BLOCK)
    gates = jnp.dot(xc.astype(bf16), wg_ref[n], preferred_element_type=f32)
    gate_a = jax.nn.sigmoid(gates[:, :LRU_BLOCK] + ba_ref[:, lanes])
    gate_x = jax.nn.sigmoid(gates[:, LRU_BLOCK:] + bx_ref[:, lanes])
    log_a = -LRU_C * gate_a * jax.nn.softplus(-lam_ref[:, lanes])
    a = jnp.exp(log_a)
    y = -jnp.tanh(log_a) * (1.0 + a * a)
    mult = jnp.where(y > 0.0, y * lax.rsqrt(y), 0.0)
    return a, xc * gate_x, mult


def _tile_prefix(a, b, row):
    for d in (1, 2, 4):
        a_prev = pltpu.roll(a, d, axis=0)
        b_prev = pltpu.roll(b, d, axis=0)
        keep = row >= d
        b = jnp.where(keep, a * b_prev + b, b)
        a = jnp.where(keep, a * a_prev, a)
    return a, b


def _scan_tile(a, b, h_in, row):
    a, b = _tile_prefix(a, b, row)
    return a * h_in + b


def _last_row(h):
    return jnp.broadcast_to(h[SUBLANES - 1:SUBLANES, :], h.shape)


def _conv_taps(x, shifted, cw, cb):
    out = cb + shifted[CONV_W - 1] * cw[0:1]
    for j in range(1, CONV_W - 1):
        out = out + shifted[CONV_W - 1 - j] * cw[j:j + 1]
    return out + x * cw[CONV_W - 1:CONV_W]


REC_PROJECTIONS_AHEAD = 5


def _projection_issue_order(n_pairs, n_z):
    return [(0, p) for p in range(n_pairs)] + [(1, p) for p in range(n_pairs)] + [(2, p) for p in range(n_z)]


def _projection_consume_order(n_pairs, n_z):
    order = [step for p in range(n_pairs) for step in ((0, p), (1, p))]
    return order + [(2, p) for p in range(n_z)]


def _rec_prompt_body(h_ref, gmix_ref, wq_ref, wxl_ref, wxh_ref, wgl_ref, wgh_ref, prev_ref, h0_ref, cw_ref, cb_ref,
                     wg_ref, ba_ref, bx_ref, lam_ref, g_ref, z_ref, o_ref, lru_ref, tail_ref, u_ref, xprev_ref,
                     a_ref, b_ref, gate_ref, carry_ref, *, tc, n_chunks):
    c = pl.program_id(1)

    @pl.when(c == 0)
    def _():
        xprev_ref[...] = prev_ref[...]
        for n in range(N_LRU_BLOCKS):
            carry_ref[n] = jnp.broadcast_to(h0_ref[:, n * LRU_BLOCK:(n + 1) * LRU_BLOCK], (SUBLANES, LRU_BLOCK))

    u_ref[...] = _rmsnorm(h_ref[...], gmix_ref[...]).astype(bf16)
    u = u_ref[...]
    pair = 2 * LRU_BLOCK
    n_pairs = LRU_W // pair
    pairs_per_half = LRU_HALF // pair
    branch_refs = ((wxl_ref, wxh_ref), (wgl_ref, wgh_ref))

    def project(branch, p):
        if branch == 2:
            w = wq_ref[:, p * pair:(p + 1) * pair]
        else:
            q = p % pairs_per_half
            w = branch_refs[branch][p // pairs_per_half][:, q * pair:(q + 1) * pair]
        return jnp.dot(u, w, preferred_element_type=f32)

    def store_z(p, z):
        z_ref[:, p * pair:(p + 1) * pair] = z

    def gates(p, x):
        lanes = slice(p * pair, (p + 1) * pair)
        big = jnp.concatenate([xprev_ref[:, lanes], x], axis=0)
        shifted = {k: pltpu.roll(big, k, axis=0)[SUBLANES:] for k in range(1, CONV_W)}
        xc = _conv_taps(x, shifted, cw_ref[:, lanes], cb_ref[:, lanes])
        for q in range(2):
            n = 2 * p + q
            sub = slice(q * LRU_BLOCK, (q + 1) * LRU_BLOCK)
            a, gated, mult = _lru_gates(xc[:, sub], n, n * LRU_BLOCK, wg_ref, ba_ref, bx_ref, lam_ref)
            b = gated * mult
            a_ref[n] = a
            b_ref[n] = b
            b_ref[n, 0:1, :] = jnp.where(c == 0, gated[0:1, :], b[0:1, :])
        return x[tc - SUBLANES:, :]

    def gelu_gate(p, gate):
        for q in range(2):
            gate_ref[2 * p + q] = jax.nn.gelu(gate[:, q * LRU_BLOCK:(q + 1) * LRU_BLOCK])

    n_z = Z_XR // pair
    issue = _projection_issue_order(n_pairs, n_z)
    consumers = [((gates, gelu_gate, store_z)[br], br, p) for br, p in _projection_consume_order(n_pairs, n_z)]
    projected, tails = {}, []
    for consume, br, p in consumers:
        while issue and (len(projected) < REC_PROJECTIONS_AHEAD or (br, p) not in projected):
            key = issue.pop(0)
            projected[key] = project(*key)
        out = consume(p, projected.pop((br, p)))
        if consume is gates:
            tails.append(out)
    tail = jnp.concatenate(tails, axis=1)
    xprev_ref[...] = tail
    tail_ref[...] = tail

    seg = tc // SUBLANES
    seg_rows = lambda r: pl.ds(r, SUBLANES, stride=seg)

    def local_scan(r, maps):
        out = []
        for n in range(N_LRU_BLOCKS):
            a = a_ref[n, seg_rows(r), :]
            a_cum = a * maps[2 * n]
            b_cum = a * maps[2 * n + 1] + b_ref[n, seg_rows(r), :]
            a_ref[n, seg_rows(r), :] = a_cum
            b_ref[n, seg_rows(r), :] = b_cum
            out += [a_cum, b_cum]
        return tuple(out)
    identity = (jnp.ones((SUBLANES, LRU_BLOCK), f32), jnp.zeros((SUBLANES, LRU_BLOCK), f32)) * N_LRU_BLOCKS
    seg_maps = lax.fori_loop(0, seg, local_scan, identity, unroll=True)

    row = lax.broadcasted_iota(jnp.int32, (SUBLANES, LRU_BLOCK), 0)
    h_in = []
    for n in range(N_LRU_BLOCKS):
        h_prev = carry_ref[n]
        a_cum, b_cum = _tile_prefix(seg_maps[2 * n], seg_maps[2 * n + 1], row)
        h_end = a_cum * h_prev + b_cum
        h_in.append(jnp.where(row == 0, h_prev, pltpu.roll(h_end, 1, axis=0)))
        carry_ref[n] = _last_row(h_end)

    def apply_scan(r, _):
        for n in range(N_LRU_BLOCKS):
            h = a_ref[n, seg_rows(r), :] * h_in[n] + b_ref[n, seg_rows(r), :]
            b_ref[n, seg_rows(r), :] = h * gate_ref[n, seg_rows(r), :]
        return 0
    lax.fori_loop(0, seg, apply_scan, 0, unroll=True)

    y = jnp.concatenate([b_ref[n] for n in range(N_LRU_BLOCKS)], axis=1)
    o_ref[...] = _rmsnorm(y, g_ref[...]).astype(o_ref.dtype)

    @pl.when(c == n_chunks - 1)
    def _():
        lru_ref[...] = jnp.concatenate([carry_ref[n] for n in range(N_LRU_BLOCKS)], axis=1)


def _rec_prompt(h, g_mix, w_in, prev8, h0, cw, cb, wg, ba, bx, lam, g, l, *, n_batch, t_len, tc):
    n_chunks = t_len // tc
    once = pl.Buffered(1)
    row_vec = pl.BlockSpec((None, 1, LRU_W), lambda b, c: (l, 0, 0), pipeline_mode=once)
    w_half = lambda col: pl.BlockSpec((D_MODEL, LRU_HALF), lambda b, c: (0, col // LRU_HALF), pipeline_mode=once)
    state = pl.BlockSpec((None, SUBLANES, LRU_W), lambda b, c: (b, 0, 0))
    rows = lambda width: pl.BlockSpec((tc, width), lambda b, c: (b * n_chunks + c, 0))
    blocks = tc * D_MODEL * 4 + tc * Z_XR * 4 + tc * LRU_W * 2 + 3 * SUBLANES * LRU_W * 4
    scratch = (D_MODEL * IN_W * 2 + wg[0].size * 2 + tc * D_MODEL * 2 + (3 * tc + 2 * SUBLANES) * LRU_W * 4
               + 8 * tc * 2 * LRU_BLOCK * 4)
    by_block = pltpu.VMEM((N_LRU_BLOCKS, tc, LRU_BLOCK), f32)
    assert tc % SUBLANES == 0
    return pl.pallas_call(
        functools.partial(_rec_prompt_body, tc=tc, n_chunks=n_chunks),
        grid=(n_batch, n_chunks),
        in_specs=[rows(D_MODEL),
                  pl.BlockSpec((None, 1, D_MODEL), lambda b, c: (l, 0, 0), pipeline_mode=once),
                  pl.BlockSpec((D_MODEL, Z_XR), lambda b, c: (0, 0), pipeline_mode=once),
                  w_half(Z_XR), w_half(Z_XR + LRU_HALF), w_half(Z_GR), w_half(Z_GR + LRU_HALF),
                  state,
                  pl.BlockSpec((None, 1, LRU_W), lambda b, c: (b, 0, 0)),
                  pl.BlockSpec((None, CONV_W, LRU_W), lambda b, c: (l, 0, 0), pipeline_mode=once),
                  row_vec,
                  pl.BlockSpec((None,) + wg.shape[1:], lambda b, c: (l, 0, 0, 0), pipeline_mode=once),
                  row_vec, row_vec, row_vec, row_vec],
        out_specs=[rows(Z_XR), rows(LRU_W), state, state],
        out_shape=[jax.ShapeDtypeStruct((n_batch * t_len, Z_XR), f32),
                   jax.ShapeDtypeStruct((n_batch * t_len, LRU_W), bf16),
                   jax.ShapeDtypeStruct((n_batch, SUBLANES, LRU_W), f32),
                   jax.ShapeDtypeStruct((n_batch, SUBLANES, LRU_W), f32)],
        scratch_shapes=[pltpu.VMEM((tc, D_MODEL), bf16),
                        pltpu.VMEM((SUBLANES, LRU_W), f32), by_block, by_block, by_block,
                        pltpu.VMEM((N_LRU_BLOCKS, SUBLANES, LRU_BLOCK), f32)],
        compiler_params=_params(("parallel", "arbitrary"), _vmem_limit(blocks, scratch)),
        name="rec_prompt",
    )(h, g_mix, w_in, w_in, w_in, w_in, w_in, prev8, h0, cw, cb, wg, ba, bx, lam, g)


def _rec_sample_body(z_ref, prev_ref, h0_ref, cw_ref, cb_ref, wg_ref, ba_ref, bx_ref, lam_ref, g_ref,
                     o_ref, lru_ref, xc_ref, a_ref, b_ref, *, n_batch):
    row = lax.broadcasted_iota(jnp.int32, (SUBLANES, LRU_W), 0)

    def conv_step(b, _):
        r = pl.multiple_of(b * SUBLANES, SUBLANES)
        x = z_ref[pl.ds(r, SUBLANES), Z_XR:Z_XR + LRU_W]
        prev = prev_ref[pl.ds(r, SUBLANES), :]
        shifted = {k: jnp.where(row >= k, pltpu.roll(x, k, axis=0), pltpu.roll(prev, k, axis=0))
                   for k in range(1, CONV_W)}
        xc_ref[pl.ds(r, SUBLANES), :] = _conv_taps(x, shifted, cw_ref[...], cb_ref[...])
        return 0
    lax.fori_loop(0, n_batch, conv_step, 0)

    def gates_step(n, _):
        c0 = pl.multiple_of(n * LRU_BLOCK, LRU_BLOCK)
        lanes = pl.ds(c0, LRU_BLOCK)
        a, gated, mult = _lru_gates(xc_ref[:, lanes], n, c0, wg_ref, ba_ref, bx_ref, lam_ref)
        a_ref[:, lanes] = a
        b_ref[:, lanes] = gated * mult
        return 0
    lax.fori_loop(0, N_LRU_BLOCKS, gates_step, 0)

    def scan_step(b, _):
        rows = pl.ds(pl.multiple_of(b * SUBLANES, SUBLANES), SUBLANES)
        h_in = jnp.broadcast_to(h0_ref[pl.ds(b, 1), :], (SUBLANES, LRU_W))
        hh = _scan_tile(a_ref[rows, :], b_ref[rows, :], h_in, row)
        b_ref[rows, :] = hh * jax.nn.gelu(z_ref[rows, Z_GR:Z_GR + LRU_W])
        lru_ref[pl.ds(b, 1), :] = hh[SUBLANES - 1:SUBLANES, :]
        return 0
    lax.fori_loop(0, n_batch, scan_step, 0)
    o_ref[...] = _rmsnorm(b_ref[...], g_ref[...]).astype(o_ref.dtype)


def _rec_sample(z, prev8, h0, cw, cb, wg, ba, bx, lam, g, l, *, n_batch, t_len):
    assert t_len == SUBLANES and PAST_LEN > 0
    rows = n_batch * t_len
    layer = lambda a: pl.BlockSpec((None,) + a.shape[1:], lambda i: (l,) + (0,) * (a.ndim - 1))
    blocks = z.size * 4 + (2 * rows + 2 * n_batch) * LRU_W * 4 + wg[0].size * 2
    return pl.pallas_call(
        functools.partial(_rec_sample_body, n_batch=n_batch),
        grid=(1,),
        in_specs=[pl.BlockSpec(z.shape, lambda i: (0, 0))] + [layer(a) for a in (prev8, h0, cw, cb, wg, ba, bx, lam, g)],
        out_specs=[pl.BlockSpec((rows, LRU_W), lambda i: (0, 0)), pl.BlockSpec((n_batch, LRU_W), lambda i: (0, 0))],
        out_shape=[jax.ShapeDtypeStruct((rows, LRU_W), bf16),
                   jax.ShapeDtypeStruct((n_batch, LRU_W), f32)],
        scratch_shapes=[pltpu.VMEM((rows, LRU_W), f32)] * 3,
        compiler_params=_params(("arbitrary",), _vmem_limit(blocks, 3 * rows * LRU_W * 4)),
        name="rec_sample",
    )(z, prev8, h0, cw, cb, wg, ba, bx, lam, g)


def _mix_mlp_body(h_ref, attn_ref, rec_ref, wo_ref, g_ref, wu_ref, wd_ref, o_ref, u_ref):
    @pl.when(pl.program_id(1) == 0)
    def _():
        o_ref[...] = h_ref[...] + jnp.dot(attn_ref[...], wo_ref[0:ATTN_W, :], preferred_element_type=f32)
        o_ref[...] += jnp.dot(rec_ref[...], wo_ref[ATTN_W:ATTN_W + LRU_W, :], preferred_element_type=f32)
        u_ref[...] = _rmsnorm(o_ref[...], g_ref[...]).astype(bf16)
    hid = jnp.dot(u_ref[...], wu_ref[...], preferred_element_type=f32)
    hid = jnp.square(jnp.maximum(hid, 0.0)).astype(bf16)
    o_ref[...] += jnp.dot(hid, wd_ref[...], preferred_element_type=f32)


def _mix_mlp(h, attn, rec, wo, g, wu, wd, l, *, tm, tf):
    m = h.shape[0]
    once = pl.Buffered(1)
    blocks = 2 * tm * D_MODEL * 4 + tm * (ATTN_W + LRU_W) * 2 + 2 * D_MODEL * tf * 2
    scratch = wo[0].size * 2 + tm * D_MODEL * 2 + tm * tf * 6
    return pl.pallas_call(
        _mix_mlp_body,
        grid=(m // tm, D_FF // tf),
        in_specs=[pl.BlockSpec((tm, D_MODEL), lambda i, f: (i, 0)),
                  pl.BlockSpec((tm, ATTN_W), lambda i, f: (i, 0)),
                  pl.BlockSpec((tm, LRU_W), lambda i, f: (i, 0)),
                  pl.BlockSpec((None,) + wo.shape[1:], lambda i, f: (l, 0, 0), pipeline_mode=once),
                  pl.BlockSpec((None, 1, D_MODEL), lambda i, f: (l, 0, 0), pipeline_mode=once),
                  pl.BlockSpec((None, D_MODEL, tf), lambda i, f: (l, 0, f)),
                  pl.BlockSpec((None, tf, D_MODEL), lambda i, f: (l, f, 0))],
        out_specs=pl.BlockSpec((tm, D_MODEL), lambda i, f: (i, 0)),
        out_shape=jax.ShapeDtypeStruct((m, D_MODEL), f32),
        scratch_shapes=[pltpu.VMEM((tm, D_MODEL), bf16)],
        compiler_params=_params(("parallel", "arbitrary"), _vmem_limit(blocks, scratch)),
        name="mix_mlp",
    )(h, attn, rec, wo, g, wu, wd)


def _mix_mlp_cast_body(h_ref, attn_ref, rec_ref, wo_ref, g_ref, wu_ref, wd_ref,
                       o_ref, wob_ref, wub_ref, wdb_ref, u_ref):
    @pl.when(pl.program_id(1) == 0)
    def _():
        wob_ref[...] = wo_ref[...].astype(bf16)
        o_ref[...] = h_ref[...] + jnp.dot(attn_ref[...], wob_ref[0:ATTN_W, :], preferred_element_type=f32)
        o_ref[...] += jnp.dot(rec_ref[...], wob_ref[ATTN_W:ATTN_W + LRU_W, :], preferred_element_type=f32)
        u_ref[...] = _rmsnorm(o_ref[...], g_ref[...]).astype(bf16)
    wub_ref[...] = wu_ref[...].astype(bf16)
    wdb_ref[...] = wd_ref[...].astype(bf16)
    hid = jnp.dot(u_ref[...], wub_ref[...], preferred_element_type=f32)
    hid = jnp.square(jnp.maximum(hid, 0.0)).astype(bf16)
    o_ref[...] += jnp.dot(hid, wdb_ref[...], preferred_element_type=f32)


def _mix_mlp_cast(h, attn, rec, wo, g, wu, wd, l, *, tf):
    m = h.shape[0]
    once = pl.Buffered(1)
    blocks = 2 * D_MODEL * tf * (4 + 2)
    scratch = (2 * m * D_MODEL * 4 + m * (ATTN_W + LRU_W) * 2 + wo[0].size * (4 + 2) + m * D_MODEL * 2
               + m * tf * 6)
    return pl.pallas_call(
        _mix_mlp_cast_body,
        grid=(1, D_FF // tf),
        in_specs=[pl.BlockSpec((m, D_MODEL), lambda i, f: (0, 0), pipeline_mode=once),
                  pl.BlockSpec((m, ATTN_W), lambda i, f: (0, 0), pipeline_mode=once),
                  pl.BlockSpec((m, LRU_W), lambda i, f: (0, 0), pipeline_mode=once),
                  pl.BlockSpec((None,) + wo.shape[1:], lambda i, f: (l, 0, 0), pipeline_mode=once),
                  pl.BlockSpec((None, 1, D_MODEL), lambda i, f: (l, 0, 0), pipeline_mode=once),
                  pl.BlockSpec((None, D_MODEL, tf), lambda i, f: (l, 0, f)),
                  pl.BlockSpec((None, tf, D_MODEL), lambda i, f: (l, f, 0))],
        out_specs=[pl.BlockSpec((m, D_MODEL), lambda i, f: (0, 0)),
                   pl.BlockSpec(wo.shape[1:], lambda i, f: (0, 0)),
                   pl.BlockSpec((D_MODEL, tf), lambda i, f: (0, f)),
                   pl.BlockSpec((tf, D_MODEL), lambda i, f: (f, 0))],
        out_shape=[jax.ShapeDtypeStruct((m, D_MODEL), f32),
                   jax.ShapeDtypeStruct(wo.shape[1:], bf16),
                   jax.ShapeDtypeStruct(wu.shape[1:], bf16),
                   jax.ShapeDtypeStruct(wd.shape[1:], bf16)],
        scratch_shapes=[pltpu.VMEM((m, D_MODEL), bf16)],
        compiler_params=_params(("arbitrary", "arbitrary"), _vmem_limit(blocks, scratch)),
        name="mix_mlp_cast",
    )(h, attn, rec, wo, g, wu, wd)


def _final_norm_body(h_ref, g_ref, o_ref):
    o_ref[0] = _rmsnorm(h_ref[0], g_ref[...])


def _final_norm(h, g, *, skip, tr):
    n_batch, t_len, _ = h.shape
    s_len = t_len - skip
    tiles = tr // SUBLANES
    skip_tiles = skip // SUBLANES
    h4 = h.reshape(n_batch, t_len // SUBLANES, SUBLANES, D_MODEL)
    out = pl.pallas_call(
        _final_norm_body,
        grid=(n_batch, s_len // tr),
        in_specs=[pl.BlockSpec((pl.Element(1), pl.Element(tiles), pl.Element(SUBLANES), pl.Element(D_MODEL)),
                               lambda b, r: (b, skip_tiles + r * tiles, 0, 0)),
                  pl.BlockSpec((1, D_MODEL), lambda b, r: (0, 0))],
        out_specs=pl.BlockSpec((1, tiles, SUBLANES, D_MODEL), lambda b, r: (b, r, 0, 0)),
        out_shape=jax.ShapeDtypeStruct((n_batch, s_len // SUBLANES, SUBLANES, D_MODEL), f32),
        compiler_params=_params(("parallel", "parallel"), _vmem_limit(2 * tr * D_MODEL * 4)),
        name="final_norm",
    )(h4, g)
    return out.reshape(n_batch, s_len, D_MODEL)


def _row_tile(m, candidates):
    for tm in candidates:
        if m % tm == 0:
            return tm
    raise ValueError(f"no row tile for {m} rows")


def kernel(x_prompt, x_sample, cache_k_win, cache_v_win, state_conv, state_lru, meta_tokens, norm_mix_g, w_in,
           conv_w, conv_b, w_gate_a, b_gate_a, w_gate_x, b_gate_x, lru_lambda, attn_sinks, rel_bias, attn_out_g,
           rec_out_g, w_out, norm_mlp_g, w_up, w_down, final_norm_g):
    n_p, s_p, _ = x_prompt.shape
    n_s, t_s, _ = x_sample.shape
    t_p = N_META + s_p
    buf = cache_k_win.shape[2]
    assert t_p % BF16_ROWS == 0 and buf == WINDOW

    w_gates = jnp.concatenate([w_gate_a, w_gate_x], axis=-1).astype(bf16)
    rows3 = lambda p: p[:, None, :]
    g_mix, g_mlp, g_attn, g_rec = rows3(norm_mix_g), rows3(norm_mlp_g), rows3(attn_out_g), rows3(rec_out_g)
    cb3, ba3, bx3, lam3 = rows3(conv_b), rows3(b_gate_a), rows3(b_gate_x), rows3(lru_lambda)
    g_attn_t = jnp.broadcast_to(
        attn_out_g.reshape(DEPTH, N_Q_HEADS, HEAD_DIM).transpose(0, 2, 1)[..., None],
        (DEPTH, HEAD_DIM, N_Q_HEADS, WINDOW)).reshape(DEPTH, HEAD_DIM, ATTN_W)

    meta = jnp.broadcast_to(meta_tokens.astype(x_prompt.dtype)[None], (n_p, N_META, D_MODEL))
    hp = jnp.concatenate([meta, x_prompt], axis=1).reshape(n_p * t_p, D_MODEL)
    hs = x_sample.reshape(n_s * t_s, D_MODEL)

    bias_qk, bias_kq = _bias_tables(rel_bias)
    bias_qk = bias_qk.reshape(N_Q_HEADS * WINDOW, 2 * WINDOW)
    prev_p = jnp.zeros((n_p, SUBLANES, LRU_W), f32)
    h0_p = jnp.zeros((n_p, 1, LRU_W), f32)
    ck = cache_k_win.reshape(DEPTH, n_s, buf * N_KV_HEADS, HEAD_DIM)
    cv = cache_v_win.reshape(DEPTH, n_s, buf * N_KV_HEADS, HEAD_DIM)
    prev_s = jnp.pad(state_conv, ((0, 0), (0, 0), (SUBLANES - (CONV_W - 1), 0), (0, 0)))
    prev_s = prev_s.reshape(DEPTH, n_s * SUBLANES, LRU_W)

    tm_p = _row_tile(n_p * t_p, (688, 344))
    tc_p = _row_tile(t_p, (688, 344, 48, 16))
    tn = IN_W // 2

    kp_l, vp_l, cp_l, lp_l, ks_l, vs_l, cs_l, ls_l = ([] for _ in range(8))
    for l in range(DEPTH):
        z, w_in_b = _inproj_cast(hs, g_mix, w_in, l, tn=tn)
        attn, nk, nv = _attn_sample(z, ck, cv, bias_qk, attn_sinks, g_attn, l, n_batch=n_s, t_len=t_s)
        rec, lru = _rec_sample(z, prev_s, state_lru, conv_w, cb3, w_gates, ba3, bx3, lam3, g_rec, l,
                               n_batch=n_s, t_len=t_s)
        hs, w_out_b, w_up_b, w_down_b = _mix_mlp_cast(hs, attn, rec, w_out, g_mlp, w_up, w_down, l, tf=512)
        z3 = z.reshape(n_s, t_s, IN_W)
        ks_l.append(nk)
        vs_l.append(nv)
        cs_l.append(z3[:, t_s - (CONV_W - 1):, Z_XR:Z_XR + LRU_W])
        ls_l.append(lru)
        z, rec, lru, x_tail = _rec_prompt(hp, g_mix, w_in_b, prev_p, h0_p, conv_w, cb3, w_gates, ba3, bx3, lam3,
                                          g_rec, l, n_batch=n_p, t_len=t_p, tc=tc_p)
        attn = _attn_prompt(z, bias_kq, attn_sinks, g_attn_t, l, n_batch=n_p, t_len=t_p)
        hp = _mix_mlp(hp, attn, rec, w_out_b[None], g_mlp[l:l + 1], w_up_b[None], w_down_b[None], 0,
                      tm=tm_p, tf=1024)
        z3 = z.reshape(n_p, t_p, Z_XR)
        kp_l.append(z3[:, t_p - WINDOW:, Z_K:Z_K + KV_W].reshape(n_p, WINDOW, N_KV_HEADS, HEAD_DIM))
        vp_l.append(z3[:, t_p - WINDOW:, Z_V:Z_V + KV_W].reshape(n_p, WINDOW, N_KV_HEADS, HEAD_DIM))
        cp_l.append(x_tail[:, SUBLANES - (CONV_W - 1):])
        lp_l.append(lru[:, 0])

    g_fin = final_norm_g[None, :]
    y_prompt = _final_norm(hp.reshape(n_p, t_p, D_MODEL), g_fin, skip=N_META, tr=512)
    y_sample = _final_norm(hs.reshape(1, n_s * t_s, D_MODEL), g_fin, skip=0, tr=n_s * t_s)
    y_sample = y_sample.reshape(n_s, t_s, D_MODEL)
    cache_shape = (DEPTH, n_s, buf, N_KV_HEADS, HEAD_DIM)
    return (y_prompt, y_sample,
            jnp.stack(kp_l), jnp.stack(vp_l), jnp.stack(cp_l), jnp.stack(lp_l),
            jnp.stack(ks_l).reshape(cache_shape), jnp.stack(vs_l).reshape(cache_shape),
            jnp.stack(cs_l), jnp.stack(ls_l))
```

```python
import functools
import math

import jax
import jax.numpy as jnp
from jax import lax
from jax.experimental import pallas as pl
from jax.experimental.pallas import tpu as pltpu

f32 = jnp.float32
bf16 = jnp.bfloat16

D_MODEL = 2048
DEPTH = 4
PAST_LEN = 16384
HEAD_DIM = 128
N_Q_HEADS = 8
N_KV_HEADS = 2
Q_PER_KV = N_Q_HEADS // N_KV_HEADS
ATTN_W = N_Q_HEADS * HEAD_DIM
KV_W = N_KV_HEADS * HEAD_DIM
LRU_W = D_MODEL // 2
N_LRU_BLOCKS = 8
LRU_BLOCK = LRU_W // N_LRU_BLOCKS
CONV_W = 4
LRU_C = 8.0
IN_W = ATTN_W + 2 * KV_W + 2 * LRU_W
D_FF = 4 * D_MODEL
WINDOW = 128
N_BUCKETS = 32
MAX_DISTANCE = 128
N_META = 16
EPS = 1e-6
ATTN_SCALE = HEAD_DIM ** -0.5
INV_ATTN_SCALE = HEAD_DIM ** 0.5
EXP2_PER_T = math.log2(math.e) / INV_ATTN_SCALE

Z_Q = 0
Z_K = ATTN_W
Z_V = Z_K + KV_W
Z_XR = Z_V + KV_W
Z_GR = Z_XR + LRU_W
LRU_HALF = LRU_W // 2

ATTN_LEAD = WINDOW + (-N_META) % WINDOW

SUBLANES = 8
BF16_ROWS = 16
V7X_VMEM_BYTES = 64 * 1024 * 1024
VMEM_CAP_BYTES = V7X_VMEM_BYTES - 2 * 1024 * 1024


def _vmem_limit(pipelined_bytes, scratch_bytes=0):
    est = 2 * pipelined_bytes + scratch_bytes
    return int(min(VMEM_CAP_BYTES, est + est // 2 + (8 << 20)))


def _params(semantics, vmem_bytes):
    return pltpu.CompilerParams(dimension_semantics=semantics, vmem_limit_bytes=vmem_bytes)


def _rms_scale(x):
    return lax.rsqrt(jnp.mean(x * x, axis=-1, keepdims=True) + EPS)


def _rmsnorm(x, g):
    return x * _rms_scale(x) * g


def _inproj_cast_body(h_ref, g_ref, w_ref, z_ref, wb_ref, u_ref):
    @pl.when(pl.program_id(0) == 0)
    def _():
        u_ref[...] = _rmsnorm(h_ref[...], g_ref[...]).astype(bf16)
    wb_ref[...] = w_ref[...].astype(bf16)
    z_ref[...] = jnp.dot(u_ref[...], wb_ref[...], preferred_element_type=f32)


def _inproj_cast(h, g, w, l, *, tn):
    m = h.shape[0]
    blocks = D_MODEL * tn * (4 + 2) + m * tn * 4
    return pl.pallas_call(
        _inproj_cast_body,
        grid=(IN_W // tn,),
        in_specs=[pl.BlockSpec((m, D_MODEL), lambda j: (0, 0)),
                  pl.BlockSpec((None, 1, D_MODEL), lambda j: (l, 0, 0)),
                  pl.BlockSpec((None, D_MODEL, tn), lambda j: (l, 0, j))],
        out_specs=[pl.BlockSpec((m, tn), lambda j: (0, j)),
                   pl.BlockSpec((D_MODEL, tn), lambda j: (0, j))],
        out_shape=[jax.ShapeDtypeStruct((m, IN_W), f32), jax.ShapeDtypeStruct(w.shape[1:], bf16)],
        scratch_shapes=[pltpu.VMEM((m, D_MODEL), bf16)],
        compiler_params=_params(("arbitrary",), _vmem_limit(blocks, m * D_MODEL * (2 * 4 + 2))),
        name="inproj_cast",
    )(h, g, w)


def _rel_bias_of(dist, rel_ref, h):
    n = jnp.maximum(dist, 0)
    max_exact = N_BUCKETS // 2
    nf = jnp.maximum(n, 1).astype(f32)
    large = max_exact + (jnp.log(nf / max_exact) / math.log(MAX_DISTANCE / max_exact)
                         * (N_BUCKETS - max_exact)).astype(jnp.int32)
    large = jnp.minimum(large, N_BUCKETS - 1)
    bucket = jnp.where(n < max_exact, n, large)
    acc = jnp.zeros(dist.shape, f32)
    for b in range(N_BUCKETS):
        acc = jnp.where(bucket == b, rel_ref[b, h], acc)
    return jnp.where((dist >= 0) & (dist < WINDOW), acc, -jnp.inf)


def _bias_table_body(rel_ref, qk_ref, kq_ref):
    shape_qk = (WINDOW, 2 * WINDOW)
    dist_qk = (lax.broadcasted_iota(jnp.int32, shape_qk, 0) + WINDOW
               - lax.broadcasted_iota(jnp.int32, shape_qk, 1))
    shape_kq = (2 * WINDOW, WINDOW)
    dist_kq = (lax.broadcasted_iota(jnp.int32, shape_kq, 1) + WINDOW
               - lax.broadcasted_iota(jnp.int32, shape_kq, 0))
    for h in range(N_Q_HEADS):
        kv, g = divmod(h, Q_PER_KV)
        qk_ref[h] = _rel_bias_of(dist_qk, rel_ref, h)
        kq_ref[kv, :, g * WINDOW:(g + 1) * WINDOW] = _rel_bias_of(dist_kq, rel_ref, h) * INV_ATTN_SCALE


def _bias_tables(rel_bias):
    return pl.pallas_call(
        _bias_table_body,
        in_specs=[pl.BlockSpec(memory_space=pltpu.SMEM)],
        out_shape=[jax.ShapeDtypeStruct((N_Q_HEADS, WINDOW, 2 * WINDOW), f32),
                   jax.ShapeDtypeStruct((N_KV_HEADS, 2 * WINDOW, Q_PER_KV * WINDOW), f32)],
        name="bias_tables",
    )(rel_bias)


def _attn_blocks_t(blocks, kpad_ref, vt_ref, bias_ref, sinks_ref, l, gt_ref):
    kv_heads = [range(kv * Q_PER_KV, (kv + 1) * Q_PER_KV) for kv in range(N_KV_HEADS)]
    sink = [jnp.concatenate([jnp.full((1, WINDOW), sinks_ref[l, h] * INV_ATTN_SCALE, f32) for h in heads], axis=1)
            for heads in kv_heads]
    chains = [(i, kv) for i in range(len(blocks)) for kv in range(N_KV_HEADS)]
    scores = {}
    for i, kv in chains:
        q_rows, r0, _ = blocks[i]
        kwin = kpad_ref[pl.ds(r0, 2 * WINDOW), kv * HEAD_DIM:(kv + 1) * HEAD_DIM]
        q = jnp.concatenate([q_rows[:, h * HEAD_DIM:(h + 1) * HEAD_DIM] for h in kv_heads[kv]],
                            axis=0).astype(bf16)
        scores[i, kv] = lax.dot_general(kwin, q, (((1,), (1,)), ((), ())), preferred_element_type=f32)
    probs, invs = {}, {}
    for i, kv in chains:
        t = scores[i, kv] + bias_ref[kv]
        lead_keys = blocks[i][2]
        if lead_keys:
            key = lax.broadcasted_iota(jnp.int32, t.shape, 0)
            t = jnp.where(key < lead_keys, -jnp.inf, t)
        m = jnp.maximum(jnp.max(t, axis=0, keepdims=True), sink[kv])
        p = jnp.exp2((t - m) * EXP2_PER_T)
        invs[i, kv] = 1.0 / (jnp.sum(p, axis=0, keepdims=True) + jnp.exp2((sink[kv] - m) * EXP2_PER_T))
        probs[i, kv] = p.astype(bf16)
    raws = {}
    for i, kv in chains:
        vt = vt_ref[kv * HEAD_DIM:(kv + 1) * HEAD_DIM, pl.ds(blocks[i][1], 2 * WINDOW)]
        raws[i, kv] = jnp.dot(vt, probs[i, kv], preferred_element_type=f32)
    outs = []
    for i in range(len(blocks)):
        raw = jnp.concatenate([raws[i, kv] for kv in range(N_KV_HEADS)], axis=1)
        inv = jnp.concatenate([invs[i, kv] for kv in range(N_KV_HEADS)], axis=1)
        sq = jnp.sum(raw * raw, axis=0, keepdims=True) * (inv * inv)
        ssq = sq[:, 0:WINDOW]
        for h in range(1, N_Q_HEADS):
            ssq = ssq + sq[:, h * WINDOW:(h + 1) * WINDOW]
        r = lax.rsqrt(ssq * (1.0 / ATTN_W) + EPS)
        yt = raw * (inv * jnp.concatenate([r] * N_Q_HEADS, axis=1)) * gt_ref[...]
        outs.append(jnp.concatenate([yt[:, h * WINDOW:(h + 1) * WINDOW].T for h in range(N_Q_HEADS)], axis=1))
    return outs


ATTN_GROUP = 5


def _attn_prompt_body(q_ref, k_ref, v_ref, bias_ref, sinks_ref, gt_ref, o_ref, kpad_ref, vpad_ref, vt_ref,
                      *, t_len, l):
    pad_rows = kpad_ref.shape[0]
    kpad_ref[0:ATTN_LEAD, :] = jnp.zeros((ATTN_LEAD, KV_W), bf16)
    kpad_ref[ATTN_LEAD:pad_rows, :] = k_ref[...].astype(bf16)
    vpad_ref[0:ATTN_LEAD, :] = jnp.zeros((ATTN_LEAD, KV_W), f32)
    vpad_ref[ATTN_LEAD:pad_rows, :] = v_ref[...]

    def transpose_step(c, _):
        r = pl.multiple_of(c * WINDOW, WINDOW)
        chunk = vpad_ref[pl.ds(r, WINDOW), :]
        for kv in range(N_KV_HEADS):
            vt_ref[kv * HEAD_DIM:(kv + 1) * HEAD_DIM, pl.ds(r, WINDOW)] = (
                chunk[:, kv * HEAD_DIM:(kv + 1) * HEAD_DIM].T.astype(bf16))
        return 0
    lax.fori_loop(0, pad_rows // WINDOW, transpose_step, 0, unroll=True)

    def q_rows_of(j):
        start = j * WINDOW - (ATTN_LEAD - WINDOW)
        return pl.ds(start if isinstance(j, int) else pl.multiple_of(start, BF16_ROWS), WINDOW)

    n_first = 2 * WINDOW - ATTN_LEAD
    q0 = jnp.concatenate([jnp.zeros((WINDOW - n_first, ATTN_W), f32), q_ref[0:n_first, :]], axis=0)
    y0, y1 = _attn_blocks_t([(q0, 0, ATTN_LEAD), (q_ref[q_rows_of(1), :], WINDOW, ATTN_LEAD - WINDOW)],
                            kpad_ref, vt_ref, bias_ref, sinks_ref, l, gt_ref)
    o_ref[0:n_first, :] = y0[WINDOW - n_first:, :].astype(o_ref.dtype)
    o_ref[q_rows_of(1), :] = y1.astype(o_ref.dtype)

    def step(i, _):
        js = [2 + i * ATTN_GROUP + g for g in range(ATTN_GROUP)]
        ys = _attn_blocks_t([(q_ref[q_rows_of(j), :], pl.multiple_of(j * WINDOW, WINDOW), 0) for j in js],
                            kpad_ref, vt_ref, bias_ref, sinks_ref, l, gt_ref)
        for j, y in zip(js, ys):
            o_ref[q_rows_of(j), :] = y.astype(o_ref.dtype)
        return 0
    n_blocks = pad_rows // WINDOW - 1
    assert (n_blocks - 2) % ATTN_GROUP == 0
    lax.fori_loop(0, (n_blocks - 2) // ATTN_GROUP, step, 0, unroll=True)


def _attn_prompt(z, bias_kq, sinks, gt, l, *, n_batch, t_len):
    pad_rows = ATTN_LEAD + t_len
    assert pad_rows % WINDOW == 0 and (ATTN_LEAD - WINDOW) % BF16_ROWS == 0
    blocks = t_len * (ATTN_W + 2 * KV_W) * 4 + t_len * ATTN_W * 2 + bias_kq.size * 4 + HEAD_DIM * ATTN_W * 4
    scratch = pad_rows * KV_W * (2 + 4 + 2)
    return pl.pallas_call(
        functools.partial(_attn_prompt_body, t_len=t_len, l=l),
        grid=(n_batch,),
        in_specs=[pl.BlockSpec((t_len, ATTN_W), lambda b: (b, Z_Q // ATTN_W)),
                  pl.BlockSpec((t_len, KV_W), lambda b: (b, Z_K // KV_W)),
                  pl.BlockSpec((t_len, KV_W), lambda b: (b, Z_V // KV_W)),
                  pl.BlockSpec(bias_kq.shape, lambda b: (0, 0, 0)),
                  pl.BlockSpec(memory_space=pltpu.SMEM),
                  pl.BlockSpec((None, HEAD_DIM, ATTN_W), lambda b: (l, 0, 0))],
        out_specs=pl.BlockSpec((t_len, ATTN_W), lambda b: (b, 0)),
        out_shape=jax.ShapeDtypeStruct((n_batch * t_len, ATTN_W), bf16),
        scratch_shapes=[pltpu.VMEM((pad_rows, KV_W), bf16), pltpu.VMEM((pad_rows, KV_W), f32),
                        pltpu.VMEM((KV_W, pad_rows), bf16)],
        compiler_params=_params(("parallel",), _vmem_limit(blocks, scratch)),
        name="attn_prompt",
    )(z, z, z, bias_kq, sinks, gt)


DECODE_GROUP = 8


def _attn_sample_body(z_ref, ck_ref, cv_ref, bias_ref, sinks_ref, g_ref, o_ref, nk_ref, nv_ref, acc_ref,
                      *, n_batch, t_len, buf, l):
    zeros = jnp.zeros((2 * WINDOW - buf - t_len, HEAD_DIM), f32)
    kept = N_KV_HEADS * (buf - t_len)
    kv_heads = [range(kv * Q_PER_KV, (kv + 1) * Q_PER_KV) for kv in range(N_KV_HEADS)]
    bias = [jnp.concatenate([bias_ref[h * WINDOW:h * WINDOW + t_len, :] for h in heads], axis=0)
            for heads in kv_heads]
    sink = [jnp.concatenate([jnp.full((t_len, 1), sinks_ref[l, h], f32) for h in heads], axis=0)
            for heads in kv_heads]

    def window(c_ref, n_ref, b, r0, col, kv):
        new = z_ref[pl.ds(r0, t_len), col + kv * HEAD_DIM:col + (kv + 1) * HEAD_DIM]
        n_ref[b, pl.ds(kept + kv, t_len, stride=N_KV_HEADS), :] = new
        old = c_ref[b, pl.ds(kv, buf, stride=N_KV_HEADS), :]
        return jnp.concatenate([old, new, zeros], axis=0).astype(bf16)

    def group_step(i, _):
        seqs = [i * DECODE_GROUP + j for j in range(DECODE_GROUP)]
        rows = [pl.multiple_of(b * t_len, t_len) for b in seqs]
        scores = {}
        for j, (b, r0) in enumerate(zip(seqs, rows)):
            nk_ref[b, 0:kept, :] = ck_ref[b, N_KV_HEADS * t_len:N_KV_HEADS * buf, :]
            nv_ref[b, 0:kept, :] = cv_ref[b, N_KV_HEADS * t_len:N_KV_HEADS * buf, :]
            for kv, heads in enumerate(kv_heads):
                q = jnp.concatenate([z_ref[pl.ds(r0, t_len), Z_Q + h * HEAD_DIM:Z_Q + (h + 1) * HEAD_DIM]
                                     for h in heads], axis=0).astype(bf16)
                k = window(ck_ref, nk_ref, b, r0, Z_K, kv)
                scores[j, kv] = lax.dot_general(q, k, (((1,), (1,)), ((), ())), preferred_element_type=f32)
        probs, denoms = {}, {}
        for key, s in scores.items():
            kv = key[1]
            s = s * ATTN_SCALE + bias[kv]
            m = jnp.maximum(jnp.max(s, axis=-1, keepdims=True), sink[kv])
            p = jnp.exp(s - m)
            denoms[key] = jnp.sum(p, axis=-1, keepdims=True) + jnp.exp(sink[kv] - m)
            probs[key] = p.astype(bf16)
        for j, (b, r0) in enumerate(zip(seqs, rows)):
            outs = []
            for kv in range(N_KV_HEADS):
                v = window(cv_ref, nv_ref, b, r0, Z_V, kv)
                o = jnp.dot(probs[j, kv], v, preferred_element_type=f32) / denoms[j, kv]
                outs.extend(o[g * t_len:(g + 1) * t_len] for g in range(Q_PER_KV))
            acc_ref[pl.ds(r0, t_len), :] = jnp.concatenate(outs, axis=1)
        return 0
    lax.fori_loop(0, n_batch // DECODE_GROUP, group_step, 0, unroll=True)
    o_ref[...] = _rmsnorm(acc_ref[...], g_ref[...]).astype(o_ref.dtype)


def _attn_sample(z, ck, cv, bias_qk, sinks, g, l, *, n_batch, t_len):
    buf = ck.shape[2] // N_KV_HEADS
    assert buf == WINDOW and t_len == SUBLANES and n_batch % DECODE_GROUP == 0
    rows = n_batch * t_len
    whole =lambda a: pl.BlockSpec(a.shape, lambda i: (0,) * a.ndim)
    cache = pl.BlockSpec((None,) + ck.shape[1:], lambda i: (l, 0, 0, 0))
    blocks = (z.size + 4 * ck[0].size + bias_qk.size) * 4 + rows * ATTN_W * 2
    return pl.pallas_call(
        functools.partial(_attn_sample_body, n_batch=n_batch, t_len=t_len, buf=buf, l=l),
        grid=(1,),
        in_specs=[whole(z), cache, cache, whole(bias_qk), pl.BlockSpec(memory_space=pltpu.SMEM),
                  pl.BlockSpec((None, 1, ATTN_W), lambda i: (l, 0, 0))],
        out_specs=[pl.BlockSpec((rows, ATTN_W), lambda i: (0, 0)),
                   pl.BlockSpec(ck.shape[1:], lambda i: (0, 0, 0)),
                   pl.BlockSpec(cv.shape[1:], lambda i: (0, 0, 0))],
        out_shape=[jax.ShapeDtypeStruct((rows, ATTN_W), bf16),
                   jax.ShapeDtypeStruct(ck.shape[1:], f32),
                   jax.ShapeDtypeStruct(cv.shape[1:], f32)],
        scratch_shapes=[pltpu.VMEM((rows, ATTN_W), f32)],
        compiler_params=_params(("arbitrary",), _vmem_limit(blocks, rows * ATTN_W * 4)),
        name="attn_sample",
    )(z, ck, cv, bias_qk, sinks, g)


def _lru_gates(xc, n, c0, wg_ref, ba_ref, bx_ref, lam_ref):
    lanes = pl.ds(c0, LRU_BLOCK)
    gates = jnp.dot(xc.astype(bf16), wg_ref[n], preferred_element_type=f32)
    gate_a = jax.nn.sigmoid(gates[:, :LRU_BLOCK] + ba_ref[:, lanes])
    gate_x = jax.nn.sigmoid(gates[:, LRU_BLOCK:] + bx_ref[:, lanes])
    log_a = -LRU_C * gate_a * jax.nn.softplus(-lam_ref[:, lanes])
    a = jnp.exp(log_a)
    y = -jnp.tanh(log_a) * (1.0 + a * a)
    mult = jnp.where(y > 0.0, y * lax.rsqrt(y), 0.0)
    return a, xc * gate_x, mult


def _tile_prefix(a, b, row):
    for d in (1, 2, 4):
        a_prev = pltpu.roll(a, d, axis=0)
        b_prev = pltpu.roll(b, d, axis=0)
        keep = row >= d
        b = jnp.where(keep, a * b_prev + b, b)
        a = jnp.where(keep, a * a_prev, a)
    return a, b


def _scan_tile(a, b, h_in, row):
    a, b = _tile_prefix(a, b, row)
    return a * h_in + b


def _last_row(h):
    return jnp.broadcast_to(h[SUBLANES - 1:SUBLANES, :], h.shape)


def _conv_taps(x, shifted, cw, cb):
    out = cb + shifted[CONV_W - 1] * cw[0:1]
    for j in range(1, CONV_W - 1):
        out = out + shifted[CONV_W - 1 - j] * cw[j:j + 1]
    return out + x * cw[CONV_W - 1:CONV_W]


REC_PROJECTIONS_AHEAD = 5


def _projection_issue_order(n_pairs, n_z):
    return [(0, p) for p in range(n_pairs)] + [(1, p) for p in range(n_pairs)] + [(2, p) for p in range(n_z)]


def _projection_consume_order(n_pairs, n_z):
    order = [step for p in range(n_pairs) for step in ((0, p), (1, p))]
    return order + [(2, p) for p in range(n_z)]


def _rec_prompt_body(h_ref, gmix_ref, wq_ref, wxl_ref, wxh_ref, wgl_ref, wgh_ref, prev_ref, h0_ref, cw_ref, cb_ref,
                     wg_ref, ba_ref, bx_ref, lam_ref, g_ref, z_ref, o_ref, lru_ref, tail_ref, u_ref, xprev_ref,
                     a_ref, b_ref, gate_ref, carry_ref, *, tc, n_chunks):
    c = pl.program_id(1)

    @pl.when(c == 0)
    def _():
        xprev_ref[...] = prev_ref[...]
        for n in range(N_LRU_BLOCKS):
            carry_ref[n] = jnp.broadcast_to(h0_ref[:, n * LRU_BLOCK:(n + 1) * LRU_BLOCK], (SUBLANES, LRU_BLOCK))

    u_ref[...] = _rmsnorm(h_ref[...], gmix_ref[...]).astype(bf16)
    u = u_ref[...]
    pair = 2 * LRU_BLOCK
    n_pairs = LRU_W // pair
    pairs_per_half = LRU_HALF // pair
    branch_refs = ((wxl_ref, wxh_ref), (wgl_ref, wgh_ref))

    def project(branch, p):
        if branch == 2:
            w = wq_ref[:, p * pair:(p + 1) * pair]
        else:
            q = p % pairs_per_half
            w = branch_refs[branch][p // pairs_per_half][:, q * pair:(q + 1) * pair]
        return jnp.dot(u, w, preferred_element_type=f32)

    def store_z(p, z):
        z_ref[:, p * pair:(p + 1) * pair] = z

    def gates(p, x):
        lanes = slice(p * pair, (p + 1) * pair)
        big = jnp.concatenate([xprev_ref[:, lanes], x], axis=0)
        shifted = {k: pltpu.roll(big, k, axis=0)[SUBLANES:] for k in range(1, CONV_W)}
        xc = _conv_taps(x, shifted, cw_ref[:, lanes], cb_ref[:, lanes])
        for q in range(2):
            n = 2 * p + q
            sub = slice(q * LRU_BLOCK, (q + 1) * LRU_BLOCK)
            a, gated, mult = _lru_gates(xc[:, sub], n, n * LRU_BLOCK, wg_ref, ba_ref, bx_ref, lam_ref)
            b = gated * mult
            a_ref[n] = a
            b_ref[n] = b
            b_ref[n, 0:1, :] = jnp.where(c == 0, gated[0:1, :], b[0:1, :])
        return x[tc - SUBLANES:, :]

    def gelu_gate(p, gate):
        for q in range(2):
            gate_ref[2 * p + q] = jax.nn.gelu(gate[:, q * LRU_BLOCK:(q + 1) * LRU_BLOCK])

    n_z = Z_XR // pair
    issue = _projection_issue_order(n_pairs, n_z)
    consumers = [((gates, gelu_gate, store_z)[br], br, p) for br, p in _projection_consume_order(n_pairs, n_z)]
    projected, tails = {}, []
    for consume, br, p in consumers:
        while issue and (len(projected) < REC_PROJECTIONS_AHEAD or (br, p) not in projected):
            key = issue.pop(0)
            projected[key] = project(*key)
        out = consume(p, projected.pop((br, p)))
        if consume is gates:
            tails.append(out)
    tail = jnp.concatenate(tails, axis=1)
    xprev_ref[...] = tail
    tail_ref[...] = tail

    seg = tc // SUBLANES
    seg_rows = lambda r: pl.ds(r, SUBLANES, stride=seg)

    def local_scan(r, maps):
        out = []
        for n in range(N_LRU_BLOCKS):
            a = a_ref[n, seg_rows(r), :]
            a_cum = a * maps[2 * n]
            b_cum = a * maps[2 * n + 1] + b_ref[n, seg_rows(r), :]
            a_ref[n, seg_rows(r), :] = a_cum
            b_ref[n, seg_rows(r), :] = b_cum
            out += [a_cum, b_cum]
        return tuple(out)
    identity = (jnp.ones((SUBLANES, LRU_BLOCK), f32), jnp.zeros((SUBLANES, LRU_BLOCK), f32)) * N_LRU_BLOCKS
    seg_maps = lax.fori_loop(0, seg, local_scan, identity, unroll=True)

    row = lax.broadcasted_iota(jnp.int32, (SUBLANES, LRU_BLOCK), 0)
    h_in = []
    for n in range(N_LRU_BLOCKS):
        h_prev = carry_ref[n]
        a_cum, b_cum = _tile_prefix(seg_maps[2 * n], seg_maps[2 * n + 1], row)
        h_end = a_cum * h_prev + b_cum
        h_in.append(jnp.where(row == 0, h_prev, pltpu.roll(h_end, 1, axis=0)))
        carry_ref[n] = _last_row(h_end)

    def apply_scan(r, _):
        for n in range(N_LRU_BLOCKS):
            h = a_ref[n, seg_rows(r), :] * h_in[n] + b_ref[n, seg_rows(r), :]
            b_ref[n, seg_rows(r), :] = h * gate_ref[n, seg_rows(r), :]
        return 0
    lax.fori_loop(0, seg, apply_scan, 0, unroll=True)

    y = jnp.concatenate([b_ref[n] for n in range(N_LRU_BLOCKS)], axis=1)
    o_ref[...] = _rmsnorm(y, g_ref[...]).astype(o_ref.dtype)

    @pl.when(c == n_chunks - 1)
    def _():
        lru_ref[...] = jnp.concatenate([carry_ref[n] for n in range(N_LRU_BLOCKS)], axis=1)


def _rec_prompt(h, g_mix, w_in, prev8, h0, cw, cb, wg, ba, bx, lam, g, l, *, n_batch, t_len, tc):
    n_chunks = t_len // tc
    once = pl.Buffered(1)
    row_vec = pl.BlockSpec((None, 1, LRU_W), lambda b, c: (l, 0, 0), pipeline_mode=once)
    w_half = lambda col: pl.BlockSpec((D_MODEL, LRU_HALF), lambda b, c: (0, col // LRU_HALF), pipeline_mode=once)
    state = pl.BlockSpec((None, SUBLANES, LRU_W), lambda b, c: (b, 0, 0))
    rows = lambda width: pl.BlockSpec((tc, width), lambda b, c: (b * n_chunks + c, 0))
    blocks = tc * D_MODEL * 4 + tc * Z_XR * 4 + tc * LRU_W * 2 + 3 * SUBLANES * LRU_W * 4
    scratch = (D_MODEL * IN_W * 2 + wg[0].size * 2 + tc * D_MODEL * 2 + (3 * tc + 2 * SUBLANES) * LRU_W * 4
               + 8 * tc * 2 * LRU_BLOCK * 4)
    by_block = pltpu.VMEM((N_LRU_BLOCKS, tc, LRU_BLOCK), f32)
    assert tc % SUBLANES == 0
    return pl.pallas_call(
        functools.partial(_rec_prompt_body, tc=tc, n_chunks=n_chunks),
        grid=(n_batch, n_chunks),
        in_specs=[rows(D_MODEL),
                  pl.BlockSpec((None, 1, D_MODEL), lambda b, c: (l, 0, 0), pipeline_mode=once),
                  pl.BlockSpec((D_MODEL, Z_XR), lambda b, c: (0, 0), pipeline_mode=once),
                  w_half(Z_XR), w_half(Z_XR + LRU_HALF), w_half(Z_GR), w_half(Z_GR + LRU_HALF),
                  state,
                  pl.BlockSpec((None, 1, LRU_W), lambda b, c: (b, 0, 0)),
                  pl.BlockSpec((None, CONV_W, LRU_W), lambda b, c: (l, 0, 0), pipeline_mode=once),
                  row_vec,
                  pl.BlockSpec((None,) + wg.shape[1:], lambda b, c: (l, 0, 0, 0), pipeline_mode=once),
                  row_vec, row_vec, row_vec, row_vec],
        out_specs=[rows(Z_XR), rows(LRU_W), state, state],
        out_shape=[jax.ShapeDtypeStruct((n_batch * t_len, Z_XR), f32),
                   jax.ShapeDtypeStruct((n_batch * t_len, LRU_W), bf16),
                   jax.ShapeDtypeStruct((n_batch, SUBLANES, LRU_W), f32),
                   jax.ShapeDtypeStruct((n_batch, SUBLANES, LRU_W), f32)],
        scratch_shapes=[pltpu.VMEM((tc, D_MODEL), bf16),
                        pltpu.VMEM((SUBLANES, LRU_W), f32), by_block, by_block, by_block,
                        pltpu.VMEM((N_LRU_BLOCKS, SUBLANES, LRU_BLOCK), f32)],
        compiler_params=_params(("parallel", "arbitrary"), _vmem_limit(blocks, scratch)),
        name="rec_prompt",
    )(h, g_mix, w_in, w_in, w_in, w_in, w_in, prev8, h0, cw, cb, wg, ba, bx, lam, g)


def _rec_sample_body(z_ref, prev_ref, h0_ref, cw_ref, cb_ref, wg_ref, ba_ref, bx_ref, lam_ref, g_ref,
                     o_ref, lru_ref, xc_ref, a_ref, b_ref, *, n_batch):
    row = lax.broadcasted_iota(jnp.int32, (SUBLANES, LRU_W), 0)

    def conv_step(b, _):
        r = pl.multiple_of(b * SUBLANES, SUBLANES)
        x = z_ref[pl.ds(r, SUBLANES), Z_XR:Z_XR + LRU_W]
        prev = prev_ref[pl.ds(r, SUBLANES), :]
        shifted = {k: jnp.where(row >= k, pltpu.roll(x, k, axis=0), pltpu.roll(prev, k, axis=0))
                   for k in range(1, CONV_W)}
        xc_ref[pl.ds(r, SUBLANES), :] = _conv_taps(x, shifted, cw_ref[...], cb_ref[...])
        return 0
    lax.fori_loop(0, n_batch, conv_step, 0)

    def gates_step(n, _):
        c0 = pl.multiple_of(n * LRU_BLOCK, LRU_BLOCK)
        lanes = pl.ds(c0, LRU_BLOCK)
        a, gated, mult = _lru_gates(xc_ref[:, lanes], n, c0, wg_ref, ba_ref, bx_ref, lam_ref)
        a_ref[:, lanes] = a
        b_ref[:, lanes] = gated * mult
        return 0
    lax.fori_loop(0, N_LRU_BLOCKS, gates_step, 0)

    def scan_step(b, _):
        rows = pl.ds(pl.multiple_of(b * SUBLANES, SUBLANES), SUBLANES)
        h_in = jnp.broadcast_to(h0_ref[pl.ds(b, 1), :], (SUBLANES, LRU_W))
        hh = _scan_tile(a_ref[rows, :], b_ref[rows, :], h_in, row)
        b_ref[rows, :] = hh * jax.nn.gelu(z_ref[rows, Z_GR:Z_GR + LRU_W])
        lru_ref[pl.ds(b, 1), :] = hh[SUBLANES - 1:SUBLANES, :]
        return 0
    lax.fori_loop(0, n_batch, scan_step, 0)
    o_ref[...] = _rmsnorm(b_ref[...], g_ref[...]).astype(o_ref.dtype)


def _rec_sample(z, prev8, h0, cw, cb, wg, ba, bx, lam, g, l, *, n_batch, t_len):
    assert t_len == SUBLANES and PAST_LEN > 0
    rows = n_batch * t_len
    layer = lambda a: pl.BlockSpec((None,) + a.shape[1:], lambda i: (l,) + (0,) * (a.ndim - 1))
    blocks = z.size * 4 + (2 * rows + 2 * n_batch) * LRU_W * 4 + wg[0].size * 2
    return pl.pallas_call(
        functools.partial(_rec_sample_body, n_batch=n_batch),
        grid=(1,),
        in_specs=[pl.BlockSpec(z.shape, lambda i: (0, 0))] + [layer(a) for a in (prev8, h0, cw, cb, wg, ba, bx, lam, g)],
        out_specs=[pl.BlockSpec((rows, LRU_W), lambda i: (0, 0)), pl.BlockSpec((n_batch, LRU_W), lambda i: (0, 0))],
        out_shape=[jax.ShapeDtypeStruct((rows, LRU_W), bf16),
                   jax.ShapeDtypeStruct((n_batch, LRU_W), f32)],
        scratch_shapes=[pltpu.VMEM((rows, LRU_W), f32)] * 3,
        compiler_params=_params(("arbitrary",), _vmem_limit(blocks, 3 * rows * LRU_W * 4)),
        name="rec_sample",
    )(z, prev8, h0, cw, cb, wg, ba, bx, lam, g)


def _mix_mlp_body(h_ref, attn_ref, rec_ref, wo_ref, g_ref, wu_ref, wd_ref, o_ref, u_ref):
    @pl.when(pl.program_id(1) == 0)
    def _():
        o_ref[...] = h_ref[...] + jnp.dot(attn_ref[...], wo_ref[0:ATTN_W, :], preferred_element_type=f32)
        o_ref[...] += jnp.dot(rec_ref[...], wo_ref[ATTN_W:ATTN_W + LRU_W, :], preferred_element_type=f32)
        u_ref[...] = _rmsnorm(o_ref[...], g_ref[...]).astype(bf16)
    hid = jnp.dot(u_ref[...], wu_ref[...], preferred_element_type=f32)
    hid = jnp.square(jnp.maximum(hid, 0.0)).astype(bf16)
    o_ref[...] += jnp.dot(hid, wd_ref[...], preferred_element_type=f32)


def _mix_mlp(h, attn, rec, wo, g, wu, wd, l, *, tm, tf):
    m = h.shape[0]
    once = pl.Buffered(1)
    blocks = 2 * tm * D_MODEL * 4 + tm * (ATTN_W + LRU_W) * 2 + 2 * D_MODEL * tf * 2
    scratch = wo[0].size * 2 + tm * D_MODEL * 2 + tm * tf * 6
    return pl.pallas_call(
        _mix_mlp_body,
        grid=(m // tm, D_FF // tf),
        in_specs=[pl.BlockSpec((tm, D_MODEL), lambda i, f: (i, 0)),
                  pl.BlockSpec((tm, ATTN_W), lambda i, f: (i, 0)),
                  pl.BlockSpec((tm, LRU_W), lambda i, f: (i, 0)),
                  pl.BlockSpec((None,) + wo.shape[1:], lambda i, f: (l, 0, 0), pipeline_mode=once),
                  pl.BlockSpec((None, 1, D_MODEL), lambda i, f: (l, 0, 0), pipeline_mode=once),
                  pl.BlockSpec((None, D_MODEL, tf), lambda i, f: (l, 0, f)),
                  pl.BlockSpec((None, tf, D_MODEL), lambda i, f: (l, f, 0))],
        out_specs=pl.BlockSpec((tm, D_MODEL), lambda i, f: (i, 0)),
        out_shape=jax.ShapeDtypeStruct((m, D_MODEL), f32),
        scratch_shapes=[pltpu.VMEM((tm, D_MODEL), bf16)],
        compiler_params=_params(("parallel", "arbitrary"), _vmem_limit(blocks, scratch)),
        name="mix_mlp",
    )(h, attn, rec, wo, g, wu, wd)


def _mix_mlp_cast_body(h_ref, attn_ref, rec_ref, wo_ref, g_ref, wu_ref, wd_ref,
                       o_ref, wob_ref, wub_ref, wdb_ref, u_ref):
    @pl.when(pl.program_id(1) == 0)
    def _():
        wob_ref[...] = wo_ref[...].astype(bf16)
        o_ref[...] = h_ref[...] + jnp.dot(attn_ref[...], wob_ref[0:ATTN_W, :], preferred_element_type=f32)
        o_ref[...] += jnp.dot(rec_ref[...], wob_ref[ATTN_W:ATTN_W + LRU_W, :], preferred_element_type=f32)
        u_ref[...] = _rmsnorm(o_ref[...], g_ref[...]).astype(bf16)
    wub_ref[...] = wu_ref[...].astype(bf16)
    wdb_ref[...] = wd_ref[...].astype(bf16)
    hid = jnp.dot(u_ref[...], wub_ref[...], preferred_element_type=f32)
    hid = jnp.square(jnp.maximum(hid, 0.0)).astype(bf16)
    o_ref[...] += jnp.dot(hid, wdb_ref[...], preferred_element_type=f32)


def _mix_mlp_cast(h, attn, rec, wo, g, wu, wd, l, *, tf):
    m = h.shape[0]
    once = pl.Buffered(1)
    blocks = 2 * D_MODEL * tf * (4 + 2)
    scratch = (2 * m * D_MODEL * 4 + m * (ATTN_W + LRU_W) * 2 + wo[0].size * (4 + 2) + m * D_MODEL * 2
               + m * tf * 6)
    return pl.pallas_call(
        _mix_mlp_cast_body,
        grid=(1, D_FF // tf),
        in_specs=[pl.BlockSpec((m, D_MODEL), lambda i, f: (0, 0), pipeline_mode=once),
                  pl.BlockSpec((m, ATTN_W), lambda i, f: (0, 0), pipeline_mode=once),
                  pl.BlockSpec((m, LRU_W), lambda i, f: (0, 0), pipeline_mode=once),
                  pl.BlockSpec((None,) + wo.shape[1:], lambda i, f: (l, 0, 0), pipeline_mode=once),
                  pl.BlockSpec((None, 1, D_MODEL), lambda i, f: (l, 0, 0), pipeline_mode=once),
                  pl.BlockSpec((None, D_MODEL, tf), lambda i, f: (l, 0, f)),
                  pl.BlockSpec((None, tf, D_MODEL), lambda i, f: (l, f, 0))],
        out_specs=[pl.BlockSpec((m, D_MODEL), lambda i, f: (0, 0)),
                   pl.BlockSpec(wo.shape[1:], lambda i, f: (0, 0)),
                   pl.BlockSpec((D_MODEL, tf), lambda i, f: (0, f)),
                   pl.BlockSpec((tf, D_MODEL), lambda i, f: (f, 0))],
        out_shape=[jax.ShapeDtypeStruct((m, D_MODEL), f32),
                   jax.ShapeDtypeStruct(wo.shape[1:], bf16),
                   jax.ShapeDtypeStruct(wu.shape[1:], bf16),
                   jax.ShapeDtypeStruct(wd.shape[1:], bf16)],
        scratch_shapes=[pltpu.VMEM((m, D_MODEL), bf16)],
        compiler_params=_params(("arbitrary", "arbitrary"), _vmem_limit(blocks, scratch)),
        name="mix_mlp_cast",
    )(h, attn, rec, wo, g, wu, wd)


def _final_norm_body(h_ref, g_ref, o_ref):
    o_ref[0] = _rmsnorm(h_ref[0], g_ref[...])


def _final_norm(h, g, *, skip, tr):
    n_batch, t_len, _ = h.shape
    s_len = t_len - skip
    tiles = tr // SUBLANES
    skip_tiles = skip // SUBLANES
    h4 = h.reshape(n_batch, t_len // SUBLANES, SUBLANES, D_MODEL)
    out = pl.pallas_call(
        _final_norm_body,
        grid=(n_batch, s_len // tr),
        in_specs=[pl.BlockSpec((pl.Element(1), pl.Element(tiles), pl.Element(SUBLANES), pl.Element(D_MODEL)),
                               lambda b, r: (b, skip_tiles + r * tiles, 0, 0)),
                  pl.BlockSpec((1, D_MODEL), lambda b, r: (0, 0))],
        out_specs=pl.BlockSpec((1, tiles, SUBLANES, D_MODEL), lambda b, r: (b, r, 0, 0)),
        out_shape=jax.ShapeDtypeStruct((n_batch, s_len // SUBLANES, SUBLANES, D_MODEL), f32),
        compiler_params=_params(("parallel", "parallel"), _vmem_limit(2 * tr * D_MODEL * 4)),
        name="final_norm",
    )(h4, g)
    return out.reshape(n_batch, s_len, D_MODEL)


def _row_tile(m, candidates):
    for tm in candidates:
        if m % tm == 0:
            return tm
    raise ValueError(f"no row tile for {m} rows")


def kernel(x_prompt, x_sample, cache_k_win, cache_v_win, state_conv, state_lru, meta_tokens, norm_mix_g, w_in,
           conv_w, conv_b, w_gate_a, b_gate_a, w_gate_x, b_gate_x, lru_lambda, attn_sinks, rel_bias, attn_out_g,
           rec_out_g, w_out, norm_mlp_g, w_up, w_down, final_norm_g):
    n_p, s_p, _ = x_prompt.shape
    n_s, t_s, _ = x_sample.shape
    t_p = N_META + s_p
    buf = cache_k_win.shape[2]
    assert t_p % BF16_ROWS == 0 and buf == WINDOW

    w_gates = jnp.concatenate([w_gate_a, w_gate_x], axis=-1).astype(bf16)
    rows3 = lambda p: p[:, None, :]
    g_mix, g_mlp, g_attn, g_rec = rows3(norm_mix_g), rows3(norm_mlp_g), rows3(attn_out_g), rows3(rec_out_g)
    cb3, ba3, bx3, lam3 = rows3(conv_b), rows3(b_gate_a), rows3(b_gate_x), rows3(lru_lambda)
    g_attn_t = jnp.broadcast_to(
        attn_out_g.reshape(DEPTH, N_Q_HEADS, HEAD_DIM).transpose(0, 2, 1)[..., None],
        (DEPTH, HEAD_DIM, N_Q_HEADS, WINDOW)).reshape(DEPTH, HEAD_DIM, ATTN_W)

    meta = jnp.broadcast_to(meta_tokens.astype(x_prompt.dtype)[None], (n_p, N_META, D_MODEL))
    hp = jnp.concatenate([meta, x_prompt], axis=1).reshape(n_p * t_p, D_MODEL)
    hs = x_sample.reshape(n_s * t_s, D_MODEL)

    bias_qk, bias_kq = _bias_tables(rel_bias)
    bias_qk = bias_qk.reshape(N_Q_HEADS * WINDOW, 2 * WINDOW)
    prev_p = jnp.zeros((n_p, SUBLANES, LRU_W), f32)
    h0_p = jnp.zeros((n_p, 1, LRU_W), f32)
    ck = cache_k_win.reshape(DEPTH, n_s, buf * N_KV_HEADS, HEAD_DIM)
    cv = cache_v_win.reshape(DEPTH, n_s, buf * N_KV_HEADS, HEAD_DIM)
    prev_s = jnp.pad(state_conv, ((0, 0), (0, 0), (SUBLANES - (CONV_W - 1), 0), (0, 0)))
    prev_s = prev_s.reshape(DEPTH, n_s * SUBLANES, LRU_W)

    tm_p = _row_tile(n_p * t_p, (688, 344))
    tc_p = _row_tile(t_p, (688, 344, 48, 16))
    tn = IN_W // 2

    kp_l, vp_l, cp_l, lp_l, ks_l, vs_l, cs_l, ls_l = ([] for _ in range(8))
    for l in range(DEPTH):
        z, w_in_b = _inproj_cast(hs, g_mix, w_in, l, tn=tn)
        attn, nk, nv = _attn_sample(z, ck, cv, bias_qk, attn_sinks, g_attn, l, n_batch=n_s, t_len=t_s)
        rec, lru = _rec_sample(z, prev_s, state_lru, conv_w, cb3, w_gates, ba3, bx3, lam3, g_rec, l,
                               n_batch=n_s, t_len=t_s)
        hs, w_out_b, w_up_b, w_down_b = _mix_mlp_cast(hs, attn, rec, w_out, g_mlp, w_up, w_down, l, tf=512)
        z3 = z.reshape(n_s, t_s, IN_W)
        ks_l.append(nk)
        vs_l.append(nv)
        cs_l.append(z3[:, t_s - (CONV_W - 1):, Z_XR:Z_XR + LRU_W])
        ls_l.append(lru)
        z, rec, lru, x_tail = _rec_prompt(hp, g_mix, w_in_b, prev_p, h0_p, conv_w, cb3, w_gates, ba3, bx3, lam3,
                                          g_rec, l, n_batch=n_p, t_len=t_p, tc=tc_p)
        attn = _attn_prompt(z, bias_kq, attn_sinks, g_attn_t, l, n_batch=n_p, t_len=t_p)
        hp = _mix_mlp(hp, attn, rec, w_out_b[None], g_mlp[l:l + 1], w_up_b[None], w_down_b[None], 0,
                      tm=tm_p, tf=1024)
        z3 = z.reshape(n_p, t_p, Z_XR)
        kp_l.append(z3[:, t_p - WINDOW:, Z_K:Z_K + KV_W].reshape(n_p, WINDOW, N_KV_HEADS, HEAD_DIM))
        vp_l.append(z3[:, t_p - WINDOW:, Z_V:Z_V + KV_W].reshape(n_p, WINDOW, N_KV_HEADS, HEAD_DIM))
        cp_l.append(x_tail[:, SUBLANES - (CONV_W - 1):])
        lp_l.append(lru[:, 0])

    g_fin = final_norm_g[None, :]
    y_prompt = _final_norm(hp.reshape(n_p, t_p, D_MODEL), g_fin, skip=N_META, tr=512)
    y_sample = _final_norm(hs.reshape(1, n_s * t_s, D_MODEL), g_fin, skip=0, tr=n_s * t_s)
    y_sample = y_sample.reshape(n_s, t_s, D_MODEL)
    cache_shape = (DEPTH, n_s, buf, N_KV_HEADS, HEAD_DIM)
    return (y_prompt, y_sample,
            jnp.stack(kp_l), jnp.stack(vp_l), jnp.stack(cp_l), jnp.stack(lp_l),
            jnp.stack(ks_l).reshape(cache_shape), jnp.stack(vs_l).reshape(cache_shape),
            jnp.stack(cs_l), jnp.stack(ls_l))
```

```python
import functools
import math

import jax
import jax.numpy as jnp
from jax import lax
from jax.experimental import pallas as pl
from jax.experimental.pallas import tpu as pltpu

f32 = jnp.float32
bf16 = jnp.bfloat16

D_MODEL = 2048
DEPTH = 4
PAST_LEN = 16384
HEAD_DIM = 128
N_Q_HEADS = 8
N_KV_HEADS = 2
Q_PER_KV = N_Q_HEADS // N_KV_HEADS
ATTN_W = N_Q_HEADS * HEAD_DIM
KV_W = N_KV_HEADS * HEAD_DIM
LRU_W = D_MODEL // 2
N_LRU_BLOCKS = 8
LRU_BLOCK = LRU_W // N_LRU_BLOCKS
CONV_W = 4
LRU_C = 8.0
IN_W = ATTN_W + 2 * KV_W + 2 * LRU_W
D_FF = 4 * D_MODEL
WINDOW = 128
N_BUCKETS = 32
MAX_DISTANCE = 128
N_META = 16
EPS = 1e-6
ATTN_SCALE = HEAD_DIM ** -0.5
INV_ATTN_SCALE = HEAD_DIM ** 0.5
EXP2_PER_T = math.log2(math.e) / INV_ATTN_SCALE

Z_Q = 0
Z_K = ATTN_W
Z_V = Z_K + KV_W
Z_XR = Z_V + KV_W
Z_GR = Z_XR + LRU_W
LRU_HALF = LRU_W // 2

ATTN_LEAD = WINDOW + (-N_META) % WINDOW

SUBLANES = 8
BF16_ROWS = 16
V7X_VMEM_BYTES = 64 * 1024 * 1024
VMEM_CAP_BYTES = V7X_VMEM_BYTES - 2 * 1024 * 1024


def _vmem_limit(pipelined_bytes, scratch_bytes=0):
    est = 2 * pipelined_bytes + scratch_bytes
    return int(min(VMEM_CAP_BYTES, est + est // 2 + (8 << 20)))


def _params(semantics, vmem_bytes):
    return pltpu.CompilerParams(dimension_semantics=semantics, vmem_limit_bytes=vmem_bytes)


def _rms_scale(x):
    return lax.rsqrt(jnp.mean(x * x, axis=-1, keepdims=True) + EPS)


def _rmsnorm(x, g):
    return x * _rms_scale(x) * g


def _inproj_cast_body(h_ref, g_ref, w_ref, z_ref, wb_ref, u_ref):
    @pl.when(pl.program_id(0) == 0)
    def _():
        u_ref[...] = _rmsnorm(h_ref[...], g_ref[...]).astype(bf16)
    wb_ref[...] = w_ref[...].astype(bf16)
    z_ref[...] = jnp.dot(u_ref[...], wb_ref[...], preferred_element_type=f32)


def _inproj_cast(h, g, w, l, *, tn):
    m = h.shape[0]
    blocks = D_MODEL * tn * (4 + 2) + m * tn * 4
    return pl.pallas_call(
        _inproj_cast_body,
        grid=(IN_W // tn,),
        in_specs=[pl.BlockSpec((m, D_MODEL), lambda j: (0, 0)),
                  pl.BlockSpec((None, 1, D_MODEL), lambda j: (l, 0, 0)),
                  pl.BlockSpec((None, D_MODEL, tn), lambda j: (l, 0, j))],
        out_specs=[pl.BlockSpec((m, tn), lambda j: (0, j)),
                   pl.BlockSpec((D_MODEL, tn), lambda j: (0, j))],
        out_shape=[jax.ShapeDtypeStruct((m, IN_W), f32), jax.ShapeDtypeStruct(w.shape[1:], bf16)],
        scratch_shapes=[pltpu.VMEM((m, D_MODEL), bf16)],
        compiler_params=_params(("arbitrary",), _vmem_limit(blocks, m * D_MODEL * (2 * 4 + 2))),
        name="inproj_cast",
    )(h, g, w)


def _rel_bias_of(dist, rel_ref, h):
    n = jnp.maximum(dist, 0)
    max_exact = N_BUCKETS // 2
    nf = jnp.maximum(n, 1).astype(f32)
    large = max_exact + (jnp.log(nf / max_exact) / math.log(MAX_DISTANCE / max_exact)
                         * (N_BUCKETS - max_exact)).astype(jnp.int32)
    large = jnp.minimum(large, N_BUCKETS - 1)
    bucket = jnp.where(n < max_exact, n, large)
    acc = jnp.zeros(dist.shape, f32)
    for b in range(N_BUCKETS):
        acc = jnp.where(bucket == b, rel_ref[b, h], acc)
    return jnp.where((dist >= 0) & (dist < WINDOW), acc, -jnp.inf)


def _bias_table_body(rel_ref, qk_ref, kq_ref):
    shape_qk = (WINDOW, 2 * WINDOW)
    dist_qk = (lax.broadcasted_iota(jnp.int32, shape_qk, 0) + WINDOW
               - lax.broadcasted_iota(jnp.int32, shape_qk, 1))
    shape_kq = (2 * WINDOW, WINDOW)
    dist_kq = (lax.broadcasted_iota(jnp.int32, shape_kq, 1) + WINDOW
               - lax.broadcasted_iota(jnp.int32, shape_kq, 0))
    for h in range(N_Q_HEADS):
        kv, g = divmod(h, Q_PER_KV)
        qk_ref[h] = _rel_bias_of(dist_qk, rel_ref, h)
        kq_ref[kv, :, g * WINDOW:(g + 1) * WINDOW] = _rel_bias_of(dist_kq, rel_ref, h) * INV_ATTN_SCALE


def _bias_tables(rel_bias):
    return pl.pallas_call(
        _bias_table_body,
        in_specs=[pl.BlockSpec(memory_space=pltpu.SMEM)],
        out_shape=[jax.ShapeDtypeStruct((N_Q_HEADS, WINDOW, 2 * WINDOW), f32),
                   jax.ShapeDtypeStruct((N_KV_HEADS, 2 * WINDOW, Q_PER_KV * WINDOW), f32)],
        name="bias_tables",
    )(rel_bias)


def _attn_blocks_t(blocks, kpad_ref, vt_ref, bias_ref, sinks_ref, l, gt_ref):
    kv_heads = [range(kv * Q_PER_KV, (kv + 1) * Q_PER_KV) for kv in range(N_KV_HEADS)]
    sink = [jnp.concatenate([jnp.full((1, WINDOW), sinks_ref[l, h] * INV_ATTN_SCALE, f32) for h in heads], axis=1)
            for heads in kv_heads]
    chains = [(i, kv) for i in range(len(blocks)) for kv in range(N_KV_HEADS)]
    scores = {}
    for i, kv in chains:
        q_rows, r0, _ = blocks[i]
        kwin = kpad_ref[pl.ds(r0, 2 * WINDOW), kv * HEAD_DIM:(kv + 1) * HEAD_DIM]
        q = jnp.concatenate([q_rows[:, h * HEAD_DIM:(h + 1) * HEAD_DIM] for h in kv_heads[kv]],
                            axis=0).astype(bf16)
        scores[i, kv] = lax.dot_general(kwin, q, (((1,), (1,)), ((), ())), preferred_element_type=f32)
    probs, invs = {}, {}
    for i, kv in chains:
        t = scores[i, kv] + bias_ref[kv]
        lead_keys = blocks[i][2]
        if lead_keys:
            key = lax.broadcasted_iota(jnp.int32, t.shape, 0)
            t = jnp.where(key < lead_keys, -jnp.inf, t)
        m = jnp.maximum(jnp.max(t, axis=0, keepdims=True), sink[kv])
        p = jnp.exp2((t - m) * EXP2_PER_T)
        invs[i, kv] = 1.0 / (jnp.sum(p, axis=0, keepdims=True) + jnp.exp2((sink[kv] - m) * EXP2_PER_T))
        probs[i, kv] = p.astype(bf16)
    raws = {}
    for i, kv in chains:
        vt = vt_ref[kv * HEAD_DIM:(kv + 1) * HEAD_DIM, pl.ds(blocks[i][1], 2 * WINDOW)]
        raws[i, kv] = jnp.dot(vt, probs[i, kv], preferred_element_type=f32)
    outs = []
    for i in range(len(blocks)):
        raw = jnp.concatenate([raws[i, kv] for kv in range(N_KV_HEADS)], axis=1)
        inv = jnp.concatenate([invs[i, kv] for kv in range(N_KV_HEADS)], axis=1)
        sq = jnp.sum(raw * raw, axis=0, keepdims=True) * (inv * inv)
        ssq = sq[:, 0:WINDOW]
        for h in range(1, N_Q_HEADS):
            ssq = ssq + sq[:, h * WINDOW:(h + 1) * WINDOW]
        r = lax.rsqrt(ssq * (1.0 / ATTN_W) + EPS)
        yt = raw * (inv * jnp.concatenate([r] * N_Q_HEADS, axis=1)) * gt_ref[...]
        outs.append(jnp.concatenate([yt[:, h * WINDOW:(h + 1) * WINDOW].T for h in range(N_Q_HEADS)], axis=1))
    return outs


ATTN_GROUP = 5


def _attn_prompt_body(q_ref, k_ref, v_ref, bias_ref, sinks_ref, gt_ref, o_ref, kpad_ref, vpad_ref, vt_ref,
                      *, t_len, l):
    pad_rows = kpad_ref.shape[0]
    kpad_ref[0:ATTN_LEAD, :] = jnp.zeros((ATTN_LEAD, KV_W), bf16)
    kpad_ref[ATTN_LEAD:pad_rows, :] = k_ref[...].astype(bf16)
    vpad_ref[0:ATTN_LEAD, :] = jnp.zeros((ATTN_LEAD, KV_W), f32)
    vpad_ref[ATTN_LEAD:pad_rows, :] = v_ref[...]

    def transpose_step(c, _):
        r = pl.multiple_of(c * WINDOW, WINDOW)
        chunk = vpad_ref[pl.ds(r, WINDOW), :]
        for kv in range(N_KV_HEADS):
            vt_ref[kv * HEAD_DIM:(kv + 1) * HEAD_DIM, pl.ds(r, WINDOW)] = (
                chunk[:, kv * HEAD_DIM:(kv + 1) * HEAD_DIM].T.astype(bf16))
        return 0
    lax.fori_loop(0, pad_rows // WINDOW, transpose_step, 0, unroll=True)

    def q_rows_of(j):
        start = j * WINDOW - (ATTN_LEAD - WINDOW)
        return pl.ds(start if isinstance(j, int) else pl.multiple_of(start, BF16_ROWS), WINDOW)

    n_first = 2 * WINDOW - ATTN_LEAD
    q0 = jnp.concatenate([jnp.zeros((WINDOW - n_first, ATTN_W), f32), q_ref[0:n_first, :]], axis=0)
    y0, y1 = _attn_blocks_t([(q0, 0, ATTN_LEAD), (q_ref[q_rows_of(1), :], WINDOW, ATTN_LEAD - WINDOW)],
                            kpad_ref, vt_ref, bias_ref, sinks_ref, l, gt_ref)
    o_ref[0:n_first, :] = y0[WINDOW - n_first:, :].astype(o_ref.dtype)
    o_ref[q_rows_of(1), :] = y1.astype(o_ref.dtype)

    def step(i, _):
        js = [2 + i * ATTN_GROUP + g for g in range(ATTN_GROUP)]
        ys = _attn_blocks_t([(q_ref[q_rows_of(j), :], pl.multiple_of(j * WINDOW, WINDOW), 0) for j in js],
                            kpad_ref, vt_ref, bias_ref, sinks_ref, l, gt_ref)
        for j, y in zip(js, ys):
            o_ref[q_rows_of(j), :] = y.astype(o_ref.dtype)
        return 0
    n_blocks = pad_rows // WINDOW - 1
    assert (n_blocks - 2) % ATTN_GROUP == 0
    lax.fori_loop(0, (n_blocks - 2) // ATTN_GROUP, step, 0, unroll=True)


def _attn_prompt(z, bias_kq, sinks, gt, l, *, n_batch, t_len):
    pad_rows = ATTN_LEAD + t_len
    assert pad_rows % WINDOW == 0 and (ATTN_LEAD - WINDOW) % BF16_ROWS == 0
    blocks = t_len * (ATTN_W + 2 * KV_W) * 4 + t_len * ATTN_W * 2 + bias_kq.size * 4 + HEAD_DIM * ATTN_W * 4
    scratch = pad_rows * KV_W * (2 + 4 + 2)
    return pl.pallas_call(
        functools.partial(_attn_prompt_body, t_len=t_len, l=l),
        grid=(n_batch,),
        in_specs=[pl.BlockSpec((t_len, ATTN_W), lambda b: (b, Z_Q // ATTN_W)),
                  pl.BlockSpec((t_len, KV_W), lambda b: (b, Z_K // KV_W)),
                  pl.BlockSpec((t_len, KV_W), lambda b: (b, Z_V // KV_W)),
                  pl.BlockSpec(bias_kq.shape, lambda b: (0, 0, 0)),
                  pl.BlockSpec(memory_space=pltpu.SMEM),
                  pl.BlockSpec((None, HEAD_DIM, ATTN_W), lambda b: (l, 0, 0))],
        out_specs=pl.BlockSpec((t_len, ATTN_W), lambda b: (b, 0)),
        out_shape=jax.ShapeDtypeStruct((n_batch * t_len, ATTN_W), bf16),
        scratch_shapes=[pltpu.VMEM((pad_rows, KV_W), bf16), pltpu.VMEM((pad_rows, KV_W), f32),
                        pltpu.VMEM((KV_W, pad_rows), bf16)],
        compiler_params=_params(("parallel",), _vmem_limit(blocks, scratch)),
        name="attn_prompt",
    )(z, z, z, bias_kq, sinks, gt)


DECODE_GROUP = 8


def _attn_sample_body(z_ref, ck_ref, cv_ref, bias_ref, sinks_ref, g_ref, o_ref, nk_ref, nv_ref, acc_ref,
                      *, n_batch, t_len, buf, l):
    zeros = jnp.zeros((2 * WINDOW - buf - t_len, HEAD_DIM), f32)
    kept = N_KV_HEADS * (buf - t_len)
    kv_heads = [range(kv * Q_PER_KV, (kv + 1) * Q_PER_KV) for kv in range(N_KV_HEADS)]
    bias = [jnp.concatenate([bias_ref[h * WINDOW:h * WINDOW + t_len, :] for h in heads], axis=0)
            for heads in kv_heads]
    sink = [jnp.concatenate([jnp.full((t_len, 1), sinks_ref[l, h], f32) for h in heads], axis=0)
            for heads in kv_heads]

    def window(c_ref, n_ref, b, r0, col, kv):
        new = z_ref[pl.ds(r0, t_len), col + kv * HEAD_DIM:col + (kv + 1) * HEAD_DIM]
        n_ref[b, pl.ds(kept + kv, t_len, stride=N_KV_HEADS), :] = new
        old = c_ref[b, pl.ds(kv, buf, stride=N_KV_HEADS), :]
        return jnp.concatenate([old, new, zeros], axis=0).astype(bf16)

    def group_step(i, _):
        seqs = [i * DECODE_GROUP + j for j in range(DECODE_GROUP)]
        rows = [pl.multiple_of(b * t_len, t_len) for b in seqs]
        scores = {}
        for j, (b, r0) in enumerate(zip(seqs, rows)):
            nk_ref[b, 0:kept, :] = ck_ref[b, N_KV_HEADS * t_len:N_KV_HEADS * buf, :]
            nv_ref[b, 0:kept, :] = cv_ref[b, N_KV_HEADS * t_len:N_KV_HEADS * buf, :]
            for kv, heads in enumerate(kv_heads):
                q = jnp.concatenate([z_ref[pl.ds(r0, t_len), Z_Q + h * HEAD_DIM:Z_Q + (h + 1) * HEAD_DIM]
                                     for h in heads], axis=0).astype(bf16)
                k = window(ck_ref, nk_ref, b, r0, Z_K, kv)
                scores[j, kv] = lax.dot_general(q, k, (((1,), (1,)), ((), ())), preferred_element_type=f32)
        probs, denoms = {}, {}
        for key, s in scores.items():
            kv = key[1]
            s = s * ATTN_SCALE + bias[kv]
            m = jnp.maximum(jnp.max(s, axis=-1, keepdims=True), sink[kv])
            p = jnp.exp(s - m)
            denoms[key] = jnp.sum(p, axis=-1, keepdims=True) + jnp.exp(sink[kv] - m)
            probs[key] = p.astype(bf16)
        for j, (b, r0) in enumerate(zip(seqs, rows)):
            outs = []
            for kv in range(N_KV_HEADS):
                v = window(cv_ref, nv_ref, b, r0, Z_V, kv)
                o = jnp.dot(probs[j, kv], v, preferred_element_type=f32) / denoms[j, kv]
                outs.extend(o[g * t_len:(g + 1) * t_len] for g in range(Q_PER_KV))
            acc_ref[pl.ds(r0, t_len), :] = jnp.concatenate(outs, axis=1)
        return 0
    lax.fori_loop(0, n_batch // DECODE_GROUP, group_step, 0)
    o_ref[...] = _rmsnorm(acc_ref[...], g_ref[...]).astype(o_ref.dtype)


def _attn_sample(z, ck, cv, bias_qk, sinks, g, l, *, n_batch, t_len):
    buf = ck.shape[2] // N_KV_HEADS
    assert buf == WINDOW and t_len == SUBLANES and n_batch % DECODE_GROUP == 0
    rows = n_batch * t_len
    whole =lambda a: pl.BlockSpec(a.shape, lambda i: (0,) * a.ndim)
    cache = pl.BlockSpec((None,) + ck.shape[1:], lambda i: (l, 0, 0, 0))
    blocks = (z.size + 4 * ck[0].size + bias_qk.size) * 4 + rows * ATTN_W * 2
    return pl.pallas_call(
        functools.partial(_attn_sample_body, n_batch=n_batch, t_len=t_len, buf=buf, l=l),
        grid=(1,),
        in_specs=[whole(z), cache, cache, whole(bias_qk), pl.BlockSpec(memory_space=pltpu.SMEM),
                  pl.BlockSpec((None, 1, ATTN_W), lambda i: (l, 0, 0))],
        out_specs=[pl.BlockSpec((rows, ATTN_W), lambda i: (0, 0)),
                   pl.BlockSpec(ck.shape[1:], lambda i: (0, 0, 0)),
                   pl.BlockSpec(cv.shape[1:], lambda i: (0, 0, 0))],
        out_shape=[jax.ShapeDtypeStruct((rows, ATTN_W), bf16),
                   jax.ShapeDtypeStruct(ck.shape[1:], f32),
                   jax.ShapeDtypeStruct(cv.shape[1:], f32)],
        scratch_shapes=[pltpu.VMEM((rows, ATTN_W), f32)],
        compiler_params=_params(("arbitrary",), _vmem_limit(blocks, rows * ATTN_W * 4)),
        name="attn_sample",
    )(z, ck, cv, bias_qk, sinks, g)


def _lru_gates(xc, n, c0, wg_ref, ba_ref, bx_ref, lam_ref):
    lanes = pl.ds(c0, LRU_BLOCK)
    gates = jnp.dot(xc.astype(bf16), wg_ref[n], preferred_element_type=f32)
    gate_a = jax.nn.sigmoid(gates[:, :LRU_BLOCK] + ba_ref[:, lanes])
    gate_x = jax.nn.sigmoid(gates[:, LRU_BLOCK:] + bx_ref[:, lanes])
    log_a = -LRU_C * gate_a * jax.nn.softplus(-lam_ref[:, lanes])
    a = jnp.exp(log_a)
    y = -jnp.tanh(log_a) * (1.0 + a * a)
    mult = jnp.where(y > 0.0, y * lax.rsqrt(y), 0.0)
    return a, xc * gate_x, mult


def _tile_prefix(a, b, row):
    for d in (1, 2, 4):
        a_prev = pltpu.roll(a, d, axis=0)
        b_prev = pltpu.roll(b, d, axis=0)
        keep = row >= d
        b = jnp.where(keep, a * b_prev + b, b)
        a = jnp.where(keep, a * a_prev, a)
    return a, b


def _scan_tile(a, b, h_in, row):
    a, b = _tile_prefix(a, b, row)
    return a * h_in + b


def _last_row(h):
    return jnp.broadcast_to(h[SUBLANES - 1:SUBLANES, :], h.shape)


def _conv_taps(x, shifted, cw, cb):
    out = cb + shifted[CONV_W - 1] * cw[0:1]
    for j in range(1, CONV_W - 1):
        out = out + shifted[CONV_W - 1 - j] * cw[j:j + 1]
    return out + x * cw[CONV_W - 1:CONV_W]


REC_PROJECTIONS_AHEAD = 5


def _projection_issue_order(n_pairs, n_z):
    return [(0, p) for p in range(n_pairs)] + [(1, p) for p in range(n_pairs)] + [(2, p) for p in range(n_z)]


def _projection_consume_order(n_pairs, n_z):
    order = [step for p in range(n_pairs) for step in ((0, p), (1, p))]
    return order + [(2, p) for p in range(n_z)]


def _rec_prompt_body(h_ref, gmix_ref, wq_ref, wxl_ref, wxh_ref, wgl_ref, wgh_ref, prev_ref, h0_ref, cw_ref, cb_ref,
                     wg_ref, ba_ref, bx_ref, lam_ref, g_ref, z_ref, o_ref, lru_ref, tail_ref, u_ref, xprev_ref,
                     a_ref, b_ref, gate_ref, carry_ref, *, tc, n_chunks):
    c = pl.program_id(1)

    @pl.when(c == 0)
    def _():
        xprev_ref[...] = prev_ref[...]
        for n in range(N_LRU_BLOCKS):
            carry_ref[n] = jnp.broadcast_to(h0_ref[:, n * LRU_BLOCK:(n + 1) * LRU_BLOCK], (SUBLANES, LRU_BLOCK))

    u_ref[...] = _rmsnorm(h_ref[...], gmix_ref[...]).astype(bf16)
    u = u_ref[...]
    pair = 2 * LRU_BLOCK
    n_pairs = LRU_W // pair
    pairs_per_half = LRU_HALF // pair
    branch_refs = ((wxl_ref, wxh_ref), (wgl_ref, wgh_ref))

    def project(branch, p):
        if branch == 2:
            w = wq_ref[:, p * pair:(p + 1) * pair]
        else:
            q = p % pairs_per_half
            w = branch_refs[branch][p // pairs_per_half][:, q * pair:(q + 1) * pair]
        return jnp.dot(u, w, preferred_element_type=f32)

    def store_z(p, z):
        z_ref[:, p * pair:(p + 1) * pair] = z

    def gates(p, x):
        lanes = slice(p * pair, (p + 1) * pair)
        big = jnp.concatenate([xprev_ref[:, lanes], x], axis=0)
        shifted = {k: pltpu.roll(big, k, axis=0)[SUBLANES:] for k in range(1, CONV_W)}
        xc = _conv_taps(x, shifted, cw_ref[:, lanes], cb_ref[:, lanes])
        for q in range(2):
            n = 2 * p + q
            sub = slice(q * LRU_BLOCK, (q + 1) * LRU_BLOCK)
            a, gated, mult = _lru_gates(xc[:, sub], n, n * LRU_BLOCK, wg_ref, ba_ref, bx_ref, lam_ref)
            b = gated * mult
            a_ref[n] = a
            b_ref[n] = b
            b_ref[n, 0:1, :] = jnp.where(c == 0, gated[0:1, :], b[0:1, :])
        return x[tc - SUBLANES:, :]

    def gelu_gate(p, gate):
        for q in range(2):
            gate_ref[2 * p + q] = jax.nn.gelu(gate[:, q * LRU_BLOCK:(q + 1) * LRU_BLOCK])

    n_z = Z_XR // pair
    issue = _projection_issue_order(n_pairs, n_z)
    consumers = [((gates, gelu_gate, store_z)[br], br, p) for br, p in _projection_consume_order(n_pairs, n_z)]
    projected, tails = {}, []
    for consume, br, p in consumers:
        while issue and (len(projected) < REC_PROJECTIONS_AHEAD or (br, p) not in projected):
            key = issue.pop(0)
            projected[key] = project(*key)
        out = consume(p, projected.pop((br, p)))
        if consume is gates:
            tails.append(out)
    tail = jnp.concatenate(tails, axis=1)
    xprev_ref[...] = tail
    tail_ref[...] = tail

    seg = tc // SUBLANES
    seg_rows = lambda r: pl.ds(r, SUBLANES, stride=seg)

    def local_scan(r, maps):
        out = []
        for n in range(N_LRU_BLOCKS):
            a = a_ref[n, seg_rows(r), :]
            a_cum = a * maps[2 * n]
            b_cum = a * maps[2 * n + 1] + b_ref[n, seg_rows(r), :]
            a_ref[n, seg_rows(r), :] = a_cum
            b_ref[n, seg_rows(r), :] = b_cum
            out += [a_cum, b_cum]
        return tuple(out)
    identity = (jnp.ones((SUBLANES, LRU_BLOCK), f32), jnp.zeros((SUBLANES, LRU_BLOCK), f32)) * N_LRU_BLOCKS
    seg_maps = lax.fori_loop(0, seg, local_scan, identity, unroll=True)

    row = lax.broadcasted_iota(jnp.int32, (SUBLANES, LRU_BLOCK), 0)
    h_in = []
    for n in range(N_LRU_BLOCKS):
        h_prev = carry_ref[n]
        a_cum, b_cum = _tile_prefix(seg_maps[2 * n], seg_maps[2 * n + 1], row)
        h_end = a_cum * h_prev + b_cum
        h_in.append(jnp.where(row == 0, h_prev, pltpu.roll(h_end, 1, axis=0)))
        carry_ref[n] = _last_row(h_end)

    def apply_scan(r, _):
        for n in range(N_LRU_BLOCKS):
            h = a_ref[n, seg_rows(r), :] * h_in[n] + b_ref[n, seg_rows(r), :]
            b_ref[n, seg_rows(r), :] = h * gate_ref[n, seg_rows(r), :]
        return 0
    lax.fori_loop(0, seg, apply_scan, 0, unroll=True)

    y = jnp.concatenate([b_ref[n] for n in range(N_LRU_BLOCKS)], axis=1)
    o_ref[...] = _rmsnorm(y, g_ref[...]).astype(o_ref.dtype)

    @pl.when(c == n_chunks - 1)
    def _():
        lru_ref[...] = jnp.concatenate([carry_ref[n] for n in range(N_LRU_BLOCKS)], axis=1)


def _rec_prompt(h, g_mix, w_in, prev8, h0, cw, cb, wg, ba, bx, lam, g, l, *, n_batch, t_len, tc):
    n_chunks = t_len // tc
    once = pl.Buffered(1)
    row_vec = pl.BlockSpec((None, 1, LRU_W), lambda b, c: (l, 0, 0), pipeline_mode=once)
    w_half = lambda col: pl.BlockSpec((D_MODEL, LRU_HALF), lambda b, c: (0, col // LRU_HALF), pipeline_mode=once)
    state = pl.BlockSpec((None, SUBLANES, LRU_W), lambda b, c: (b, 0, 0))
    rows = lambda width: pl.BlockSpec((tc, width), lambda b, c: (b * n_chunks + c, 0))
    blocks = tc * D_MODEL * 4 + tc * Z_XR * 4 + tc * LRU_W * 2 + 3 * SUBLANES * LRU_W * 4
    scratch = (D_MODEL * IN_W * 2 + wg[0].size * 2 + tc * D_MODEL * 2 + (3 * tc + 2 * SUBLANES) * LRU_W * 4
               + 8 * tc * 2 * LRU_BLOCK * 4)
    by_block = pltpu.VMEM((N_LRU_BLOCKS, tc, LRU_BLOCK), f32)
    assert tc % SUBLANES == 0
    return pl.pallas_call(
        functools.partial(_rec_prompt_body, tc=tc, n_chunks=n_chunks),
        grid=(n_batch, n_chunks),
        in_specs=[rows(D_MODEL),
                  pl.BlockSpec((None, 1, D_MODEL), lambda b, c: (l, 0, 0), pipeline_mode=once),
                  pl.BlockSpec((D_MODEL, Z_XR), lambda b, c: (0, 0), pipeline_mode=once),
                  w_half(Z_XR), w_half(Z_XR + LRU_HALF), w_half(Z_GR), w_half(Z_GR + LRU_HALF),
                  state,
                  pl.BlockSpec((None, 1, LRU_W), lambda b, c: (b, 0, 0)),
                  pl.BlockSpec((None, CONV_W, LRU_W), lambda b, c: (l, 0, 0), pipeline_mode=once),
                  row_vec,
                  pl.BlockSpec((None,) + wg.shape[1:], lambda b, c: (l, 0, 0, 0), pipeline_mode=once),
                  row_vec, row_vec, row_vec, row_vec],
        out_specs=[rows(Z_XR), rows(LRU_W), state, state],
        out_shape=[jax.ShapeDtypeStruct((n_batch * t_len, Z_XR), f32),
                   jax.ShapeDtypeStruct((n_batch * t_len, LRU_W), bf16),
                   jax.ShapeDtypeStruct((n_batch, SUBLANES, LRU_W), f32),
                   jax.ShapeDtypeStruct((n_batch, SUBLANES, LRU_W), f32)],
        scratch_shapes=[pltpu.VMEM((tc, D_MODEL), bf16),
                        pltpu.VMEM((SUBLANES, LRU_W), f32), by_block, by_block, by_block,
                        pltpu.VMEM((N_LRU_BLOCKS, SUBLANES, LRU_BLOCK), f32)],
        compiler_params=_params(("parallel", "arbitrary"), _vmem_limit(blocks, scratch)),
        name="rec_prompt",
    )(h, g_mix, w_in, w_in, w_in, w_in, w_in, prev8, h0, cw, cb, wg, ba, bx, lam, g)


def _rec_sample_body(z_ref, prev_ref, h0_ref, cw_ref, cb_ref, wg_ref, ba_ref, bx_ref, lam_ref, g_ref,
                     o_ref, lru_ref, xc_ref, a_ref, b_ref, *, n_batch):
    row = lax.broadcasted_iota(jnp.int32, (SUBLANES, LRU_W), 0)

    def conv_step(b, _):
        r = pl.multiple_of(b * SUBLANES, SUBLANES)
        x = z_ref[pl.ds(r, SUBLANES), Z_XR:Z_XR + LRU_W]
        prev = prev_ref[pl.ds(r, SUBLANES), :]
        shifted = {k: jnp.where(row >= k, pltpu.roll(x, k, axis=0), pltpu.roll(prev, k, axis=0))
                   for k in range(1, CONV_W)}
        xc_ref[pl.ds(r, SUBLANES), :] = _conv_taps(x, shifted, cw_ref[...], cb_ref[...])
        return 0
    lax.fori_loop(0, n_batch, conv_step, 0)

    def gates_step(n, _):
        c0 = pl.multiple_of(n * LRU_BLOCK, LRU_BLOCK)
        lanes = pl.ds(c0, LRU_BLOCK)
        a, gated, mult = _lru_gates(xc_ref[:, lanes], n, c0, wg_ref, ba_ref, bx_ref, lam_ref)
        a_ref[:, lanes] = a
        b_ref[:, lanes] = gated * mult
        return 0
    lax.fori_loop(0, N_LRU_BLOCKS, gates_step, 0)

    def scan_step(b, _):
        rows = pl.ds(pl.multiple_of(b * SUBLANES, SUBLANES), SUBLANES)
        h_in = jnp.broadcast_to(h0_ref[pl.ds(b, 1), :], (SUBLANES, LRU_W))
        hh = _scan_tile(a_ref[rows, :], b_ref[rows, :], h_in, row)
        b_ref[rows, :] = hh * jax.nn.gelu(z_ref[rows, Z_GR:Z_GR + LRU_W])
        lru_ref[pl.ds(b, 1), :] = hh[SUBLANES - 1:SUBLANES, :]
        return 0
    lax.fori_loop(0, n_batch, scan_step, 0)
    o_ref[...] = _rmsnorm(b_ref[...], g_ref[...]).astype(o_ref.dtype)


def _rec_sample(z, prev8, h0, cw, cb, wg, ba, bx, lam, g, l, *, n_batch, t_len):
    assert t_len == SUBLANES and PAST_LEN > 0
    rows = n_batch * t_len
    layer = lambda a: pl.BlockSpec((None,) + a.shape[1:], lambda i: (l,) + (0,) * (a.ndim - 1))
    blocks = z.size * 4 + (2 * rows + 2 * n_batch) * LRU_W * 4 + wg[0].size * 2
    return pl.pallas_call(
        functools.partial(_rec_sample_body, n_batch=n_batch),
        grid=(1,),
        in_specs=[pl.BlockSpec(z.shape, lambda i: (0, 0))] + [layer(a) for a in (prev8, h0, cw, cb, wg, ba, bx, lam, g)],
        out_specs=[pl.BlockSpec((rows, LRU_W), lambda i: (0, 0)), pl.BlockSpec((n_batch, LRU_W), lambda i: (0, 0))],
        out_shape=[jax.ShapeDtypeStruct((rows, LRU_W), bf16),
                   jax.ShapeDtypeStruct((n_batch, LRU_W), f32)],
        scratch_shapes=[pltpu.VMEM((rows, LRU_W), f32)] * 3,
        compiler_params=_params(("arbitrary",), _vmem_limit(blocks, 3 * rows * LRU_W * 4)),
        name="rec_sample",
    )(z, prev8, h0, cw, cb, wg, ba, bx, lam, g)


def _mix_mlp_body(h_ref, attn_ref, rec_ref, wo_ref, g_ref, wu_ref, wd_ref, o_ref, u_ref):
    @pl.when(pl.program_id(1) == 0)
    def _():
        o_ref[...] = h_ref[...] + jnp.dot(attn_ref[...], wo_ref[0:ATTN_W, :], preferred_element_type=f32)
        o_ref[...] += jnp.dot(rec_ref[...], wo_ref[ATTN_W:ATTN_W + LRU_W, :], preferred_element_type=f32)
        u_ref[...] = _rmsnorm(o_ref[...], g_ref[...]).astype(bf16)
    hid = jnp.dot(u_ref[...], wu_ref[...], preferred_element_type=f32)
    hid = jnp.square(jnp.maximum(hid, 0.0)).astype(bf16)
    o_ref[...] += jnp.dot(hid, wd_ref[...], preferred_element_type=f32)


def _mix_mlp(h, attn, rec, wo, g, wu, wd, l, *, tm, tf):
    m = h.shape[0]
    once = pl.Buffered(1)
    blocks = 2 * tm * D_MODEL * 4 + tm * (ATTN_W + LRU_W) * 2 + 2 * D_MODEL * tf * 2
    scratch = wo[0].size * 2 + tm * D_MODEL * 2 + tm * tf * 6
    return pl.pallas_call(
        _mix_mlp_body,
        grid=(m // tm, D_FF // tf),
        in_specs=[pl.BlockSpec((tm, D_MODEL), lambda i, f: (i, 0)),
                  pl.BlockSpec((tm, ATTN_W), lambda i, f: (i, 0)),
                  pl.BlockSpec((tm, LRU_W), lambda i, f: (i, 0)),
                  pl.BlockSpec((None,) + wo.shape[1:], lambda i, f: (l, 0, 0), pipeline_mode=once),
                  pl.BlockSpec((None, 1, D_MODEL), lambda i, f: (l, 0, 0), pipeline_mode=once),
                  pl.BlockSpec((None, D_MODEL, tf), lambda i, f: (l, 0, f)),
                  pl.BlockSpec((None, tf, D_MODEL), lambda i, f: (l, f, 0))],
        out_specs=pl.BlockSpec((tm, D_MODEL), lambda i, f: (i, 0)),
        out_shape=jax.ShapeDtypeStruct((m, D_MODEL), f32),
        scratch_shapes=[pltpu.VMEM((tm, D_MODEL), bf16)],
        compiler_params=_params(("parallel", "arbitrary"), _vmem_limit(blocks, scratch)),
        name="mix_mlp",
    )(h, attn, rec, wo, g, wu, wd)


def _mix_mlp_cast_body(h_ref, attn_ref, rec_ref, wo_ref, g_ref, wu_ref, wd_ref,
                       o_ref, wob_ref, wub_ref, wdb_ref, u_ref):
    @pl.when(pl.program_id(1) == 0)
    def _():
        wob_ref[...] = wo_ref[...].astype(bf16)
        o_ref[...] = h_ref[...] + jnp.dot(attn_ref[...], wob_ref[0:ATTN_W, :], preferred_element_type=f32)
        o_ref[...] += jnp.dot(rec_ref[...], wob_ref[ATTN_W:ATTN_W + LRU_W, :], preferred_element_type=f32)
        u_ref[...] = _rmsnorm(o_ref[...], g_ref[...]).astype(bf16)
    wub_ref[...] = wu_ref[...].astype(bf16)
    wdb_ref[...] = wd_ref[...].astype(bf16)
    hid = jnp.dot(u_ref[...], wub_ref[...], preferred_element_type=f32)
    hid = jnp.square(jnp.maximum(hid, 0.0)).astype(bf16)
    o_ref[...] += jnp.dot(hid, wdb_ref[...], preferred_element_type=f32)


def _mix_mlp_cast(h, attn, rec, wo, g, wu, wd, l, *, tf):
    m = h.shape[0]
    once = pl.Buffered(1)
    blocks = 2 * D_MODEL * tf * (4 + 2)
    scratch = (2 * m * D_MODEL * 4 + m * (ATTN_W + LRU_W) * 2 + wo[0].size * (4 + 2) + m * D_MODEL * 2
               + m * tf * 6)
    return pl.pallas_call(
        _mix_mlp_cast_body,
        grid=(1, D_FF // tf),
        in_specs=[pl.BlockSpec((m, D_MODEL), lambda i, f: (0, 0), pipeline_mode=once),
                  pl.BlockSpec((m, ATTN_W), lambda i, f: (0, 0), pipeline_mode=once),
                  pl.BlockSpec((m, LRU_W), lambda i, f: (0, 0), pipeline_mode=once),
                  pl.BlockSpec((None,) + wo.shape[1:], lambda i, f: (l, 0, 0), pipeline_mode=once),
                  pl.BlockSpec((None, 1, D_MODEL), lambda i, f: (l, 0, 0), pipeline_mode=once),
                  pl.BlockSpec((None, D_MODEL, tf), lambda i, f: (l, 0, f)),
                  pl.BlockSpec((None, tf, D_MODEL), lambda i, f: (l, f, 0))],
        out_specs=[pl.BlockSpec((m, D_MODEL), lambda i, f: (0, 0)),
                   pl.BlockSpec(wo.shape[1:], lambda i, f: (0, 0)),
                   pl.BlockSpec((D_MODEL, tf), lambda i, f: (0, f)),
                   pl.BlockSpec((tf, D_MODEL), lambda i, f: (f, 0))],
        out_shape=[jax.ShapeDtypeStruct((m, D_MODEL), f32),
                   jax.ShapeDtypeStruct(wo.shape[1:], bf16),
                   jax.ShapeDtypeStruct(wu.shape[1:], bf16),
                   jax.ShapeDtypeStruct(wd.shape[1:], bf16)],
        scratch_shapes=[pltpu.VMEM((m, D_MODEL), bf16)],
        compiler_params=_params(("arbitrary", "arbitrary"), _vmem_limit(blocks, scratch)),
        name="mix_mlp_cast",
    )(h, attn, rec, wo, g, wu, wd)


def _final_norm_body(h_ref, g_ref, o_ref):
    o_ref[0] = _rmsnorm(h_ref[0], g_ref[...])


def _final_norm(h, g, *, skip, tr):
    n_batch, t_len, _ = h.shape
    s_len = t_len - skip
    tiles = tr // SUBLANES
    skip_tiles = skip // SUBLANES
    h4 = h.reshape(n_batch, t_len // SUBLANES, SUBLANES, D_MODEL)
    out = pl.pallas_call(
        _final_norm_body,
        grid=(n_batch, s_len // tr),
        in_specs=[pl.BlockSpec((pl.Element(1), pl.Element(tiles), pl.Element(SUBLANES), pl.Element(D_MODEL)),
                               lambda b, r: (b, skip_tiles + r * tiles, 0, 0)),
                  pl.BlockSpec((1, D_MODEL), lambda b, r: (0, 0))],
        out_specs=pl.BlockSpec((1, tiles, SUBLANES, D_MODEL), lambda b, r: (b, r, 0, 0)),
        out_shape=jax.ShapeDtypeStruct((n_batch, s_len // SUBLANES, SUBLANES, D_MODEL), f32),
        compiler_params=_params(("parallel", "parallel"), _vmem_limit(2 * tr * D_MODEL * 4)),
        name="final_norm",
    )(h4, g)
    return out.reshape(n_batch, s_len, D_MODEL)


def _row_tile(m, candidates):
    for tm in candidates:
        if m % tm == 0:
            return tm
    raise ValueError(f"no row tile for {m} rows")


def kernel(x_prompt, x_sample, cache_k_win, cache_v_win, state_conv, state_lru, meta_tokens, norm_mix_g, w_in,
           conv_w, conv_b, w_gate_a, b_gate_a, w_gate_x, b_gate_x, lru_lambda, attn_sinks, rel_bias, attn_out_g,
           rec_out_g, w_out, norm_mlp_g, w_up, w_down, final_norm_g):
    n_p, s_p, _ = x_prompt.shape
    n_s, t_s, _ = x_sample.shape
    t_p = N_META + s_p
    buf = cache_k_win.shape[2]
    assert t_p % BF16_ROWS == 0 and buf == WINDOW

    w_gates = jnp.concatenate([w_gate_a, w_gate_x], axis=-1).astype(bf16)
    rows3 = lambda p: p[:, None, :]
    g_mix, g_mlp, g_attn, g_rec = rows3(norm_mix_g), rows3(norm_mlp_g), rows3(attn_out_g), rows3(rec_out_g)
    cb3, ba3, bx3, lam3 = rows3(conv_b), rows3(b_gate_a), rows3(b_gate_x), rows3(lru_lambda)
    g_attn_t = jnp.broadcast_to(
        attn_out_g.reshape(DEPTH, N_Q_HEADS, HEAD_DIM).transpose(0, 2, 1)[..., None],
        (DEPTH, HEAD_DIM, N_Q_HEADS, WINDOW)).reshape(DEPTH, HEAD_DIM, ATTN_W)

    meta = jnp.broadcast_to(meta_tokens.astype(x_prompt.dtype)[None], (n_p, N_META, D_MODEL))
    hp = jnp.concatenate([meta, x_prompt], axis=1).reshape(n_p * t_p, D_MODEL)
    hs = x_sample.reshape(n_s * t_s, D_MODEL)

    bias_qk, bias_kq = _bias_tables(rel_bias)
    bias_qk = bias_qk.reshape(N_Q_HEADS * WINDOW, 2 * WINDOW)
    prev_p = jnp.zeros((n_p, SUBLANES, LRU_W), f32)
    h0_p = jnp.zeros((n_p, 1, LRU_W), f32)
    ck = cache_k_win.reshape(DEPTH, n_s, buf * N_KV_HEADS, HEAD_DIM)
    cv = cache_v_win.reshape(DEPTH, n_s, buf * N_KV_HEADS, HEAD_DIM)
    prev_s = jnp.pad(state_conv, ((0, 0), (0, 0), (SUBLANES - (CONV_W - 1), 0), (0, 0)))
    prev_s = prev_s.reshape(DEPTH, n_s * SUBLANES, LRU_W)

    tm_p = _row_tile(n_p * t_p, (688, 344))
    tc_p = _row_tile(t_p, (688, 344, 48, 16))
    tn = IN_W // 2

    kp_l, vp_l, cp_l, lp_l, ks_l, vs_l, cs_l, ls_l = ([] for _ in range(8))
    for l in range(DEPTH):
        z, w_in_b = _inproj_cast(hs, g_mix, w_in, l, tn=tn)
        attn, nk, nv = _attn_sample(z, ck, cv, bias_qk, attn_sinks, g_attn, l, n_batch=n_s, t_len=t_s)
        rec, lru = _rec_sample(z, prev_s, state_lru, conv_w, cb3, w_gates, ba3, bx3, lam3, g_rec, l,
                               n_batch=n_s, t_len=t_s)
        hs, w_out_b, w_up_b, w_down_b = _mix_mlp_cast(hs, attn, rec, w_out, g_mlp, w_up, w_down, l, tf=512)
        z3 = z.reshape(n_s, t_s, IN_W)
        ks_l.append(nk)
        vs_l.append(nv)
        cs_l.append(z3[:, t_s - (CONV_W - 1):, Z_XR:Z_XR + LRU_W])
        ls_l.append(lru)
        z, rec, lru, x_tail = _rec_prompt(hp, g_mix, w_in_b, prev_p, h0_p, conv_w, cb3, w_gates, ba3, bx3, lam3,
                                          g_rec, l, n_batch=n_p, t_len=t_p, tc=tc_p)
        attn = _attn_prompt(z, bias_kq, attn_sinks, g_attn_t, l, n_batch=n_p, t_len=t_p)
        hp = _mix_mlp(hp, attn, rec, w_out_b[None], g_mlp[l:l + 1], w_up_b[None], w_down_b[None], 0,
                      tm=tm_p, tf=1024)
        z3 = z.reshape(n_p, t_p, Z_XR)
        kp_l.append(z3[:, t_p - WINDOW:, Z_K:Z_K + KV_W].reshape(n_p, WINDOW, N_KV_HEADS, HEAD_DIM))
        vp_l.append(z3[:, t_p - WINDOW:, Z_V:Z_V + KV_W].reshape(n_p, WINDOW, N_KV_HEADS, HEAD_DIM))
        cp_l.append(x_tail[:, SUBLANES - (CONV_W - 1):])
        lp_l.append(lru[:, 0])

    g_fin = final_norm_g[None, :]
    y_prompt = _final_norm(hp.reshape(n_p, t_p, D_MODEL), g_fin, skip=N_META, tr=512)
    y_sample = _final_norm(hs.reshape(1, n_s * t_s, D_MODEL), g_fin, skip=0, tr=n_s * t_s)
    y_sample = y_sample.reshape(n_s, t_s, D_MODEL)
    cache_shape = (DEPTH, n_s, buf, N_KV_HEADS, HEAD_DIM)
    return (y_prompt, y_sample,
            jnp.stack(kp_l), jnp.stack(vp_l), jnp.stack(cp_l), jnp.stack(lp_l),
            jnp.stack(ks_l).reshape(cache_shape), jnp.stack(vs_l).reshape(cache_shape),
            jnp.stack(cs_l), jnp.stack(ls_l))
```
